```python
import math
import functools
import jax
import jax.numpy as jnp
from jax import lax
import numpy as np

D_MODEL = 1024
BATCH = 4
SEQ = 8192
DEPTH = 1

GRID_W = 64
CTX_LEN = 256

D_HYENA = D_MODEL // 2
SHORT_CONV = 3
FILTER_BANDS = 16
FILTER_EMB = 1 + 2 * FILTER_BANDS
FILTER_ORDER = 64
DECAY_TARGET = 1e-2
FAST_DECAY_PCT = 0.3
SLOW_DECAY_PCT = 1.5

HEAD_DIM = 64
D_ATTN = D_MODEL // 2
N_HEADS = D_ATTN // HEAD_DIM
N_KV_HEADS = N_HEADS // 4
GQA_GROUP = N_HEADS // N_KV_HEADS
D_KV = N_KV_HEADS * HEAD_DIM
WINDOW = 128
BLOCK = 128
NB_SIDE = WINDOW // BLOCK
KEYS_PER_BLOCK = (2 * NB_SIDE + 1) * BLOCK
ROPE_BASE = 10000.0
NEG_INF = -1e30

N_BRANCHES = 2
IN_SPLITS = (3 * D_HYENA, 3 * D_HYENA + D_ATTN, 3 * D_HYENA + D_ATTN + D_KV, 3 * D_HYENA + D_ATTN + 2 * D_KV)
IN_COLS = IN_SPLITS[-1] + N_BRANCHES * D_MODEL

N_GROUPS = 4
EXPERTS_PER_GROUP = 4
N_EXPERTS = N_GROUPS * EXPERTS_PER_GROUP
TOP_K = 2
D_EXPERT = D_MODEL // 4

LN_EPS = 1e-5
DEEPNORM_ALPHA = (2.0 * DEPTH) ** 0.25
DEEPNORM_BETA = (8.0 * DEPTH) ** -0.25

kernel_name = 'hybrid_hyena_swa_hmoe_diffusion_block'


def standardize(x):
    xf = x.astype(jnp.float32)
    mu = jnp.mean(xf, axis=-1, keepdims=True)
    var = jnp.mean(jnp.square(xf - mu), axis=-1, keepdims=True)
    return (xf - mu) * lax.rsqrt(var + LN_EPS)


def layer_norm(x, g, b):
    return (standardize(x) * g.astype(jnp.float32) + b.astype(jnp.float32)).astype(x.dtype)


def modulate(x, shift, scale):
    return (standardize(x) * (1.0 + scale.astype(jnp.float32)) + shift.astype(jnp.float32)).astype(x.dtype)


def adaln(cond, w, b):
    m = jax.nn.silu(cond) @ w + b
    return [t[:, None, :] for t in jnp.split(m, 6, axis=-1)]


def axial_rope(rows):
    row = jnp.repeat(jnp.arange(rows, dtype=jnp.float32), GRID_W)
    col = jnp.tile(jnp.arange(GRID_W, dtype=jnp.float32), rows)
    half = HEAD_DIM // 2
    inv_freq = ROPE_BASE ** (-jnp.arange(0, half, 2, dtype=jnp.float32) / half)
    ang = jnp.concatenate([row[:, None] * inv_freq, col[:, None] * inv_freq], axis=-1)
    return jnp.cos(ang), jnp.sin(ang)


def apply_rope(x, cos, sin):
    xf = x.astype(jnp.float32).reshape(x.shape[:-1] + (HEAD_DIM // 2, 2))
    x1, x2 = xf[..., 0], xf[..., 1]
    cs, sn = cos[None, :, None, :], sin[None, :, None, :]
    return jnp.stack([x1 * cs - x2 * sn, x1 * sn + x2 * cs], axis=-1).reshape(x.shape).astype(x.dtype)


def short_conv(u, w, b):
    r = SHORT_CONV // 2
    y = lax.conv_general_dilated(u, w, window_strides=(1,), padding=((r, r),),
                                 dimension_numbers=('NWC', 'WIO', 'NWC'), feature_group_count=u.shape[-1])
    return y + b


def hyena_filters(L, w1, b1, w2, b2, w3, b3, w4, freq):
    f32 = jnp.float32
    t = jnp.linspace(0.0, 1.0, L, dtype=f32)[:, None]
    w = 2.0 * math.pi * jnp.arange(L, dtype=f32)[:, None] / L
    bands = jnp.linspace(1e-4, FILTER_BANDS - 1, FILTER_BANDS, dtype=f32)
    z = jnp.concatenate([t, jnp.cos(bands * w), -jnp.sin(bands * w)], axis=-1)
    fr = freq.astype(f32)
    hdn = jnp.sin(fr * (z @ w1.astype(f32) + b1.astype(f32)))
    hdn = jnp.sin(fr * (hdn @ w2.astype(f32) + b2.astype(f32)))
    hdn = jnp.sin(fr * (hdn @ w3.astype(f32) + b3.astype(f32)))
    h = hdn @ w4.astype(f32)
    max_decay = math.log(DECAY_TARGET) / FAST_DECAY_PCT
    min_decay = math.log(DECAY_TARGET) / SLOW_DECAY_PCT
    deltas = jnp.linspace(min_decay, max_decay, D_HYENA, dtype=f32)
    decay = jnp.exp(-t * jnp.abs(deltas))
    h_fwd = h[:, :D_HYENA] * decay
    h_bwd = h[:, D_HYENA:] * decay
    h2 = jnp.concatenate([h_fwd, h_bwd[::-1]], axis=0)
    return h2 * lax.rsqrt(jnp.sum(h2 * h2, axis=0, keepdims=True) + 1e-6)


def fft_long_conv(v, h2):
    L = v.shape[1]
    vf = jnp.fft.rfft(v.astype(jnp.float32), n=2 * L, axis=1)
    hf = jnp.fft.rfft(h2, n=2 * L, axis=0)
    return jnp.fft.irfft(vf * hf[None], n=2 * L, axis=1)[:, :L]


def hyena_branch(u, conv_w, conv_b, w1, b1, w2, b2, w3, b3, w4, freq, bias):
    u = short_conv(u, conv_w, conv_b)
    x0, x1, v = jnp.split(u, 3, axis=-1)
    h2 = hyena_filters(u.shape[1], w1, b1, w2, b2, w3, b3, w4, freq)
    v = v * x1
    v = fft_long_conv(v, h2).astype(u.dtype) + bias * v
    return v * x0


def windowed_attention(q, k, v, k_ctx, v_ctx, sinks):
    B, L = q.shape[:2]
    nb = L // BLOCK
    pad = NB_SIDE * BLOCK
    qb = q.reshape(B, nb, BLOCK, N_KV_HEADS, GQA_GROUP, HEAD_DIM)

    def neighbourhood(t):
        tb = jnp.pad(t, ((0, 0), (pad, pad), (0, 0), (0, 0))).reshape(B, nb + 2 * NB_SIDE, BLOCK, N_KV_HEADS, HEAD_DIM)
        return jnp.concatenate([tb[:, j:j + nb] for j in range(2 * NB_SIDE + 1)], axis=2)

    k_win, v_win = neighbourhood(k), neighbourhood(v)
    scale = HEAD_DIM ** -0.5
    s_win = jnp.einsum('bnqhgd,bnkhd->bnhgqk', qb, k_win, preferred_element_type=jnp.float32) * scale
    s_ctx = jnp.einsum('bnqhgd,bchd->bnhgqc', qb, k_ctx, preferred_element_type=jnp.float32) * scale
    key_off = jnp.arange(KEYS_PER_BLOCK)[None, :] - pad
    band = jnp.abs(key_off - jnp.arange(BLOCK)[:, None]) <= WINDOW
    k_abs = jnp.arange(nb)[:, None] * BLOCK + key_off
    mask = band[None] & ((k_abs >= 0) & (k_abs < L))[:, None, :]
    s_win = jnp.where(mask[None, :, None, None], s_win, NEG_INF)
    sink = sinks.astype(jnp.float32).reshape(1, 1, N_KV_HEADS, GQA_GROUP, 1, 1)
    m = jnp.maximum(jnp.maximum(s_win.max(-1, keepdims=True), s_ctx.max(-1, keepdims=True)), sink)
    e_win = jnp.exp(s_win - m)
    e_ctx = jnp.exp(s_ctx - m)
    denom = e_win.sum(-1, keepdims=True) + e_ctx.sum(-1, keepdims=True) + jnp.exp(sink - m)
    p_win = (e_win / denom).astype(v.dtype)
    p_ctx = (e_ctx / denom).astype(v.dtype)
    o = jnp.einsum('bnhgqk,bnkhd->bnqhgd', p_win, v_win) + jnp.einsum('bnhgqc,bchd->bnqhgd', p_ctx, v_ctx)
    return o.reshape(B, L, D_ATTN)


def latent_attention(q, k, v, cos, sin, k_ctx, v_ctx, sinks):
    return windowed_attention(apply_rope(q, cos, sin), apply_rope(k, cos, sin), v, k_ctx, v_ctx, sinks)


def context_attention(q, k, v, sinks):
    B, C = q.shape[:2]
    qg = q.reshape(B, C, N_KV_HEADS, GQA_GROUP, HEAD_DIM)
    s = jnp.einsum('bqhgd,bkhd->bhgqk', qg, k, preferred_element_type=jnp.float32) * HEAD_DIM ** -0.5
    sink = jnp.broadcast_to(sinks.astype(jnp.float32).reshape(1, N_KV_HEADS, GQA_GROUP, 1, 1), s.shape[:-1] + (1,))
    p = jax.nn.softmax(jnp.concatenate([s, sink], axis=-1), axis=-1)[..., :C]
    o = jnp.einsum('bhgqk,bkhd->bqhgd', p.astype(v.dtype), v)
    return o.reshape(B, C, D_ATTN)


def context_kv(h_ctx, w_in_l):
    B, C = h_ctx.shape[:2]
    k, v = jnp.split(h_ctx @ w_in_l[:, IN_SPLITS[1]:IN_SPLITS[3]], 2, axis=-1)
    return k.reshape(B, C, N_KV_HEADS, HEAD_DIM), v.reshape(B, C, N_KV_HEADS, HEAD_DIM)


def token_mixer(h, w_in_l, hyena_p, w_bh, w_ba, w_o, attend):
    B, L, _ = h.shape
    u_hy, q, k, v, gates = jnp.split(h @ w_in_l, IN_SPLITS, axis=-1)
    y_hy = hyena_branch(u_hy, *hyena_p)
    y_at = attend(q.reshape(B, L, N_HEADS, HEAD_DIM), k.reshape(B, L, N_KV_HEADS, HEAD_DIM),
                  v.reshape(B, L, N_KV_HEADS, HEAD_DIM))
    g_hy, g_at = jnp.split(jax.nn.sigmoid(gates.astype(jnp.float32)).astype(h.dtype), N_BRANCHES, axis=-1)
    merged = g_hy * (y_hy @ w_bh) + g_at * (y_at @ w_ba)
    return merged @ w_o


def hier_moe(h, w_group, b_group, w_router, b_router, w_gate, w_up, w_down):
    B, L, D = h.shape
    t = h.reshape(B * L, D)
    group_logits = (t @ w_group).astype(jnp.float32) + b_group.astype(jnp.float32)
    group_sel = jnp.argmax(group_logits, axis=-1)
    group_p = jnp.take_along_axis(jax.nn.softmax(group_logits, axis=-1), group_sel[:, None], axis=-1)
    expert_logits = ((t @ w_router).astype(jnp.float32) + b_router.astype(jnp.float32)).reshape(-1, N_GROUPS, EXPERTS_PER_GROUP)
    in_group = jnp.take_along_axis(expert_logits, group_sel[:, None, None], axis=1)[:, 0]
    top_val, top_idx = lax.top_k(in_group, TOP_K)
    top_w = jax.nn.softmax(top_val, axis=-1) * group_p
    expert_id = group_sel[:, None] * EXPERTS_PER_GROUP + top_idx
    gate = jnp.einsum('tk,tke->te', top_w, jax.nn.one_hot(expert_id, N_EXPERTS, dtype=jnp.float32)).astype(t.dtype)
    y = jnp.zeros_like(t)
    for gi in range(N_GROUPS):
        sl = slice(gi * EXPERTS_PER_GROUP, (gi + 1) * EXPERTS_PER_GROUP)
        hid = jax.nn.silu(jnp.einsum('td,edf->tef', t, w_gate[sl])) * jnp.einsum('td,edf->tef', t, w_up[sl])
        y = y + jnp.einsum('tef,efd->td', hid * gate[:, sl, None], w_down[sl])
    return y.reshape(B, L, D)


def setup_inputs(seed: int = 0) -> dict:
    key = jax.random.key(seed)
    ks = iter(jax.random.split(key, 40))

    def nrm(shape, scale=1.0):
        return jax.random.normal(next(ks), shape, jnp.float32) * scale

    D = D_MODEL
    return {
        'x': nrm((BATCH, SEQ, D)),
        'c': nrm((BATCH, D)),
        'ctx': nrm((BATCH, CTX_LEN, D)),
        'c_ctx': nrm((D,)),
        'ada_w': nrm((DEPTH, D, 6 * D), 0.5 * D ** -0.5),
        'ada_b': nrm((DEPTH, 6 * D), 0.02),
        'w_in': nrm((DEPTH, D, IN_COLS), D ** -0.5),
        'hy_conv_w': nrm((DEPTH, SHORT_CONV, 1, 3 * D_HYENA), SHORT_CONV ** -0.5),
        'hy_conv_b': nrm((DEPTH, 3 * D_HYENA), 0.02),
        'hy_w1': nrm((DEPTH, FILTER_EMB, FILTER_ORDER), FILTER_EMB ** -0.5),
        'hy_b1': nrm((DEPTH, FILTER_ORDER), 0.1),
        'hy_w2': nrm((DEPTH, FILTER_ORDER, FILTER_ORDER), FILTER_ORDER ** -0.5),
        'hy_b2': nrm((DEPTH, FILTER_ORDER), 0.1),
        'hy_w3': nrm((DEPTH, FILTER_ORDER, FILTER_ORDER), FILTER_ORDER ** -0.5),
        'hy_b3': nrm((DEPTH, FILTER_ORDER), 0.1),
        'hy_w4': nrm((DEPTH, FILTER_ORDER, 2 * D_HYENA), FILTER_ORDER ** -0.5),
        'hy_freq': 1.0 + nrm((DEPTH, FILTER_ORDER), 0.02),
        'hy_bias': nrm((DEPTH, D_HYENA)),
        'attn_sinks': nrm((DEPTH, N_HEADS)),
        'w_branch_hy': nrm((DEPTH, D_HYENA, D), D_HYENA ** -0.5),
        'w_branch_attn': nrm((DEPTH, D_ATTN, D), D_ATTN ** -0.5),
        'w_out': nrm((DEPTH, D, D), DEEPNORM_BETA * D ** -0.5),
        'ln1_g': 1.0 + nrm((DEPTH, D), 0.02),
        'ln1_b': nrm((DEPTH, D), 0.02),
        'w_group': nrm((DEPTH, D, N_GROUPS), D ** -0.5),
        'b_group': nrm((DEPTH, N_GROUPS), 0.01),
        'w_router': nrm((DEPTH, D, N_EXPERTS), D ** -0.5),
        'b_router': nrm((DEPTH, N_EXPERTS), 0.01),
        'w_gate_e': nrm((DEPTH, N_EXPERTS, D, D_EXPERT), D ** -0.5),
        'w_up_e': nrm((DEPTH, N_EXPERTS, D, D_EXPERT), D ** -0.5),
        'w_down_e': nrm((DEPTH, N_EXPERTS, D_EXPERT, D), DEEPNORM_BETA * D_EXPERT ** -0.5),
        'ln2_g': 1.0 + nrm((DEPTH, D), 0.02),
        'ln2_b': nrm((DEPTH, D), 0.02),
    }


def reference(x, c, ctx, c_ctx, ada_w, ada_b, w_in, hy_conv_w, hy_conv_b, hy_w1, hy_b1, hy_w2, hy_b2,
              hy_w3, hy_b3, hy_w4, hy_freq, hy_bias, attn_sinks, w_branch_hy, w_branch_attn, w_out,
              ln1_g, ln1_b, w_group, b_group, w_router, b_router, w_gate_e, w_up_e, w_down_e, ln2_g, ln2_b):
    L = x.shape[1]
    rows = L // GRID_W
    cos, sin = axial_rope(rows)
    for l in range(DEPTH):
        hyena_p = (hy_conv_w[l], hy_conv_b[l], hy_w1[l], hy_b1[l], hy_w2[l], hy_b2[l],
                   hy_w3[l], hy_b3[l], hy_w4[l], hy_freq[l], hy_bias[l])
        moe_p = (w_group[l], b_group[l], w_router[l], b_router[l], w_gate_e[l], w_up_e[l], w_down_e[l])
        sh1, sc1, g1, sh2, sc2, g2 = adaln(c, ada_w[l], ada_b[l])
        csh1, csc1, cg1, csh2, csc2, cg2 = adaln(c_ctx[None], ada_w[l], ada_b[l])
        h_ctx = modulate(ctx, csh1, csc1)
        k_ctx, v_ctx = context_kv(h_ctx, w_in[l])
        attend_lat = functools.partial(latent_attention, cos=cos, sin=sin, k_ctx=k_ctx, v_ctx=v_ctx, sinks=attn_sinks[l])
        mix = token_mixer(modulate(x, sh1, sc1), w_in[l], hyena_p, w_branch_hy[l], w_branch_attn[l], w_out[l], attend_lat)
        x = layer_norm(DEEPNORM_ALPHA * x + g1 * mix, ln1_g[l], ln1_b[l])
        x = layer_norm(DEEPNORM_ALPHA * x + g2 * hier_moe(modulate(x, sh2, sc2), *moe_p), ln2_g[l], ln2_b[l])
        if l < DEPTH - 1:
            attend_ctx = functools.partial(context_attention, sinks=attn_sinks[l])
            mix_c = token_mixer(h_ctx, w_in[l], hyena_p, w_branch_hy[l], w_branch_attn[l], w_out[l], attend_ctx)
            ctx = layer_norm(DEEPNORM_ALPHA * ctx + cg1 * mix_c, ln1_g[l], ln1_b[l])
            ctx = layer_norm(DEEPNORM_ALPHA * ctx + cg2 * hier_moe(modulate(ctx, csh2, csc2), *moe_p), ln2_g[l], ln2_b[l])
    return x
```

```python
import functools
import math

import numpy as np
import jax
import jax.numpy as jnp
from jax import lax
from jax.experimental import pallas as pl
from jax.experimental.pallas import tpu as pltpu

F32 = jnp.float32
BF16 = jnp.bfloat16

D_MODEL = 1024
GRID_W = 64
D_HYENA = D_MODEL // 2
SHORT_CONV = 3
FILTER_BANDS = 16
FILTER_EMB = 1 + 2 * FILTER_BANDS
FILTER_ORDER = 64
DECAY_TARGET = 1e-2
FAST_DECAY_PCT = 0.3
SLOW_DECAY_PCT = 1.5
HEAD_DIM = 64
D_ATTN = D_MODEL // 2
N_HEADS = D_ATTN // HEAD_DIM
N_KV_HEADS = N_HEADS // 4
GQA_GROUP = N_HEADS // N_KV_HEADS
D_KV = N_KV_HEADS * HEAD_DIM
WINDOW = 128
BLOCK = 128
ROPE_BASE = 10000.0
NEG_INF = -1e30
N_GROUPS = 4
EXPERTS_PER_GROUP = 4
N_EXPERTS = N_GROUPS * EXPERTS_PER_GROUP
D_EXPERT = D_MODEL // 4
LN_EPS = 1e-5
DEPTH = 1
DEEPNORM_ALPHA = (2.0 * DEPTH) ** 0.25

LANES = 128
SUBLANES = 8
VMEM_LIMIT = 56 * 1024 * 1024

FFT_N2 = 128
KH = 65
KP = 72
ZPITCH = 2 * FFT_N2 + SUBLANES
TPITCH = FFT_N2 + SUBLANES


def _dot(a, b):
    return jnp.dot(a, b, preferred_element_type=F32)


def _dot_nt(a, b):
    return lax.dot_general(a, b, (((1,), (1,)), ((), ())), preferred_element_type=F32)


def _split(a):
    hi = a.astype(BF16)
    lo = (a - hi.astype(F32)).astype(BF16)
    return hi, lo


def _dot3(a, b):
    ah, al = _split(a)
    bh, bl = _split(b)
    return _dot(ah, bh) + _dot(al, bh) + _dot(ah, bl)


def _standardize(x):
    mu = jnp.mean(x, axis=-1, keepdims=True)
    xc = x - mu
    var = jnp.mean(xc * xc, axis=-1, keepdims=True)
    return xc * lax.rsqrt(var + LN_EPS)


def _params(sem, vmem=VMEM_LIMIT):
    return pltpu.CompilerParams(dimension_semantics=sem, vmem_limit_bytes=vmem)


def _const_spec(shape):
    nd = len(shape)
    return pl.BlockSpec(shape, lambda *_: (0,) * nd, pipeline_mode=pl.Buffered(1))


def _adaln_kernel(c_ref, w_ref, b_ref, o_ref):
    s = c_ref[...]
    s = s * jax.nn.sigmoid(s)
    o_ref[...] = _dot3(s, w_ref[...]) + b_ref[...]


def _adaln(cond, w, b):
    n, d = cond.shape
    cols = w.shape[1]
    bc = 1024
    return pl.pallas_call(
        _adaln_kernel,
        grid=(cols // bc,),
        in_specs=[pl.BlockSpec((n, d), lambda j: (0, 0)),
                  pl.BlockSpec((d, bc), lambda j: (0, j)),
                  pl.BlockSpec((1, bc), lambda j: (0, j))],
        out_specs=pl.BlockSpec((n, bc), lambda j: (0, j)),
        out_shape=jax.ShapeDtypeStruct((n, cols), F32),
        compiler_params=_params(("arbitrary",)),
        name="adaln",
    )(cond, w, b.reshape(1, cols))


def _ctx_kv_kernel(x_ref, sh_ref, sc_ref, w_ref, k_ref, v_ref):
    h = _standardize(x_ref[...]) * (1.0 + sc_ref[...]) + sh_ref[...]
    kv = _dot(h.astype(BF16), w_ref[...])
    half = k_ref.shape[1]
    k_ref[...] = kv[:, :half].astype(BF16)
    v_ref[...] = kv[:, half:].astype(BF16)


def _ctx_kv(ctx2d, sh, sc, w_kv, rows):
    n, d = ctx2d.shape
    half = w_kv.shape[1] // 2
    return pl.pallas_call(
        _ctx_kv_kernel,
        grid=(n // rows,),
        in_specs=[pl.BlockSpec((rows, d), lambda i: (i, 0)),
                  pl.BlockSpec((1, d), lambda i: (0, 0)),
                  pl.BlockSpec((1, d), lambda i: (0, 0)),
                  pl.BlockSpec(w_kv.shape, lambda i: (0, 0))],
        out_specs=[pl.BlockSpec((rows, half), lambda i: (i, 0)),
                   pl.BlockSpec((rows, half), lambda i: (i, 0))],
        out_shape=[jax.ShapeDtypeStruct((n, half), BF16)] * 2,
        compiler_params=_params(("arbitrary",)),
        name="ctx_kv",
    )(ctx2d, sh, sc, w_kv)


def _rope(x, cos_t, sin_t):
    width = x.shape[1]
    reps = width // LANES
    c = jnp.concatenate([cos_t] * reps, axis=1)
    s = jnp.concatenate([sin_t] * reps, axis=1)
    half = HEAD_DIM // 2
    lane = lax.broadcasted_iota(jnp.int32, x.shape, 1)
    first_half = (lane & (HEAD_DIM - 1)) < half
    partner = jnp.where(first_half, pltpu.roll(x, width - half, 1), pltpu.roll(x, half, 1))
    return x * c + partner * s


def _in_proj_kernel(x_ref, sh_ref, sc_ref, w_ref, cos_ref, sin_ref, u_ref, q_ref, k_ref, v_ref):
    h = (_standardize(x_ref[...]) * (1.0 + sc_ref[0]) + sh_ref[0]).astype(BF16)
    n_u = u_ref.shape[1]
    n_q = q_ref.shape[1]
    n_k = k_ref.shape[1]
    u_ref[...] = _dot(h, w_ref[:, :n_u]).astype(BF16)
    cos_t = cos_ref[...]
    sin_t = sin_ref[...]
    q = _dot(h, w_ref[:, n_u:n_u + n_q])
    q_ref[...] = (_rope(q, cos_t, sin_t) * (HEAD_DIM ** -0.5)).astype(BF16)
    k = _dot(h, w_ref[:, n_u + n_q:n_u + n_q + n_k])
    k_ref[...] = _rope(k, cos_t, sin_t).astype(BF16)
    v_ref[...] = _dot(h, w_ref[:, n_u + n_q + n_k:]).astype(BF16)


def _in_proj(x2d, sh, sc, w_c, cos_t, sin_t, seq, tm):
    t, d = x2d.shape
    per_b = seq // tm
    n_u, n_q, n_k = 3 * D_HYENA, D_ATTN, 2 * D_KV
    row = lambda i: (i, 0)
    mod = lambda i: (i // per_b, 0, 0)
    pos = lambda i: (i % per_b, 0)
    return pl.pallas_call(
        _in_proj_kernel,
        grid=(t // tm,),
        in_specs=[pl.BlockSpec((tm, d), row),
                  pl.BlockSpec((1, 1, d), mod),
                  pl.BlockSpec((1, 1, d), mod),
                  _const_spec(w_c.shape),
                  pl.BlockSpec((tm, LANES), pos),
                  pl.BlockSpec((tm, LANES), pos)],
        out_specs=[pl.BlockSpec((tm, n_u), row),
                   pl.BlockSpec((tm, n_q), row),
                   pl.BlockSpec((tm, n_k), row),
                   pl.BlockSpec((tm, n_k), row)],
        out_shape=[jax.ShapeDtypeStruct((t, n_u), BF16),
                   jax.ShapeDtypeStruct((t, n_q), BF16),
                   jax.ShapeDtypeStruct((t, n_k), BF16),
                   jax.ShapeDtypeStruct((t, n_k), BF16)],
        compiler_params=_params(("arbitrary",)),
        name="in_proj",
    )(x2d, sh, sc, w_c, cos_t, sin_t)


def _filt_mlp_kernel(z_ref, w1_ref, b1_ref, w2_ref, b2_ref, w3_ref, b3_ref, w4_ref, fr_ref, dec_ref,
                     h_ref, ss_ref):
    fr = fr_ref[...]
    a = jnp.sin(fr * (_dot3(z_ref[...], w1_ref[...]) + b1_ref[...]))
    a = jnp.sin(fr * (_dot3(a, w2_ref[...]) + b2_ref[...]))
    a = jnp.sin(fr * (_dot3(a, w3_ref[...]) + b3_ref[...]))
    h = _dot3(a, w4_ref[0]) * dec_ref[...]
    h_ref[...] = h

    @pl.when(pl.program_id(0) == 0)
    def _():
        ss_ref[...] = jnp.zeros_like(ss_ref)

    ss_ref[...] += jnp.sum(h * h, axis=0, keepdims=True)


def _filt_mlp(z2, w1p, b1, w2, b2, w3, b3, w4s, fr, dec2, rows):
    n, emb = z2.shape
    c = dec2.shape[1]
    half_steps = (n // 2) // rows
    vec = lambda a: pl.BlockSpec(a.shape, lambda i: (0,) * a.ndim)
    return pl.pallas_call(
        _filt_mlp_kernel,
        grid=(n // rows,),
        in_specs=[pl.BlockSpec((rows, emb), lambda i: (i, 0)),
                  vec(w1p), vec(b1), vec(w2), vec(b2), vec(w3), vec(b3),
                  pl.BlockSpec((1,) + w4s.shape[1:], lambda i: (i // half_steps, 0, 0)),
                  vec(fr),
                  pl.BlockSpec((rows, c), lambda i: (i, 0))],
        out_specs=[pl.BlockSpec((rows, c), lambda i: (i, 0)),
                   pl.BlockSpec((1, c), lambda i: (0, 0))],
        out_shape=[jax.ShapeDtypeStruct((n, c), F32), jax.ShapeDtypeStruct((1, c), F32)],
        compiler_params=_params(("arbitrary",)),
        name="filt_mlp",
    )(z2, w1p, b1, w2, b2, w3, b3, w4s, fr, dec2)


def _filt_fft_kernel(h_ref, ss_ref, fa_ref, fb_ref, o_ref, zs_ref):
    scale = lax.rsqrt(ss_ref[...] + 1e-6)
    n1 = h_ref.shape[0] // FFT_N2

    def stage_a(n2, carry):
        slab = h_ref[pl.ds(n2, n1, stride=FFT_N2), :]
        hi, lo = _split(slab)
        fa = fa_ref[n2]
        z = _dot(fa, hi) + _dot(fa, lo)
        zs_ref[pl.ds(n2, KP, stride=ZPITCH), :] = z[:KP]
        zs_ref[pl.ds(FFT_N2 + n2, KP, stride=ZPITCH), :] = z[KP:]
        return carry

    lax.fori_loop(0, FFT_N2, stage_a, 0)

    def stage_b(k1, carry):
        base = pl.multiple_of(k1 * ZPITCH, SUBLANES)
        hi, lo = _split(zs_ref[pl.ds(base, 2 * FFT_N2), :])
        fb = fb_ref[...]
        o_ref[0, k1] = (_dot(fb, hi) + _dot(fb, lo)) * scale
        return carry

    lax.fori_loop(0, KH, stage_b, 0)


def _filt_fft(h2u, ss, fa_full, fb):
    n, c = h2u.shape
    nblk = c // LANES
    return pl.pallas_call(
        _filt_fft_kernel,
        grid=(nblk,),
        in_specs=[pl.BlockSpec((n, LANES), lambda j: (0, j), pipeline_mode=pl.Buffered(1)),
                  pl.BlockSpec((1, LANES), lambda j: (0, j)),
                  _const_spec(fa_full.shape),
                  _const_spec(fb.shape)],
        out_specs=pl.BlockSpec((1, KH, 2 * FFT_N2, LANES), lambda j: (j, 0, 0, 0)),
        out_shape=jax.ShapeDtypeStruct((nblk, KH, 2 * FFT_N2, LANES), F32),
        scratch_shapes=[pltpu.VMEM((KP * ZPITCH, LANES), F32)],
        compiler_params=_params(("arbitrary",)),
        name="filt_fft",
    )(h2u, ss, fa_full, fb)


def _conv_slab(u_ref, cw_ref, j, n_slabs):
    r0 = pl.multiple_of(j * FFT_N2, FFT_N2)
    cur = u_ref[0, pl.ds(r0, FFT_N2), :].astype(F32)
    grp = 2 * SUBLANES
    pr0 = pl.multiple_of(jnp.maximum(j * FFT_N2 - grp, 0), grp)
    nr0 = pl.multiple_of(jnp.minimum((j + 1) * FFT_N2, (n_slabs - 1) * FFT_N2), grp)
    prev_row = u_ref[0, pl.ds(pr0, grp), :].astype(F32)[grp - 1:grp]
    next_row = u_ref[0, pl.ds(nr0, grp), :].astype(F32)[0:1]
    prev_row = jnp.where(j > 0, prev_row, 0.0)
    next_row = jnp.where(j < n_slabs - 1, next_row, 0.0)
    row = lax.broadcasted_iota(jnp.int32, cur.shape, 0)
    before = jnp.where(row == 0, prev_row, pltpu.roll(cur, 1, 0))
    after = jnp.where(row == FFT_N2 - 1, next_row, pltpu.roll(cur, FFT_N2 - 1, 0))
    w = cw_ref[...]
    return before * w[0:1] + cur * w[1:2] + after * w[2:3] + w[3:4]


def _hyena_kernel(x0_ref, x1_ref, v_ref, cw0_ref, cw1_ref, cwv_ref, bias_ref, hf_ref,
                  fa_ref, fai_ref, fb_ref, fbi_ref, o_ref, ts_ref, zs_ref):
    n_slabs = x0_ref.shape[1] // FFT_N2

    def gated_value(j):
        return _conv_slab(x1_ref, cw1_ref, j, n_slabs) * _conv_slab(v_ref, cwv_ref, j, n_slabs)

    def fill(j, carry):
        ts_ref[pl.ds(pl.multiple_of(j * TPITCH, SUBLANES), FFT_N2), :] = gated_value(j)
        return carry

    lax.fori_loop(0, n_slabs, fill, 0)

    def stage_a(n2, carry):
        slab = ts_ref[pl.ds(n2, n_slabs, stride=TPITCH), :]
        z = _dot(fa_ref[n2], slab.astype(BF16))
        zs_ref[pl.ds(n2, KP, stride=ZPITCH), :] = z[:KP]
        zs_ref[pl.ds(FFT_N2 + n2, KP, stride=ZPITCH), :] = z[KP:]
        return carry

    lax.fori_loop(0, FFT_N2, stage_a, 0, unroll=2)

    def stage_b(k1, carry):
        base = pl.multiple_of(k1 * ZPITCH, SUBLANES)
        x = _dot(fb_ref[...], zs_ref[pl.ds(base, 2 * FFT_N2), :].astype(BF16))
        h = hf_ref[0, k1]
        xr, xi = x[:FFT_N2], x[FFT_N2:]
        hr, hi = h[:FFT_N2], h[FFT_N2:]
        p = jnp.concatenate([xr * hr - xi * hi, xr * hi + xi * hr], axis=0)
        zs_ref[pl.ds(base, 2 * FFT_N2), :] = _dot(fbi_ref[...], p.astype(BF16))
        return carry

    lax.fori_loop(0, KH, stage_b, 0)

    def stage_ai(n2, carry):
        yr = zs_ref[pl.ds(n2, KP, stride=ZPITCH), :]
        yi = zs_ref[pl.ds(FFT_N2 + n2, KP, stride=ZPITCH), :]
        y = jnp.concatenate([yr, yi], axis=0).astype(BF16)
        ts_ref[pl.ds(n2, n_slabs, stride=TPITCH), :] = _dot(fai_ref[n2], y)
        return carry

    lax.fori_loop(0, FFT_N2, stage_ai, 0, unroll=2)

    def finish(j, carry):
        conv = ts_ref[pl.ds(pl.multiple_of(j * TPITCH, SUBLANES), FFT_N2), :]
        vx = gated_value(j)
        y = _conv_slab(x0_ref, cw0_ref, j, n_slabs) * (conv + bias_ref[...] * vx)
        o_ref[0, pl.ds(pl.multiple_of(j * FFT_N2, FFT_N2), FFT_N2), :] = y.astype(BF16)
        return carry

    lax.fori_loop(0, n_slabs, finish, 0)


def _hyena(u, cw, bias, hf, fa, fai, fb, fbi):
    b, seq, c3 = u.shape
    nblk = D_HYENA // LANES
    n_slabs = seq // FFT_N2
    stream = lambda k: pl.BlockSpec((1, seq, LANES), lambda j, i, k=k: (i, 0, k * nblk + j))
    cwspec = lambda k: pl.BlockSpec((SUBLANES, LANES), lambda j, i, k=k: (0, k * nblk + j))
    return pl.pallas_call(
        _hyena_kernel,
        grid=(nblk, b),
        in_specs=[stream(0), stream(1), stream(2), cwspec(0), cwspec(1), cwspec(2),
                  pl.BlockSpec((1, LANES), lambda j, i: (0, j)),
                  pl.BlockSpec((1, KH, 2 * FFT_N2, LANES), lambda j, i: (j, 0, 0, 0),
                               pipeline_mode=pl.Buffered(1)),
                  _const_spec(fa.shape), _const_spec(fai.shape),
                  _const_spec(fb.shape), _const_spec(fbi.shape)],
        out_specs=pl.BlockSpec((1, seq, LANES), lambda j, i: (i, 0, j)),
        out_shape=jax.ShapeDtypeStruct((b, seq, D_HYENA), BF16),
        scratch_shapes=[pltpu.VMEM((n_slabs * TPITCH, LANES), F32),
                        pltpu.VMEM((KP * ZPITCH, LANES), F32)],
        compiler_params=_params(("arbitrary", "arbitrary")),
        name="hyena",
    )(u, u, u, cw, cw, cw, bias, hf, fa, fai, fb, fbi)


def _attn_kernel(sink_ref, q_ref, kp_ref, km_ref, kn_ref, vp_ref, vm_ref, vn_ref, kc_ref, vc_ref,
                 o_ref, ka_ref, va_ref):
    i = pl.program_id(1)
    n_i = pl.num_programs(1)
    tq = q_ref.shape[0]
    nqb = tq // BLOCK
    ka_ref[0:BLOCK] = kp_ref[...]
    ka_ref[BLOCK:BLOCK + tq] = km_ref[...]
    ka_ref[BLOCK + tq:] = kn_ref[...]
    va_ref[0:BLOCK] = vp_ref[...]
    va_ref[BLOCK:BLOCK + tq] = vm_ref[...]
    va_ref[BLOCK + tq:] = vn_ref[...]

    nkw = 3 * BLOCK
    qi = lax.broadcasted_iota(jnp.int32, (BLOCK, nkw), 0)
    kj = lax.broadcasted_iota(jnp.int32, (BLOCK, nkw), 1)
    band = jnp.abs(kj - BLOCK - qi) <= WINDOW
    lane = lax.broadcasted_iota(jnp.int32, (BLOCK, LANES), 1)
    low = lane < HEAD_DIM
    hrow = lax.broadcasted_iota(jnp.int32, (GQA_GROUP * BLOCK, 1), 0) // BLOCK

    for j in range(nqb):
        valid = band
        if j == 0:
            valid = valid & ((kj >= BLOCK) | (i > 0))
        if j == nqb - 1:
            valid = valid & ((kj < 2 * BLOCK) | (i < n_i - 1))
        bias1 = jnp.where(valid, 0.0, NEG_INF).astype(F32)
        bias = jnp.concatenate([bias1] * GQA_GROUP, axis=0)
        qb = q_ref[j * BLOCK:(j + 1) * BLOCK, :]
        for g in range(N_KV_HEADS):
            gl = slice(g * LANES, (g + 1) * LANES)
            parts = []
            for hh in range(GQA_GROUP):
                h = g * GQA_GROUP + hh
                qp = qb[:, (h // 2) * LANES:(h // 2 + 1) * LANES]
                parts.append(jnp.where(low if h % 2 == 0 else ~low, qp, jnp.zeros_like(qp)))
            qs = jnp.concatenate(parts, axis=0)
            kw = ka_ref[j * BLOCK:j * BLOCK + nkw, gl]
            vw = va_ref[j * BLOCK:j * BLOCK + nkw, gl]
            s_w = _dot_nt(qs, kw) + bias
            s_c = _dot_nt(qs, kc_ref[:, gl])
            sink = jnp.zeros((GQA_GROUP * BLOCK, 1), F32)
            for hh in range(GQA_GROUP):
                sink = jnp.where(hrow == hh, sink_ref[g * GQA_GROUP + hh], sink)
            m = jnp.maximum(jnp.maximum(jnp.max(s_w, axis=1, keepdims=True),
                                        jnp.max(s_c, axis=1, keepdims=True)), sink)
            e_w = jnp.exp(s_w - m)
            e_c = jnp.exp(s_c - m)
            den = (jnp.sum(e_w, axis=1, keepdims=True) + jnp.sum(e_c, axis=1, keepdims=True)
                   + jnp.exp(sink - m))
            o = (_dot(e_w.astype(BF16), vw) + _dot(e_c.astype(BF16), vc_ref[:, gl])) / den
            for pp in range(GQA_GROUP // 2):
                pair = jnp.where(low, o[(2 * pp) * BLOCK:(2 * pp + 1) * BLOCK],
                                 o[(2 * pp + 1) * BLOCK:(2 * pp + 2) * BLOCK])
                col = (g * (GQA_GROUP // 2) + pp) * LANES
                o_ref[j * BLOCK:(j + 1) * BLOCK, col:col + LANES] = pair.astype(BF16)


def _attention(sinks, q, kd, vd, kc, vc, batch, seq, tq):
    t = q.shape[0]
    per_b = seq // tq
    nqb = tq // BLOCK
    nb = seq // BLOCK
    n_ctx = kc.shape[0] // batch
    main = lambda b, i: (b * per_b + i, 0)
    prev = lambda b, i: (b * nb + jnp.maximum(i * nqb - 1, 0), 0)
    nxt = lambda b, i: (b * nb + jnp.minimum(i * nqb + nqb, nb - 1), 0)
    kvw = kd.shape[1]
    return pl.pallas_call(
        _attn_kernel,
        grid=(batch, per_b),
        in_specs=[pl.BlockSpec(memory_space=pltpu.SMEM),
                  pl.BlockSpec((tq, D_ATTN), main),
                  pl.BlockSpec((BLOCK, kvw), prev), pl.BlockSpec((tq, kvw), main), pl.BlockSpec((BLOCK, kvw), nxt),
                  pl.BlockSpec((BLOCK, kvw), prev), pl.BlockSpec((tq, kvw), main), pl.BlockSpec((BLOCK, kvw), nxt),
                  pl.BlockSpec((n_ctx, kvw), lambda b, i: (b, 0)),
                  pl.BlockSpec((n_ctx, kvw), lambda b, i: (b, 0))],
        out_specs=pl.BlockSpec((tq, D_ATTN), main),
        out_shape=jax.ShapeDtypeStruct((t, D_ATTN), BF16),
        scratch_shapes=[pltpu.VMEM((tq + 2 * BLOCK, kvw), BF16),
                        pltpu.VMEM((tq + 2 * BLOCK, kvw), BF16)],
        compiler_params=_params(("arbitrary", "arbitrary")),
        name="attention",
    )(sinks, q, kd, kd, kd, vd, vd, vd, kc, vc)


def _route(t, wr_hi_ref, wr_lo_ref, br_ref):
    th, tl = _split(t)
    logits = _dot(th, wr_hi_ref[...]) + _dot(tl, wr_hi_ref[...]) + _dot(th, wr_lo_ref[...]) + br_ref[...]
    lane_i = lax.broadcasted_iota(jnp.int32, logits.shape, 1)
    lane = lane_i.astype(F32)
    grp_of_lane = (lane_i >> 2).astype(F32)
    ninf = -jnp.inf
    far = float(LANES)
    is_g = (lane_i >= N_EXPERTS) & (lane_i < N_EXPERTS + N_GROUPS)
    glog = jnp.where(is_g, logits, ninf)
    gmax = jnp.max(glog, axis=1, keepdims=True)
    gidx = jnp.min(jnp.where(glog == gmax, lane - float(N_EXPERTS), far), axis=1, keepdims=True)
    group_p = 1.0 / jnp.sum(jnp.exp(glog - gmax), axis=1, keepdims=True)
    in_grp = (lane_i < N_EXPERTS) & (grp_of_lane == gidx)
    elog = jnp.where(in_grp, logits, ninf)
    v1 = jnp.max(elog, axis=1, keepdims=True)
    i1 = jnp.min(jnp.where(elog == v1, lane, far), axis=1, keepdims=True)
    elog2 = jnp.where(lane == i1, ninf, elog)
    v2 = jnp.max(elog2, axis=1, keepdims=True)
    i2 = jnp.min(jnp.where(elog2 == v2, lane, far), axis=1, keepdims=True)
    e = jnp.exp(v2 - v1)
    w1 = group_p / (1.0 + e)
    w2 = group_p * e / (1.0 + e)
    return jnp.where(lane == i1, w1, 0.0) + jnp.where(lane == i2, w2, 0.0)


def _mix_out_kernel(x_ref, yh_ref, ya_ref, sh1_ref, sc1_ref, g1_ref, sh2_ref, sc2_ref,
                    wg_ref, wbh_ref, wba_ref, wo_ref, lng_ref, lnb_ref, wrh_ref, wrl_ref, br_ref,
                    x1_ref, gate_ref):
    x = x_ref[...]
    d = x.shape[1]
    h = (_standardize(x) * (1.0 + sc1_ref[0]) + sh1_ref[0]).astype(BF16)
    g_hy = jax.nn.sigmoid(_dot(h, wg_ref[:, :d]))
    merged = g_hy * _dot(yh_ref[...], wbh_ref[...])
    g_at = jax.nn.sigmoid(_dot(h, wg_ref[:, d:]))
    merged = merged + g_at * _dot(ya_ref[...], wba_ref[...])
    mix = _dot(merged.astype(BF16), wo_ref[...])
    x1 = _standardize(DEEPNORM_ALPHA * x + g1_ref[0] * mix) * lng_ref[...] + lnb_ref[...]
    x1_ref[...] = x1
    t = _standardize(x1) * (1.0 + sc2_ref[0]) + sh2_ref[0]
    gate_ref[...] = _route(t, wrh_ref, wrl_ref, br_ref)


def _mix_out(x2d, yh, ya, mods, w_g, w_bh, w_ba, w_o, ln_g, ln_b, wr_hi, wr_lo, br, seq, tm):
    t, d = x2d.shape
    per_b = seq // tm
    row = lambda i: (i, 0)
    mod = lambda i: (i // per_b, 0, 0)
    mspec = pl.BlockSpec((1, 1, d), mod)
    sh1, sc1, g1, sh2, sc2 = mods
    return pl.pallas_call(
        _mix_out_kernel,
        grid=(t // tm,),
        in_specs=[pl.BlockSpec((tm, d), row),
                  pl.BlockSpec((tm, yh.shape[1]), row),
                  pl.BlockSpec((tm, ya.shape[1]), row),
                  mspec, mspec, mspec, mspec, mspec,
                  _const_spec(w_g.shape), _const_spec(w_bh.shape), _const_spec(w_ba.shape),
                  _const_spec(w_o.shape), _const_spec(ln_g.shape), _const_spec(ln_b.shape),
                  _const_spec(wr_hi.shape), _const_spec(wr_lo.shape), _const_spec(br.shape)],
        out_specs=[pl.BlockSpec((tm, d), row), pl.BlockSpec((tm, LANES), row)],
        out_shape=[jax.ShapeDtypeStruct((t, d), F32), jax.ShapeDtypeStruct((t, LANES), F32)],
        compiler_params=_params(("arbitrary",)),
        name="mix_out",
    )(x2d, yh, ya, sh1, sc1, g1, sh2, sc2, w_g, w_bh, w_ba, w_o, ln_g, ln_b, wr_hi, wr_lo, br)


def _moe_kernel(x1_ref, gate_ref, sh2_ref, sc2_ref, g2_ref, wg_ref, wu_ref, wd_ref, lng_ref, lnb_ref,
                o_ref, hid_ref):
    x1 = x1_ref[...]
    rows = x1.shape[0]
    t = (_standardize(x1) * (1.0 + sc2_ref[0]) + sh2_ref[0]).astype(BF16)
    gate = gate_ref[...]
    gw = EXPERTS_PER_GROUP * D_EXPERT
    for grp in range(N_GROUPS):
        cols = slice(grp * gw, (grp + 1) * gw)
        a = _dot(t, wg_ref[:, cols])
        u = _dot(t, wu_ref[:, cols])
        gcols = [jnp.broadcast_to(gate[:, e:e + 1], (rows, D_EXPERT))
                 for e in range(grp * EXPERTS_PER_GROUP, (grp + 1) * EXPERTS_PER_GROUP)]
        hid = a * jax.nn.sigmoid(a) * u * jnp.concatenate(gcols, axis=1)
        hid_ref[:, cols] = hid.astype(BF16)
    y = _dot(hid_ref[...], wd_ref[...])
    out = _standardize(DEEPNORM_ALPHA * x1 + g2_ref[0] * y) * lng_ref[...] + lnb_ref[...]
    o_ref[...] = out


def _moe(x1, gate, sh2, sc2, g2, wg_all, wu_all, wd_all, ln_g, ln_b, seq, tm):
    t, d = x1.shape
    per_b = seq // tm
    row = lambda i: (i, 0)
    mspec = pl.BlockSpec((1, 1, d), lambda i: (i // per_b, 0, 0))
    return pl.pallas_call(
        _moe_kernel,
        grid=(t // tm,),
        in_specs=[pl.BlockSpec((tm, d), row), pl.BlockSpec((tm, LANES), row),
                  mspec, mspec, mspec,
                  _const_spec(wg_all.shape), _const_spec(wu_all.shape), _const_spec(wd_all.shape),
                  _const_spec(ln_g.shape), _const_spec(ln_b.shape)],
        out_specs=pl.BlockSpec((tm, d), row),
        out_shape=jax.ShapeDtypeStruct((t, d), F32),
        scratch_shapes=[pltpu.VMEM((tm, wg_all.shape[1]), BF16)],
        compiler_params=_params(("arbitrary",)),
        name="moe",
    )(x1, gate, sh2, sc2, g2, wg_all, wu_all, wd_all, ln_g, ln_b)


def _dft_tables(seq):
    n = 2 * seq
    n1_full = n // FFT_N2
    n1_data = seq // FFT_N2
    k1 = np.arange(KH, dtype=np.float64)[None, :, None]
    n2 = np.arange(FFT_N2, dtype=np.float64)[:, None, None]
    n1 = np.arange(n1_full, dtype=np.float64)[None, None, :]
    ang = 2.0 * np.pi * k1 * (FFT_N2 * n1 + n2) / n
    fa = np.zeros((FFT_N2, 2 * KP, n1_full))
    fa[:, :KH] = np.cos(ang)
    fa[:, KP:KP + KH] = -np.sin(ang)
    wgt = np.full((KH,), 2.0)
    wgt[0] = 1.0
    wgt[KH - 1] = 1.0
    fai = np.zeros((FFT_N2, n1_data, 2 * KP))
    angt = np.transpose(ang[:, :, :n1_data], (0, 2, 1))
    fai[:, :, :KH] = np.cos(angt) * wgt / n
    fai[:, :, KP:KP + KH] = -np.sin(angt) * wgt / n
    kk = np.arange(FFT_N2, dtype=np.float64)
    a2 = 2.0 * np.pi * np.outer(kk, kk) / FFT_N2
    fr, fi = np.cos(a2), -np.sin(a2)
    fb = np.block([[fr, -fi], [fi, fr]])
    fbi = np.block([[fr, fi], [-fi, fr]])
    f32 = lambda a: jnp.asarray(a.astype(np.float32))
    return f32(fa[:, :, :n1_data]), f32(fa), f32(fai), f32(fb), f32(fbi)


def _filter_features(seq):
    t = jnp.linspace(0.0, 1.0, seq, dtype=F32)[:, None]
    w = 2.0 * math.pi * jnp.arange(seq, dtype=F32)[:, None] / seq
    bands = jnp.linspace(1e-4, FILTER_BANDS - 1, FILTER_BANDS, dtype=F32)
    z = jnp.concatenate([t, jnp.cos(bands * w), -jnp.sin(bands * w)], axis=-1)
    max_decay = math.log(DECAY_TARGET) / FAST_DECAY_PCT
    min_decay = math.log(DECAY_TARGET) / SLOW_DECAY_PCT
    deltas = jnp.linspace(min_decay, max_decay, D_HYENA, dtype=F32)
    decay = jnp.exp(-t * jnp.abs(deltas))
    pad = jnp.zeros((seq, FILTER_ORDER - FILTER_EMB), F32)
    z = jnp.concatenate([z, pad], axis=-1)
    return jnp.concatenate([z, z[::-1]], axis=0), jnp.concatenate([decay, decay[::-1]], axis=0)


def _rope_tables(seq):
    rows = seq // GRID_W
    row = jnp.repeat(jnp.arange(rows, dtype=F32), GRID_W)
    col = jnp.tile(jnp.arange(GRID_W, dtype=F32), rows)
    half = HEAD_DIM // 2
    inv_freq = ROPE_BASE ** (-jnp.arange(0, half, 2, dtype=F32) / half)
    ang = jnp.concatenate([row[:, None] * inv_freq, col[:, None] * inv_freq], axis=-1)
    cos, sin = jnp.cos(ang), jnp.sin(ang)
    c64 = jnp.concatenate([cos, cos], axis=-1)
    s64 = jnp.concatenate([-sin, sin], axis=-1)
    return jnp.concatenate([c64, c64], axis=-1), jnp.concatenate([s64, s64], axis=-1)


def _head_perm(n_heads):
    idx = []
    for h in range(n_heads):
        base = h * HEAD_DIM
        idx += [base + 2 * j for j in range(HEAD_DIM // 2)]
        idx += [base + 2 * j + 1 for j in range(HEAD_DIM // 2)]
    return np.asarray(idx, dtype=np.int32)


def _dup_heads(w):
    parts = []
    for g in range(N_KV_HEADS):
        blk = w[:, g * HEAD_DIM:(g + 1) * HEAD_DIM]
        parts += [blk, blk]
    return jnp.concatenate(parts, axis=1)


def kernel(x, c, ctx, c_ctx, ada_w, ada_b, w_in, hy_conv_w, hy_conv_b, hy_w1, hy_b1, hy_w2, hy_b2, hy_w3, hy_b3, hy_w4, hy_freq, hy_bias, attn_sinks, w_branch_hy, w_branch_attn, w_out, ln1_g, ln1_b, w_group, b_group, w_router, b_router, w_gate_e, w_up_e, w_down_e, ln2_g, ln2_b):
    batch, seq, d = x.shape
    n_ctx = ctx.shape[1]
    assert d == D_MODEL and ada_w.shape[0] == DEPTH == 1
    assert 2 * seq == FFT_N2 * FFT_N2 and seq % 512 == 0
    tm = 512
    l = 0

    w = w_in[l]
    s0, s1, s2, s3 = 3 * D_HYENA, 3 * D_HYENA + D_ATTN, 3 * D_HYENA + D_ATTN + D_KV, 3 * D_HYENA + D_ATTN + 2 * D_KV
    w_q = w[:, s0:s1][:, _head_perm(N_HEADS)]
    w_k = _dup_heads(w[:, s1:s2][:, _head_perm(N_KV_HEADS)])
    w_v = _dup_heads(w[:, s2:s3])
    w_c = jnp.concatenate([w[:, :s0], w_q, w_k, w_v], axis=1).astype(BF16)
    w_kv = jnp.concatenate([w_k, w_v], axis=1).astype(BF16)
    w_g = w[:, s3:].astype(BF16)
    w_bh = w_branch_hy[l].astype(BF16)
    w_ba = w_branch_attn[l].astype(BF16)
    w_o = w_out[l].astype(BF16)
    wr = jnp.zeros((d, LANES), F32)
    wr = wr.at[:, :N_EXPERTS].set(w_router[l]).at[:, N_EXPERTS:N_EXPERTS + N_GROUPS].set(w_group[l])
    wr_hi = wr.astype(BF16)
    wr_lo = (wr - wr_hi.astype(F32)).astype(BF16)
    br = jnp.zeros((1, LANES), F32)
    br = br.at[0, :N_EXPERTS].set(b_router[l]).at[0, N_EXPERTS:N_EXPERTS + N_GROUPS].set(b_group[l])
    wg_all = jnp.transpose(w_gate_e[l], (1, 0, 2)).reshape(d, N_EXPERTS * D_EXPERT).astype(BF16)
    wu_all = jnp.transpose(w_up_e[l], (1, 0, 2)).reshape(d, N_EXPERTS * D_EXPERT).astype(BF16)
    wd_all = w_down_e[l].reshape(N_EXPERTS * D_EXPERT, d).astype(BF16)
    cw = jnp.concatenate([hy_conv_w[l][:, 0, :], hy_conv_b[l][None, :],
                          jnp.zeros((SUBLANES - SHORT_CONV - 1, 3 * D_HYENA), F32)], axis=0)
    w1p = jnp.concatenate([hy_w1[l], jnp.zeros((FILTER_ORDER - FILTER_EMB, FILTER_ORDER), F32)], axis=0)
    w4s = jnp.transpose(hy_w4[l].reshape(FILTER_ORDER, 2, D_HYENA), (1, 0, 2))
    row2 = lambda a: a.reshape(1, -1)

    fa, fa_full, fai, fb, fbi = _dft_tables(seq)
    fa, fa_full, fai, fb, fbi = (a.astype(BF16) for a in (fa, fa_full, fai, fb, fbi))
    z2, dec2 = _filter_features(seq)
    cos_t, sin_t = _rope_tables(seq)

    cond = jnp.concatenate([c, c_ctx[None], jnp.zeros((SUBLANES - batch - 1, d), F32)], axis=0)
    mods = _adaln(cond, ada_w[l], ada_b[l])
    m6 = [mods[:, k * d:(k + 1) * d].reshape(SUBLANES, 1, d) for k in range(6)]
    sh1, sc1, g1, sh2, sc2, g2 = m6

    kc, vc = _ctx_kv(ctx.reshape(batch * n_ctx, d), sh1[batch], sc1[batch], w_kv, n_ctx)

    x2d = x.reshape(batch * seq, d)
    u, q, kd, vd = _in_proj(x2d, sh1, sc1, w_c, cos_t, sin_t, seq, tm)
    h2u, ss = _filt_mlp(z2, w1p, row2(hy_b1[l]), hy_w2[l], row2(hy_b2[l]), hy_w3[l], row2(hy_b3[l]),
                        w4s, row2(hy_freq[l]), dec2, 2048)
    hf = _filt_fft(h2u, ss, fa_full, fb)
    y_hy = _hyena(u.reshape(batch, seq, 3 * D_HYENA), cw, row2(hy_bias[l]), hf, fa, fai, fb, fbi)
    y_at = _attention(attn_sinks[l], q, kd, vd, kc, vc, batch, seq, 512)
    x1, gate = _mix_out(x2d, y_hy.reshape(batch * seq, D_HYENA), y_at, (sh1, sc1, g1, sh2, sc2),
                        w_g, w_bh, w_ba, w_o, row2(ln1_g[l]), row2(ln1_b[l]), wr_hi, wr_lo, br, seq, tm)

    out = _moe(x1, gate, sh2, sc2, g2, wg_all, wu_all, wd_all, row2(ln2_g[l]), row2(ln2_b[l]), seq, tm)
    return out.reshape(batch, seq, d)
```

```python
import functools
import math

import numpy as np
import jax
import jax.numpy as jnp
from jax import lax
from jax.experimental import pallas as pl
from jax.experimental.pallas import tpu as pltpu

F32 = jnp.float32
BF16 = jnp.bfloat16

D_MODEL = 1024
GRID_W = 64
D_HYENA = D_MODEL // 2
SHORT_CONV = 3
FILTER_BANDS = 16
FILTER_EMB = 1 + 2 * FILTER_BANDS
FILTER_ORDER = 64
DECAY_TARGET = 1e-2
FAST_DECAY_PCT = 0.3
SLOW_DECAY_PCT = 1.5
HEAD_DIM = 64
D_ATTN = D_MODEL // 2
N_HEADS = D_ATTN // HEAD_DIM
N_KV_HEADS = N_HEADS // 4
GQA_GROUP = N_HEADS // N_KV_HEADS
D_KV = N_KV_HEADS * HEAD_DIM
WINDOW = 128
BLOCK = 128
ROPE_BASE = 10000.0
NEG_INF = -1e30
N_GROUPS = 4
EXPERTS_PER_GROUP = 4
N_EXPERTS = N_GROUPS * EXPERTS_PER_GROUP
D_EXPERT = D_MODEL // 4
LN_EPS = 1e-5
DEPTH = 1
DEEPNORM_ALPHA = (2.0 * DEPTH) ** 0.25

LANES = 128
SUBLANES = 8
VMEM_LIMIT = 56 * 1024 * 1024

SUB_ROWS = 512
TM_PROJ = 1024
TM_MOE = 512
TQ_ATTN = 512
FILT_ROWS = 2048

FFT_N2 = 128
KH = 65
KHP = 66
KP = 72
ZPITCH = 2 * FFT_N2 + SUBLANES
TPITCH = FFT_N2 + SUBLANES


def _dot(a, b):
    return jnp.dot(a, b, preferred_element_type=F32)


def _dot_nt(a, b):
    return lax.dot_general(a, b, (((1,), (1,)), ((), ())), preferred_element_type=F32)


def _split(a):
    hi = a.astype(BF16)
    lo = (a - hi.astype(F32)).astype(BF16)
    return hi, lo


def _dot3(a, b):
    ah, al = _split(a)
    bh, bl = _split(b)
    return _dot(ah, bh) + _dot(al, bh) + _dot(ah, bl)


def _standardize(x):
    mu = jnp.mean(x, axis=-1, keepdims=True)
    xc = x - mu
    var = jnp.mean(xc * xc, axis=-1, keepdims=True)
    return xc * lax.rsqrt(var + LN_EPS)


def _params(sem, vmem=VMEM_LIMIT):
    return pltpu.CompilerParams(dimension_semantics=sem, vmem_limit_bytes=vmem)


def _const_spec(shape):
    nd = len(shape)
    return pl.BlockSpec(shape, lambda *_: (0,) * nd, pipeline_mode=pl.Buffered(1))


def _adaln_kernel(c_ref, w_ref, b_ref, o_ref):
    s = c_ref[...]
    s = s * jax.nn.sigmoid(s)
    o_ref[...] = _dot3(s, w_ref[...]) + b_ref[...]


def _adaln(cond, w, b):
    n, d = cond.shape
    cols = w.shape[1]
    bc = 1024
    return pl.pallas_call(
        _adaln_kernel,
        grid=(cols // bc,),
        in_specs=[pl.BlockSpec((n, d), lambda j: (0, 0)),
                  pl.BlockSpec((d, bc), lambda j: (0, j)),
                  pl.BlockSpec((1, bc), lambda j: (0, j))],
        out_specs=pl.BlockSpec((n, bc), lambda j: (0, j)),
        out_shape=jax.ShapeDtypeStruct((n, cols), F32),
        compiler_params=_params(("arbitrary",)),
        name="adaln",
    )(cond, w, b.reshape(1, cols))


def _ctx_kv_kernel(x_ref, sh_ref, sc_ref, w_ref, k_ref, v_ref):
    h = _standardize(x_ref[...]) * (1.0 + sc_ref[...]) + sh_ref[...]
    kv = _dot(h.astype(BF16), w_ref[...])
    half = k_ref.shape[1]
    k_ref[...] = kv[:, :half].astype(BF16)
    v_ref[...] = kv[:, half:].astype(BF16)


def _ctx_kv(ctx2d, sh, sc, w_kv, rows):
    n, d = ctx2d.shape
    half = w_kv.shape[1] // 2
    return pl.pallas_call(
        _ctx_kv_kernel,
        grid=(n // rows,),
        in_specs=[pl.BlockSpec((rows, d), lambda i: (i, 0)),
                  pl.BlockSpec((1, d), lambda i: (0, 0)),
                  pl.BlockSpec((1, d), lambda i: (0, 0)),
                  pl.BlockSpec(w_kv.shape, lambda i: (0, 0))],
        out_specs=[pl.BlockSpec((rows, half), lambda i: (i, 0)),
                   pl.BlockSpec((rows, half), lambda i: (i, 0))],
        out_shape=[jax.ShapeDtypeStruct((n, half), BF16)] * 2,
        compiler_params=_params(("arbitrary",)),
        name="ctx_kv",
    )(ctx2d, sh, sc, w_kv)


def _rope(x, cos_t, sin_t):
    width = x.shape[1]
    reps = width // LANES
    c = jnp.concatenate([cos_t] * reps, axis=1)
    s = jnp.concatenate([sin_t] * reps, axis=1)
    half = HEAD_DIM // 2
    lane = lax.broadcasted_iota(jnp.int32, x.shape, 1)
    first_half = (lane & (HEAD_DIM - 1)) < half
    partner = jnp.where(first_half, pltpu.roll(x, width - half, 1), pltpu.roll(x, half, 1))
    return x * c + partner * s


def _in_proj_kernel(x_ref, sh_ref, sc_ref, w_ref, cos_ref, sin_ref, u_ref, q_ref, k_ref, v_ref):
    n_u = u_ref.shape[1]
    n_q = q_ref.shape[1]
    n_k = k_ref.shape[1]
    for r0 in range(0, x_ref.shape[0], SUB_ROWS):
        rows = slice(r0, r0 + SUB_ROWS)
        h = (_standardize(x_ref[rows, :]) * (1.0 + sc_ref[0]) + sh_ref[0]).astype(BF16)
        u_ref[rows, :] = _dot(h, w_ref[:, :n_u]).astype(BF16)
        cos_t = cos_ref[rows, :]
        sin_t = sin_ref[rows, :]
        q = _dot(h, w_ref[:, n_u:n_u + n_q])
        q_ref[rows, :] = (_rope(q, cos_t, sin_t) * (HEAD_DIM ** -0.5)).astype(BF16)
        k = _dot(h, w_ref[:, n_u + n_q:n_u + n_q + n_k])
        k_ref[rows, :] = _rope(k, cos_t, sin_t).astype(BF16)
        v_ref[rows, :] = _dot(h, w_ref[:, n_u + n_q + n_k:]).astype(BF16)


def _in_proj(x2d, sh, sc, w_c, cos_t, sin_t, seq, tm):
    t, d = x2d.shape
    per_b = seq // tm
    n_u, n_q, n_k = 3 * D_HYENA, D_ATTN, 2 * D_KV
    row = lambda i: (i, 0)
    mod = lambda i: (i // per_b, 0, 0)
    pos = lambda i: (i % per_b, 0)
    return pl.pallas_call(
        _in_proj_kernel,
        grid=(t // tm,),
        in_specs=[pl.BlockSpec((tm, d), row),
                  pl.BlockSpec((1, 1, d), mod),
                  pl.BlockSpec((1, 1, d), mod),
                  _const_spec(w_c.shape),
                  pl.BlockSpec((tm, LANES), pos),
                  pl.BlockSpec((tm, LANES), pos)],
        out_specs=[pl.BlockSpec((tm, n_u), row),
                   pl.BlockSpec((tm, n_q), row),
                   pl.BlockSpec((tm, n_k), row),
                   pl.BlockSpec((tm, n_k), row)],
        out_shape=[jax.ShapeDtypeStruct((t, n_u), BF16),
                   jax.ShapeDtypeStruct((t, n_q), BF16),
                   jax.ShapeDtypeStruct((t, n_k), BF16),
                   jax.ShapeDtypeStruct((t, n_k), BF16)],
        compiler_params=_params(("arbitrary",)),
        name="in_proj",
    )(x2d, sh, sc, w_c, cos_t, sin_t)


def _filt_mlp_kernel(z_ref, w1_ref, b1_ref, w2_ref, b2_ref, w3_ref, b3_ref, w4_ref, fr_ref, dec_ref,
                     h_ref, ss_ref):
    fr = fr_ref[...]
    a = jnp.sin(fr * (_dot3(z_ref[...], w1_ref[...]) + b1_ref[...]))
    a = jnp.sin(fr * (_dot3(a, w2_ref[...]) + b2_ref[...]))
    a = jnp.sin(fr * (_dot3(a, w3_ref[...]) + b3_ref[...]))
    half = a.shape[0]
    ss = jnp.zeros(ss_ref.shape, F32)
    for k in range(2):
        h = _dot3(a, w4_ref[0, k]) * dec_ref[k * half:(k + 1) * half, :]
        h_ref[k * half:(k + 1) * half, :] = h
        ss = ss + jnp.sum(h * h, axis=0, keepdims=True)

    @pl.when(pl.program_id(0) == 0)
    def _():
        ss_ref[...] = jnp.zeros_like(ss_ref)

    ss_ref[...] += ss


def _filt_mlp(zp, w1p, b1, w2, b2, w3, b3, w4s, fr, dec2, rows):
    n, c = dec2.shape
    half_steps = (n // 2) // rows
    vec = lambda a: pl.BlockSpec(a.shape, lambda i: (0,) * a.ndim)
    return pl.pallas_call(
        _filt_mlp_kernel,
        grid=(n // rows,),
        in_specs=[pl.BlockSpec((rows // 2, zp.shape[1]), lambda i: (i, 0)),
                  vec(w1p), vec(b1), vec(w2), vec(b2), vec(w3), vec(b3),
                  pl.BlockSpec((1,) + w4s.shape[1:], lambda i: (i // half_steps, 0, 0, 0)),
                  vec(fr),
                  pl.BlockSpec((rows, c), lambda i: (i, 0))],
        out_specs=[pl.BlockSpec((rows, c), lambda i: (i, 0)),
                   pl.BlockSpec((1, c), lambda i: (0, 0))],
        out_shape=[jax.ShapeDtypeStruct((n, c), F32), jax.ShapeDtypeStruct((1, c), F32)],
        compiler_params=_params(("arbitrary",)),
        name="filt_mlp",
    )(zp, w1p, b1, w2, b2, w3, b3, w4s, fr, dec2)


def _filt_fft_kernel(h_ref, ss_ref, bias_ref, fa_ref, fb_ref, o_ref, zs_ref):
    scale = lax.rsqrt(ss_ref[...] + 1e-6)
    n1 = h_ref.shape[0] // FFT_N2
    row = lax.broadcasted_iota(jnp.int32, (2 * FFT_N2, LANES), 0)
    impulse = jnp.where(row < FFT_N2, bias_ref[...], 0.0)

    def stage_a(n2, carry):
        slab = h_ref[pl.ds(n2, n1, stride=FFT_N2), :]
        hi, lo = _split(slab)
        fa = fa_ref[n2]
        z = _dot(fa, hi) + _dot(fa, lo)
        zs_ref[pl.ds(n2, KP, stride=ZPITCH), :] = z[:KP]
        zs_ref[pl.ds(FFT_N2 + n2, KP, stride=ZPITCH), :] = z[KP:]
        return carry

    lax.fori_loop(0, FFT_N2, stage_a, 0, unroll=8)

    def stage_b(p, carry):
        b0 = pl.multiple_of(2 * p * ZPITCH, SUBLANES)
        b1 = pl.multiple_of(b0 + ZPITCH, SUBLANES)
        z = jnp.concatenate([zs_ref[pl.ds(b0, 2 * FFT_N2), :], zs_ref[pl.ds(b1, 2 * FFT_N2), :]], axis=1)
        hi, lo = _split(z)
        fb = fb_ref[...]
        x = _dot(fb, hi) + _dot(fb, lo)
        o_ref[0, 2 * p] = x[:, :LANES] * scale + impulse
        o_ref[0, 2 * p + 1] = x[:, LANES:] * scale + impulse
        return carry

    lax.fori_loop(0, KHP // 2, stage_b, 0, unroll=11)


def _filt_fft(h2u, ss, bias, fa_full, fb):
    n, c = h2u.shape
    nblk = c // LANES
    return pl.pallas_call(
        _filt_fft_kernel,
        grid=(nblk,),
        in_specs=[pl.BlockSpec((n, LANES), lambda j: (0, j), pipeline_mode=pl.Buffered(1)),
                  pl.BlockSpec((1, LANES), lambda j: (0, j)),
                  pl.BlockSpec((1, LANES), lambda j: (0, j)),
                  _const_spec(fa_full.shape),
                  _const_spec(fb.shape)],
        out_specs=pl.BlockSpec((1, KHP, 2 * FFT_N2, LANES), lambda j: (j, 0, 0, 0)),
        out_shape=jax.ShapeDtypeStruct((nblk, KHP, 2 * FFT_N2, LANES), F32),
        scratch_shapes=[pltpu.VMEM((KP * ZPITCH, LANES), F32)],
        compiler_params=_params(("arbitrary",)),
        name="filt_fft",
    )(h2u, ss, bias, fa_full, fb)


def _conv_slab(u_ref, cw_ref, j, n_slabs):
    r0 = pl.multiple_of(j * FFT_N2, FFT_N2)
    cur = u_ref[0, pl.ds(r0, FFT_N2), :].astype(F32)
    grp = 2 * SUBLANES
    pr0 = pl.multiple_of(jnp.maximum(j * FFT_N2 - grp, 0), grp)
    nr0 = pl.multiple_of(jnp.minimum((j + 1) * FFT_N2, (n_slabs - 1) * FFT_N2), grp)
    prev_row = u_ref[0, pl.ds(pr0, grp), :].astype(F32)[grp - 1:grp]
    next_row = u_ref[0, pl.ds(nr0, grp), :].astype(F32)[0:1]
    prev_row = jnp.where(j > 0, prev_row, 0.0)
    next_row = jnp.where(j < n_slabs - 1, next_row, 0.0)
    row = lax.broadcasted_iota(jnp.int32, cur.shape, 0)
    before = jnp.where(row == 0, prev_row, pltpu.roll(cur, 1, 0))
    after = jnp.where(row == FFT_N2 - 1, next_row, pltpu.roll(cur, FFT_N2 - 1, 0))
    w = cw_ref[...]
    return before * w[0:1] + cur * w[1:2] + after * w[2:3] + w[3:4]


def _hyena_kernel(x0_ref, x1_ref, v_ref, cw0_ref, cw1_ref, cwv_ref, hf_ref,
                  fa_ref, fai_ref, fb_ref, fbi_ref, o_ref, ts_ref, zs_ref):
    n_slabs = x0_ref.shape[1] // FFT_N2

    def gated_value(j):
        return _conv_slab(x1_ref, cw1_ref, j, n_slabs) * _conv_slab(v_ref, cwv_ref, j, n_slabs)

    def fill(j, carry):
        ts_ref[pl.ds(pl.multiple_of(j * TPITCH, SUBLANES), FFT_N2), :] = gated_value(j)
        return carry

    lax.fori_loop(0, n_slabs, fill, 0, unroll=2)

    def stage_a(n2, carry):
        slab = ts_ref[pl.ds(n2, n_slabs, stride=TPITCH), :]
        z = _dot(fa_ref[n2], slab.astype(BF16))
        zs_ref[pl.ds(n2, KP, stride=ZPITCH), :] = z[:KP]
        zs_ref[pl.ds(FFT_N2 + n2, KP, stride=ZPITCH), :] = z[KP:]
        return carry

    lax.fori_loop(0, FFT_N2, stage_a, 0, unroll=16)

    def stage_b(p, carry):
        b0 = pl.multiple_of(2 * p * ZPITCH, SUBLANES)
        b1 = pl.multiple_of(b0 + ZPITCH, SUBLANES)
        z = jnp.concatenate([zs_ref[pl.ds(b0, 2 * FFT_N2), :], zs_ref[pl.ds(b1, 2 * FFT_N2), :]], axis=1)
        x = _dot(fb_ref[...], z.astype(BF16))
        h = jnp.concatenate([hf_ref[0, 2 * p], hf_ref[0, 2 * p + 1]], axis=1)
        xr, xi = x[:FFT_N2], x[FFT_N2:]
        hr, hi = h[:FFT_N2], h[FFT_N2:]
        prod = jnp.concatenate([xr * hr - xi * hi, xr * hi + xi * hr], axis=0)
        y = _dot(fbi_ref[...], prod.astype(BF16))
        zs_ref[pl.ds(b0, 2 * FFT_N2), :] = y[:, :LANES]
        zs_ref[pl.ds(b1, 2 * FFT_N2), :] = y[:, LANES:]
        return carry

    lax.fori_loop(0, KHP // 2, stage_b, 0, unroll=11)

    def stage_ai(n2, carry):
        yr = zs_ref[pl.ds(n2, KP, stride=ZPITCH), :]
        yi = zs_ref[pl.ds(FFT_N2 + n2, KP, stride=ZPITCH), :]
        y = jnp.concatenate([yr, yi], axis=0).astype(BF16)
        ts_ref[pl.ds(n2, n_slabs, stride=TPITCH), :] = _dot(fai_ref[n2], y)
        return carry

    lax.fori_loop(0, FFT_N2, stage_ai, 0, unroll=16)

    def finish(j, carry):
        conv = ts_ref[pl.ds(pl.multiple_of(j * TPITCH, SUBLANES), FFT_N2), :]
        y = _conv_slab(x0_ref, cw0_ref, j, n_slabs) * conv
        o_ref[0, pl.ds(pl.multiple_of(j * FFT_N2, FFT_N2), FFT_N2), :] = y.astype(BF16)
        return carry

    lax.fori_loop(0, n_slabs, finish, 0, unroll=2)


def _hyena(u, cw, hf, fa, fai, fb, fbi):
    b, seq, c3 = u.shape
    nblk = D_HYENA // LANES
    n_slabs = seq // FFT_N2
    stream = lambda k: pl.BlockSpec((1, seq, LANES), lambda j, i, k=k: (i, 0, k * nblk + j))
    cwspec = lambda k: pl.BlockSpec((SUBLANES, LANES), lambda j, i, k=k: (0, k * nblk + j))
    return pl.pallas_call(
        _hyena_kernel,
        grid=(nblk, b),
        in_specs=[stream(0), stream(1), stream(2), cwspec(0), cwspec(1), cwspec(2),
                  pl.BlockSpec((1, KHP, 2 * FFT_N2, LANES), lambda j, i: (j, 0, 0, 0),
                               pipeline_mode=pl.Buffered(1)),
                  _const_spec(fa.shape), _const_spec(fai.shape),
                  _const_spec(fb.shape), _const_spec(fbi.shape)],
        out_specs=pl.BlockSpec((1, seq, LANES), lambda j, i: (i, 0, j)),
        out_shape=jax.ShapeDtypeStruct((b, seq, D_HYENA), BF16),
        scratch_shapes=[pltpu.VMEM((n_slabs * TPITCH, LANES), F32),
                        pltpu.VMEM((KP * ZPITCH, LANES), F32)],
        compiler_params=_params(("arbitrary", "arbitrary")),
        name="hyena",
    )(u, u, u, cw, cw, cw, hf, fa, fai, fb, fbi)


def _attn_kernel(sink_ref, q_ref, kp_ref, km_ref, kn_ref, vp_ref, vm_ref, vn_ref, kc_ref, vc_ref,
                 o_ref, ka_ref, va_ref):
    i = pl.program_id(1)
    n_i = pl.num_programs(1)
    tq = q_ref.shape[0]
    nqb = tq // BLOCK
    ka_ref[0:BLOCK] = kp_ref[...]
    ka_ref[BLOCK:BLOCK + tq] = km_ref[...]
    ka_ref[BLOCK + tq:] = kn_ref[...]
    va_ref[0:BLOCK] = vp_ref[...]
    va_ref[BLOCK:BLOCK + tq] = vm_ref[...]
    va_ref[BLOCK + tq:] = vn_ref[...]

    nkw = 3 * BLOCK
    qi = lax.broadcasted_iota(jnp.int32, (BLOCK, nkw), 0)
    kj = lax.broadcasted_iota(jnp.int32, (BLOCK, nkw), 1)
    band = jnp.abs(kj - BLOCK - qi) <= WINDOW
    lane = lax.broadcasted_iota(jnp.int32, (BLOCK, LANES), 1)
    low = lane < HEAD_DIM
    hrow = lax.broadcasted_iota(jnp.int32, (GQA_GROUP * BLOCK, 1), 0) // BLOCK

    for j in range(nqb):
        valid = band
        if j == 0:
            valid = valid & ((kj >= BLOCK) | (i > 0))
        if j == nqb - 1:
            valid = valid & ((kj < 2 * BLOCK) | (i < n_i - 1))
        bias1 = jnp.where(valid, 0.0, NEG_INF).astype(F32)
        bias = jnp.concatenate([bias1] * GQA_GROUP, axis=0)
        qb = q_ref[j * BLOCK:(j + 1) * BLOCK, :]
        for g in range(N_KV_HEADS):
            gl = slice(g * LANES, (g + 1) * LANES)
            parts = []
            for hh in range(GQA_GROUP):
                h = g * GQA_GROUP + hh
                qp = qb[:, (h // 2) * LANES:(h // 2 + 1) * LANES]
                parts.append(jnp.where(low if h % 2 == 0 else ~low, qp, jnp.zeros_like(qp)))
            qs = jnp.concatenate(parts, axis=0)
            kw = ka_ref[j * BLOCK:j * BLOCK + nkw, gl]
            vw = va_ref[j * BLOCK:j * BLOCK + nkw, gl]
            s_w = _dot_nt(qs, kw) + bias
            s_c = _dot_nt(qs, kc_ref[:, gl])
            sink = jnp.zeros((GQA_GROUP * BLOCK, 1), F32)
            for hh in range(GQA_GROUP):
                sink = jnp.where(hrow == hh, sink_ref[g * GQA_GROUP + hh], sink)
            m = jnp.maximum(jnp.maximum(jnp.max(s_w, axis=1, keepdims=True),
                                        jnp.max(s_c, axis=1, keepdims=True)), sink)
            e_w = jnp.exp(s_w - m)
            e_c = jnp.exp(s_c - m)
            den = (jnp.sum(e_w, axis=1, keepdims=True) + jnp.sum(e_c, axis=1, keepdims=True)
                   + jnp.exp(sink - m))
            o = (_dot(e_w.astype(BF16), vw) + _dot(e_c.astype(BF16), vc_ref[:, gl])) / den
            for pp in range(GQA_GROUP // 2):
                pair = jnp.where(low, o[(2 * pp) * BLOCK:(2 * pp + 1) * BLOCK],
                                 o[(2 * pp + 1) * BLOCK:(2 * pp + 2) * BLOCK])
                col = (g * (GQA_GROUP // 2) + pp) * LANES
                o_ref[j * BLOCK:(j + 1) * BLOCK, col:col + LANES] = pair.astype(BF16)


def _attention(sinks, q, kd, vd, kc, vc, batch, seq, tq):
    t = q.shape[0]
    per_b = seq // tq
    nqb = tq // BLOCK
    nb = seq // BLOCK
    n_ctx = kc.shape[0] // batch
    main = lambda b, i: (b * per_b + i, 0)
    prev = lambda b, i: (b * nb + jnp.maximum(i * nqb - 1, 0), 0)
    nxt = lambda b, i: (b * nb + jnp.minimum(i * nqb + nqb, nb - 1), 0)
    kvw = kd.shape[1]
    return pl.pallas_call(
        _attn_kernel,
        grid=(batch, per_b),
        in_specs=[pl.BlockSpec(memory_space=pltpu.SMEM),
                  pl.BlockSpec((tq, D_ATTN), main),
                  pl.BlockSpec((BLOCK, kvw), prev), pl.BlockSpec((tq, kvw), main), pl.BlockSpec((BLOCK, kvw), nxt),
                  pl.BlockSpec((BLOCK, kvw), prev), pl.BlockSpec((tq, kvw), main), pl.BlockSpec((BLOCK, kvw), nxt),
                  pl.BlockSpec((n_ctx, kvw), lambda b, i: (b, 0)),
                  pl.BlockSpec((n_ctx, kvw), lambda b, i: (b, 0))],
        out_specs=pl.BlockSpec((tq, D_ATTN), main),
        out_shape=jax.ShapeDtypeStruct((t, D_ATTN), BF16),
        scratch_shapes=[pltpu.VMEM((tq + 2 * BLOCK, kvw), BF16),
                        pltpu.VMEM((tq + 2 * BLOCK, kvw), BF16)],
        compiler_params=_params(("arbitrary", "arbitrary")),
        name="attention",
    )(sinks, q, kd, kd, kd, vd, vd, vd, kc, vc)


def _route(t, wr_hi_ref, wr_lo_ref, br_ref):
    th, tl = _split(t)
    logits = _dot(th, wr_hi_ref[...]) + _dot(tl, wr_hi_ref[...]) + _dot(th, wr_lo_ref[...]) + br_ref[...]
    lane_i = lax.broadcasted_iota(jnp.int32, logits.shape, 1)
    lane = lane_i.astype(F32)
    grp_of_lane = (lane_i >> 2).astype(F32)
    ninf = -jnp.inf
    far = float(LANES)
    is_g = (lane_i >= N_EXPERTS) & (lane_i < N_EXPERTS + N_GROUPS)
    glog = jnp.where(is_g, logits, ninf)
    gmax = jnp.max(glog, axis=1, keepdims=True)
    gidx = jnp.min(jnp.where(glog == gmax, lane - float(N_EXPERTS), far), axis=1, keepdims=True)
    group_p = 1.0 / jnp.sum(jnp.exp(glog - gmax), axis=1, keepdims=True)
    in_grp = (lane_i < N_EXPERTS) & (grp_of_lane == gidx)
    elog = jnp.where(in_grp, logits, ninf)
    v1 = jnp.max(elog, axis=1, keepdims=True)
    i1 = jnp.min(jnp.where(elog == v1, lane, far), axis=1, keepdims=True)
    elog2 = jnp.where(lane == i1, ninf, elog)
    v2 = jnp.max(elog2, axis=1, keepdims=True)
    i2 = jnp.min(jnp.where(elog2 == v2, lane, far), axis=1, keepdims=True)
    e = jnp.exp(v2 - v1)
    w1 = group_p / (1.0 + e)
    w2 = group_p * e / (1.0 + e)
    return jnp.where(lane == i1, w1, 0.0) + jnp.where(lane == i2, w2, 0.0)


def _mix_out_kernel(x_ref, yh_ref, ya_ref, sh1_ref, sc1_ref, g1_ref, sh2_ref, sc2_ref,
                    wg_ref, wbh_ref, wba_ref, wo_ref, lng_ref, lnb_ref, wrh_ref, wrl_ref, br_ref,
                    x1_ref, gate_ref):
    d = x_ref.shape[1]
    for r0 in range(0, x_ref.shape[0], SUB_ROWS):
        rows = slice(r0, r0 + SUB_ROWS)
        x = x_ref[rows, :]
        h = (_standardize(x) * (1.0 + sc1_ref[0]) + sh1_ref[0]).astype(BF16)
        g_hy = jax.nn.sigmoid(_dot(h, wg_ref[:, :d]))
        merged = g_hy * _dot(yh_ref[rows, :], wbh_ref[...])
        g_at = jax.nn.sigmoid(_dot(h, wg_ref[:, d:]))
        merged = merged + g_at * _dot(ya_ref[rows, :], wba_ref[...])
        mix = _dot(merged.astype(BF16), wo_ref[...])
        x1 = _standardize(DEEPNORM_ALPHA * x + g1_ref[0] * mix) * lng_ref[...] + lnb_ref[...]
        x1_ref[rows, :] = x1
        t = _standardize(x1) * (1.0 + sc2_ref[0]) + sh2_ref[0]
        gate_ref[rows, :] = _route(t, wrh_ref, wrl_ref, br_ref)


def _mix_out(x2d, yh, ya, mods, w_g, w_bh, w_ba, w_o, ln_g, ln_b, wr_hi, wr_lo, br, seq, tm):
    t, d = x2d.shape
    per_b = seq // tm
    row = lambda i: (i, 0)
    mod = lambda i: (i // per_b, 0, 0)
    mspec = pl.BlockSpec((1, 1, d), mod)
    sh1, sc1, g1, sh2, sc2 = mods
    return pl.pallas_call(
        _mix_out_kernel,
        grid=(t // tm,),
        in_specs=[pl.BlockSpec((tm, d), row),
                  pl.BlockSpec((tm, yh.shape[1]), row),
                  pl.BlockSpec((tm, ya.shape[1]), row),
                  mspec, mspec, mspec, mspec, mspec,
                  _const_spec(w_g.shape), _const_spec(w_bh.shape), _const_spec(w_ba.shape),
                  _const_spec(w_o.shape), _const_spec(ln_g.shape), _const_spec(ln_b.shape),
                  _const_spec(wr_hi.shape), _const_spec(wr_lo.shape), _const_spec(br.shape)],
        out_specs=[pl.BlockSpec((tm, d), row), pl.BlockSpec((tm, LANES), row)],
        out_shape=[jax.ShapeDtypeStruct((t, d), F32), jax.ShapeDtypeStruct((t, LANES), F32)],
        compiler_params=_params(("arbitrary",)),
        name="mix_out",
    )(x2d, yh, ya, sh1, sc1, g1, sh2, sc2, w_g, w_bh, w_ba, w_o, ln_g, ln_b, wr_hi, wr_lo, br)


def _moe_kernel(x1_ref, gate_ref, sh2_ref, sc2_ref, g2_ref, wg_ref, wu_ref, wd_ref, lng_ref, lnb_ref,
                o_ref, hid_ref):
    x1 = x1_ref[...]
    rows = x1.shape[0]
    t = (_standardize(x1) * (1.0 + sc2_ref[0]) + sh2_ref[0]).astype(BF16)
    gate = gate_ref[...]
    gw = EXPERTS_PER_GROUP * D_EXPERT
    for grp in range(N_GROUPS):
        cols = slice(grp * gw, (grp + 1) * gw)
        a = _dot(t, wg_ref[:, cols])
        u = _dot(t, wu_ref[:, cols])
        gcols = [jnp.broadcast_to(gate[:, e:e + 1], (rows, D_EXPERT))
                 for e in range(grp * EXPERTS_PER_GROUP, (grp + 1) * EXPERTS_PER_GROUP)]
        hid = a * jax.nn.sigmoid(a) * u * jnp.concatenate(gcols, axis=1)
        hid_ref[:, cols] = hid.astype(BF16)
    y = _dot(hid_ref[...], wd_ref[...])
    out = _standardize(DEEPNORM_ALPHA * x1 + g2_ref[0] * y) * lng_ref[...] + lnb_ref[...]
    o_ref[...] = out


def _moe(x1, gate, sh2, sc2, g2, wg_all, wu_all, wd_all, ln_g, ln_b, seq, tm):
    t, d = x1.shape
    per_b = seq // tm
    row = lambda i: (i, 0)
    mspec = pl.BlockSpec((1, 1, d), lambda i: (i // per_b, 0, 0))
    return pl.pallas_call(
        _moe_kernel,
        grid=(t // tm,),
        in_specs=[pl.BlockSpec((tm, d), row), pl.BlockSpec((tm, LANES), row),
                  mspec, mspec, mspec,
                  _const_spec(wg_all.shape), _const_spec(wu_all.shape), _const_spec(wd_all.shape),
                  _const_spec(ln_g.shape), _const_spec(ln_b.shape)],
        out_specs=pl.BlockSpec((tm, d), row),
        out_shape=jax.ShapeDtypeStruct((t, d), F32),
        scratch_shapes=[pltpu.VMEM((tm, wg_all.shape[1]), BF16)],
        compiler_params=_params(("arbitrary",)),
        name="moe",
    )(x1, gate, sh2, sc2, g2, wg_all, wu_all, wd_all, ln_g, ln_b)


def _dft_tables(seq):
    n = 2 * seq
    n1_full = n // FFT_N2
    n1_data = seq // FFT_N2
    k1 = np.arange(KH, dtype=np.float64)[None, :, None]
    n2 = np.arange(FFT_N2, dtype=np.float64)[:, None, None]
    n1 = np.arange(n1_full, dtype=np.float64)[None, None, :]
    ang = 2.0 * np.pi * k1 * (FFT_N2 * n1 + n2) / n
    fa = np.zeros((FFT_N2, 2 * KP, n1_full))
    fa[:, :KH] = np.cos(ang)
    fa[:, KP:KP + KH] = -np.sin(ang)
    wgt = np.full((KH,), 2.0)
    wgt[0] = 1.0
    wgt[KH - 1] = 1.0
    fai = np.zeros((FFT_N2, n1_data, 2 * KP))
    angt = np.transpose(ang[:, :, :n1_data], (0, 2, 1))
    fai[:, :, :KH] = np.cos(angt) * wgt / n
    fai[:, :, KP:KP + KH] = -np.sin(angt) * wgt / n
    kk = np.arange(FFT_N2, dtype=np.float64)
    a2 = 2.0 * np.pi * np.outer(kk, kk) / FFT_N2
    fr, fi = np.cos(a2), -np.sin(a2)
    fb = np.block([[fr, -fi], [fi, fr]])
    fbi = np.block([[fr, fi], [-fi, fr]])
    f32 = lambda a: jnp.asarray(a.astype(np.float32))
    return f32(fa[:, :, :n1_data]), f32(fa), f32(fai), f32(fb), f32(fbi)


def _filter_features(seq, rows):
    t = jnp.linspace(0.0, 1.0, seq, dtype=F32)
    w = 2.0 * math.pi * jnp.arange(seq, dtype=F32) / seq
    t2 = jnp.concatenate([t, t[::-1]])
    w2 = jnp.concatenate([w, w[::-1]])
    bands = jnp.linspace(1e-4, FILTER_BANDS - 1, FILTER_BANDS, dtype=F32)
    max_decay = math.log(DECAY_TARGET) / FAST_DECAY_PCT
    min_decay = math.log(DECAY_TARGET) / SLOW_DECAY_PCT
    deltas = jnp.linspace(min_decay, max_decay, D_HYENA, dtype=F32)

    def feats(k):
        tp = t2.reshape(-1, 2, rows // 2)[:, k].reshape(-1, 1)
        wp = w2.reshape(-1, 2, rows // 2)[:, k].reshape(-1, 1)
        pad = jnp.zeros((tp.shape[0], FILTER_ORDER - FILTER_EMB), F32)
        return jnp.concatenate([tp, jnp.cos(bands * wp), -jnp.sin(bands * wp), pad], axis=-1)

    zp = jnp.concatenate([feats(0), feats(1)], axis=-1)
    decay = jnp.exp(-t2[:, None] * jnp.abs(deltas))
    return zp, decay


def _rope_tables(seq):
    rows = seq // GRID_W
    row = jnp.repeat(jnp.arange(rows, dtype=F32), GRID_W)
    col = jnp.tile(jnp.arange(GRID_W, dtype=F32), rows)
    half = HEAD_DIM // 2
    inv_freq = ROPE_BASE ** (-jnp.arange(0, half, 2, dtype=F32) / half)
    ang = jnp.concatenate([row[:, None] * inv_freq, col[:, None] * inv_freq], axis=-1)
    cos, sin = jnp.cos(ang), jnp.sin(ang)
    c64 = jnp.concatenate([cos, cos], axis=-1)
    s64 = jnp.concatenate([-sin, sin], axis=-1)
    return jnp.concatenate([c64, c64], axis=-1), jnp.concatenate([s64, s64], axis=-1)


def _head_perm(n_heads):
    idx = []
    for h in range(n_heads):
        base = h * HEAD_DIM
        idx += [base + 2 * j for j in range(HEAD_DIM // 2)]
        idx += [base + 2 * j + 1 for j in range(HEAD_DIM // 2)]
    return np.asarray(idx, dtype=np.int32)


def _dup_heads(w):
    parts = []
    for g in range(N_KV_HEADS):
        blk = w[:, g * HEAD_DIM:(g + 1) * HEAD_DIM]
        parts += [blk, blk]
    return jnp.concatenate(parts, axis=1)


def kernel(x, c, ctx, c_ctx, ada_w, ada_b, w_in, hy_conv_w, hy_conv_b, hy_w1, hy_b1, hy_w2, hy_b2, hy_w3, hy_b3, hy_w4, hy_freq, hy_bias, attn_sinks, w_branch_hy, w_branch_attn, w_out, ln1_g, ln1_b, w_group, b_group, w_router, b_router, w_gate_e, w_up_e, w_down_e, ln2_g, ln2_b):
    batch, seq, d = x.shape
    n_ctx = ctx.shape[1]
    assert d == D_MODEL and ada_w.shape[0] == DEPTH == 1
    assert 2 * seq == FFT_N2 * FFT_N2 and seq % TM_PROJ == 0
    l = 0

    w = w_in[l]
    s0, s1, s2, s3 = 3 * D_HYENA, 3 * D_HYENA + D_ATTN, 3 * D_HYENA + D_ATTN + D_KV, 3 * D_HYENA + D_ATTN + 2 * D_KV
    w_q = w[:, s0:s1][:, _head_perm(N_HEADS)]
    w_k = _dup_heads(w[:, s1:s2][:, _head_perm(N_KV_HEADS)])
    w_v = _dup_heads(w[:, s2:s3])
    w_c = jnp.concatenate([w[:, :s0], w_q, w_k, w_v], axis=1).astype(BF16)
    w_kv = jnp.concatenate([w_k, w_v], axis=1).astype(BF16)
    w_g = w[:, s3:].astype(BF16)
    w_bh = w_branch_hy[l].astype(BF16)
    w_ba = w_branch_attn[l].astype(BF16)
    w_o = w_out[l].astype(BF16)
    wr = jnp.zeros((d, LANES), F32)
    wr = wr.at[:, :N_EXPERTS].set(w_router[l]).at[:, N_EXPERTS:N_EXPERTS + N_GROUPS].set(w_group[l])
    wr_hi = wr.astype(BF16)
    wr_lo = (wr - wr_hi.astype(F32)).astype(BF16)
    br = jnp.zeros((1, LANES), F32)
    br = br.at[0, :N_EXPERTS].set(b_router[l]).at[0, N_EXPERTS:N_EXPERTS + N_GROUPS].set(b_group[l])
    wg_all = jnp.transpose(w_gate_e[l], (1, 0, 2)).reshape(d, N_EXPERTS * D_EXPERT).astype(BF16)
    wu_all = jnp.transpose(w_up_e[l], (1, 0, 2)).reshape(d, N_EXPERTS * D_EXPERT).astype(BF16)
    wd_all = w_down_e[l].reshape(N_EXPERTS * D_EXPERT, d).astype(BF16)
    cw = jnp.concatenate([hy_conv_w[l][:, 0, :], hy_conv_b[l][None, :],
                          jnp.zeros((SUBLANES - SHORT_CONV - 1, 3 * D_HYENA), F32)], axis=0)
    row2 = lambda a: a.reshape(1, -1)
    pair = lambda a: jnp.concatenate([a, a], axis=-1)
    zero_o = jnp.zeros((FILTER_ORDER, FILTER_ORDER), F32)
    bdiag = lambda a: jnp.concatenate([jnp.concatenate([a, zero_o], axis=1),
                                       jnp.concatenate([zero_o, a], axis=1)], axis=0)
    w1p = bdiag(jnp.concatenate([hy_w1[l], jnp.zeros((FILTER_ORDER - FILTER_EMB, FILTER_ORDER), F32)], axis=0))
    w4h = jnp.transpose(hy_w4[l].reshape(FILTER_ORDER, 2, D_HYENA), (1, 0, 2))
    zero_w4 = jnp.zeros_like(w4h)
    w4s = jnp.stack([jnp.concatenate([w4h, zero_w4], axis=1),
                     jnp.concatenate([zero_w4, w4h], axis=1)], axis=1)

    fa, fa_full, fai, fb, fbi = _dft_tables(seq)
    fa, fa_full, fai, fb, fbi = (a.astype(BF16) for a in (fa, fa_full, fai, fb, fbi))
    zp, dec2 = _filter_features(seq, FILT_ROWS)
    cos_t, sin_t = _rope_tables(seq)

    cond = jnp.concatenate([c, c_ctx[None], jnp.zeros((SUBLANES - batch - 1, d), F32)], axis=0)
    mods = _adaln(cond, ada_w[l], ada_b[l])
    m6 = [mods[:, k * d:(k + 1) * d].reshape(SUBLANES, 1, d) for k in range(6)]
    sh1, sc1, g1, sh2, sc2, g2 = m6

    kc, vc = _ctx_kv(ctx.reshape(batch * n_ctx, d), sh1[batch], sc1[batch], w_kv, n_ctx)

    x2d = x.reshape(batch * seq, d)
    u, q, kd, vd = _in_proj(x2d, sh1, sc1, w_c, cos_t, sin_t, seq, TM_PROJ)
    h2u, ss = _filt_mlp(zp, w1p, pair(row2(hy_b1[l])), bdiag(hy_w2[l]), pair(row2(hy_b2[l])),
                        bdiag(hy_w3[l]), pair(row2(hy_b3[l])), w4s, pair(row2(hy_freq[l])), dec2, FILT_ROWS)
    hf = _filt_fft(h2u, ss, row2(hy_bias[l]), fa_full, fb)
    y_hy = _hyena(u.reshape(batch, seq, 3 * D_HYENA), cw, hf, fa, fai, fb, fbi)
    y_at = _attention(attn_sinks[l], q, kd, vd, kc, vc, batch, seq, TQ_ATTN)
    x1, gate = _mix_out(x2d, y_hy.reshape(batch * seq, D_HYENA), y_at, (sh1, sc1, g1, sh2, sc2),
                        w_g, w_bh, w_ba, w_o, row2(ln1_g[l]), row2(ln1_b[l]), wr_hi, wr_lo, br, seq, TM_PROJ)

    out = _moe(x1, gate, sh2, sc2, g2, wg_all, wu_all, wd_all, row2(ln2_g[l]), row2(ln2_b[l]), seq, TM_MOE)
    return out.reshape(batch, seq, d)
```

```python
import functools
import math

import numpy as np
import jax
import jax.numpy as jnp
from jax import lax
from jax.experimental import pallas as pl
from jax.experimental.pallas import tpu as pltpu

F32 = jnp.float32
BF16 = jnp.bfloat16

D_MODEL = 1024
GRID_W = 64
D_HYENA = D_MODEL // 2
SHORT_CONV = 3
FILTER_BANDS = 16
FILTER_EMB = 1 + 2 * FILTER_BANDS
FILTER_ORDER = 64
DECAY_TARGET = 1e-2
FAST_DECAY_PCT = 0.3
SLOW_DECAY_PCT = 1.5
HEAD_DIM = 64
D_ATTN = D_MODEL // 2
N_HEADS = D_ATTN // HEAD_DIM
N_KV_HEADS = N_HEADS // 4
GQA_GROUP = N_HEADS // N_KV_HEADS
D_KV = N_KV_HEADS * HEAD_DIM
WINDOW = 128
BLOCK = 128
ROPE_BASE = 10000.0
NEG_INF = -1e30
N_GROUPS = 4
EXPERTS_PER_GROUP = 4
N_EXPERTS = N_GROUPS * EXPERTS_PER_GROUP
D_EXPERT = D_MODEL // 4
LN_EPS = 1e-5
DEPTH = 1
DEEPNORM_ALPHA = (2.0 * DEPTH) ** 0.25

LANES = 128
SUBLANES = 8
VMEM_LIMIT = 56 * 1024 * 1024

SUB_ROWS = 512
TM_PROJ = 1024
TM_MOE = 512
TQ_ATTN = 512
FILT_ROWS = 2048

FFT_N2 = 128
KH = 65
KHP = 66
KP = 72
ZPITCH = 2 * FFT_N2 + SUBLANES
TPITCH = FFT_N2 + SUBLANES


def _dot(a, b):
    return jnp.dot(a, b, preferred_element_type=F32)


def _dot_nt(a, b):
    return lax.dot_general(a, b, (((1,), (1,)), ((), ())), preferred_element_type=F32)


def _split(a):
    hi = a.astype(BF16)
    lo = (a - hi.astype(F32)).astype(BF16)
    return hi, lo


def _dot3(a, b):
    ah, al = _split(a)
    bh, bl = _split(b)
    return _dot(ah, bh) + _dot(al, bh) + _dot(ah, bl)


def _standardize(x):
    mu = jnp.mean(x, axis=-1, keepdims=True)
    xc = x - mu
    var = jnp.mean(xc * xc, axis=-1, keepdims=True)
    return xc * lax.rsqrt(var + LN_EPS)


def _params(sem, vmem=VMEM_LIMIT):
    return pltpu.CompilerParams(dimension_semantics=sem, vmem_limit_bytes=vmem)


def _const_spec(shape):
    nd = len(shape)
    return pl.BlockSpec(shape, lambda *_: (0,) * nd, pipeline_mode=pl.Buffered(1))


def _adaln_kernel(c_ref, w_ref, b_ref, o_ref):
    s = c_ref[...]
    s = s * jax.nn.sigmoid(s)
    o_ref[...] = _dot3(s, w_ref[...]) + b_ref[...]


def _adaln(cond, w, b):
    n, d = cond.shape
    cols = w.shape[1]
    bc = 1024
    return pl.pallas_call(
        _adaln_kernel,
        grid=(cols // bc,),
        in_specs=[pl.BlockSpec((n, d), lambda j: (0, 0)),
                  pl.BlockSpec((d, bc), lambda j: (0, j)),
                  pl.BlockSpec((1, bc), lambda j: (0, j))],
        out_specs=pl.BlockSpec((n, bc), lambda j: (0, j)),
        out_shape=jax.ShapeDtypeStruct((n, cols), F32),
        compiler_params=_params(("arbitrary",)),
        name="adaln",
    )(cond, w, b.reshape(1, cols))


def _ctx_kv_kernel(x_ref, sh_ref, sc_ref, w_ref, k_ref, v_ref):
    h = _standardize(x_ref[...]) * (1.0 + sc_ref[...]) + sh_ref[...]
    kv = _dot(h.astype(BF16), w_ref[...])
    half = k_ref.shape[1]
    k_ref[...] = kv[:, :half].astype(BF16)
    v_ref[...] = kv[:, half:].astype(BF16)


def _ctx_kv(ctx2d, sh, sc, w_kv, rows):
    n, d = ctx2d.shape
    half = w_kv.shape[1] // 2
    return pl.pallas_call(
        _ctx_kv_kernel,
        grid=(n // rows,),
        in_specs=[pl.BlockSpec((rows, d), lambda i: (i, 0)),
                  pl.BlockSpec((1, d), lambda i: (0, 0)),
                  pl.BlockSpec((1, d), lambda i: (0, 0)),
                  pl.BlockSpec(w_kv.shape, lambda i: (0, 0))],
        out_specs=[pl.BlockSpec((rows, half), lambda i: (i, 0)),
                   pl.BlockSpec((rows, half), lambda i: (i, 0))],
        out_shape=[jax.ShapeDtypeStruct((n, half), BF16)] * 2,
        compiler_params=_params(("arbitrary",)),
        name="ctx_kv",
    )(ctx2d, sh, sc, w_kv)


def _rope(x, cos_t, sin_t):
    width = x.shape[1]
    reps = width // LANES
    c = jnp.concatenate([cos_t] * reps, axis=1)
    s = jnp.concatenate([sin_t] * reps, axis=1)
    half = HEAD_DIM // 2
    lane = lax.broadcasted_iota(jnp.int32, x.shape, 1)
    first_half = (lane & (HEAD_DIM - 1)) < half
    partner = jnp.where(first_half, pltpu.roll(x, width - half, 1), pltpu.roll(x, half, 1))
    return x * c + partner * s


def _in_proj_kernel(x_ref, sh_ref, sc_ref, w_ref, cos_ref, sin_ref, u_ref, q_ref, k_ref, v_ref):
    n_u = u_ref.shape[1]
    n_q = q_ref.shape[1]
    n_k = k_ref.shape[1]
    for r0 in range(0, x_ref.shape[0], SUB_ROWS):
        rows = slice(r0, r0 + SUB_ROWS)
        h = (_standardize(x_ref[rows, :]) * (1.0 + sc_ref[0]) + sh_ref[0]).astype(BF16)
        u_ref[rows, :] = _dot(h, w_ref[:, :n_u]).astype(BF16)
        cos_t = cos_ref[rows, :]
        sin_t = sin_ref[rows, :]
        q = _dot(h, w_ref[:, n_u:n_u + n_q])
        q_ref[rows, :] = (_rope(q, cos_t, sin_t) * (HEAD_DIM ** -0.5)).astype(BF16)
        k = _dot(h, w_ref[:, n_u + n_q:n_u + n_q + n_k])
        k_ref[rows, :] = _rope(k, cos_t, sin_t).astype(BF16)
        v_ref[rows, :] = _dot(h, w_ref[:, n_u + n_q + n_k:]).astype(BF16)


def _in_proj(x2d, sh, sc, w_c, cos_t, sin_t, seq, tm):
    t, d = x2d.shape
    per_b = seq // tm
    n_u, n_q, n_k = 3 * D_HYENA, D_ATTN, 2 * D_KV
    row = lambda i: (i, 0)
    mod = lambda i: (i // per_b, 0, 0)
    pos = lambda i: (i % per_b, 0)
    return pl.pallas_call(
        _in_proj_kernel,
        grid=(t // tm,),
        in_specs=[pl.BlockSpec((tm, d), row),
                  pl.BlockSpec((1, 1, d), mod),
                  pl.BlockSpec((1, 1, d), mod),
                  _const_spec(w_c.shape),
                  pl.BlockSpec((tm, LANES), pos),
                  pl.BlockSpec((tm, LANES), pos)],
        out_specs=[pl.BlockSpec((tm, n_u), row),
                   pl.BlockSpec((tm, n_q), row),
                   pl.BlockSpec((tm, n_k), row),
                   pl.BlockSpec((tm, n_k), row)],
        out_shape=[jax.ShapeDtypeStruct((t, n_u), BF16),
                   jax.ShapeDtypeStruct((t, n_q), BF16),
                   jax.ShapeDtypeStruct((t, n_k), BF16),
                   jax.ShapeDtypeStruct((t, n_k), BF16)],
        compiler_params=_params(("arbitrary",)),
        name="in_proj",
    )(x2d, sh, sc, w_c, cos_t, sin_t)


def _filt_mlp_kernel(z_ref, w1_ref, b1_ref, w2_ref, b2_ref, w3_ref, b3_ref, w4_ref, fr_ref, dec_ref,
                     h_ref, ss_ref):
    fr = fr_ref[...]
    a = jnp.sin(fr * (_dot3(z_ref[...], w1_ref[...]) + b1_ref[...]))
    a = jnp.sin(fr * (_dot3(a, w2_ref[...]) + b2_ref[...]))
    a = jnp.sin(fr * (_dot3(a, w3_ref[...]) + b3_ref[...]))
    half = a.shape[0]
    ss = jnp.zeros(ss_ref.shape, F32)
    for k in range(2):
        h = _dot3(a, w4_ref[0, k]) * dec_ref[k * half:(k + 1) * half, :]
        h_ref[k * half:(k + 1) * half, :] = h
        ss = ss + jnp.sum(h * h, axis=0, keepdims=True)

    @pl.when(pl.program_id(0) == 0)
    def _():
        ss_ref[...] = jnp.zeros_like(ss_ref)

    ss_ref[...] += ss


def _filt_mlp(zp, w1p, b1, w2, b2, w3, b3, w4s, fr, dec2, rows):
    n, c = dec2.shape
    half_steps = (n // 2) // rows
    vec = lambda a: pl.BlockSpec(a.shape, lambda i: (0,) * a.ndim)
    return pl.pallas_call(
        _filt_mlp_kernel,
        grid=(n // rows,),
        in_specs=[pl.BlockSpec((rows // 2, zp.shape[1]), lambda i: (i, 0)),
                  vec(w1p), vec(b1), vec(w2), vec(b2), vec(w3), vec(b3),
                  pl.BlockSpec((1,) + w4s.shape[1:], lambda i: (i // half_steps, 0, 0, 0)),
                  vec(fr),
                  pl.BlockSpec((rows, c), lambda i: (i, 0))],
        out_specs=[pl.BlockSpec((rows, c), lambda i: (i, 0)),
                   pl.BlockSpec((1, c), lambda i: (0, 0))],
        out_shape=[jax.ShapeDtypeStruct((n, c), F32), jax.ShapeDtypeStruct((1, c), F32)],
        compiler_params=_params(("arbitrary",)),
        name="filt_mlp",
    )(zp, w1p, b1, w2, b2, w3, b3, w4s, fr, dec2)


def _filt_fft_kernel(h_ref, ss_ref, bias_ref, fa_ref, fb_ref, o_ref, zs_ref):
    scale = lax.rsqrt(ss_ref[...] + 1e-6)
    n1 = h_ref.shape[0] // FFT_N2
    row = lax.broadcasted_iota(jnp.int32, (2 * FFT_N2, LANES), 0)
    impulse = jnp.where(row < FFT_N2, bias_ref[...], 0.0)

    def stage_a(n2, carry):
        slab = h_ref[pl.ds(n2, n1, stride=FFT_N2), :]
        z = _dot(fa_ref[n2], slab.astype(BF16))
        zs_ref[pl.ds(n2, KP, stride=ZPITCH), :] = z[:KP]
        zs_ref[pl.ds(FFT_N2 + n2, KP, stride=ZPITCH), :] = z[KP:]
        return carry

    lax.fori_loop(0, FFT_N2, stage_a, 0, unroll=8)

    def stage_b(p, carry):
        b0 = pl.multiple_of(2 * p * ZPITCH, SUBLANES)
        b1 = pl.multiple_of(b0 + ZPITCH, SUBLANES)
        z = jnp.concatenate([zs_ref[pl.ds(b0, 2 * FFT_N2), :], zs_ref[pl.ds(b1, 2 * FFT_N2), :]], axis=1)
        x = _dot(fb_ref[...], z.astype(BF16))
        o_ref[0, 2 * p] = x[:, :LANES] * scale + impulse
        o_ref[0, 2 * p + 1] = x[:, LANES:] * scale + impulse
        return carry

    lax.fori_loop(0, KHP // 2, stage_b, 0, unroll=11)


def _filt_fft(h2u, ss, bias, fa_full, fb):
    n, c = h2u.shape
    nblk = c // LANES
    return pl.pallas_call(
        _filt_fft_kernel,
        grid=(nblk,),
        in_specs=[pl.BlockSpec((n, LANES), lambda j: (0, j), pipeline_mode=pl.Buffered(1)),
                  pl.BlockSpec((1, LANES), lambda j: (0, j)),
                  pl.BlockSpec((1, LANES), lambda j: (0, j)),
                  _const_spec(fa_full.shape),
                  _const_spec(fb.shape)],
        out_specs=pl.BlockSpec((1, KHP, 2 * FFT_N2, LANES), lambda j: (j, 0, 0, 0)),
        out_shape=jax.ShapeDtypeStruct((nblk, KHP, 2 * FFT_N2, LANES), F32),
        scratch_shapes=[pltpu.VMEM((KP * ZPITCH, LANES), F32)],
        compiler_params=_params(("arbitrary",)),
        name="filt_fft",
    )(h2u, ss, bias, fa_full, fb)


def _conv_slab(u_ref, cw_ref, j, n_slabs):
    r0 = pl.multiple_of(j * FFT_N2, FFT_N2)
    cur = u_ref[0, pl.ds(r0, FFT_N2), :].astype(F32)
    grp = 2 * SUBLANES
    pr0 = pl.multiple_of(jnp.maximum(j * FFT_N2 - grp, 0), grp)
    nr0 = pl.multiple_of(jnp.minimum((j + 1) * FFT_N2, (n_slabs - 1) * FFT_N2), grp)
    prev_row = u_ref[0, pl.ds(pr0, grp), :].astype(F32)[grp - 1:grp]
    next_row = u_ref[0, pl.ds(nr0, grp), :].astype(F32)[0:1]
    prev_row = jnp.where(j > 0, prev_row, 0.0)
    next_row = jnp.where(j < n_slabs - 1, next_row, 0.0)
    row = lax.broadcasted_iota(jnp.int32, cur.shape, 0)
    before = jnp.where(row == 0, prev_row, pltpu.roll(cur, 1, 0))
    after = jnp.where(row == FFT_N2 - 1, next_row, pltpu.roll(cur, FFT_N2 - 1, 0))
    w = cw_ref[...]
    return before * w[0:1] + cur * w[1:2] + after * w[2:3] + w[3:4]


def _hyena_kernel(x0_ref, x1_ref, v_ref, cw0_ref, cw1_ref, cwv_ref, hf_ref,
                  fa_ref, fai_ref, fb_ref, fbi_ref, o_ref, ts_ref, zs_ref):
    n_slabs = x0_ref.shape[1] // FFT_N2

    def gated_value(j):
        return _conv_slab(x1_ref, cw1_ref, j, n_slabs) * _conv_slab(v_ref, cwv_ref, j, n_slabs)

    def fill(j, carry):
        ts_ref[pl.ds(pl.multiple_of(j * TPITCH, SUBLANES), FFT_N2), :] = gated_value(j)
        return carry

    lax.fori_loop(0, n_slabs, fill, 0, unroll=2)

    def stage_a(n2, carry):
        slab = ts_ref[pl.ds(n2, n_slabs, stride=TPITCH), :]
        z = _dot(fa_ref[n2], slab.astype(BF16))
        zs_ref[pl.ds(n2, KP, stride=ZPITCH), :] = z[:KP]
        zs_ref[pl.ds(FFT_N2 + n2, KP, stride=ZPITCH), :] = z[KP:]
        return carry

    lax.fori_loop(0, FFT_N2, stage_a, 0, unroll=16)

    def stage_b(p, carry):
        b0 = pl.multiple_of(2 * p * ZPITCH, SUBLANES)
        b1 = pl.multiple_of(b0 + ZPITCH, SUBLANES)
        z = jnp.concatenate([zs_ref[pl.ds(b0, 2 * FFT_N2), :], zs_ref[pl.ds(b1, 2 * FFT_N2), :]], axis=1)
        x = _dot(fb_ref[...], z.astype(BF16))
        h = jnp.concatenate([hf_ref[0, 2 * p], hf_ref[0, 2 * p + 1]], axis=1)
        xr, xi = x[:FFT_N2], x[FFT_N2:]
        hr, hi = h[:FFT_N2], h[FFT_N2:]
        prod = jnp.concatenate([xr * hr - xi * hi, xr * hi + xi * hr], axis=0)
        y = _dot(fbi_ref[...], prod.astype(BF16))
        zs_ref[pl.ds(b0, 2 * FFT_N2), :] = y[:, :LANES]
        zs_ref[pl.ds(b1, 2 * FFT_N2), :] = y[:, LANES:]
        return carry

    lax.fori_loop(0, KHP // 2, stage_b, 0, unroll=11)

    def stage_ai(n2, carry):
        yr = zs_ref[pl.ds(n2, KP, stride=ZPITCH), :]
        yi = zs_ref[pl.ds(FFT_N2 + n2, KP, stride=ZPITCH), :]
        y = jnp.concatenate([yr, yi], axis=0).astype(BF16)
        ts_ref[pl.ds(n2, n_slabs, stride=TPITCH), :] = _dot(fai_ref[n2], y)
        return carry

    lax.fori_loop(0, FFT_N2, stage_ai, 0, unroll=16)

    def finish(j, carry):
        conv = ts_ref[pl.ds(pl.multiple_of(j * TPITCH, SUBLANES), FFT_N2), :]
        y = _conv_slab(x0_ref, cw0_ref, j, n_slabs) * conv
        o_ref[0, pl.ds(pl.multiple_of(j * FFT_N2, FFT_N2), FFT_N2), :] = y.astype(BF16)
        return carry

    lax.fori_loop(0, n_slabs, finish, 0, unroll=2)


def _hyena(u, cw, hf, fa, fai, fb, fbi):
    b, seq, c3 = u.shape
    nblk = D_HYENA // LANES
    n_slabs = seq // FFT_N2
    stream = lambda k: pl.BlockSpec((1, seq, LANES), lambda j, i, k=k: (i, 0, k * nblk + j))
    cwspec = lambda k: pl.BlockSpec((SUBLANES, LANES), lambda j, i, k=k: (0, k * nblk + j))
    return pl.pallas_call(
        _hyena_kernel,
        grid=(nblk, b),
        in_specs=[stream(0), stream(1), stream(2), cwspec(0), cwspec(1), cwspec(2),
                  pl.BlockSpec((1, KHP, 2 * FFT_N2, LANES), lambda j, i: (j, 0, 0, 0),
                               pipeline_mode=pl.Buffered(1)),
                  _const_spec(fa.shape), _const_spec(fai.shape),
                  _const_spec(fb.shape), _const_spec(fbi.shape)],
        out_specs=pl.BlockSpec((1, seq, LANES), lambda j, i: (i, 0, j)),
        out_shape=jax.ShapeDtypeStruct((b, seq, D_HYENA), BF16),
        scratch_shapes=[pltpu.VMEM((n_slabs * TPITCH, LANES), F32),
                        pltpu.VMEM((KP * ZPITCH, LANES), F32)],
        compiler_params=_params(("arbitrary", "arbitrary")),
        name="hyena",
    )(u, u, u, cw, cw, cw, hf, fa, fai, fb, fbi)


def _attn_kernel(sink_ref, q_ref, kp_ref, km_ref, kn_ref, vp_ref, vm_ref, vn_ref, kc_ref, vc_ref,
                 o_ref, ka_ref, va_ref):
    i = pl.program_id(1)
    n_i = pl.num_programs(1)
    tq = q_ref.shape[0]
    nqb = tq // BLOCK
    ka_ref[0:BLOCK] = kp_ref[...]
    ka_ref[BLOCK:BLOCK + tq] = km_ref[...]
    ka_ref[BLOCK + tq:] = kn_ref[...]
    va_ref[0:BLOCK] = vp_ref[...]
    va_ref[BLOCK:BLOCK + tq] = vm_ref[...]
    va_ref[BLOCK + tq:] = vn_ref[...]

    qi = lax.broadcasted_iota(jnp.int32, (BLOCK, BLOCK), 0)
    kj = lax.broadcasted_iota(jnp.int32, (BLOCK, BLOCK), 1)
    lane = lax.broadcasted_iota(jnp.int32, (BLOCK, LANES), 1)
    low = lane < HEAD_DIM
    half = GQA_GROUP // 2
    rows2 = half * BLOCK
    hrow = lax.broadcasted_iota(jnp.int32, (GQA_GROUP * BLOCK, 1), 0) // BLOCK
    head_order = [hh for hh in range(GQA_GROUP) if hh % 2 == 0] + [hh for hh in range(GQA_GROUP) if hh % 2 == 1]
    one = jnp.ones((), BF16)

    def with_ones(v):
        lanes_low = lax.broadcasted_iota(jnp.int32, v.shape, 1) < HEAD_DIM
        return jnp.where(lanes_low, v, one), jnp.where(lanes_low, one, v)

    for g in range(N_KV_HEADS):
        gl = slice(g * LANES, (g + 1) * LANES)
        vc_even, vc_odd = with_ones(vc_ref[:, gl])
        sink = jnp.zeros((GQA_GROUP * BLOCK, 1), F32)
        for pos, hh in enumerate(head_order):
            sink = jnp.where(hrow == pos, sink_ref[g * GQA_GROUP + hh], sink)
        for j in range(nqb):
            prev_ok = kj >= qi
            next_ok = kj <= qi
            if j == 0:
                prev_ok = prev_ok & (i > 0)
            if j == nqb - 1:
                next_ok = next_ok & (i < n_i - 1)
            bias_p = jnp.concatenate([jnp.where(prev_ok, 0.0, NEG_INF).astype(F32)] * GQA_GROUP, axis=0)
            bias_n = jnp.concatenate([jnp.where(next_ok, 0.0, NEG_INF).astype(F32)] * GQA_GROUP, axis=0)
            qb = q_ref[j * BLOCK:(j + 1) * BLOCK, :]
            parts = []
            for hh in head_order:
                h = g * GQA_GROUP + hh
                qp = qb[:, (h // 2) * LANES:(h // 2 + 1) * LANES]
                parts.append(jnp.where(low if h % 2 == 0 else ~low, qp, jnp.zeros_like(qp)))
            qs = jnp.concatenate(parts, axis=0)
            kw = ka_ref[j * BLOCK:(j + 3) * BLOCK, gl]
            vw_even, vw_odd = with_ones(va_ref[j * BLOCK:(j + 3) * BLOCK, gl])
            s_w = _dot_nt(qs, kw)
            s_p = s_w[:, :BLOCK] + bias_p
            s_m = s_w[:, BLOCK:2 * BLOCK]
            s_n = s_w[:, 2 * BLOCK:] + bias_n
            s_c = _dot_nt(qs, kc_ref[:, gl])
            m = jnp.maximum(jnp.maximum(jnp.max(jnp.maximum(jnp.maximum(s_p, s_m), s_n), axis=1, keepdims=True),
                                        jnp.max(s_c, axis=1, keepdims=True)), sink)
            e_w = jnp.concatenate([jnp.exp(s_p - m), jnp.exp(s_m - m), jnp.exp(s_n - m)], axis=1).astype(BF16)
            e_c = jnp.exp(s_c - m).astype(BF16)
            e_sink = jnp.exp(sink - m)
            outs = []
            for par, (vw, vcx) in enumerate(((vw_even, vc_even), (vw_odd, vc_odd))):
                rs = slice(par * rows2, (par + 1) * rows2)
                acc = _dot(e_w[rs], vw) + _dot(e_c[rs], vcx)
                den = pltpu.roll(acc, HEAD_DIM, 1) + e_sink[rs]
                outs.append(acc / den)
            for pp in range(half):
                pair = jnp.where(low, outs[0][pp * BLOCK:(pp + 1) * BLOCK], outs[1][pp * BLOCK:(pp + 1) * BLOCK])
                col = (g * half + pp) * LANES
                o_ref[j * BLOCK:(j + 1) * BLOCK, col:col + LANES] = pair.astype(BF16)


def _attention(sinks, q, kd, vd, kc, vc, batch, seq, tq):
    t = q.shape[0]
    per_b = seq // tq
    nqb = tq // BLOCK
    nb = seq // BLOCK
    n_ctx = kc.shape[0] // batch
    main = lambda b, i: (b * per_b + i, 0)
    prev = lambda b, i: (b * nb + jnp.maximum(i * nqb - 1, 0), 0)
    nxt = lambda b, i: (b * nb + jnp.minimum(i * nqb + nqb, nb - 1), 0)
    kvw = kd.shape[1]
    return pl.pallas_call(
        _attn_kernel,
        grid=(batch, per_b),
        in_specs=[pl.BlockSpec(memory_space=pltpu.SMEM),
                  pl.BlockSpec((tq, D_ATTN), main),
                  pl.BlockSpec((BLOCK, kvw), prev), pl.BlockSpec((tq, kvw), main), pl.BlockSpec((BLOCK, kvw), nxt),
                  pl.BlockSpec((BLOCK, kvw), prev), pl.BlockSpec((tq, kvw), main), pl.BlockSpec((BLOCK, kvw), nxt),
                  pl.BlockSpec((n_ctx, kvw), lambda b, i: (b, 0)),
                  pl.BlockSpec((n_ctx, kvw), lambda b, i: (b, 0))],
        out_specs=pl.BlockSpec((tq, D_ATTN), main),
        out_shape=jax.ShapeDtypeStruct((t, D_ATTN), BF16),
        scratch_shapes=[pltpu.VMEM((tq + 2 * BLOCK, kvw), BF16),
                        pltpu.VMEM((tq + 2 * BLOCK, kvw), BF16)],
        compiler_params=_params(("arbitrary", "arbitrary")),
        name="attention",
    )(sinks, q, kd, kd, kd, vd, vd, vd, kc, vc)


def _route(t, wr_hi_ref, wr_hl_ref, br_ref):
    th, tl = _split(t)
    both = _dot(th, wr_hl_ref[...])
    logits = both[:, :LANES] + both[:, LANES:] + _dot(tl, wr_hi_ref[...]) + br_ref[...]
    lane_i = lax.broadcasted_iota(jnp.int32, logits.shape, 1)
    lane = lane_i.astype(F32)
    grp_of_lane = (lane_i >> 2).astype(F32)
    ninf = -jnp.inf
    far = float(LANES)
    is_g = (lane_i >= N_EXPERTS) & (lane_i < N_EXPERTS + N_GROUPS)
    glog = jnp.where(is_g, logits, ninf)
    gmax = jnp.max(glog, axis=1, keepdims=True)
    gidx = jnp.min(jnp.where(glog == gmax, lane - float(N_EXPERTS), far), axis=1, keepdims=True)
    group_p = 1.0 / jnp.sum(jnp.exp(glog - gmax), axis=1, keepdims=True)
    in_grp = (lane_i < N_EXPERTS) & (grp_of_lane == gidx)
    elog = jnp.where(in_grp, logits, ninf)
    v1 = jnp.max(elog, axis=1, keepdims=True)
    i1 = jnp.min(jnp.where(elog == v1, lane, far), axis=1, keepdims=True)
    elog2 = jnp.where(lane == i1, ninf, elog)
    v2 = jnp.max(elog2, axis=1, keepdims=True)
    i2 = jnp.min(jnp.where(elog2 == v2, lane, far), axis=1, keepdims=True)
    e = jnp.exp(v2 - v1)
    w1 = group_p / (1.0 + e)
    w2 = group_p * e / (1.0 + e)
    return jnp.where(lane == i1, w1, 0.0) + jnp.where(lane == i2, w2, 0.0)


def _mix_out_kernel(x_ref, yh_ref, ya_ref, sh1_ref, sc1_ref, g1_ref, sh2_ref, sc2_ref,
                    wg_ref, wbh_ref, wba_ref, wo_ref, lng_ref, lnb_ref, wrh_ref, wrhl_ref, br_ref,
                    x1_ref, gate_ref):
    d = x_ref.shape[1]
    for r0 in range(0, x_ref.shape[0], SUB_ROWS):
        rows = slice(r0, r0 + SUB_ROWS)
        x = x_ref[rows, :]
        h = (_standardize(x) * (1.0 + sc1_ref[0]) + sh1_ref[0]).astype(BF16)
        g_hy = jax.nn.sigmoid(_dot(h, wg_ref[:, :d]))
        merged = g_hy * _dot(yh_ref[rows, :], wbh_ref[...])
        g_at = jax.nn.sigmoid(_dot(h, wg_ref[:, d:]))
        merged = merged + g_at * _dot(ya_ref[rows, :], wba_ref[...])
        mix = _dot(merged.astype(BF16), wo_ref[...])
        x1 = _standardize(DEEPNORM_ALPHA * x + g1_ref[0] * mix) * lng_ref[...] + lnb_ref[...]
        x1_ref[rows, :] = x1
        t = _standardize(x1) * (1.0 + sc2_ref[0]) + sh2_ref[0]
        gate_ref[rows, :] = _route(t, wrh_ref, wrhl_ref, br_ref)


def _mix_out(x2d, yh, ya, mods, w_g, w_bh, w_ba, w_o, ln_g, ln_b, wr_hi, wr_hl, br, seq, tm):
    t, d = x2d.shape
    per_b = seq // tm
    row = lambda i: (i, 0)
    mod = lambda i: (i // per_b, 0, 0)
    mspec = pl.BlockSpec((1, 1, d), mod)
    sh1, sc1, g1, sh2, sc2 = mods
    return pl.pallas_call(
        _mix_out_kernel,
        grid=(t // tm,),
        in_specs=[pl.BlockSpec((tm, d), row),
                  pl.BlockSpec((tm, yh.shape[1]), row),
                  pl.BlockSpec((tm, ya.shape[1]), row),
                  mspec, mspec, mspec, mspec, mspec,
                  _const_spec(w_g.shape), _const_spec(w_bh.shape), _const_spec(w_ba.shape),
                  _const_spec(w_o.shape), _const_spec(ln_g.shape), _const_spec(ln_b.shape),
                  _const_spec(wr_hi.shape), _const_spec(wr_hl.shape), _const_spec(br.shape)],
        out_specs=[pl.BlockSpec((tm, d), row), pl.BlockSpec((tm, LANES), row)],
        out_shape=[jax.ShapeDtypeStruct((t, d), F32), jax.ShapeDtypeStruct((t, LANES), F32)],
        compiler_params=_params(("arbitrary",)),
        name="mix_out",
    )(x2d, yh, ya, sh1, sc1, g1, sh2, sc2, w_g, w_bh, w_ba, w_o, ln_g, ln_b, wr_hi, wr_hl, br)


def _moe_kernel(x1_ref, gate_ref, sh2_ref, sc2_ref, g2_ref, wg_ref, wu_ref, wd_ref, lng_ref, lnb_ref,
                o_ref, hid_ref):
    x1 = x1_ref[...]
    rows = x1.shape[0]
    t = (_standardize(x1) * (1.0 + sc2_ref[0]) + sh2_ref[0]).astype(BF16)
    gate = gate_ref[...]
    gw = EXPERTS_PER_GROUP * D_EXPERT
    for grp in range(N_GROUPS):
        cols = slice(grp * gw, (grp + 1) * gw)
        a = _dot(t, wg_ref[:, cols])
        u = _dot(t, wu_ref[:, cols])
        gcols = [jnp.broadcast_to(gate[:, e:e + 1], (rows, D_EXPERT))
                 for e in range(grp * EXPERTS_PER_GROUP, (grp + 1) * EXPERTS_PER_GROUP)]
        hid = a * jax.nn.sigmoid(a) * u * jnp.concatenate(gcols, axis=1)
        hid_ref[:, cols] = hid.astype(BF16)
    y = _dot(hid_ref[...], wd_ref[...])
    out = _standardize(DEEPNORM_ALPHA * x1 + g2_ref[0] * y) * lng_ref[...] + lnb_ref[...]
    o_ref[...] = out


def _moe(x1, gate, sh2, sc2, g2, wg_all, wu_all, wd_all, ln_g, ln_b, seq, tm):
    t, d = x1.shape
    per_b = seq // tm
    row = lambda i: (i, 0)
    mspec = pl.BlockSpec((1, 1, d), lambda i: (i // per_b, 0, 0))
    return pl.pallas_call(
        _moe_kernel,
        grid=(t // tm,),
        in_specs=[pl.BlockSpec((tm, d), row), pl.BlockSpec((tm, LANES), row),
                  mspec, mspec, mspec,
                  _const_spec(wg_all.shape), _const_spec(wu_all.shape), _const_spec(wd_all.shape),
                  _const_spec(ln_g.shape), _const_spec(ln_b.shape)],
        out_specs=pl.BlockSpec((tm, d), row),
        out_shape=jax.ShapeDtypeStruct((t, d), F32),
        scratch_shapes=[pltpu.VMEM((tm, wg_all.shape[1]), BF16)],
        compiler_params=_params(("arbitrary",)),
        name="moe",
    )(x1, gate, sh2, sc2, g2, wg_all, wu_all, wd_all, ln_g, ln_b)


def _dft_tables(seq):
    n = 2 * seq
    n1_full = n // FFT_N2
    n1_data = seq // FFT_N2
    k1 = np.arange(KH, dtype=np.float64)[None, :, None]
    n2 = np.arange(FFT_N2, dtype=np.float64)[:, None, None]
    n1 = np.arange(n1_full, dtype=np.float64)[None, None, :]
    ang = 2.0 * np.pi * k1 * (FFT_N2 * n1 + n2) / n
    fa = np.zeros((FFT_N2, 2 * KP, n1_full))
    fa[:, :KH] = np.cos(ang)
    fa[:, KP:KP + KH] = -np.sin(ang)
    wgt = np.full((KH,), 2.0)
    wgt[0] = 1.0
    wgt[KH - 1] = 1.0
    fai = np.zeros((FFT_N2, n1_data, 2 * KP))
    angt = np.transpose(ang[:, :, :n1_data], (0, 2, 1))
    fai[:, :, :KH] = np.cos(angt) * wgt / n
    fai[:, :, KP:KP + KH] = -np.sin(angt) * wgt / n
    kk = np.arange(FFT_N2, dtype=np.float64)
    a2 = 2.0 * np.pi * np.outer(kk, kk) / FFT_N2
    fr, fi = np.cos(a2), -np.sin(a2)
    fb = np.block([[fr, -fi], [fi, fr]])
    fbi = np.block([[fr, fi], [-fi, fr]])
    f32 = lambda a: jnp.asarray(a.astype(np.float32))
    return f32(fa[:, :, :n1_data]), f32(fa), f32(fai), f32(fb), f32(fbi)


def _filter_features(seq, rows):
    t = jnp.linspace(0.0, 1.0, seq, dtype=F32)
    w = 2.0 * math.pi * jnp.arange(seq, dtype=F32) / seq
    t2 = jnp.concatenate([t, t[::-1]])
    w2 = jnp.concatenate([w, w[::-1]])
    bands = jnp.linspace(1e-4, FILTER_BANDS - 1, FILTER_BANDS, dtype=F32)
    max_decay = math.log(DECAY_TARGET) / FAST_DECAY_PCT
    min_decay = math.log(DECAY_TARGET) / SLOW_DECAY_PCT
    deltas = jnp.linspace(min_decay, max_decay, D_HYENA, dtype=F32)

    def feats(k):
        tp = t2.reshape(-1, 2, rows // 2)[:, k].reshape(-1, 1)
        wp = w2.reshape(-1, 2, rows // 2)[:, k].reshape(-1, 1)
        pad = jnp.zeros((tp.shape[0], FILTER_ORDER - FILTER_EMB), F32)
        return jnp.concatenate([tp, jnp.cos(bands * wp), -jnp.sin(bands * wp), pad], axis=-1)

    zp = jnp.concatenate([feats(0), feats(1)], axis=-1)
    decay = jnp.exp(-t2[:, None] * jnp.abs(deltas))
    return zp, decay


def _rope_tables(seq):
    rows = seq // GRID_W
    row = jnp.repeat(jnp.arange(rows, dtype=F32), GRID_W)
    col = jnp.tile(jnp.arange(GRID_W, dtype=F32), rows)
    half = HEAD_DIM // 2
    inv_freq = ROPE_BASE ** (-jnp.arange(0, half, 2, dtype=F32) / half)
    ang = jnp.concatenate([row[:, None] * inv_freq, col[:, None] * inv_freq], axis=-1)
    cos, sin = jnp.cos(ang), jnp.sin(ang)
    c64 = jnp.concatenate([cos, cos], axis=-1)
    s64 = jnp.concatenate([-sin, sin], axis=-1)
    return jnp.concatenate([c64, c64], axis=-1), jnp.concatenate([s64, s64], axis=-1)


def _head_perm(n_heads):
    idx = []
    for h in range(n_heads):
        base = h * HEAD_DIM
        idx += [base + 2 * j for j in range(HEAD_DIM // 2)]
        idx += [base + 2 * j + 1 for j in range(HEAD_DIM // 2)]
    return np.asarray(idx, dtype=np.int32)


def _dup_heads(w):
    parts = []
    for g in range(N_KV_HEADS):
        blk = w[:, g * HEAD_DIM:(g + 1) * HEAD_DIM]
        parts += [blk, blk]
    return jnp.concatenate(parts, axis=1)


def kernel(x, c, ctx, c_ctx, ada_w, ada_b, w_in, hy_conv_w, hy_conv_b, hy_w1, hy_b1, hy_w2, hy_b2, hy_w3, hy_b3, hy_w4, hy_freq, hy_bias, attn_sinks, w_branch_hy, w_branch_attn, w_out, ln1_g, ln1_b, w_group, b_group, w_router, b_router, w_gate_e, w_up_e, w_down_e, ln2_g, ln2_b):
    batch, seq, d = x.shape
    n_ctx = ctx.shape[1]
    assert d == D_MODEL and ada_w.shape[0] == DEPTH == 1
    assert 2 * seq == FFT_N2 * FFT_N2 and seq % TM_PROJ == 0
    l = 0

    w = w_in[l]
    s0, s1, s2, s3 = 3 * D_HYENA, 3 * D_HYENA + D_ATTN, 3 * D_HYENA + D_ATTN + D_KV, 3 * D_HYENA + D_ATTN + 2 * D_KV
    w_q = w[:, s0:s1][:, _head_perm(N_HEADS)]
    w_k = _dup_heads(w[:, s1:s2][:, _head_perm(N_KV_HEADS)])
    w_v = _dup_heads(w[:, s2:s3])
    w_c = jnp.concatenate([w[:, :s0], w_q, w_k, w_v], axis=1).astype(BF16)
    w_kv = jnp.concatenate([w_k, w_v], axis=1).astype(BF16)
    w_g = w[:, s3:].astype(BF16)
    w_bh = w_branch_hy[l].astype(BF16)
    w_ba = w_branch_attn[l].astype(BF16)
    w_o = w_out[l].astype(BF16)
    wr = jnp.zeros((d, LANES), F32)
    wr = wr.at[:, :N_EXPERTS].set(w_router[l]).at[:, N_EXPERTS:N_EXPERTS + N_GROUPS].set(w_group[l])
    wr_hi = wr.astype(BF16)
    wr_hl = jnp.concatenate([wr_hi, (wr - wr_hi.astype(F32)).astype(BF16)], axis=1)
    br = jnp.zeros((1, LANES), F32)
    br = br.at[0, :N_EXPERTS].set(b_router[l]).at[0, N_EXPERTS:N_EXPERTS + N_GROUPS].set(b_group[l])
    wg_all = jnp.transpose(w_gate_e[l], (1, 0, 2)).reshape(d, N_EXPERTS * D_EXPERT).astype(BF16)
    wu_all = jnp.transpose(w_up_e[l], (1, 0, 2)).reshape(d, N_EXPERTS * D_EXPERT).astype(BF16)
    wd_all = w_down_e[l].reshape(N_EXPERTS * D_EXPERT, d).astype(BF16)
    cw = jnp.concatenate([hy_conv_w[l][:, 0, :], hy_conv_b[l][None, :],
                          jnp.zeros((SUBLANES - SHORT_CONV - 1, 3 * D_HYENA), F32)], axis=0)
    row2 = lambda a: a.reshape(1, -1)
    pair = lambda a: jnp.concatenate([a, a], axis=-1)
    zero_o = jnp.zeros((FILTER_ORDER, FILTER_ORDER), F32)
    bdiag = lambda a: jnp.concatenate([jnp.concatenate([a, zero_o], axis=1),
                                       jnp.concatenate([zero_o, a], axis=1)], axis=0)
    w1p = bdiag(jnp.concatenate([hy_w1[l], jnp.zeros((FILTER_ORDER - FILTER_EMB, FILTER_ORDER), F32)], axis=0))
    w4h = jnp.transpose(hy_w4[l].reshape(FILTER_ORDER, 2, D_HYENA), (1, 0, 2))
    zero_w4 = jnp.zeros_like(w4h)
    w4s = jnp.stack([jnp.concatenate([w4h, zero_w4], axis=1),
                     jnp.concatenate([zero_w4, w4h], axis=1)], axis=1)

    fa, fa_full, fai, fb, fbi = _dft_tables(seq)
    fa, fa_full, fai, fb, fbi = (a.astype(BF16) for a in (fa, fa_full, fai, fb, fbi))
    zp, dec2 = _filter_features(seq, FILT_ROWS)
    cos_t, sin_t = _rope_tables(seq)

    cond = jnp.concatenate([c, c_ctx[None], jnp.zeros((SUBLANES - batch - 1, d), F32)], axis=0)
    mods = _adaln(cond, ada_w[l], ada_b[l])
    m6 = [mods[:, k * d:(k + 1) * d].reshape(SUBLANES, 1, d) for k in range(6)]
    sh1, sc1, g1, sh2, sc2, g2 = m6

    kc, vc = _ctx_kv(ctx.reshape(batch * n_ctx, d), sh1[batch], sc1[batch], w_kv, n_ctx)

    x2d = x.reshape(batch * seq, d)
    u, q, kd, vd = _in_proj(x2d, sh1, sc1, w_c, cos_t, sin_t, seq, TM_PROJ)
    h2u, ss = _filt_mlp(zp, w1p, pair(row2(hy_b1[l])), bdiag(hy_w2[l]), pair(row2(hy_b2[l])),
                        bdiag(hy_w3[l]), pair(row2(hy_b3[l])), w4s, pair(row2(hy_freq[l])), dec2, FILT_ROWS)
    hf = _filt_fft(h2u, ss, row2(hy_bias[l]), fa_full, fb)
    y_hy = _hyena(u.reshape(batch, seq, 3 * D_HYENA), cw, hf, fa, fai, fb, fbi)
    y_at = _attention(attn_sinks[l], q, kd, vd, kc, vc, batch, seq, TQ_ATTN)
    x1, gate = _mix_out(x2d, y_hy.reshape(batch * seq, D_HYENA), y_at, (sh1, sc1, g1, sh2, sc2),
                        w_g, w_bh, w_ba, w_o, row2(ln1_g[l]), row2(ln1_b[l]), wr_hi, wr_hl, br, seq, TM_PROJ)

    out = _moe(x1, gate, sh2, sc2, g2, wg_all, wu_all, wd_all, row2(ln2_g[l]), row2(ln2_b[l]), seq, TM_MOE)
    return out.reshape(batch, seq, d)
```

```python
import functools
import math

import numpy as np
import jax
import jax.numpy as jnp
from jax import lax
from jax.experimental import pallas as pl
from jax.experimental.pallas import tpu as pltpu
from jax.experimental.pallas import tpu_sc as plsc

F32 = jnp.float32
BF16 = jnp.bfloat16

D_MODEL = 1024
GRID_W = 64
D_HYENA = D_MODEL // 2
SHORT_CONV = 3
FILTER_BANDS = 16
FILTER_EMB = 1 + 2 * FILTER_BANDS
FILTER_ORDER = 64
DECAY_TARGET = 1e-2
FAST_DECAY_PCT = 0.3
SLOW_DECAY_PCT = 1.5
HEAD_DIM = 64
D_ATTN = D_MODEL // 2
N_HEADS = D_ATTN // HEAD_DIM
N_KV_HEADS = N_HEADS // 4
GQA_GROUP = N_HEADS // N_KV_HEADS
D_KV = N_KV_HEADS * HEAD_DIM
WINDOW = 128
BLOCK = 128
ROPE_BASE = 10000.0
NEG_INF = -1e30
N_GROUPS = 4
EXPERTS_PER_GROUP = 4
N_EXPERTS = N_GROUPS * EXPERTS_PER_GROUP
D_EXPERT = D_MODEL // 4
LN_EPS = 1e-5
DEPTH = 1
DEEPNORM_ALPHA = (2.0 * DEPTH) ** 0.25

LANES = 128
SUBLANES = 8
VMEM_LIMIT = 56 * 1024 * 1024

SUB_ROWS = 512
TM_PROJ = 1024
MOE_CHUNK = 512
TM_FINAL = 1024
GROUP_LANE = N_EXPERTS
SC_WINDOW = 128
SC_ROW_WORDS = 256
TQ_ATTN = 512
FILT_ROWS = 2048

FFT_N2 = 128
KH = 65
KHP = 66
KP = 72
ZPITCH = 2 * FFT_N2 + SUBLANES
TPITCH = FFT_N2 + SUBLANES


def _dot(a, b):
    return jnp.dot(a, b, preferred_element_type=F32)


def _dot_nt(a, b):
    return lax.dot_general(a, b, (((1,), (1,)), ((), ())), preferred_element_type=F32)


def _split(a):
    hi = a.astype(BF16)
    lo = (a - hi.astype(F32)).astype(BF16)
    return hi, lo


def _dot3(a, b):
    ah, al = _split(a)
    bh, bl = _split(b)
    return _dot(ah, bh) + _dot(al, bh) + _dot(ah, bl)


def _standardize(x):
    mu = jnp.mean(x, axis=-1, keepdims=True)
    xc = x - mu
    var = jnp.mean(xc * xc, axis=-1, keepdims=True)
    return xc * lax.rsqrt(var + LN_EPS)


def _params(sem, vmem=VMEM_LIMIT):
    return pltpu.CompilerParams(dimension_semantics=sem, vmem_limit_bytes=vmem)


def _const_spec(shape):
    nd = len(shape)
    return pl.BlockSpec(shape, lambda *_: (0,) * nd, pipeline_mode=pl.Buffered(1))


def _adaln_kernel(c_ref, w_ref, b_ref, o_ref):
    s = c_ref[...]
    s = s * jax.nn.sigmoid(s)
    o_ref[...] = _dot3(s, w_ref[...]) + b_ref[...]


def _adaln(cond, w, b):
    n, d = cond.shape
    cols = w.shape[1]
    bc = 1024
    return pl.pallas_call(
        _adaln_kernel,
        grid=(cols // bc,),
        in_specs=[pl.BlockSpec((n, d), lambda j: (0, 0)),
                  pl.BlockSpec((d, bc), lambda j: (0, j)),
                  pl.BlockSpec((1, bc), lambda j: (0, j))],
        out_specs=pl.BlockSpec((n, bc), lambda j: (0, j)),
        out_shape=jax.ShapeDtypeStruct((n, cols), F32),
        compiler_params=_params(("arbitrary",)),
        name="adaln",
    )(cond, w, b.reshape(1, cols))


def _ctx_kv_kernel(x_ref, sh_ref, sc_ref, w_ref, k_ref, v_ref):
    h = _standardize(x_ref[...]) * (1.0 + sc_ref[...]) + sh_ref[...]
    kv = _dot(h.astype(BF16), w_ref[...])
    half = k_ref.shape[1]
    k_ref[...] = kv[:, :half].astype(BF16)
    v_ref[...] = kv[:, half:].astype(BF16)


def _ctx_kv(ctx2d, sh, sc, w_kv, rows):
    n, d = ctx2d.shape
    half = w_kv.shape[1] // 2
    return pl.pallas_call(
        _ctx_kv_kernel,
        grid=(n // rows,),
        in_specs=[pl.BlockSpec((rows, d), lambda i: (i, 0)),
                  pl.BlockSpec((1, d), lambda i: (0, 0)),
                  pl.BlockSpec((1, d), lambda i: (0, 0)),
                  pl.BlockSpec(w_kv.shape, lambda i: (0, 0))],
        out_specs=[pl.BlockSpec((rows, half), lambda i: (i, 0)),
                   pl.BlockSpec((rows, half), lambda i: (i, 0))],
        out_shape=[jax.ShapeDtypeStruct((n, half), BF16)] * 2,
        compiler_params=_params(("arbitrary",)),
        name="ctx_kv",
    )(ctx2d, sh, sc, w_kv)


def _rope(x, cos_t, sin_t):
    width = x.shape[1]
    reps = width // LANES
    c = jnp.concatenate([cos_t] * reps, axis=1)
    s = jnp.concatenate([sin_t] * reps, axis=1)
    half = HEAD_DIM // 2
    lane = lax.broadcasted_iota(jnp.int32, x.shape, 1)
    first_half = (lane & (HEAD_DIM - 1)) < half
    partner = jnp.where(first_half, pltpu.roll(x, width - half, 1), pltpu.roll(x, half, 1))
    return x * c + partner * s


def _in_proj_kernel(x_ref, sh_ref, sc_ref, w_ref, cos_ref, sin_ref, u_ref, q_ref, k_ref, v_ref):
    n_u = u_ref.shape[1]
    n_q = q_ref.shape[1]
    n_k = k_ref.shape[1]
    for r0 in range(0, x_ref.shape[0], SUB_ROWS):
        rows = slice(r0, r0 + SUB_ROWS)
        h = (_standardize(x_ref[rows, :]) * (1.0 + sc_ref[0]) + sh_ref[0]).astype(BF16)
        u_ref[rows, :] = _dot(h, w_ref[:, :n_u]).astype(BF16)
        cos_t = cos_ref[rows, :]
        sin_t = sin_ref[rows, :]
        q = _dot(h, w_ref[:, n_u:n_u + n_q])
        q_ref[rows, :] = (_rope(q, cos_t, sin_t) * (HEAD_DIM ** -0.5)).astype(BF16)
        k = _dot(h, w_ref[:, n_u + n_q:n_u + n_q + n_k])
        k_ref[rows, :] = _rope(k, cos_t, sin_t).astype(BF16)
        v_ref[rows, :] = _dot(h, w_ref[:, n_u + n_q + n_k:]).astype(BF16)


def _in_proj(x2d, sh, sc, w_c, cos_t, sin_t, seq, tm):
    t, d = x2d.shape
    per_b = seq // tm
    n_u, n_q, n_k = 3 * D_HYENA, D_ATTN, 2 * D_KV
    row = lambda i: (i, 0)
    mod = lambda i: (i // per_b, 0, 0)
    pos = lambda i: (i % per_b, 0)
    return pl.pallas_call(
        _in_proj_kernel,
        grid=(t // tm,),
        in_specs=[pl.BlockSpec((tm, d), row),
                  pl.BlockSpec((1, 1, d), mod),
                  pl.BlockSpec((1, 1, d), mod),
                  _const_spec(w_c.shape),
                  pl.BlockSpec((tm, LANES), pos),
                  pl.BlockSpec((tm, LANES), pos)],
        out_specs=[pl.BlockSpec((tm, n_u), row),
                   pl.BlockSpec((tm, n_q), row),
                   pl.BlockSpec((tm, n_k), row),
                   pl.BlockSpec((tm, n_k), row)],
        out_shape=[jax.ShapeDtypeStruct((t, n_u), BF16),
                   jax.ShapeDtypeStruct((t, n_q), BF16),
                   jax.ShapeDtypeStruct((t, n_k), BF16),
                   jax.ShapeDtypeStruct((t, n_k), BF16)],
        compiler_params=_params(("arbitrary",)),
        name="in_proj",
    )(x2d, sh, sc, w_c, cos_t, sin_t)


def _filt_mlp_kernel(z_ref, w1_ref, b1_ref, w2_ref, b2_ref, w3_ref, b3_ref, w4_ref, fr_ref, dec_ref,
                     h_ref, ss_ref):
    fr = fr_ref[...]
    a = jnp.sin(fr * (_dot3(z_ref[...], w1_ref[...]) + b1_ref[...]))
    a = jnp.sin(fr * (_dot3(a, w2_ref[...]) + b2_ref[...]))
    a = jnp.sin(fr * (_dot3(a, w3_ref[...]) + b3_ref[...]))
    half = a.shape[0]
    ss = jnp.zeros(ss_ref.shape, F32)
    for k in range(2):
        h = _dot3(a, w4_ref[0, k]) * dec_ref[k * half:(k + 1) * half, :]
        h_ref[k * half:(k + 1) * half, :] = h
        ss = ss + jnp.sum(h * h, axis=0, keepdims=True)

    @pl.when(pl.program_id(0) == 0)
    def _():
        ss_ref[...] = jnp.zeros_like(ss_ref)

    ss_ref[...] += ss


def _filt_mlp(zp, w1p, b1, w2, b2, w3, b3, w4s, fr, dec2, rows):
    n, c = dec2.shape
    half_steps = (n // 2) // rows
    vec = lambda a: pl.BlockSpec(a.shape, lambda i: (0,) * a.ndim)
    return pl.pallas_call(
        _filt_mlp_kernel,
        grid=(n // rows,),
        in_specs=[pl.BlockSpec((rows // 2, zp.shape[1]), lambda i: (i, 0)),
                  vec(w1p), vec(b1), vec(w2), vec(b2), vec(w3), vec(b3),
                  pl.BlockSpec((1,) + w4s.shape[1:], lambda i: (i // half_steps, 0, 0, 0)),
                  vec(fr),
                  pl.BlockSpec((rows, c), lambda i: (i, 0))],
        out_specs=[pl.BlockSpec((rows, c), lambda i: (i, 0)),
                   pl.BlockSpec((1, c), lambda i: (0, 0))],
        out_shape=[jax.ShapeDtypeStruct((n, c), F32), jax.ShapeDtypeStruct((1, c), F32)],
        compiler_params=_params(("arbitrary",)),
        name="filt_mlp",
    )(zp, w1p, b1, w2, b2, w3, b3, w4s, fr, dec2)


def _filt_fft_kernel(h_ref, ss_ref, bias_ref, fa_ref, fb_ref, o_ref, zs_ref):
    scale = lax.rsqrt(ss_ref[...] + 1e-6)
    n1 = h_ref.shape[0] // FFT_N2
    row = lax.broadcasted_iota(jnp.int32, (2 * FFT_N2, LANES), 0)
    impulse = jnp.where(row < FFT_N2, bias_ref[...], 0.0)

    def stage_a(n2, carry):
        slab = h_ref[pl.ds(n2, n1, stride=FFT_N2), :]
        z = _dot(fa_ref[n2], slab.astype(BF16))
        zs_ref[pl.ds(n2, KP, stride=ZPITCH), :] = z[:KP]
        zs_ref[pl.ds(FFT_N2 + n2, KP, stride=ZPITCH), :] = z[KP:]
        return carry

    lax.fori_loop(0, FFT_N2, stage_a, 0, unroll=8)

    def stage_b(p, carry):
        b0 = pl.multiple_of(2 * p * ZPITCH, SUBLANES)
        b1 = pl.multiple_of(b0 + ZPITCH, SUBLANES)
        z = jnp.concatenate([zs_ref[pl.ds(b0, 2 * FFT_N2), :], zs_ref[pl.ds(b1, 2 * FFT_N2), :]], axis=1)
        x = _dot(fb_ref[...], z.astype(BF16))
        o_ref[0, 2 * p] = x[:, :LANES] * scale + impulse
        o_ref[0, 2 * p + 1] = x[:, LANES:] * scale + impulse
        return carry

    lax.fori_loop(0, KHP // 2, stage_b, 0, unroll=11)


def _filt_fft(h2u, ss, bias, fa_full, fb):
    n, c = h2u.shape
    nblk = c // LANES
    return pl.pallas_call(
        _filt_fft_kernel,
        grid=(nblk,),
        in_specs=[pl.BlockSpec((n, LANES), lambda j: (0, j), pipeline_mode=pl.Buffered(1)),
                  pl.BlockSpec((1, LANES), lambda j: (0, j)),
                  pl.BlockSpec((1, LANES), lambda j: (0, j)),
                  _const_spec(fa_full.shape),
                  _const_spec(fb.shape)],
        out_specs=pl.BlockSpec((1, KHP, 2 * FFT_N2, LANES), lambda j: (j, 0, 0, 0)),
        out_shape=jax.ShapeDtypeStruct((nblk, KHP, 2 * FFT_N2, LANES), F32),
        scratch_shapes=[pltpu.VMEM((KP * ZPITCH, LANES), F32)],
        compiler_params=_params(("arbitrary",)),
        name="filt_fft",
    )(h2u, ss, bias, fa_full, fb)


def _conv_slab(u_ref, cw_ref, j, n_slabs):
    r0 = pl.multiple_of(j * FFT_N2, FFT_N2)
    cur = u_ref[0, pl.ds(r0, FFT_N2), :].astype(F32)
    grp = 2 * SUBLANES
    pr0 = pl.multiple_of(jnp.maximum(j * FFT_N2 - grp, 0), grp)
    nr0 = pl.multiple_of(jnp.minimum((j + 1) * FFT_N2, (n_slabs - 1) * FFT_N2), grp)
    prev_row = u_ref[0, pl.ds(pr0, grp), :].astype(F32)[grp - 1:grp]
    next_row = u_ref[0, pl.ds(nr0, grp), :].astype(F32)[0:1]
    prev_row = jnp.where(j > 0, prev_row, 0.0)
    next_row = jnp.where(j < n_slabs - 1, next_row, 0.0)
    row = lax.broadcasted_iota(jnp.int32, cur.shape, 0)
    before = jnp.where(row == 0, prev_row, pltpu.roll(cur, 1, 0))
    after = jnp.where(row == FFT_N2 - 1, next_row, pltpu.roll(cur, FFT_N2 - 1, 0))
    w = cw_ref[...]
    return before * w[0:1] + cur * w[1:2] + after * w[2:3] + w[3:4]


def _hyena_kernel(x0_ref, x1_ref, v_ref, cw0_ref, cw1_ref, cwv_ref, hf_ref,
                  fa_ref, fai_ref, fb_ref, fbi_ref, o_ref, ts_ref, zs_ref):
    n_slabs = x0_ref.shape[1] // FFT_N2

    def gated_value(j):
        return _conv_slab(x1_ref, cw1_ref, j, n_slabs) * _conv_slab(v_ref, cwv_ref, j, n_slabs)

    def fill(j, carry):
        ts_ref[pl.ds(pl.multiple_of(j * TPITCH, SUBLANES), FFT_N2), :] = gated_value(j)
        return carry

    lax.fori_loop(0, n_slabs, fill, 0, unroll=2)

    def stage_a(n2, carry):
        slab = ts_ref[pl.ds(n2, n_slabs, stride=TPITCH), :]
        z = _dot(fa_ref[n2], slab.astype(BF16))
        zs_ref[pl.ds(n2, KP, stride=ZPITCH), :] = z[:KP]
        zs_ref[pl.ds(FFT_N2 + n2, KP, stride=ZPITCH), :] = z[KP:]
        return carry

    lax.fori_loop(0, FFT_N2, stage_a, 0, unroll=16)

    def stage_b(p, carry):
        b0 = pl.multiple_of(2 * p * ZPITCH, SUBLANES)
        b1 = pl.multiple_of(b0 + ZPITCH, SUBLANES)
        z = jnp.concatenate([zs_ref[pl.ds(b0, 2 * FFT_N2), :], zs_ref[pl.ds(b1, 2 * FFT_N2), :]], axis=1)
        x = _dot(fb_ref[...], z.astype(BF16))
        h = jnp.concatenate([hf_ref[0, 2 * p], hf_ref[0, 2 * p + 1]], axis=1)
        xr, xi = x[:FFT_N2], x[FFT_N2:]
        hr, hi = h[:FFT_N2], h[FFT_N2:]
        prod = jnp.concatenate([xr * hr - xi * hi, xr * hi + xi * hr], axis=0)
        y = _dot(fbi_ref[...], prod.astype(BF16))
        zs_ref[pl.ds(b0, 2 * FFT_N2), :] = y[:, :LANES]
        zs_ref[pl.ds(b1, 2 * FFT_N2), :] = y[:, LANES:]
        return carry

    lax.fori_loop(0, KHP // 2, stage_b, 0, unroll=11)

    def stage_ai(n2, carry):
        yr = zs_ref[pl.ds(n2, KP, stride=ZPITCH), :]
        yi = zs_ref[pl.ds(FFT_N2 + n2, KP, stride=ZPITCH), :]
        y = jnp.concatenate([yr, yi], axis=0).astype(BF16)
        ts_ref[pl.ds(n2, n_slabs, stride=TPITCH), :] = _dot(fai_ref[n2], y)
        return carry

    lax.fori_loop(0, FFT_N2, stage_ai, 0, unroll=16)

    def finish(j, carry):
        conv = ts_ref[pl.ds(pl.multiple_of(j * TPITCH, SUBLANES), FFT_N2), :]
        y = _conv_slab(x0_ref, cw0_ref, j, n_slabs) * conv
        o_ref[0, pl.ds(pl.multiple_of(j * FFT_N2, FFT_N2), FFT_N2), :] = y.astype(BF16)
        return carry

    lax.fori_loop(0, n_slabs, finish, 0, unroll=2)


def _hyena(u, cw, hf, fa, fai, fb, fbi):
    b, seq, c3 = u.shape
    nblk = D_HYENA // LANES
    n_slabs = seq // FFT_N2
    stream = lambda k: pl.BlockSpec((1, seq, LANES), lambda j, i, k=k: (i, 0, k * nblk + j))
    cwspec = lambda k: pl.BlockSpec((SUBLANES, LANES), lambda j, i, k=k: (0, k * nblk + j))
    return pl.pallas_call(
        _hyena_kernel,
        grid=(nblk, b),
        in_specs=[stream(0), stream(1), stream(2), cwspec(0), cwspec(1), cwspec(2),
                  pl.BlockSpec((1, KHP, 2 * FFT_N2, LANES), lambda j, i: (j, 0, 0, 0),
                               pipeline_mode=pl.Buffered(1)),
                  _const_spec(fa.shape), _const_spec(fai.shape),
                  _const_spec(fb.shape), _const_spec(fbi.shape)],
        out_specs=pl.BlockSpec((1, seq, LANES), lambda j, i: (i, 0, j)),
        out_shape=jax.ShapeDtypeStruct((b, seq, D_HYENA), BF16),
        scratch_shapes=[pltpu.VMEM((n_slabs * TPITCH, LANES), F32),
                        pltpu.VMEM((KP * ZPITCH, LANES), F32)],
        compiler_params=_params(("arbitrary", "arbitrary")),
        name="hyena",
    )(u, u, u, cw, cw, cw, hf, fa, fai, fb, fbi)


def _attn_kernel(sink_ref, q_ref, kp_ref, km_ref, kn_ref, vp_ref, vm_ref, vn_ref, kc_ref, vc_ref,
                 o_ref, ka_ref, va_ref):
    i = pl.program_id(1)
    n_i = pl.num_programs(1)
    tq = q_ref.shape[0]
    nqb = tq // BLOCK
    ka_ref[0:BLOCK] = kp_ref[...]
    ka_ref[BLOCK:BLOCK + tq] = km_ref[...]
    ka_ref[BLOCK + tq:] = kn_ref[...]
    va_ref[0:BLOCK] = vp_ref[...]
    va_ref[BLOCK:BLOCK + tq] = vm_ref[...]
    va_ref[BLOCK + tq:] = vn_ref[...]

    qi = lax.broadcasted_iota(jnp.int32, (BLOCK, BLOCK), 0)
    kj = lax.broadcasted_iota(jnp.int32, (BLOCK, BLOCK), 1)
    lane = lax.broadcasted_iota(jnp.int32, (BLOCK, LANES), 1)
    low = lane < HEAD_DIM
    half = GQA_GROUP // 2
    rows2 = half * BLOCK
    hrow = lax.broadcasted_iota(jnp.int32, (GQA_GROUP * BLOCK, 1), 0) // BLOCK
    head_order = [hh for hh in range(GQA_GROUP) if hh % 2 == 0] + [hh for hh in range(GQA_GROUP) if hh % 2 == 1]
    one = jnp.ones((), BF16)

    def with_ones(v):
        lanes_low = lax.broadcasted_iota(jnp.int32, v.shape, 1) < HEAD_DIM
        return jnp.where(lanes_low, v, one), jnp.where(lanes_low, one, v)

    for g in range(N_KV_HEADS):
        gl = slice(g * LANES, (g + 1) * LANES)
        vc_even, vc_odd = with_ones(vc_ref[:, gl])
        sink = jnp.zeros((GQA_GROUP * BLOCK, 1), F32)
        for pos, hh in enumerate(head_order):
            sink = jnp.where(hrow == pos, sink_ref[g * GQA_GROUP + hh], sink)
        for j in range(nqb):
            prev_ok = kj >= qi
            next_ok = kj <= qi
            if j == 0:
                prev_ok = prev_ok & (i > 0)
            if j == nqb - 1:
                next_ok = next_ok & (i < n_i - 1)
            bias_p = jnp.concatenate([jnp.where(prev_ok, 0.0, NEG_INF).astype(F32)] * GQA_GROUP, axis=0)
            bias_n = jnp.concatenate([jnp.where(next_ok, 0.0, NEG_INF).astype(F32)] * GQA_GROUP, axis=0)
            qb = q_ref[j * BLOCK:(j + 1) * BLOCK, :]
            parts = []
            for hh in head_order:
                h = g * GQA_GROUP + hh
                qp = qb[:, (h // 2) * LANES:(h // 2 + 1) * LANES]
                parts.append(jnp.where(low if h % 2 == 0 else ~low, qp, jnp.zeros_like(qp)))
            qs = jnp.concatenate(parts, axis=0)
            kw = ka_ref[j * BLOCK:(j + 3) * BLOCK, gl]
            vw_even, vw_odd = with_ones(va_ref[j * BLOCK:(j + 3) * BLOCK, gl])
            s_w = _dot_nt(qs, kw)
            s_p = s_w[:, :BLOCK] + bias_p
            s_m = s_w[:, BLOCK:2 * BLOCK]
            s_n = s_w[:, 2 * BLOCK:] + bias_n
            s_c = _dot_nt(qs, kc_ref[:, gl])
            m = jnp.maximum(jnp.maximum(jnp.max(jnp.maximum(jnp.maximum(s_p, s_m), s_n), axis=1, keepdims=True),
                                        jnp.max(s_c, axis=1, keepdims=True)), sink)
            e_w = jnp.concatenate([jnp.exp(s_p - m), jnp.exp(s_m - m), jnp.exp(s_n - m)], axis=1).astype(BF16)
            e_c = jnp.exp(s_c - m).astype(BF16)
            e_sink = jnp.exp(sink - m)
            outs = []
            for par, (vw, vcx) in enumerate(((vw_even, vc_even), (vw_odd, vc_odd))):
                rs = slice(par * rows2, (par + 1) * rows2)
                acc = _dot(e_w[rs], vw) + _dot(e_c[rs], vcx)
                den = pltpu.roll(acc, HEAD_DIM, 1) + e_sink[rs]
                outs.append(acc / den)
            for pp in range(half):
                pair = jnp.where(low, outs[0][pp * BLOCK:(pp + 1) * BLOCK], outs[1][pp * BLOCK:(pp + 1) * BLOCK])
                col = (g * half + pp) * LANES
                o_ref[j * BLOCK:(j + 1) * BLOCK, col:col + LANES] = pair.astype(BF16)


def _attention(sinks, q, kd, vd, kc, vc, batch, seq, tq):
    t = q.shape[0]
    per_b = seq // tq
    nqb = tq // BLOCK
    nb = seq // BLOCK
    n_ctx = kc.shape[0] // batch
    main = lambda b, i: (b * per_b + i, 0)
    prev = lambda b, i: (b * nb + jnp.maximum(i * nqb - 1, 0), 0)
    nxt = lambda b, i: (b * nb + jnp.minimum(i * nqb + nqb, nb - 1), 0)
    kvw = kd.shape[1]
    return pl.pallas_call(
        _attn_kernel,
        grid=(batch, per_b),
        in_specs=[pl.BlockSpec(memory_space=pltpu.SMEM),
                  pl.BlockSpec((tq, D_ATTN), main),
                  pl.BlockSpec((BLOCK, kvw), prev), pl.BlockSpec((tq, kvw), main), pl.BlockSpec((BLOCK, kvw), nxt),
                  pl.BlockSpec((BLOCK, kvw), prev), pl.BlockSpec((tq, kvw), main), pl.BlockSpec((BLOCK, kvw), nxt),
                  pl.BlockSpec((n_ctx, kvw), lambda b, i: (b, 0)),
                  pl.BlockSpec((n_ctx, kvw), lambda b, i: (b, 0))],
        out_specs=pl.BlockSpec((tq, D_ATTN), main),
        out_shape=jax.ShapeDtypeStruct((t, D_ATTN), BF16),
        scratch_shapes=[pltpu.VMEM((tq + 2 * BLOCK, kvw), BF16),
                        pltpu.VMEM((tq + 2 * BLOCK, kvw), BF16)],
        compiler_params=_params(("arbitrary", "arbitrary")),
        name="attention",
    )(sinks, q, kd, kd, kd, vd, vd, vd, kc, vc)


def _route(t, wr_hi_ref, wr_hl_ref, br_ref):
    th, tl = _split(t)
    both = _dot(th, wr_hl_ref[...])
    logits = both[:, :LANES] + both[:, LANES:] + _dot(tl, wr_hi_ref[...]) + br_ref[...]
    lane_i = lax.broadcasted_iota(jnp.int32, logits.shape, 1)
    lane = lane_i.astype(F32)
    grp_of_lane = (lane_i >> 2).astype(F32)
    ninf = -jnp.inf
    far = float(LANES)
    is_g = (lane_i >= N_EXPERTS) & (lane_i < N_EXPERTS + N_GROUPS)
    glog = jnp.where(is_g, logits, ninf)
    gmax = jnp.max(glog, axis=1, keepdims=True)
    gidx = jnp.min(jnp.where(glog == gmax, lane - float(N_EXPERTS), far), axis=1, keepdims=True)
    group_p = 1.0 / jnp.sum(jnp.exp(glog - gmax), axis=1, keepdims=True)
    in_grp = (lane_i < N_EXPERTS) & (grp_of_lane == gidx)
    elog = jnp.where(in_grp, logits, ninf)
    v1 = jnp.max(elog, axis=1, keepdims=True)
    i1 = jnp.min(jnp.where(elog == v1, lane, far), axis=1, keepdims=True)
    elog2 = jnp.where(lane == i1, ninf, elog)
    v2 = jnp.max(elog2, axis=1, keepdims=True)
    i2 = jnp.min(jnp.where(elog2 == v2, lane, far), axis=1, keepdims=True)
    e = jnp.exp(v2 - v1)
    w1 = group_p / (1.0 + e)
    w2 = group_p * e / (1.0 + e)
    gate = jnp.where(lane == i1, w1, 0.0) + jnp.where(lane == i2, w2, 0.0)
    return gate + jnp.where(lane_i == GROUP_LANE, gidx, 0.0)


def _mix_out_kernel(x_ref, yh_ref, ya_ref, sh1_ref, sc1_ref, g1_ref, sh2_ref, sc2_ref,
                    wg_ref, wbh_ref, wba_ref, wo_ref, lng_ref, lnb_ref, wrh_ref, wrhl_ref, br_ref,
                    x1_ref, gate_ref, *t_refs):
    d = x_ref.shape[1]
    for r0 in range(0, x_ref.shape[0], SUB_ROWS):
        rows = slice(r0, r0 + SUB_ROWS)
        x = x_ref[rows, :]
        h = (_standardize(x) * (1.0 + sc1_ref[0]) + sh1_ref[0]).astype(BF16)
        g_hy = jax.nn.sigmoid(_dot(h, wg_ref[:, :d]))
        merged = g_hy * _dot(yh_ref[rows, :], wbh_ref[...])
        g_at = jax.nn.sigmoid(_dot(h, wg_ref[:, d:]))
        merged = merged + g_at * _dot(ya_ref[rows, :], wba_ref[...])
        mix = _dot(merged.astype(BF16), wo_ref[...])
        x1 = _standardize(DEEPNORM_ALPHA * x + g1_ref[0] * mix) * lng_ref[...] + lnb_ref[...]
        x1_ref[rows, :] = x1
        t = _standardize(x1) * (1.0 + sc2_ref[0]) + sh2_ref[0]
        gate_ref[rows, :] = _route(t, wrh_ref, wrhl_ref, br_ref)
        for p, t_ref in enumerate(t_refs):
            t_ref[rows, :] = t[:, p * SC_ROW_WORDS:(p + 1) * SC_ROW_WORDS]


def _mix_out(x2d, yh, ya, mods, w_g, w_bh, w_ba, w_o, ln_g, ln_b, wr_hi, wr_hl, br, seq, tm):
    t, d = x2d.shape
    per_b = seq // tm
    row = lambda i: (i, 0)
    mod = lambda i: (i // per_b, 0, 0)
    mspec = pl.BlockSpec((1, 1, d), mod)
    sh1, sc1, g1, sh2, sc2 = mods
    return pl.pallas_call(
        _mix_out_kernel,
        grid=(t // tm,),
        in_specs=[pl.BlockSpec((tm, d), row),
                  pl.BlockSpec((tm, yh.shape[1]), row),
                  pl.BlockSpec((tm, ya.shape[1]), row),
                  mspec, mspec, mspec, mspec, mspec,
                  _const_spec(w_g.shape), _const_spec(w_bh.shape), _const_spec(w_ba.shape),
                  _const_spec(w_o.shape), _const_spec(ln_g.shape), _const_spec(ln_b.shape),
                  _const_spec(wr_hi.shape), _const_spec(wr_hl.shape), _const_spec(br.shape)],
        out_specs=[pl.BlockSpec((tm, d), row), pl.BlockSpec((tm, LANES), row)]
        + [pl.BlockSpec((tm, SC_ROW_WORDS), row)] * (d // SC_ROW_WORDS),
        out_shape=[jax.ShapeDtypeStruct((t, d), F32), jax.ShapeDtypeStruct((t, LANES), F32)]
        + [jax.ShapeDtypeStruct((t, SC_ROW_WORDS), F32)] * (d // SC_ROW_WORDS),
        compiler_params=_params(("arbitrary",)),
        name="mix_out",
    )(x2d, yh, ya, sh1, sc1, g1, sh2, sc2, w_g, w_bh, w_ba, w_o, ln_g, ln_b, wr_hi, wr_hl, br)


def _slots_kernel(g_ref, upper_ref, lower_ref, dest_ref, meta_ref):
    gsel = g_ref[...]
    dest = jnp.zeros(gsel.shape, F32)
    base = jnp.zeros((1, 1), F32)
    chunk_start = lax.broadcasted_iota(jnp.int32, (1, LANES), 1).astype(F32) * float(MOE_CHUNK)
    owner = jnp.zeros((1, LANES), F32)
    for g in range(N_GROUPS):
        onehot = jnp.where(gsel == float(g), 1.0, 0.0)
        in_row = _dot(onehot.astype(BF16), upper_ref[...])
        row_tot = jnp.sum(onehot, axis=1, keepdims=True)
        rows_before = _dot(lower_ref[...], jnp.broadcast_to(row_tot, onehot.shape).astype(BF16))
        dest = dest + onehot * (base + rows_before + in_row)
        if g > 0:
            owner = owner + jnp.where(chunk_start >= base, 1.0, 0.0)
        n_g = jnp.sum(row_tot, axis=0, keepdims=True)
        base = base + jnp.floor((n_g + float(MOE_CHUNK - 1)) * (1.0 / MOE_CHUNK)) * float(MOE_CHUNK)
    dest_ref[...] = dest.astype(jnp.int32)
    row = lax.broadcasted_iota(jnp.int32, meta_ref.shape, 0)
    meta_ref[...] = jnp.where(row == 0, owner, base * (1.0 / MOE_CHUNK)).astype(jnp.int32)


def _slots(gsel):
    r = gsel.shape[0]
    upper = jnp.asarray(np.triu(np.ones((LANES, LANES), np.float32), 1)).astype(BF16)
    lower = jnp.asarray(np.tril(np.ones((r, r), np.float32), -1)).astype(BF16)
    full = lambda a: pl.BlockSpec(a.shape, lambda i: (0,) * a.ndim)
    return pl.pallas_call(
        _slots_kernel,
        grid=(1,),
        in_specs=[full(gsel), full(upper), full(lower)],
        out_specs=[pl.BlockSpec((r, LANES), lambda i: (0, 0)), pl.BlockSpec((SUBLANES, LANES), lambda i: (0, 0))],
        out_shape=[jax.ShapeDtypeStruct((r, LANES), jnp.int32), jax.ShapeDtypeStruct((SUBLANES, LANES), jnp.int32)],
        compiler_params=_params(("arbitrary",)),
        name="moe_slots",
    )(gsel, upper, lower)


def _experts_kernel(owner_ref, used_ref, *refs):
    n_p = D_MODEL // SC_ROW_WORDS
    t_refs, gate_ref = refs[:n_p], refs[n_p]
    wg_ref, wu_ref, wd_ref = refs[n_p + 1:n_p + 4]
    y_refs = refs[n_p + 4:]
    c = pl.program_id(0)

    @pl.when(c < used_ref[0])
    def _():
        t = jnp.concatenate([r[...] for r in t_refs], axis=1).astype(BF16)
        gate = gate_ref[...]
        lane = lax.broadcasted_iota(jnp.int32, gate.shape, 1)
        first = owner_ref[c] * EXPERTS_PER_GROUP
        a = _dot(t, wg_ref[0])
        u = _dot(t, wu_ref[0])
        parts = []
        for e in range(EXPERTS_PER_GROUP):
            cols = slice(e * D_EXPERT, (e + 1) * D_EXPERT)
            ge = jnp.sum(jnp.where(lane == first + e, gate, 0.0), axis=1, keepdims=True)
            parts.append((a[:, cols] * jax.nn.sigmoid(a[:, cols]) * u[:, cols] * ge).astype(BF16))
        y = _dot(jnp.concatenate(parts, axis=1), wd_ref[0])
        for p, y_ref in enumerate(y_refs):
            y_ref[...] = y[:, p * SC_ROW_WORDS:(p + 1) * SC_ROW_WORDS]

    @pl.when(c >= used_ref[0])
    def _():
        for y_ref in y_refs:
            y_ref[...] = jnp.zeros_like(y_ref)


def _experts(owner, used, t_pieces, gate_sorted, wg3, wu3, wd3):
    n_slots = gate_sorted.shape[0]
    row = lambda c, owner, used: (c, 0)
    by_owner = lambda c, owner, used: (owner[c], 0, 0)
    piece = pl.BlockSpec((MOE_CHUNK, SC_ROW_WORDS), row)
    wspec = lambda w: pl.BlockSpec((1,) + w.shape[1:], by_owner)
    return pl.pallas_call(
        _experts_kernel,
        grid_spec=pltpu.PrefetchScalarGridSpec(
            num_scalar_prefetch=2,
            grid=(n_slots // MOE_CHUNK,),
            in_specs=[piece] * len(t_pieces) + [pl.BlockSpec((MOE_CHUNK, LANES), row),
                                                wspec(wg3), wspec(wu3), wspec(wd3)],
            out_specs=[piece] * len(t_pieces),
        ),
        out_shape=[jax.ShapeDtypeStruct((n_slots, SC_ROW_WORDS), F32)] * len(t_pieces),
        compiler_params=_params(("arbitrary",)),
        name="moe_experts",
    )(owner, used, *t_pieces, gate_sorted, wg3, wu3, wd3)


def _final_kernel(x1_ref, g2_ref, lng_ref, lnb_ref, *refs):
    y_refs, o_ref = refs[:-1], refs[-1]
    y = jnp.concatenate([r[...] for r in y_refs], axis=1)
    o_ref[...] = _standardize(DEEPNORM_ALPHA * x1_ref[...] + g2_ref[0] * y) * lng_ref[...] + lnb_ref[...]


def _final(x1, g2, ln_g, ln_b, y_pieces, seq, tm):
    t, d = x1.shape
    per_b = seq // tm
    row = lambda i: (i, 0)
    return pl.pallas_call(
        _final_kernel,
        grid=(t // tm,),
        in_specs=[pl.BlockSpec((tm, d), row), pl.BlockSpec((1, 1, d), lambda i: (i // per_b, 0, 0)),
                  _const_spec(ln_g.shape), _const_spec(ln_b.shape)]
        + [pl.BlockSpec((tm, SC_ROW_WORDS), row)] * len(y_pieces),
        out_specs=pl.BlockSpec((tm, d), row),
        out_shape=jax.ShapeDtypeStruct((t, d), F32),
        compiler_params=_params(("arbitrary",)),
        name="moe_final",
    )(x1, g2, ln_g, ln_b, *y_pieces)


def _sc_move_rows(tables, idx, n_out, scatter, name):
    n = idx.shape[0]
    mesh = plsc.VectorSubcoreMesh(core_axis_name="c", subcore_axis_name="s")
    out_type = [jax.ShapeDtypeStruct((n_out, t.shape[1]), t.dtype) for t in tables]
    window = lambda i: (i, 0)
    index_win = pl.BlockSpec((1, SC_WINDOW), lambda i: (0, i))
    split = dict(core_axis_name=("c", "s"), dimension_semantics=(pltpu.PARALLEL,))

    @functools.partial(pl.kernel, out_type=out_type, mesh=mesh, scratch_types=[], name=name)
    def move(*refs):
        srcs, i_hbm, dsts = refs[:len(tables)], refs[len(tables)], refs[len(tables) + 1:]
        for src, dst in zip(srcs, dsts):
            rows_win = pl.BlockSpec((SC_WINDOW, src.shape[1]), window)
            if scatter:
                def body(x_vmem, i_vmem, dst=dst):
                    pltpu.sync_copy(x_vmem, dst.at[i_vmem.at[0]])

                pltpu.emit_pipeline(body, grid=(n // SC_WINDOW,), in_specs=[rows_win, index_win],
                                    out_specs=[], **split)(src, i_hbm)
            else:
                def body(i_vmem, o_vmem, src=src):
                    pltpu.sync_copy(src.at[i_vmem.at[0]], o_vmem)

                pltpu.emit_pipeline(body, grid=(n // SC_WINDOW,), in_specs=[index_win],
                                    out_specs=[rows_win], **split)(i_hbm, dst)

    return move(*tables, idx.reshape(1, n))


def _dft_tables(seq):
    n = 2 * seq
    n1_full = n // FFT_N2
    n1_data = seq // FFT_N2
    k1 = np.arange(KH, dtype=np.float64)[None, :, None]
    n2 = np.arange(FFT_N2, dtype=np.float64)[:, None, None]
    n1 = np.arange(n1_full, dtype=np.float64)[None, None, :]
    ang = 2.0 * np.pi * k1 * (FFT_N2 * n1 + n2) / n
    fa = np.zeros((FFT_N2, 2 * KP, n1_full))
    fa[:, :KH] = np.cos(ang)
    fa[:, KP:KP + KH] = -np.sin(ang)
    wgt = np.full((KH,), 2.0)
    wgt[0] = 1.0
    wgt[KH - 1] = 1.0
    fai = np.zeros((FFT_N2, n1_data, 2 * KP))
    angt = np.transpose(ang[:, :, :n1_data], (0, 2, 1))
    fai[:, :, :KH] = np.cos(angt) * wgt / n
    fai[:, :, KP:KP + KH] = -np.sin(angt) * wgt / n
    kk = np.arange(FFT_N2, dtype=np.float64)
    a2 = 2.0 * np.pi * np.outer(kk, kk) / FFT_N2
    fr, fi = np.cos(a2), -np.sin(a2)
    fb = np.block([[fr, -fi], [fi, fr]])
    fbi = np.block([[fr, fi], [-fi, fr]])
    f32 = lambda a: jnp.asarray(a.astype(np.float32))
    return f32(fa[:, :, :n1_data]), f32(fa), f32(fai), f32(fb), f32(fbi)


def _filter_features(seq, rows):
    t = jnp.linspace(0.0, 1.0, seq, dtype=F32)
    w = 2.0 * math.pi * jnp.arange(seq, dtype=F32) / seq
    t2 = jnp.concatenate([t, t[::-1]])
    w2 = jnp.concatenate([w, w[::-1]])
    bands = jnp.linspace(1e-4, FILTER_BANDS - 1, FILTER_BANDS, dtype=F32)
    max_decay = math.log(DECAY_TARGET) / FAST_DECAY_PCT
    min_decay = math.log(DECAY_TARGET) / SLOW_DECAY_PCT
    deltas = jnp.linspace(min_decay, max_decay, D_HYENA, dtype=F32)

    def feats(k):
        tp = t2.reshape(-1, 2, rows // 2)[:, k].reshape(-1, 1)
        wp = w2.reshape(-1, 2, rows // 2)[:, k].reshape(-1, 1)
        pad = jnp.zeros((tp.shape[0], FILTER_ORDER - FILTER_EMB), F32)
        return jnp.concatenate([tp, jnp.cos(bands * wp), -jnp.sin(bands * wp), pad], axis=-1)

    zp = jnp.concatenate([feats(0), feats(1)], axis=-1)
    decay = jnp.exp(-t2[:, None] * jnp.abs(deltas))
    return zp, decay


def _rope_tables(seq):
    rows = seq // GRID_W
    row = jnp.repeat(jnp.arange(rows, dtype=F32), GRID_W)
    col = jnp.tile(jnp.arange(GRID_W, dtype=F32), rows)
    half = HEAD_DIM // 2
    inv_freq = ROPE_BASE ** (-jnp.arange(0, half, 2, dtype=F32) / half)
    ang = jnp.concatenate([row[:, None] * inv_freq, col[:, None] * inv_freq], axis=-1)
    cos, sin = jnp.cos(ang), jnp.sin(ang)
    c64 = jnp.concatenate([cos, cos], axis=-1)
    s64 = jnp.concatenate([-sin, sin], axis=-1)
    return jnp.concatenate([c64, c64], axis=-1), jnp.concatenate([s64, s64], axis=-1)


def _head_perm(n_heads):
    idx = []
    for h in range(n_heads):
        base = h * HEAD_DIM
        idx += [base + 2 * j for j in range(HEAD_DIM // 2)]
        idx += [base + 2 * j + 1 for j in range(HEAD_DIM // 2)]
    return np.asarray(idx, dtype=np.int32)


def _dup_heads(w):
    parts = []
    for g in range(N_KV_HEADS):
        blk = w[:, g * HEAD_DIM:(g + 1) * HEAD_DIM]
        parts += [blk, blk]
    return jnp.concatenate(parts, axis=1)


def kernel(x, c, ctx, c_ctx, ada_w, ada_b, w_in, hy_conv_w, hy_conv_b, hy_w1, hy_b1, hy_w2, hy_b2, hy_w3, hy_b3, hy_w4, hy_freq, hy_bias, attn_sinks, w_branch_hy, w_branch_attn, w_out, ln1_g, ln1_b, w_group, b_group, w_router, b_router, w_gate_e, w_up_e, w_down_e, ln2_g, ln2_b):
    batch, seq, d = x.shape
    n_ctx = ctx.shape[1]
    assert d == D_MODEL and ada_w.shape[0] == DEPTH == 1
    assert 2 * seq == FFT_N2 * FFT_N2 and seq % TM_PROJ == 0
    l = 0

    w = w_in[l]
    s0, s1, s2, s3 = 3 * D_HYENA, 3 * D_HYENA + D_ATTN, 3 * D_HYENA + D_ATTN + D_KV, 3 * D_HYENA + D_ATTN + 2 * D_KV
    w_q = w[:, s0:s1][:, _head_perm(N_HEADS)]
    w_k = _dup_heads(w[:, s1:s2][:, _head_perm(N_KV_HEADS)])
    w_v = _dup_heads(w[:, s2:s3])
    w_c = jnp.concatenate([w[:, :s0], w_q, w_k, w_v], axis=1).astype(BF16)
    w_kv = jnp.concatenate([w_k, w_v], axis=1).astype(BF16)
    w_g = w[:, s3:].astype(BF16)
    w_bh = w_branch_hy[l].astype(BF16)
    w_ba = w_branch_attn[l].astype(BF16)
    w_o = w_out[l].astype(BF16)
    wr = jnp.zeros((d, LANES), F32)
    wr = wr.at[:, :N_EXPERTS].set(w_router[l]).at[:, N_EXPERTS:N_EXPERTS + N_GROUPS].set(w_group[l])
    wr_hi = wr.astype(BF16)
    wr_hl = jnp.concatenate([wr_hi, (wr - wr_hi.astype(F32)).astype(BF16)], axis=1)
    br = jnp.zeros((1, LANES), F32)
    br = br.at[0, :N_EXPERTS].set(b_router[l]).at[0, N_EXPERTS:N_EXPERTS + N_GROUPS].set(b_group[l])
    gw = EXPERTS_PER_GROUP * D_EXPERT
    per_group = lambda w: jnp.transpose(w.astype(BF16).reshape(N_GROUPS, EXPERTS_PER_GROUP, d, D_EXPERT),
                                        (0, 2, 1, 3)).reshape(N_GROUPS, d, gw)
    wg3 = per_group(w_gate_e[l])
    wu3 = per_group(w_up_e[l])
    wd3 = w_down_e[l].astype(BF16).reshape(N_GROUPS, gw, d)
    cw = jnp.concatenate([hy_conv_w[l][:, 0, :], hy_conv_b[l][None, :],
                          jnp.zeros((SUBLANES - SHORT_CONV - 1, 3 * D_HYENA), F32)], axis=0)
    row2 = lambda a: a.reshape(1, -1)
    pair = lambda a: jnp.concatenate([a, a], axis=-1)
    zero_o = jnp.zeros((FILTER_ORDER, FILTER_ORDER), F32)
    bdiag = lambda a: jnp.concatenate([jnp.concatenate([a, zero_o], axis=1),
                                       jnp.concatenate([zero_o, a], axis=1)], axis=0)
    w1p = bdiag(jnp.concatenate([hy_w1[l], jnp.zeros((FILTER_ORDER - FILTER_EMB, FILTER_ORDER), F32)], axis=0))
    w4h = jnp.transpose(hy_w4[l].reshape(FILTER_ORDER, 2, D_HYENA), (1, 0, 2))
    zero_w4 = jnp.zeros_like(w4h)
    w4s = jnp.stack([jnp.concatenate([w4h, zero_w4], axis=1),
                     jnp.concatenate([zero_w4, w4h], axis=1)], axis=1)

    fa, fa_full, fai, fb, fbi = _dft_tables(seq)
    fa, fa_full, fai, fb, fbi = (a.astype(BF16) for a in (fa, fa_full, fai, fb, fbi))
    zp, dec2 = _filter_features(seq, FILT_ROWS)
    cos_t, sin_t = _rope_tables(seq)

    cond = jnp.concatenate([c, c_ctx[None], jnp.zeros((SUBLANES - batch - 1, d), F32)], axis=0)
    mods = _adaln(cond, ada_w[l], ada_b[l])
    m6 = [mods[:, k * d:(k + 1) * d].reshape(SUBLANES, 1, d) for k in range(6)]
    sh1, sc1, g1, sh2, sc2, g2 = m6

    kc, vc = _ctx_kv(ctx.reshape(batch * n_ctx, d), sh1[batch], sc1[batch], w_kv, n_ctx)

    x2d = x.reshape(batch * seq, d)
    u, q, kd, vd = _in_proj(x2d, sh1, sc1, w_c, cos_t, sin_t, seq, TM_PROJ)
    h2u, ss = _filt_mlp(zp, w1p, pair(row2(hy_b1[l])), bdiag(hy_w2[l]), pair(row2(hy_b2[l])),
                        bdiag(hy_w3[l]), pair(row2(hy_b3[l])), w4s, pair(row2(hy_freq[l])), dec2, FILT_ROWS)
    hf = _filt_fft(h2u, ss, row2(hy_bias[l]), fa_full, fb)
    y_hy = _hyena(u.reshape(batch, seq, 3 * D_HYENA), cw, hf, fa, fai, fb, fbi)
    y_at = _attention(attn_sinks[l], q, kd, vd, kc, vc, batch, seq, TQ_ATTN)
    x1, gate, *t_pieces = _mix_out(x2d, y_hy.reshape(batch * seq, D_HYENA), y_at, (sh1, sc1, g1, sh2, sc2),
                                   w_g, w_bh, w_ba, w_o, row2(ln1_g[l]), row2(ln1_b[l]), wr_hi, wr_hl, br,
                                   seq, TM_PROJ)

    n_tok = batch * seq
    n_slots = n_tok + N_GROUPS * MOE_CHUNK
    dest2d, meta = _slots(gate[:, GROUP_LANE].reshape(n_tok // LANES, LANES))
    dest = dest2d.reshape(n_tok)
    *t_sorted, gate_sorted = _sc_move_rows(t_pieces + [gate], dest, n_slots, True, "moe_sort")
    y_sorted = _experts(meta[0], meta[1, :1], t_sorted, gate_sorted, wg3, wu3, wd3)
    y_pieces = _sc_move_rows(y_sorted, dest, n_tok, False, "moe_unsort")
    out = _final(x1, g2, row2(ln2_g[l]), row2(ln2_b[l]), y_pieces, seq, TM_FINAL)
    return out.reshape(batch, seq, d)
```

```python
import functools
import math

import numpy as np
import jax
import jax.numpy as jnp
from jax import lax
from jax.experimental import pallas as pl
from jax.experimental.pallas import tpu as pltpu
from jax.experimental.pallas import tpu_sc as plsc

F32 = jnp.float32
BF16 = jnp.bfloat16

D_MODEL = 1024
GRID_W = 64
D_HYENA = D_MODEL // 2
SHORT_CONV = 3
FILTER_BANDS = 16
FILTER_EMB = 1 + 2 * FILTER_BANDS
FILTER_ORDER = 64
DECAY_TARGET = 1e-2
FAST_DECAY_PCT = 0.3
SLOW_DECAY_PCT = 1.5
HEAD_DIM = 64
D_ATTN = D_MODEL // 2
N_HEADS = D_ATTN // HEAD_DIM
N_KV_HEADS = N_HEADS // 4
GQA_GROUP = N_HEADS // N_KV_HEADS
D_KV = N_KV_HEADS * HEAD_DIM
WINDOW = 128
BLOCK = 128
ROPE_BASE = 10000.0
NEG_INF = -1e30
N_GROUPS = 4
EXPERTS_PER_GROUP = 4
N_EXPERTS = N_GROUPS * EXPERTS_PER_GROUP
D_EXPERT = D_MODEL // 4
LN_EPS = 1e-5
DEPTH = 1
DEEPNORM_ALPHA = (2.0 * DEPTH) ** 0.25

LANES = 128
SUBLANES = 8
VMEM_LIMIT = 56 * 1024 * 1024

SUB_ROWS = 512
TM_PROJ = 1024
MOE_CHUNK = 512
TM_FINAL = 1024
GROUP_LANE = N_EXPERTS
SC_WINDOW = 128
SC_ROW_WORDS = 256
N_PIECES = D_MODEL // (2 * SC_ROW_WORDS)
TQ_ATTN = 512
FILT_ROWS = 2048

FFT_N2 = 128
KH = 65
KHP = 66
KP = 72
ZPITCH = 2 * FFT_N2 + SUBLANES
TPITCH = FFT_N2 + SUBLANES


def _dot(a, b):
    return jnp.dot(a, b, preferred_element_type=F32)


def _dot_nt(a, b):
    return lax.dot_general(a, b, (((1,), (1,)), ((), ())), preferred_element_type=F32)


def _split(a):
    hi = a.astype(BF16)
    lo = (a - hi.astype(F32)).astype(BF16)
    return hi, lo


def _pack_pieces(x):
    w = SC_ROW_WORDS
    pieces = []
    for p in range(x.shape[1] // (2 * w)):
        hi = lax.bitcast_convert_type(x[:, 2 * p * w:(2 * p + 1) * w].astype(BF16).astype(F32), jnp.uint32)
        lo = lax.bitcast_convert_type(x[:, (2 * p + 1) * w:(2 * p + 2) * w].astype(BF16).astype(F32), jnp.uint32)
        pieces.append(hi | (lo >> 16))
    return pieces


def _unpack_pieces(pieces):
    cols = []
    for word in pieces:
        cols.append(lax.bitcast_convert_type(word & jnp.uint32(0xFFFF0000), F32))
        cols.append(lax.bitcast_convert_type(word << 16, F32))
    return jnp.concatenate(cols, axis=1)


def _dot3(a, b):
    ah, al = _split(a)
    bh, bl = _split(b)
    return _dot(ah, bh) + _dot(al, bh) + _dot(ah, bl)


def _standardize(x):
    mu = jnp.mean(x, axis=-1, keepdims=True)
    xc = x - mu
    var = jnp.mean(xc * xc, axis=-1, keepdims=True)
    return xc * lax.rsqrt(var + LN_EPS)


def _params(sem, vmem=VMEM_LIMIT):
    return pltpu.CompilerParams(dimension_semantics=sem, vmem_limit_bytes=vmem)


def _const_spec(shape):
    nd = len(shape)
    return pl.BlockSpec(shape, lambda *_: (0,) * nd, pipeline_mode=pl.Buffered(1))


def _adaln_kernel(c_ref, w_ref, b_ref, o_ref):
    s = c_ref[...]
    s = s * jax.nn.sigmoid(s)
    o_ref[...] = _dot3(s, w_ref[...]) + b_ref[...]


def _adaln(cond, w, b):
    n, d = cond.shape
    cols = w.shape[1]
    bc = 1024
    return pl.pallas_call(
        _adaln_kernel,
        grid=(cols // bc,),
        in_specs=[pl.BlockSpec((n, d), lambda j: (0, 0)),
                  pl.BlockSpec((d, bc), lambda j: (0, j)),
                  pl.BlockSpec((1, bc), lambda j: (0, j))],
        out_specs=pl.BlockSpec((n, bc), lambda j: (0, j)),
        out_shape=jax.ShapeDtypeStruct((n, cols), F32),
        compiler_params=_params(("arbitrary",)),
        name="adaln",
    )(cond, w, b.reshape(1, cols))


def _ctx_kv_kernel(x_ref, sh_ref, sc_ref, w_ref, k_ref, v_ref):
    h = _standardize(x_ref[...]) * (1.0 + sc_ref[...]) + sh_ref[...]
    kv = _dot(h.astype(BF16), w_ref[...])
    half = k_ref.shape[1]
    k_ref[...] = kv[:, :half].astype(BF16)
    v_ref[...] = kv[:, half:].astype(BF16)


def _ctx_kv(ctx2d, sh, sc, w_kv, rows):
    n, d = ctx2d.shape
    half = w_kv.shape[1] // 2
    return pl.pallas_call(
        _ctx_kv_kernel,
        grid=(n // rows,),
        in_specs=[pl.BlockSpec((rows, d), lambda i: (i, 0)),
                  pl.BlockSpec((1, d), lambda i: (0, 0)),
                  pl.BlockSpec((1, d), lambda i: (0, 0)),
                  pl.BlockSpec(w_kv.shape, lambda i: (0, 0))],
        out_specs=[pl.BlockSpec((rows, half), lambda i: (i, 0)),
                   pl.BlockSpec((rows, half), lambda i: (i, 0))],
        out_shape=[jax.ShapeDtypeStruct((n, half), BF16)] * 2,
        compiler_params=_params(("arbitrary",)),
        name="ctx_kv",
    )(ctx2d, sh, sc, w_kv)


def _rope(x, cos_t, sin_t):
    width = x.shape[1]
    reps = width // LANES
    c = jnp.concatenate([cos_t] * reps, axis=1)
    s = jnp.concatenate([sin_t] * reps, axis=1)
    half = HEAD_DIM // 2
    lane = lax.broadcasted_iota(jnp.int32, x.shape, 1)
    first_half = (lane & (HEAD_DIM - 1)) < half
    partner = jnp.where(first_half, pltpu.roll(x, width - half, 1), pltpu.roll(x, half, 1))
    return x * c + partner * s


def _in_proj_kernel(x_ref, sh_ref, sc_ref, w_ref, cos_ref, sin_ref, u_ref, q_ref, k_ref, v_ref):
    n_u = u_ref.shape[1]
    n_q = q_ref.shape[1]
    n_k = k_ref.shape[1]
    for r0 in range(0, x_ref.shape[0], SUB_ROWS):
        rows = slice(r0, r0 + SUB_ROWS)
        h = (_standardize(x_ref[rows, :]) * (1.0 + sc_ref[0]) + sh_ref[0]).astype(BF16)
        u_ref[rows, :] = _dot(h, w_ref[:, :n_u]).astype(BF16)
        cos_t = cos_ref[rows, :]
        sin_t = sin_ref[rows, :]
        q = _dot(h, w_ref[:, n_u:n_u + n_q])
        q_ref[rows, :] = (_rope(q, cos_t, sin_t) * (HEAD_DIM ** -0.5)).astype(BF16)
        k = _dot(h, w_ref[:, n_u + n_q:n_u + n_q + n_k])
        k_ref[rows, :] = _rope(k, cos_t, sin_t).astype(BF16)
        v_ref[rows, :] = _dot(h, w_ref[:, n_u + n_q + n_k:]).astype(BF16)


def _in_proj(x2d, sh, sc, w_c, cos_t, sin_t, seq, tm):
    t, d = x2d.shape
    per_b = seq // tm
    n_u, n_q, n_k = 3 * D_HYENA, D_ATTN, 2 * D_KV
    row = lambda i: (i, 0)
    mod = lambda i: (i // per_b, 0, 0)
    pos = lambda i: (i % per_b, 0)
    return pl.pallas_call(
        _in_proj_kernel,
        grid=(t // tm,),
        in_specs=[pl.BlockSpec((tm, d), row),
                  pl.BlockSpec((1, 1, d), mod),
                  pl.BlockSpec((1, 1, d), mod),
                  _const_spec(w_c.shape),
                  pl.BlockSpec((tm, LANES), pos),
                  pl.BlockSpec((tm, LANES), pos)],
        out_specs=[pl.BlockSpec((tm, n_u), row),
                   pl.BlockSpec((tm, n_q), row),
                   pl.BlockSpec((tm, n_k), row),
                   pl.BlockSpec((tm, n_k), row)],
        out_shape=[jax.ShapeDtypeStruct((t, n_u), BF16),
                   jax.ShapeDtypeStruct((t, n_q), BF16),
                   jax.ShapeDtypeStruct((t, n_k), BF16),
                   jax.ShapeDtypeStruct((t, n_k), BF16)],
        compiler_params=_params(("arbitrary",)),
        name="in_proj",
    )(x2d, sh, sc, w_c, cos_t, sin_t)


def _filt_mlp_kernel(z_ref, w1_ref, b1_ref, w2_ref, b2_ref, w3_ref, b3_ref, w4_ref, fr_ref, dec_ref,
                     h_ref, ss_ref):
    fr = fr_ref[...]
    a = jnp.sin(fr * (_dot3(z_ref[...], w1_ref[...]) + b1_ref[...]))
    a = jnp.sin(fr * (_dot3(a, w2_ref[...]) + b2_ref[...]))
    a = jnp.sin(fr * (_dot3(a, w3_ref[...]) + b3_ref[...]))
    half = a.shape[0]
    ss = jnp.zeros(ss_ref.shape, F32)
    for k in range(2):
        h = _dot3(a, w4_ref[0, k]) * dec_ref[k * half:(k + 1) * half, :]
        h_ref[k * half:(k + 1) * half, :] = h
        ss = ss + jnp.sum(h * h, axis=0, keepdims=True)

    @pl.when(pl.program_id(0) == 0)
    def _():
        ss_ref[...] = jnp.zeros_like(ss_ref)

    ss_ref[...] += ss


def _filt_mlp(zp, w1p, b1, w2, b2, w3, b3, w4s, fr, dec2, rows):
    n, c = dec2.shape
    half_steps = (n // 2) // rows
    vec = lambda a: pl.BlockSpec(a.shape, lambda i: (0,) * a.ndim)
    return pl.pallas_call(
        _filt_mlp_kernel,
        grid=(n // rows,),
        in_specs=[pl.BlockSpec((rows // 2, zp.shape[1]), lambda i: (i, 0)),
                  vec(w1p), vec(b1), vec(w2), vec(b2), vec(w3), vec(b3),
                  pl.BlockSpec((1,) + w4s.shape[1:], lambda i: (i // half_steps, 0, 0, 0)),
                  vec(fr),
                  pl.BlockSpec((rows, c), lambda i: (i, 0))],
        out_specs=[pl.BlockSpec((rows, c), lambda i: (i, 0)),
                   pl.BlockSpec((1, c), lambda i: (0, 0))],
        out_shape=[jax.ShapeDtypeStruct((n, c), F32), jax.ShapeDtypeStruct((1, c), F32)],
        compiler_params=_params(("arbitrary",)),
        name="filt_mlp",
    )(zp, w1p, b1, w2, b2, w3, b3, w4s, fr, dec2)


def _filt_fft_kernel(h_ref, ss_ref, bias_ref, fa_ref, fb_ref, o_ref, zs_ref):
    scale = lax.rsqrt(ss_ref[...] + 1e-6)
    n1 = h_ref.shape[0] // FFT_N2
    row = lax.broadcasted_iota(jnp.int32, (2 * FFT_N2, LANES), 0)
    impulse = jnp.where(row < FFT_N2, bias_ref[...], 0.0)

    def stage_a(n2, carry):
        slab = h_ref[pl.ds(n2, n1, stride=FFT_N2), :]
        z = _dot(fa_ref[n2], slab.astype(BF16))
        zs_ref[pl.ds(n2, KP, stride=ZPITCH), :] = z[:KP]
        zs_ref[pl.ds(FFT_N2 + n2, KP, stride=ZPITCH), :] = z[KP:]
        return carry

    lax.fori_loop(0, FFT_N2, stage_a, 0, unroll=8)

    def stage_b(p, carry):
        b0 = pl.multiple_of(2 * p * ZPITCH, SUBLANES)
        b1 = pl.multiple_of(b0 + ZPITCH, SUBLANES)
        z = jnp.concatenate([zs_ref[pl.ds(b0, 2 * FFT_N2), :], zs_ref[pl.ds(b1, 2 * FFT_N2), :]], axis=1)
        x = _dot(fb_ref[...], z.astype(BF16))
        o_ref[0, 2 * p] = x[:, :LANES] * scale + impulse
        o_ref[0, 2 * p + 1] = x[:, LANES:] * scale + impulse
        return carry

    lax.fori_loop(0, KHP // 2, stage_b, 0, unroll=11)


def _filt_fft(h2u, ss, bias, fa_full, fb):
    n, c = h2u.shape
    nblk = c // LANES
    return pl.pallas_call(
        _filt_fft_kernel,
        grid=(nblk,),
        in_specs=[pl.BlockSpec((n, LANES), lambda j: (0, j), pipeline_mode=pl.Buffered(1)),
                  pl.BlockSpec((1, LANES), lambda j: (0, j)),
                  pl.BlockSpec((1, LANES), lambda j: (0, j)),
                  _const_spec(fa_full.shape),
                  _const_spec(fb.shape)],
        out_specs=pl.BlockSpec((1, KHP, 2 * FFT_N2, LANES), lambda j: (j, 0, 0, 0)),
        out_shape=jax.ShapeDtypeStruct((nblk, KHP, 2 * FFT_N2, LANES), F32),
        scratch_shapes=[pltpu.VMEM((KP * ZPITCH, LANES), F32)],
        compiler_params=_params(("arbitrary",)),
        name="filt_fft",
    )(h2u, ss, bias, fa_full, fb)


def _conv_slab(u_ref, cw_ref, j, n_slabs):
    r0 = pl.multiple_of(j * FFT_N2, FFT_N2)
    cur = u_ref[0, pl.ds(r0, FFT_N2), :].astype(F32)
    grp = 2 * SUBLANES
    pr0 = pl.multiple_of(jnp.maximum(j * FFT_N2 - grp, 0), grp)
    nr0 = pl.multiple_of(jnp.minimum((j + 1) * FFT_N2, (n_slabs - 1) * FFT_N2), grp)
    prev_row = u_ref[0, pl.ds(pr0, grp), :].astype(F32)[grp - 1:grp]
    next_row = u_ref[0, pl.ds(nr0, grp), :].astype(F32)[0:1]
    prev_row = jnp.where(j > 0, prev_row, 0.0)
    next_row = jnp.where(j < n_slabs - 1, next_row, 0.0)
    row = lax.broadcasted_iota(jnp.int32, cur.shape, 0)
    before = jnp.where(row == 0, prev_row, pltpu.roll(cur, 1, 0))
    after = jnp.where(row == FFT_N2 - 1, next_row, pltpu.roll(cur, FFT_N2 - 1, 0))
    w = cw_ref[...]
    return before * w[0:1] + cur * w[1:2] + after * w[2:3] + w[3:4]


def _hyena_kernel(x0_ref, x1_ref, v_ref, cw0_ref, cw1_ref, cwv_ref, hf_ref,
                  fa_ref, fai_ref, fb_ref, fbi_ref, o_ref, ts_ref, zs_ref):
    n_slabs = x0_ref.shape[1] // FFT_N2

    def gated_value(j):
        return _conv_slab(x1_ref, cw1_ref, j, n_slabs) * _conv_slab(v_ref, cwv_ref, j, n_slabs)

    def fill(j, carry):
        ts_ref[pl.ds(pl.multiple_of(j * TPITCH, SUBLANES), FFT_N2), :] = gated_value(j)
        return carry

    lax.fori_loop(0, n_slabs, fill, 0, unroll=2)

    def stage_a(n2, carry):
        slab = ts_ref[pl.ds(n2, n_slabs, stride=TPITCH), :]
        z = _dot(fa_ref[n2], slab.astype(BF16))
        zs_ref[pl.ds(n2, KP, stride=ZPITCH), :] = z[:KP]
        zs_ref[pl.ds(FFT_N2 + n2, KP, stride=ZPITCH), :] = z[KP:]
        return carry

    lax.fori_loop(0, FFT_N2, stage_a, 0, unroll=16)

    def stage_b(p, carry):
        b0 = pl.multiple_of(2 * p * ZPITCH, SUBLANES)
        b1 = pl.multiple_of(b0 + ZPITCH, SUBLANES)
        z = jnp.concatenate([zs_ref[pl.ds(b0, 2 * FFT_N2), :], zs_ref[pl.ds(b1, 2 * FFT_N2), :]], axis=1)
        x = _dot(fb_ref[...], z.astype(BF16))
        h = jnp.concatenate([hf_ref[0, 2 * p], hf_ref[0, 2 * p + 1]], axis=1)
        xr, xi = x[:FFT_N2], x[FFT_N2:]
        hr, hi = h[:FFT_N2], h[FFT_N2:]
        prod = jnp.concatenate([xr * hr - xi * hi, xr * hi + xi * hr], axis=0)
        y = _dot(fbi_ref[...], prod.astype(BF16))
        zs_ref[pl.ds(b0, 2 * FFT_N2), :] = y[:, :LANES]
        zs_ref[pl.ds(b1, 2 * FFT_N2), :] = y[:, LANES:]
        return carry

    lax.fori_loop(0, KHP // 2, stage_b, 0, unroll=11)

    def stage_ai(n2, carry):
        yr = zs_ref[pl.ds(n2, KP, stride=ZPITCH), :]
        yi = zs_ref[pl.ds(FFT_N2 + n2, KP, stride=ZPITCH), :]
        y = jnp.concatenate([yr, yi], axis=0).astype(BF16)
        ts_ref[pl.ds(n2, n_slabs, stride=TPITCH), :] = _dot(fai_ref[n2], y)
        return carry

    lax.fori_loop(0, FFT_N2, stage_ai, 0, unroll=16)

    def finish(j, carry):
        conv = ts_ref[pl.ds(pl.multiple_of(j * TPITCH, SUBLANES), FFT_N2), :]
        y = _conv_slab(x0_ref, cw0_ref, j, n_slabs) * conv
        o_ref[0, pl.ds(pl.multiple_of(j * FFT_N2, FFT_N2), FFT_N2), :] = y.astype(BF16)
        return carry

    lax.fori_loop(0, n_slabs, finish, 0, unroll=2)


def _hyena(u, cw, hf, fa, fai, fb, fbi):
    b, seq, c3 = u.shape
    nblk = D_HYENA // LANES
    n_slabs = seq // FFT_N2
    stream = lambda k: pl.BlockSpec((1, seq, LANES), lambda j, i, k=k: (i, 0, k * nblk + j))
    cwspec = lambda k: pl.BlockSpec((SUBLANES, LANES), lambda j, i, k=k: (0, k * nblk + j))
    return pl.pallas_call(
        _hyena_kernel,
        grid=(nblk, b),
        in_specs=[stream(0), stream(1), stream(2), cwspec(0), cwspec(1), cwspec(2),
                  pl.BlockSpec((1, KHP, 2 * FFT_N2, LANES), lambda j, i: (j, 0, 0, 0),
                               pipeline_mode=pl.Buffered(1)),
                  _const_spec(fa.shape), _const_spec(fai.shape),
                  _const_spec(fb.shape), _const_spec(fbi.shape)],
        out_specs=pl.BlockSpec((1, seq, LANES), lambda j, i: (i, 0, j)),
        out_shape=jax.ShapeDtypeStruct((b, seq, D_HYENA), BF16),
        scratch_shapes=[pltpu.VMEM((n_slabs * TPITCH, LANES), F32),
                        pltpu.VMEM((KP * ZPITCH, LANES), F32)],
        compiler_params=_params(("arbitrary", "arbitrary")),
        name="hyena",
    )(u, u, u, cw, cw, cw, hf, fa, fai, fb, fbi)


def _attn_kernel(sink_ref, q_ref, kp_ref, km_ref, kn_ref, vp_ref, vm_ref, vn_ref, kc_ref, vc_ref,
                 o_ref, ka_ref, va_ref):
    i = pl.program_id(1)
    n_i = pl.num_programs(1)
    tq = q_ref.shape[0]
    nqb = tq // BLOCK
    ka_ref[0:BLOCK] = kp_ref[...]
    ka_ref[BLOCK:BLOCK + tq] = km_ref[...]
    ka_ref[BLOCK + tq:] = kn_ref[...]
    va_ref[0:BLOCK] = vp_ref[...]
    va_ref[BLOCK:BLOCK + tq] = vm_ref[...]
    va_ref[BLOCK + tq:] = vn_ref[...]

    qi = lax.broadcasted_iota(jnp.int32, (BLOCK, BLOCK), 0)
    kj = lax.broadcasted_iota(jnp.int32, (BLOCK, BLOCK), 1)
    lane = lax.broadcasted_iota(jnp.int32, (BLOCK, LANES), 1)
    low = lane < HEAD_DIM
    half = GQA_GROUP // 2
    rows2 = half * BLOCK
    hrow = lax.broadcasted_iota(jnp.int32, (GQA_GROUP * BLOCK, 1), 0) // BLOCK
    head_order = [hh for hh in range(GQA_GROUP) if hh % 2 == 0] + [hh for hh in range(GQA_GROUP) if hh % 2 == 1]
    one = jnp.ones((), BF16)

    def with_ones(v):
        lanes_low = lax.broadcasted_iota(jnp.int32, v.shape, 1) < HEAD_DIM
        return jnp.where(lanes_low, v, one), jnp.where(lanes_low, one, v)

    for g in range(N_KV_HEADS):
        gl = slice(g * LANES, (g + 1) * LANES)
        vc_even, vc_odd = with_ones(vc_ref[:, gl])
        sink = jnp.zeros((GQA_GROUP * BLOCK, 1), F32)
        for pos, hh in enumerate(head_order):
            sink = jnp.where(hrow == pos, sink_ref[g * GQA_GROUP + hh], sink)
        for j in range(nqb):
            prev_ok = kj >= qi
            next_ok = kj <= qi
            if j == 0:
                prev_ok = prev_ok & (i > 0)
            if j == nqb - 1:
                next_ok = next_ok & (i < n_i - 1)
            bias_p = jnp.concatenate([jnp.where(prev_ok, 0.0, NEG_INF).astype(F32)] * GQA_GROUP, axis=0)
            bias_n = jnp.concatenate([jnp.where(next_ok, 0.0, NEG_INF).astype(F32)] * GQA_GROUP, axis=0)
            qb = q_ref[j * BLOCK:(j + 1) * BLOCK, :]
            parts = []
            for hh in head_order:
                h = g * GQA_GROUP + hh
                qp = qb[:, (h // 2) * LANES:(h // 2 + 1) * LANES]
                parts.append(jnp.where(low if h % 2 == 0 else ~low, qp, jnp.zeros_like(qp)))
            qs = jnp.concatenate(parts, axis=0)
            kw = ka_ref[j * BLOCK:(j + 3) * BLOCK, gl]
            vw_even, vw_odd = with_ones(va_ref[j * BLOCK:(j + 3) * BLOCK, gl])
            s_w = _dot_nt(qs, kw)
            s_p = s_w[:, :BLOCK] + bias_p
            s_m = s_w[:, BLOCK:2 * BLOCK]
            s_n = s_w[:, 2 * BLOCK:] + bias_n
            s_c = _dot_nt(qs, kc_ref[:, gl])
            m = jnp.maximum(jnp.maximum(jnp.max(jnp.maximum(jnp.maximum(s_p, s_m), s_n), axis=1, keepdims=True),
                                        jnp.max(s_c, axis=1, keepdims=True)), sink)
            e_w = jnp.concatenate([jnp.exp(s_p - m), jnp.exp(s_m - m), jnp.exp(s_n - m)], axis=1).astype(BF16)
            e_c = jnp.exp(s_c - m).astype(BF16)
            e_sink = jnp.exp(sink - m)
            outs = []
            for par, (vw, vcx) in enumerate(((vw_even, vc_even), (vw_odd, vc_odd))):
                rs = slice(par * rows2, (par + 1) * rows2)
                acc = _dot(e_w[rs], vw) + _dot(e_c[rs], vcx)
                den = pltpu.roll(acc, HEAD_DIM, 1) + e_sink[rs]
                outs.append(acc / den)
            for pp in range(half):
                pair = jnp.where(low, outs[0][pp * BLOCK:(pp + 1) * BLOCK], outs[1][pp * BLOCK:(pp + 1) * BLOCK])
                col = (g * half + pp) * LANES
                o_ref[j * BLOCK:(j + 1) * BLOCK, col:col + LANES] = pair.astype(BF16)


def _attention(sinks, q, kd, vd, kc, vc, batch, seq, tq):
    t = q.shape[0]
    per_b = seq // tq
    nqb = tq // BLOCK
    nb = seq // BLOCK
    n_ctx = kc.shape[0] // batch
    main = lambda b, i: (b * per_b + i, 0)
    prev = lambda b, i: (b * nb + jnp.maximum(i * nqb - 1, 0), 0)
    nxt = lambda b, i: (b * nb + jnp.minimum(i * nqb + nqb, nb - 1), 0)
    kvw = kd.shape[1]
    return pl.pallas_call(
        _attn_kernel,
        grid=(batch, per_b),
        in_specs=[pl.BlockSpec(memory_space=pltpu.SMEM),
                  pl.BlockSpec((tq, D_ATTN), main),
                  pl.BlockSpec((BLOCK, kvw), prev), pl.BlockSpec((tq, kvw), main), pl.BlockSpec((BLOCK, kvw), nxt),
                  pl.BlockSpec((BLOCK, kvw), prev), pl.BlockSpec((tq, kvw), main), pl.BlockSpec((BLOCK, kvw), nxt),
                  pl.BlockSpec((n_ctx, kvw), lambda b, i: (b, 0)),
                  pl.BlockSpec((n_ctx, kvw), lambda b, i: (b, 0))],
        out_specs=pl.BlockSpec((tq, D_ATTN), main),
        out_shape=jax.ShapeDtypeStruct((t, D_ATTN), BF16),
        scratch_shapes=[pltpu.VMEM((tq + 2 * BLOCK, kvw), BF16),
                        pltpu.VMEM((tq + 2 * BLOCK, kvw), BF16)],
        compiler_params=_params(("arbitrary", "arbitrary")),
        name="attention",
    )(sinks, q, kd, kd, kd, vd, vd, vd, kc, vc)


def _route(t, wr_hi_ref, wr_hl_ref, br_ref):
    th, tl = _split(t)
    both = _dot(th, wr_hl_ref[...])
    logits = both[:, :LANES] + both[:, LANES:] + _dot(tl, wr_hi_ref[...]) + br_ref[...]
    lane_i = lax.broadcasted_iota(jnp.int32, logits.shape, 1)
    lane = lane_i.astype(F32)
    grp_of_lane = (lane_i >> 2).astype(F32)
    ninf = -jnp.inf
    far = float(LANES)
    is_g = (lane_i >= N_EXPERTS) & (lane_i < N_EXPERTS + N_GROUPS)
    glog = jnp.where(is_g, logits, ninf)
    gmax = jnp.max(glog, axis=1, keepdims=True)
    gidx = jnp.min(jnp.where(glog == gmax, lane - float(N_EXPERTS), far), axis=1, keepdims=True)
    group_p = 1.0 / jnp.sum(jnp.exp(glog - gmax), axis=1, keepdims=True)
    in_grp = (lane_i < N_EXPERTS) & (grp_of_lane == gidx)
    elog = jnp.where(in_grp, logits, ninf)
    v1 = jnp.max(elog, axis=1, keepdims=True)
    i1 = jnp.min(jnp.where(elog == v1, lane, far), axis=1, keepdims=True)
    elog2 = jnp.where(lane == i1, ninf, elog)
    v2 = jnp.max(elog2, axis=1, keepdims=True)
    i2 = jnp.min(jnp.where(elog2 == v2, lane, far), axis=1, keepdims=True)
    e = jnp.exp(v2 - v1)
    w1 = group_p / (1.0 + e)
    w2 = group_p * e / (1.0 + e)
    gate = jnp.where(lane == i1, w1, 0.0) + jnp.where(lane == i2, w2, 0.0)
    return gate + jnp.where(lane_i == GROUP_LANE, gidx, 0.0)


def _mix_out_kernel(x_ref, yh_ref, ya_ref, sh1_ref, sc1_ref, g1_ref, sh2_ref, sc2_ref,
                    wg_ref, wbh_ref, wba_ref, wo_ref, lng_ref, lnb_ref, wrh_ref, wrhl_ref, br_ref,
                    x1_ref, gate_ref, *t_refs):
    d = x_ref.shape[1]
    for r0 in range(0, x_ref.shape[0], SUB_ROWS):
        rows = slice(r0, r0 + SUB_ROWS)
        x = x_ref[rows, :]
        h = (_standardize(x) * (1.0 + sc1_ref[0]) + sh1_ref[0]).astype(BF16)
        g_hy = jax.nn.sigmoid(_dot(h, wg_ref[:, :d]))
        merged = g_hy * _dot(yh_ref[rows, :], wbh_ref[...])
        g_at = jax.nn.sigmoid(_dot(h, wg_ref[:, d:]))
        merged = merged + g_at * _dot(ya_ref[rows, :], wba_ref[...])
        mix = _dot(merged.astype(BF16), wo_ref[...])
        x1 = _standardize(DEEPNORM_ALPHA * x + g1_ref[0] * mix) * lng_ref[...] + lnb_ref[...]
        x1_ref[rows, :] = x1
        t = _standardize(x1) * (1.0 + sc2_ref[0]) + sh2_ref[0]
        gate_ref[rows, :] = _route(t, wrh_ref, wrhl_ref, br_ref)
        for t_ref, piece in zip(t_refs, _pack_pieces(t)):
            t_ref[rows, :] = piece


def _mix_out(x2d, yh, ya, mods, w_g, w_bh, w_ba, w_o, ln_g, ln_b, wr_hi, wr_hl, br, seq, tm):
    t, d = x2d.shape
    per_b = seq // tm
    row = lambda i: (i, 0)
    mod = lambda i: (i // per_b, 0, 0)
    mspec = pl.BlockSpec((1, 1, d), mod)
    sh1, sc1, g1, sh2, sc2 = mods
    return pl.pallas_call(
        _mix_out_kernel,
        grid=(t // tm,),
        in_specs=[pl.BlockSpec((tm, d), row),
                  pl.BlockSpec((tm, yh.shape[1]), row),
                  pl.BlockSpec((tm, ya.shape[1]), row),
                  mspec, mspec, mspec, mspec, mspec,
                  _const_spec(w_g.shape), _const_spec(w_bh.shape), _const_spec(w_ba.shape),
                  _const_spec(w_o.shape), _const_spec(ln_g.shape), _const_spec(ln_b.shape),
                  _const_spec(wr_hi.shape), _const_spec(wr_hl.shape), _const_spec(br.shape)],
        out_specs=[pl.BlockSpec((tm, d), row), pl.BlockSpec((tm, LANES), row)]
        + [pl.BlockSpec((tm, SC_ROW_WORDS), row)] * N_PIECES,
        out_shape=[jax.ShapeDtypeStruct((t, d), F32), jax.ShapeDtypeStruct((t, LANES), F32)]
        + [jax.ShapeDtypeStruct((t, SC_ROW_WORDS), jnp.uint32)] * N_PIECES,
        compiler_params=_params(("arbitrary",)),
        name="mix_out",
    )(x2d, yh, ya, sh1, sc1, g1, sh2, sc2, w_g, w_bh, w_ba, w_o, ln_g, ln_b, wr_hi, wr_hl, br)


def _slots_kernel(g_ref, upper_ref, lower_ref, dest_ref, meta_ref):
    gsel = g_ref[...]
    dest = jnp.zeros(gsel.shape, F32)
    base = jnp.zeros((1, 1), F32)
    chunk_start = lax.broadcasted_iota(jnp.int32, (1, LANES), 1).astype(F32) * float(MOE_CHUNK)
    owner = jnp.zeros((1, LANES), F32)
    for g in range(N_GROUPS):
        onehot = jnp.where(gsel == float(g), 1.0, 0.0)
        in_row = _dot(onehot.astype(BF16), upper_ref[...])
        row_tot = jnp.sum(onehot, axis=1, keepdims=True)
        rows_before = _dot(lower_ref[...], jnp.broadcast_to(row_tot, onehot.shape).astype(BF16))
        dest = dest + onehot * (base + rows_before + in_row)
        if g > 0:
            owner = owner + jnp.where(chunk_start >= base, 1.0, 0.0)
        n_g = jnp.sum(row_tot, axis=0, keepdims=True)
        base = base + jnp.floor((n_g + float(MOE_CHUNK - 1)) * (1.0 / MOE_CHUNK)) * float(MOE_CHUNK)
    dest_ref[...] = dest.astype(jnp.int32)
    row = lax.broadcasted_iota(jnp.int32, meta_ref.shape, 0)
    meta_ref[...] = jnp.where(row == 0, owner, base * (1.0 / MOE_CHUNK)).astype(jnp.int32)


def _slots(gsel):
    r = gsel.shape[0]
    upper = jnp.asarray(np.triu(np.ones((LANES, LANES), np.float32), 1)).astype(BF16)
    lower = jnp.asarray(np.tril(np.ones((r, r), np.float32), -1)).astype(BF16)
    full = lambda a: pl.BlockSpec(a.shape, lambda i: (0,) * a.ndim)
    return pl.pallas_call(
        _slots_kernel,
        grid=(1,),
        in_specs=[full(gsel), full(upper), full(lower)],
        out_specs=[pl.BlockSpec((r, LANES), lambda i: (0, 0)), pl.BlockSpec((SUBLANES, LANES), lambda i: (0, 0))],
        out_shape=[jax.ShapeDtypeStruct((r, LANES), jnp.int32), jax.ShapeDtypeStruct((SUBLANES, LANES), jnp.int32)],
        compiler_params=_params(("arbitrary",)),
        name="moe_slots",
    )(gsel, upper, lower)


def _experts_kernel(owner_ref, used_ref, *refs):
    n_p = N_PIECES
    t_refs, gate_ref = refs[:n_p], refs[n_p]
    wg_ref, wu_ref, wd_ref = refs[n_p + 1:n_p + 4]
    y_refs = refs[n_p + 4:]
    c = pl.program_id(0)

    @pl.when(c < used_ref[0])
    def _():
        t = _unpack_pieces([r[...] for r in t_refs]).astype(BF16)
        gate = gate_ref[...]
        lane = lax.broadcasted_iota(jnp.int32, gate.shape, 1)
        first = owner_ref[c] * EXPERTS_PER_GROUP
        parts = []
        for e in range(EXPERTS_PER_GROUP):
            a = _dot(t, wg_ref[e])
            u = _dot(t, wu_ref[e])
            ge = jnp.sum(jnp.where(lane == first + e, gate, 0.0), axis=1, keepdims=True)
            parts.append((a * jax.nn.sigmoid(a) * u * ge).astype(BF16))
        y = _dot(jnp.concatenate(parts, axis=1), wd_ref[0])
        for y_ref, piece in zip(y_refs, _pack_pieces(y)):
            y_ref[...] = piece

    @pl.when(c >= used_ref[0])
    def _():
        for y_ref in y_refs:
            y_ref[...] = jnp.zeros_like(y_ref)


def _experts(owner, used, t_pieces, gate_sorted, wg16, wu16, wd3):
    n_slots = gate_sorted.shape[0]
    row = lambda c, owner, used: (c, 0)
    by_owner = lambda c, owner, used: (owner[c], 0, 0)
    piece = pl.BlockSpec((MOE_CHUNK, SC_ROW_WORDS), row)
    group_of_experts = lambda w: pl.BlockSpec((EXPERTS_PER_GROUP,) + w.shape[1:], by_owner)
    return pl.pallas_call(
        _experts_kernel,
        grid_spec=pltpu.PrefetchScalarGridSpec(
            num_scalar_prefetch=2,
            grid=(n_slots // MOE_CHUNK,),
            in_specs=[piece] * len(t_pieces) + [pl.BlockSpec((MOE_CHUNK, LANES), row),
                                                group_of_experts(wg16), group_of_experts(wu16),
                                                pl.BlockSpec((1,) + wd3.shape[1:], by_owner)],
            out_specs=[piece] * len(t_pieces),
        ),
        out_shape=[jax.ShapeDtypeStruct((n_slots, SC_ROW_WORDS), jnp.uint32)] * len(t_pieces),
        compiler_params=_params(("arbitrary",)),
        name="moe_experts",
    )(owner, used, *t_pieces, gate_sorted, wg16, wu16, wd3)


def _final_kernel(x1_ref, g2_ref, lng_ref, lnb_ref, *refs):
    y_refs, o_ref = refs[:-1], refs[-1]
    y = _unpack_pieces([r[...] for r in y_refs])
    o_ref[...] = _standardize(DEEPNORM_ALPHA * x1_ref[...] + g2_ref[0] * y) * lng_ref[...] + lnb_ref[...]


def _final(x1, g2, ln_g, ln_b, y_pieces, seq, tm):
    t, d = x1.shape
    per_b = seq // tm
    row = lambda i: (i, 0)
    return pl.pallas_call(
        _final_kernel,
        grid=(t // tm,),
        in_specs=[pl.BlockSpec((tm, d), row), pl.BlockSpec((1, 1, d), lambda i: (i // per_b, 0, 0)),
                  _const_spec(ln_g.shape), _const_spec(ln_b.shape)]
        + [pl.BlockSpec((tm, SC_ROW_WORDS), row)] * len(y_pieces),
        out_specs=pl.BlockSpec((tm, d), row),
        out_shape=jax.ShapeDtypeStruct((t, d), F32),
        compiler_params=_params(("arbitrary",)),
        name="moe_final",
    )(x1, g2, ln_g, ln_b, *y_pieces)


def _sc_move_rows(tables, idx, n_out, scatter, name):
    n = idx.shape[0]
    mesh = plsc.VectorSubcoreMesh(core_axis_name="c", subcore_axis_name="s")
    out_type = [jax.ShapeDtypeStruct((n_out, t.shape[1]), t.dtype) for t in tables]
    window = lambda i: (i, 0)
    index_win = pl.BlockSpec((1, SC_WINDOW), lambda i: (0, i))
    split = dict(core_axis_name=("c", "s"), dimension_semantics=(pltpu.PARALLEL,))

    @functools.partial(pl.kernel, out_type=out_type, mesh=mesh, scratch_types=[], name=name)
    def move(*refs):
        srcs, i_hbm, dsts = refs[:len(tables)], refs[len(tables)], refs[len(tables) + 1:]
        for src, dst in zip(srcs, dsts):
            rows_win = pl.BlockSpec((SC_WINDOW, src.shape[1]), window)
            if scatter:
                def body(x_vmem, i_vmem, dst=dst):
                    pltpu.sync_copy(x_vmem, dst.at[i_vmem.at[0]])

                pltpu.emit_pipeline(body, grid=(n // SC_WINDOW,), in_specs=[rows_win, index_win],
                                    out_specs=[], **split)(src, i_hbm)
            else:
                def body(i_vmem, o_vmem, src=src):
                    pltpu.sync_copy(src.at[i_vmem.at[0]], o_vmem)

                pltpu.emit_pipeline(body, grid=(n // SC_WINDOW,), in_specs=[index_win],
                                    out_specs=[rows_win], **split)(i_hbm, dst)

    return move(*tables, idx.reshape(1, n))


def _dft_tables(seq):
    n = 2 * seq
    n1_full = n // FFT_N2
    n1_data = seq // FFT_N2
    k1 = np.arange(KH, dtype=np.float64)[None, :, None]
    n2 = np.arange(FFT_N2, dtype=np.float64)[:, None, None]
    n1 = np.arange(n1_full, dtype=np.float64)[None, None, :]
    ang = 2.0 * np.pi * k1 * (FFT_N2 * n1 + n2) / n
    fa = np.zeros((FFT_N2, 2 * KP, n1_full))
    fa[:, :KH] = np.cos(ang)
    fa[:, KP:KP + KH] = -np.sin(ang)
    wgt = np.full((KH,), 2.0)
    wgt[0] = 1.0
    wgt[KH - 1] = 1.0
    fai = np.zeros((FFT_N2, n1_data, 2 * KP))
    angt = np.transpose(ang[:, :, :n1_data], (0, 2, 1))
    fai[:, :, :KH] = np.cos(angt) * wgt / n
    fai[:, :, KP:KP + KH] = -np.sin(angt) * wgt / n
    kk = np.arange(FFT_N2, dtype=np.float64)
    a2 = 2.0 * np.pi * np.outer(kk, kk) / FFT_N2
    fr, fi = np.cos(a2), -np.sin(a2)
    fb = np.block([[fr, -fi], [fi, fr]])
    fbi = np.block([[fr, fi], [-fi, fr]])
    f32 = lambda a: jnp.asarray(a.astype(np.float32))
    return f32(fa[:, :, :n1_data]), f32(fa), f32(fai), f32(fb), f32(fbi)


def _filter_features(seq, rows):
    f32 = np.float32
    t = np.linspace(0.0, 1.0, seq, dtype=f32)
    w = (f32(2.0 * math.pi) * np.arange(seq, dtype=f32) / f32(seq)).astype(f32)
    t2 = np.concatenate([t, t[::-1]])
    w2 = np.concatenate([w, w[::-1]])
    bands = np.linspace(1e-4, FILTER_BANDS - 1, FILTER_BANDS, dtype=f32)
    max_decay = math.log(DECAY_TARGET) / FAST_DECAY_PCT
    min_decay = math.log(DECAY_TARGET) / SLOW_DECAY_PCT
    deltas = jnp.linspace(min_decay, max_decay, D_HYENA, dtype=F32)

    def feats(k):
        tp = t2.reshape(-1, 2, rows // 2)[:, k].reshape(-1, 1)
        wp = w2.reshape(-1, 2, rows // 2)[:, k].reshape(-1, 1)
        pad = np.zeros((tp.shape[0], FILTER_ORDER - FILTER_EMB), f32)
        arg = (bands * wp).astype(f32)
        return np.concatenate([tp, np.cos(arg), -np.sin(arg), pad], axis=-1).astype(f32)

    zp = jnp.asarray(np.concatenate([feats(0), feats(1)], axis=-1))
    decay = jnp.exp(-jnp.asarray(t2)[:, None] * jnp.abs(deltas))
    return zp, decay


def _rope_tables(seq):
    f32 = np.float32
    rows = seq // GRID_W
    row = np.repeat(np.arange(rows, dtype=f32), GRID_W)
    col = np.tile(np.arange(GRID_W, dtype=f32), rows)
    half = HEAD_DIM // 2
    inv_freq = (f32(ROPE_BASE) ** (-np.arange(0, half, 2, dtype=f32) / f32(half))).astype(f32)
    ang = np.concatenate([row[:, None] * inv_freq, col[:, None] * inv_freq], axis=-1).astype(f32)
    cos, sin = np.cos(ang), np.sin(ang)
    c64 = np.concatenate([cos, cos], axis=-1)
    s64 = np.concatenate([-sin, sin], axis=-1)
    return (jnp.asarray(np.concatenate([c64, c64], axis=-1).astype(f32)),
            jnp.asarray(np.concatenate([s64, s64], axis=-1).astype(f32)))


def _head_perm(n_heads):
    idx = []
    for h in range(n_heads):
        base = h * HEAD_DIM
        idx += [base + 2 * j for j in range(HEAD_DIM // 2)]
        idx += [base + 2 * j + 1 for j in range(HEAD_DIM // 2)]
    return np.asarray(idx, dtype=np.int32)


def _dup_heads(w):
    parts = []
    for g in range(N_KV_HEADS):
        blk = w[:, g * HEAD_DIM:(g + 1) * HEAD_DIM]
        parts += [blk, blk]
    return jnp.concatenate(parts, axis=1)


def kernel(x, c, ctx, c_ctx, ada_w, ada_b, w_in, hy_conv_w, hy_conv_b, hy_w1, hy_b1, hy_w2, hy_b2, hy_w3, hy_b3, hy_w4, hy_freq, hy_bias, attn_sinks, w_branch_hy, w_branch_attn, w_out, ln1_g, ln1_b, w_group, b_group, w_router, b_router, w_gate_e, w_up_e, w_down_e, ln2_g, ln2_b):
    batch, seq, d = x.shape
    n_ctx = ctx.shape[1]
    assert d == D_MODEL and ada_w.shape[0] == DEPTH == 1
    assert 2 * seq == FFT_N2 * FFT_N2 and seq % TM_PROJ == 0
    l = 0

    w = w_in[l]
    s0, s1, s2, s3 = 3 * D_HYENA, 3 * D_HYENA + D_ATTN, 3 * D_HYENA + D_ATTN + D_KV, 3 * D_HYENA + D_ATTN + 2 * D_KV
    w_q = w[:, s0:s1][:, _head_perm(N_HEADS)]
    w_k = _dup_heads(w[:, s1:s2][:, _head_perm(N_KV_HEADS)])
    w_v = _dup_heads(w[:, s2:s3])
    w_c = jnp.concatenate([w[:, :s0], w_q, w_k, w_v], axis=1).astype(BF16)
    w_kv = jnp.concatenate([w_k, w_v], axis=1).astype(BF16)
    w_g = w[:, s3:].astype(BF16)
    w_bh = w_branch_hy[l].astype(BF16)
    w_ba = w_branch_attn[l].astype(BF16)
    w_o = w_out[l].astype(BF16)
    wr = jnp.zeros((d, LANES), F32)
    wr = wr.at[:, :N_EXPERTS].set(w_router[l]).at[:, N_EXPERTS:N_EXPERTS + N_GROUPS].set(w_group[l])
    wr_hi = wr.astype(BF16)
    wr_hl = jnp.concatenate([wr_hi, (wr - wr_hi.astype(F32)).astype(BF16)], axis=1)
    br = jnp.zeros((1, LANES), F32)
    br = br.at[0, :N_EXPERTS].set(b_router[l]).at[0, N_EXPERTS:N_EXPERTS + N_GROUPS].set(b_group[l])
    wg16 = w_gate_e[l].astype(BF16)
    wu16 = w_up_e[l].astype(BF16)
    wd3 = w_down_e[l].astype(BF16).reshape(N_GROUPS, EXPERTS_PER_GROUP * D_EXPERT, d)
    cw = jnp.concatenate([hy_conv_w[l][:, 0, :], hy_conv_b[l][None, :],
                          jnp.zeros((SUBLANES - SHORT_CONV - 1, 3 * D_HYENA), F32)], axis=0)
    row2 = lambda a: a.reshape(1, -1)
    pair = lambda a: jnp.concatenate([a, a], axis=-1)
    zero_o = jnp.zeros((FILTER_ORDER, FILTER_ORDER), F32)
    bdiag = lambda a: jnp.concatenate([jnp.concatenate([a, zero_o], axis=1),
                                       jnp.concatenate([zero_o, a], axis=1)], axis=0)
    w1p = bdiag(jnp.concatenate([hy_w1[l], jnp.zeros((FILTER_ORDER - FILTER_EMB, FILTER_ORDER), F32)], axis=0))
    w4h = jnp.transpose(hy_w4[l].reshape(FILTER_ORDER, 2, D_HYENA), (1, 0, 2))
    zero_w4 = jnp.zeros_like(w4h)
    w4s = jnp.stack([jnp.concatenate([w4h, zero_w4], axis=1),
                     jnp.concatenate([zero_w4, w4h], axis=1)], axis=1)

    fa, fa_full, fai, fb, fbi = _dft_tables(seq)
    fa, fa_full, fai, fb, fbi = (a.astype(BF16) for a in (fa, fa_full, fai, fb, fbi))
    zp, dec2 = _filter_features(seq, FILT_ROWS)
    cos_t, sin_t = _rope_tables(seq)

    cond = jnp.concatenate([c, c_ctx[None], jnp.zeros((SUBLANES - batch - 1, d), F32)], axis=0)
    mods = _adaln(cond, ada_w[l], ada_b[l])
    m6 = [mods[:, k * d:(k + 1) * d].reshape(SUBLANES, 1, d) for k in range(6)]
    sh1, sc1, g1, sh2, sc2, g2 = m6

    kc, vc = _ctx_kv(ctx.reshape(batch * n_ctx, d), sh1[batch], sc1[batch], w_kv, n_ctx)

    x2d = x.reshape(batch * seq, d)
    u, q, kd, vd = _in_proj(x2d, sh1, sc1, w_c, cos_t, sin_t, seq, TM_PROJ)
    h2u, ss = _filt_mlp(zp, w1p, pair(row2(hy_b1[l])), bdiag(hy_w2[l]), pair(row2(hy_b2[l])),
                        bdiag(hy_w3[l]), pair(row2(hy_b3[l])), w4s, pair(row2(hy_freq[l])), dec2, FILT_ROWS)
    hf = _filt_fft(h2u, ss, row2(hy_bias[l]), fa_full, fb)
    y_hy = _hyena(u.reshape(batch, seq, 3 * D_HYENA), cw, hf, fa, fai, fb, fbi)
    y_at = _attention(attn_sinks[l], q, kd, vd, kc, vc, batch, seq, TQ_ATTN)
    x1, gate, *t_pieces = _mix_out(x2d, y_hy.reshape(batch * seq, D_HYENA), y_at, (sh1, sc1, g1, sh2, sc2),
                                   w_g, w_bh, w_ba, w_o, row2(ln1_g[l]), row2(ln1_b[l]), wr_hi, wr_hl, br,
                                   seq, TM_PROJ)

    n_tok = batch * seq
    n_slots = n_tok + N_GROUPS * MOE_CHUNK
    dest2d, meta = _slots(gate[:, GROUP_LANE].reshape(n_tok // LANES, LANES))
    dest = dest2d.reshape(n_tok)
    *t_sorted, gate_sorted = _sc_move_rows(t_pieces + [gate], dest, n_slots, True, "moe_sort")
    y_sorted = _experts(meta[0], meta[1, :1], t_sorted, gate_sorted, wg16, wu16, wd3)
    y_pieces = _sc_move_rows(y_sorted, dest, n_tok, False, "moe_unsort")
    out = _final(x1, g2, row2(ln2_g[l]), row2(ln2_b[l]), y_pieces, seq, TM_FINAL)
    return out.reshape(batch, seq, d)
```

```python
import functools
import math

import numpy as np
import jax
import jax.numpy as jnp
from jax import lax
from jax.experimental import pallas as pl
from jax.experimental.pallas import tpu as pltpu
from jax.experimental.pallas import tpu_sc as plsc

F32 = jnp.float32
BF16 = jnp.bfloat16

D_MODEL = 1024
GRID_W = 64
D_HYENA = D_MODEL // 2
SHORT_CONV = 3
FILTER_BANDS = 16
FILTER_EMB = 1 + 2 * FILTER_BANDS
FILTER_ORDER = 64
DECAY_TARGET = 1e-2
FAST_DECAY_PCT = 0.3
SLOW_DECAY_PCT = 1.5
HEAD_DIM = 64
D_ATTN = D_MODEL // 2
N_HEADS = D_ATTN // HEAD_DIM
N_KV_HEADS = N_HEADS // 4
GQA_GROUP = N_HEADS // N_KV_HEADS
D_KV = N_KV_HEADS * HEAD_DIM
WINDOW = 128
BLOCK = 128
ROPE_BASE = 10000.0
NEG_INF = -1e30
N_GROUPS = 4
EXPERTS_PER_GROUP = 4
N_EXPERTS = N_GROUPS * EXPERTS_PER_GROUP
D_EXPERT = D_MODEL // 4
LN_EPS = 1e-5
DEPTH = 1
DEEPNORM_ALPHA = (2.0 * DEPTH) ** 0.25

LANES = 128
SUBLANES = 8
VMEM_LIMIT = 56 * 1024 * 1024

SUB_ROWS = 512
TM_PROJ = 1024
MOE_CHUNK = 512
TM_FINAL = 1024
MOE_PARTS = 2
GROUP_LANE = N_EXPERTS
SC_WINDOW = 128
SC_ROW_WORDS = 256
N_PIECES = D_MODEL // (2 * SC_ROW_WORDS)
TQ_ATTN = 512
FILT_ROWS = 2048

FFT_N2 = 128
KH = 65
KHP = 66
KP = 72
ZPITCH = 2 * FFT_N2 + SUBLANES
TPITCH = FFT_N2 + SUBLANES


def _dot(a, b):
    return jnp.dot(a, b, preferred_element_type=F32)


def _dot_nt(a, b):
    return lax.dot_general(a, b, (((1,), (1,)), ((), ())), preferred_element_type=F32)


def _split(a):
    hi = a.astype(BF16)
    lo = (a - hi.astype(F32)).astype(BF16)
    return hi, lo


def _pack_pieces(x):
    w = SC_ROW_WORDS
    pieces = []
    for p in range(x.shape[1] // (2 * w)):
        hi = lax.bitcast_convert_type(x[:, 2 * p * w:(2 * p + 1) * w].astype(BF16).astype(F32), jnp.uint32)
        lo = lax.bitcast_convert_type(x[:, (2 * p + 1) * w:(2 * p + 2) * w].astype(BF16).astype(F32), jnp.uint32)
        pieces.append(hi | (lo >> 16))
    return pieces


def _unpack_pieces(pieces):
    cols = []
    for word in pieces:
        cols.append(lax.bitcast_convert_type(word & jnp.uint32(0xFFFF0000), F32))
        cols.append(lax.bitcast_convert_type(word << 16, F32))
    return jnp.concatenate(cols, axis=1)


def _dot3(a, b):
    ah, al = _split(a)
    bh, bl = _split(b)
    return _dot(ah, bh) + _dot(al, bh) + _dot(ah, bl)


def _standardize(x):
    mu = jnp.mean(x, axis=-1, keepdims=True)
    xc = x - mu
    var = jnp.mean(xc * xc, axis=-1, keepdims=True)
    return xc * lax.rsqrt(var + LN_EPS)


def _params(sem, vmem=VMEM_LIMIT):
    return pltpu.CompilerParams(dimension_semantics=sem, vmem_limit_bytes=vmem)


def _const_spec(shape):
    nd = len(shape)
    return pl.BlockSpec(shape, lambda *_: (0,) * nd, pipeline_mode=pl.Buffered(1))


def _adaln_kernel(c_ref, w_ref, b_ref, o_ref):
    s = c_ref[...]
    s = s * jax.nn.sigmoid(s)
    o_ref[...] = _dot3(s, w_ref[...]) + b_ref[...]


def _adaln(cond, w, b):
    n, d = cond.shape
    cols = w.shape[1]
    bc = 1024
    return pl.pallas_call(
        _adaln_kernel,
        grid=(cols // bc,),
        in_specs=[pl.BlockSpec((n, d), lambda j: (0, 0)),
                  pl.BlockSpec((d, bc), lambda j: (0, j)),
                  pl.BlockSpec((1, bc), lambda j: (0, j))],
        out_specs=pl.BlockSpec((n, bc), lambda j: (0, j)),
        out_shape=jax.ShapeDtypeStruct((n, cols), F32),
        compiler_params=_params(("arbitrary",)),
        name="adaln",
    )(cond, w, b.reshape(1, cols))


def _ctx_kv_kernel(x_ref, sh_ref, sc_ref, w_ref, k_ref, v_ref):
    h = _standardize(x_ref[...]) * (1.0 + sc_ref[...]) + sh_ref[...]
    kv = _dot(h.astype(BF16), w_ref[...])
    half = k_ref.shape[1]
    k_ref[...] = kv[:, :half].astype(BF16)
    v_ref[...] = kv[:, half:].astype(BF16)


def _ctx_kv(ctx2d, sh, sc, w_kv, rows):
    n, d = ctx2d.shape
    half = w_kv.shape[1] // 2
    return pl.pallas_call(
        _ctx_kv_kernel,
        grid=(n // rows,),
        in_specs=[pl.BlockSpec((rows, d), lambda i: (i, 0)),
                  pl.BlockSpec((1, d), lambda i: (0, 0)),
                  pl.BlockSpec((1, d), lambda i: (0, 0)),
                  pl.BlockSpec(w_kv.shape, lambda i: (0, 0))],
        out_specs=[pl.BlockSpec((rows, half), lambda i: (i, 0)),
                   pl.BlockSpec((rows, half), lambda i: (i, 0))],
        out_shape=[jax.ShapeDtypeStruct((n, half), BF16)] * 2,
        compiler_params=_params(("arbitrary",)),
        name="ctx_kv",
    )(ctx2d, sh, sc, w_kv)


def _rope(x, cos_t, sin_t):
    width = x.shape[1]
    reps = width // LANES
    c = jnp.concatenate([cos_t] * reps, axis=1)
    s = jnp.concatenate([sin_t] * reps, axis=1)
    half = HEAD_DIM // 2
    lane = lax.broadcasted_iota(jnp.int32, x.shape, 1)
    first_half = (lane & (HEAD_DIM - 1)) < half
    partner = jnp.where(first_half, pltpu.roll(x, width - half, 1), pltpu.roll(x, half, 1))
    return x * c + partner * s


def _in_proj_kernel(x_ref, sh_ref, sc_ref, w_ref, cos_ref, sin_ref, u_ref, q_ref, k_ref, v_ref):
    n_u = u_ref.shape[1]
    n_q = q_ref.shape[1]
    n_k = k_ref.shape[1]
    for r0 in range(0, x_ref.shape[0], SUB_ROWS):
        rows = slice(r0, r0 + SUB_ROWS)
        h = (_standardize(x_ref[rows, :]) * (1.0 + sc_ref[0]) + sh_ref[0]).astype(BF16)
        u_ref[rows, :] = _dot(h, w_ref[:, :n_u]).astype(BF16)
        cos_t = cos_ref[rows, :]
        sin_t = sin_ref[rows, :]
        q = _dot(h, w_ref[:, n_u:n_u + n_q])
        q_ref[rows, :] = (_rope(q, cos_t, sin_t) * (HEAD_DIM ** -0.5)).astype(BF16)
        k = _dot(h, w_ref[:, n_u + n_q:n_u + n_q + n_k])
        k_ref[rows, :] = _rope(k, cos_t, sin_t).astype(BF16)
        v_ref[rows, :] = _dot(h, w_ref[:, n_u + n_q + n_k:]).astype(BF16)


def _in_proj(x2d, sh, sc, w_c, cos_t, sin_t, seq, tm):
    t, d = x2d.shape
    per_b = seq // tm
    n_u, n_q, n_k = 3 * D_HYENA, D_ATTN, 2 * D_KV
    row = lambda i: (i, 0)
    mod = lambda i: (i // per_b, 0, 0)
    pos = lambda i: (i % per_b, 0)
    return pl.pallas_call(
        _in_proj_kernel,
        grid=(t // tm,),
        in_specs=[pl.BlockSpec((tm, d), row),
                  pl.BlockSpec((1, 1, d), mod),
                  pl.BlockSpec((1, 1, d), mod),
                  _const_spec(w_c.shape),
                  pl.BlockSpec((tm, LANES), pos),
                  pl.BlockSpec((tm, LANES), pos)],
        out_specs=[pl.BlockSpec((tm, n_u), row),
                   pl.BlockSpec((tm, n_q), row),
                   pl.BlockSpec((tm, n_k), row),
                   pl.BlockSpec((tm, n_k), row)],
        out_shape=[jax.ShapeDtypeStruct((t, n_u), BF16),
                   jax.ShapeDtypeStruct((t, n_q), BF16),
                   jax.ShapeDtypeStruct((t, n_k), BF16),
                   jax.ShapeDtypeStruct((t, n_k), BF16)],
        compiler_params=_params(("arbitrary",)),
        name="in_proj",
    )(x2d, sh, sc, w_c, cos_t, sin_t)


def _filt_mlp_kernel(z_ref, w1_ref, b1_ref, w2_ref, b2_ref, w3_ref, b3_ref, w4_ref, fr_ref, dec_ref,
                     h_ref, ss_ref):
    fr = fr_ref[...]
    a = jnp.sin(fr * (_dot3(z_ref[...], w1_ref[...]) + b1_ref[...]))
    a = jnp.sin(fr * (_dot3(a, w2_ref[...]) + b2_ref[...]))
    a = jnp.sin(fr * (_dot3(a, w3_ref[...]) + b3_ref[...]))
    half = a.shape[0]
    ss = jnp.zeros(ss_ref.shape, F32)
    for k in range(2):
        h = _dot3(a, w4_ref[0, k]) * dec_ref[k * half:(k + 1) * half, :]
        h_ref[k * half:(k + 1) * half, :] = h
        ss = ss + jnp.sum(h * h, axis=0, keepdims=True)

    @pl.when(pl.program_id(0) == 0)
    def _():
        ss_ref[...] = jnp.zeros_like(ss_ref)

    ss_ref[...] += ss


def _filt_mlp(zp, w1p, b1, w2, b2, w3, b3, w4s, fr, dec2, rows):
    n, c = dec2.shape
    half_steps = (n // 2) // rows
    vec = lambda a: pl.BlockSpec(a.shape, lambda i: (0,) * a.ndim)
    return pl.pallas_call(
        _filt_mlp_kernel,
        grid=(n // rows,),
        in_specs=[pl.BlockSpec((rows // 2, zp.shape[1]), lambda i: (i, 0)),
                  vec(w1p), vec(b1), vec(w2), vec(b2), vec(w3), vec(b3),
                  pl.BlockSpec((1,) + w4s.shape[1:], lambda i: (i // half_steps, 0, 0, 0)),
                  vec(fr),
                  pl.BlockSpec((rows, c), lambda i: (i, 0))],
        out_specs=[pl.BlockSpec((rows, c), lambda i: (i, 0)),
                   pl.BlockSpec((1, c), lambda i: (0, 0))],
        out_shape=[jax.ShapeDtypeStruct((n, c), F32), jax.ShapeDtypeStruct((1, c), F32)],
        compiler_params=_params(("arbitrary",)),
        name="filt_mlp",
    )(zp, w1p, b1, w2, b2, w3, b3, w4s, fr, dec2)


def _filt_fft_kernel(h_ref, ss_ref, bias_ref, fa_ref, fb_ref, o_ref, zs_ref):
    scale = lax.rsqrt(ss_ref[...] + 1e-6)
    n1 = h_ref.shape[0] // FFT_N2
    row = lax.broadcasted_iota(jnp.int32, (2 * FFT_N2, LANES), 0)
    impulse = jnp.where(row < FFT_N2, bias_ref[...], 0.0)

    def stage_a(n2, carry):
        slab = h_ref[pl.ds(n2, n1, stride=FFT_N2), :]
        z = _dot(fa_ref[n2], slab.astype(BF16))
        zs_ref[pl.ds(n2, KP, stride=ZPITCH), :] = z[:KP]
        zs_ref[pl.ds(FFT_N2 + n2, KP, stride=ZPITCH), :] = z[KP:]
        return carry

    lax.fori_loop(0, FFT_N2, stage_a, 0, unroll=8)

    def stage_b(p, carry):
        b0 = pl.multiple_of(2 * p * ZPITCH, SUBLANES)
        b1 = pl.multiple_of(b0 + ZPITCH, SUBLANES)
        z = jnp.concatenate([zs_ref[pl.ds(b0, 2 * FFT_N2), :], zs_ref[pl.ds(b1, 2 * FFT_N2), :]], axis=1)
        x = _dot(fb_ref[...], z.astype(BF16))
        o_ref[0, 2 * p] = x[:, :LANES] * scale + impulse
        o_ref[0, 2 * p + 1] = x[:, LANES:] * scale + impulse
        return carry

    lax.fori_loop(0, KHP // 2, stage_b, 0, unroll=11)


def _filt_fft(h2u, ss, bias, fa_full, fb):
    n, c = h2u.shape
    nblk = c // LANES
    return pl.pallas_call(
        _filt_fft_kernel,
        grid=(nblk,),
        in_specs=[pl.BlockSpec((n, LANES), lambda j: (0, j), pipeline_mode=pl.Buffered(1)),
                  pl.BlockSpec((1, LANES), lambda j: (0, j)),
                  pl.BlockSpec((1, LANES), lambda j: (0, j)),
                  _const_spec(fa_full.shape),
                  _const_spec(fb.shape)],
        out_specs=pl.BlockSpec((1, KHP, 2 * FFT_N2, LANES), lambda j: (j, 0, 0, 0)),
        out_shape=jax.ShapeDtypeStruct((nblk, KHP, 2 * FFT_N2, LANES), F32),
        scratch_shapes=[pltpu.VMEM((KP * ZPITCH, LANES), F32)],
        compiler_params=_params(("arbitrary",)),
        name="filt_fft",
    )(h2u, ss, bias, fa_full, fb)


def _conv_slab(u_ref, cw_ref, j, n_slabs):
    r0 = pl.multiple_of(j * FFT_N2, FFT_N2)
    cur = u_ref[0, pl.ds(r0, FFT_N2), :].astype(F32)
    grp = 2 * SUBLANES
    pr0 = pl.multiple_of(jnp.maximum(j * FFT_N2 - grp, 0), grp)
    nr0 = pl.multiple_of(jnp.minimum((j + 1) * FFT_N2, (n_slabs - 1) * FFT_N2), grp)
    prev_row = u_ref[0, pl.ds(pr0, grp), :].astype(F32)[grp - 1:grp]
    next_row = u_ref[0, pl.ds(nr0, grp), :].astype(F32)[0:1]
    prev_row = jnp.where(j > 0, prev_row, 0.0)
    next_row = jnp.where(j < n_slabs - 1, next_row, 0.0)
    row = lax.broadcasted_iota(jnp.int32, cur.shape, 0)
    before = jnp.where(row == 0, prev_row, pltpu.roll(cur, 1, 0))
    after = jnp.where(row == FFT_N2 - 1, next_row, pltpu.roll(cur, FFT_N2 - 1, 0))
    w = cw_ref[...]
    return before * w[0:1] + cur * w[1:2] + after * w[2:3] + w[3:4]


def _hyena_kernel(x0_ref, x1_ref, v_ref, cw0_ref, cw1_ref, cwv_ref, hf_ref,
                  fa_ref, fai_ref, fb_ref, fbi_ref, o_ref, ts_ref, zs_ref):
    n_slabs = x0_ref.shape[1] // FFT_N2

    def gated_value(j):
        return _conv_slab(x1_ref, cw1_ref, j, n_slabs) * _conv_slab(v_ref, cwv_ref, j, n_slabs)

    def fill(j, carry):
        ts_ref[pl.ds(pl.multiple_of(j * TPITCH, SUBLANES), FFT_N2), :] = gated_value(j)
        return carry

    lax.fori_loop(0, n_slabs, fill, 0, unroll=2)

    def stage_a(n2, carry):
        slab = ts_ref[pl.ds(n2, n_slabs, stride=TPITCH), :]
        z = _dot(fa_ref[n2], slab.astype(BF16))
        zs_ref[pl.ds(n2, KP, stride=ZPITCH), :] = z[:KP]
        zs_ref[pl.ds(FFT_N2 + n2, KP, stride=ZPITCH), :] = z[KP:]
        return carry

    lax.fori_loop(0, FFT_N2, stage_a, 0, unroll=16)

    def stage_b(p, carry):
        b0 = pl.multiple_of(2 * p * ZPITCH, SUBLANES)
        b1 = pl.multiple_of(b0 + ZPITCH, SUBLANES)
        z = jnp.concatenate([zs_ref[pl.ds(b0, 2 * FFT_N2), :], zs_ref[pl.ds(b1, 2 * FFT_N2), :]], axis=1)
        x = _dot(fb_ref[...], z.astype(BF16))
        h = jnp.concatenate([hf_ref[0, 2 * p], hf_ref[0, 2 * p + 1]], axis=1)
        xr, xi = x[:FFT_N2], x[FFT_N2:]
        hr, hi = h[:FFT_N2], h[FFT_N2:]
        prod = jnp.concatenate([xr * hr - xi * hi, xr * hi + xi * hr], axis=0)
        y = _dot(fbi_ref[...], prod.astype(BF16))
        zs_ref[pl.ds(b0, 2 * FFT_N2), :] = y[:, :LANES]
        zs_ref[pl.ds(b1, 2 * FFT_N2), :] = y[:, LANES:]
        return carry

    lax.fori_loop(0, KHP // 2, stage_b, 0, unroll=11)

    def stage_ai(n2, carry):
        yr = zs_ref[pl.ds(n2, KP, stride=ZPITCH), :]
        yi = zs_ref[pl.ds(FFT_N2 + n2, KP, stride=ZPITCH), :]
        y = jnp.concatenate([yr, yi], axis=0).astype(BF16)
        ts_ref[pl.ds(n2, n_slabs, stride=TPITCH), :] = _dot(fai_ref[n2], y)
        return carry

    lax.fori_loop(0, FFT_N2, stage_ai, 0, unroll=16)

    def finish(j, carry):
        conv = ts_ref[pl.ds(pl.multiple_of(j * TPITCH, SUBLANES), FFT_N2), :]
        y = _conv_slab(x0_ref, cw0_ref, j, n_slabs) * conv
        o_ref[0, pl.ds(pl.multiple_of(j * FFT_N2, FFT_N2), FFT_N2), :] = y.astype(BF16)
        return carry

    lax.fori_loop(0, n_slabs, finish, 0, unroll=2)


def _hyena(u, cw, hf, fa, fai, fb, fbi):
    b, seq, c3 = u.shape
    nblk = D_HYENA // LANES
    n_slabs = seq // FFT_N2
    stream = lambda k: pl.BlockSpec((1, seq, LANES), lambda j, i, k=k: (i, 0, k * nblk + j))
    cwspec = lambda k: pl.BlockSpec((SUBLANES, LANES), lambda j, i, k=k: (0, k * nblk + j))
    return pl.pallas_call(
        _hyena_kernel,
        grid=(nblk, b),
        in_specs=[stream(0), stream(1), stream(2), cwspec(0), cwspec(1), cwspec(2),
                  pl.BlockSpec((1, KHP, 2 * FFT_N2, LANES), lambda j, i: (j, 0, 0, 0),
                               pipeline_mode=pl.Buffered(1)),
                  _const_spec(fa.shape), _const_spec(fai.shape),
                  _const_spec(fb.shape), _const_spec(fbi.shape)],
        out_specs=pl.BlockSpec((1, seq, LANES), lambda j, i: (i, 0, j)),
        out_shape=jax.ShapeDtypeStruct((b, seq, D_HYENA), BF16),
        scratch_shapes=[pltpu.VMEM((n_slabs * TPITCH, LANES), F32),
                        pltpu.VMEM((KP * ZPITCH, LANES), F32)],
        compiler_params=_params(("arbitrary", "arbitrary")),
        name="hyena",
    )(u, u, u, cw, cw, cw, hf, fa, fai, fb, fbi)


def _attn_kernel(sink_ref, q_ref, kp_ref, km_ref, kn_ref, vp_ref, vm_ref, vn_ref, kc_ref, vc_ref,
                 o_ref, ka_ref, va_ref):
    i = pl.program_id(1)
    n_i = pl.num_programs(1)
    tq = q_ref.shape[0]
    nqb = tq // BLOCK
    ka_ref[0:BLOCK] = kp_ref[...]
    ka_ref[BLOCK:BLOCK + tq] = km_ref[...]
    ka_ref[BLOCK + tq:] = kn_ref[...]
    va_ref[0:BLOCK] = vp_ref[...]
    va_ref[BLOCK:BLOCK + tq] = vm_ref[...]
    va_ref[BLOCK + tq:] = vn_ref[...]

    qi = lax.broadcasted_iota(jnp.int32, (BLOCK, BLOCK), 0)
    kj = lax.broadcasted_iota(jnp.int32, (BLOCK, BLOCK), 1)
    lane = lax.broadcasted_iota(jnp.int32, (BLOCK, LANES), 1)
    low = lane < HEAD_DIM
    half = GQA_GROUP // 2
    rows2 = half * BLOCK
    hrow = lax.broadcasted_iota(jnp.int32, (GQA_GROUP * BLOCK, 1), 0) // BLOCK
    head_order = [hh for hh in range(GQA_GROUP) if hh % 2 == 0] + [hh for hh in range(GQA_GROUP) if hh % 2 == 1]
    one = jnp.ones((), BF16)

    def with_ones(v):
        lanes_low = lax.broadcasted_iota(jnp.int32, v.shape, 1) < HEAD_DIM
        return jnp.where(lanes_low, v, one), jnp.where(lanes_low, one, v)

    for g in range(N_KV_HEADS):
        gl = slice(g * LANES, (g + 1) * LANES)
        vc_even, vc_odd = with_ones(vc_ref[:, gl])
        sink = jnp.zeros((GQA_GROUP * BLOCK, 1), F32)
        for pos, hh in enumerate(head_order):
            sink = jnp.where(hrow == pos, sink_ref[g * GQA_GROUP + hh], sink)
        for j in range(nqb):
            prev_ok = kj >= qi
            next_ok = kj <= qi
            if j == 0:
                prev_ok = prev_ok & (i > 0)
            if j == nqb - 1:
                next_ok = next_ok & (i < n_i - 1)
            bias_p = jnp.concatenate([jnp.where(prev_ok, 0.0, NEG_INF).astype(F32)] * GQA_GROUP, axis=0)
            bias_n = jnp.concatenate([jnp.where(next_ok, 0.0, NEG_INF).astype(F32)] * GQA_GROUP, axis=0)
            qb = q_ref[j * BLOCK:(j + 1) * BLOCK, :]
            parts = []
            for hh in head_order:
                h = g * GQA_GROUP + hh
                qp = qb[:, (h // 2) * LANES:(h // 2 + 1) * LANES]
                parts.append(jnp.where(low if h % 2 == 0 else ~low, qp, jnp.zeros_like(qp)))
            qs = jnp.concatenate(parts, axis=0)
            kw = ka_ref[j * BLOCK:(j + 3) * BLOCK, gl]
            vw_even, vw_odd = with_ones(va_ref[j * BLOCK:(j + 3) * BLOCK, gl])
            s_w = _dot_nt(qs, kw)
            s_p = s_w[:, :BLOCK] + bias_p
            s_m = s_w[:, BLOCK:2 * BLOCK]
            s_n = s_w[:, 2 * BLOCK:] + bias_n
            s_c = _dot_nt(qs, kc_ref[:, gl])
            m = jnp.maximum(jnp.maximum(jnp.max(jnp.maximum(jnp.maximum(s_p, s_m), s_n), axis=1, keepdims=True),
                                        jnp.max(s_c, axis=1, keepdims=True)), sink)
            e_w = jnp.concatenate([jnp.exp(s_p - m), jnp.exp(s_m - m), jnp.exp(s_n - m)], axis=1).astype(BF16)
            e_c = jnp.exp(s_c - m).astype(BF16)
            e_sink = jnp.exp(sink - m)
            outs = []
            for par, (vw, vcx) in enumerate(((vw_even, vc_even), (vw_odd, vc_odd))):
                rs = slice(par * rows2, (par + 1) * rows2)
                acc = _dot(e_w[rs], vw) + _dot(e_c[rs], vcx)
                den = pltpu.roll(acc, HEAD_DIM, 1) + e_sink[rs]
                outs.append(acc / den)
            for pp in range(half):
                pair = jnp.where(low, outs[0][pp * BLOCK:(pp + 1) * BLOCK], outs[1][pp * BLOCK:(pp + 1) * BLOCK])
                col = (g * half + pp) * LANES
                o_ref[j * BLOCK:(j + 1) * BLOCK, col:col + LANES] = pair.astype(BF16)


def _attention(sinks, q, kd, vd, kc, vc, batch, seq, tq):
    t = q.shape[0]
    per_b = seq // tq
    nqb = tq // BLOCK
    nb = seq // BLOCK
    n_ctx = kc.shape[0] // batch
    main = lambda b, i: (b * per_b + i, 0)
    prev = lambda b, i: (b * nb + jnp.maximum(i * nqb - 1, 0), 0)
    nxt = lambda b, i: (b * nb + jnp.minimum(i * nqb + nqb, nb - 1), 0)
    kvw = kd.shape[1]
    return pl.pallas_call(
        _attn_kernel,
        grid=(batch, per_b),
        in_specs=[pl.BlockSpec(memory_space=pltpu.SMEM),
                  pl.BlockSpec((tq, D_ATTN), main),
                  pl.BlockSpec((BLOCK, kvw), prev), pl.BlockSpec((tq, kvw), main), pl.BlockSpec((BLOCK, kvw), nxt),
                  pl.BlockSpec((BLOCK, kvw), prev), pl.BlockSpec((tq, kvw), main), pl.BlockSpec((BLOCK, kvw), nxt),
                  pl.BlockSpec((n_ctx, kvw), lambda b, i: (b, 0)),
                  pl.BlockSpec((n_ctx, kvw), lambda b, i: (b, 0))],
        out_specs=pl.BlockSpec((tq, D_ATTN), main),
        out_shape=jax.ShapeDtypeStruct((t, D_ATTN), BF16),
        scratch_shapes=[pltpu.VMEM((tq + 2 * BLOCK, kvw), BF16),
                        pltpu.VMEM((tq + 2 * BLOCK, kvw), BF16)],
        compiler_params=_params(("arbitrary", "arbitrary")),
        name="attention",
    )(sinks, q, kd, kd, kd, vd, vd, vd, kc, vc)


def _route(t, wr_hi_ref, wr_hl_ref, br_ref):
    th, tl = _split(t)
    both = _dot(th, wr_hl_ref[...])
    logits = both[:, :LANES] + both[:, LANES:] + _dot(tl, wr_hi_ref[...]) + br_ref[...]
    lane_i = lax.broadcasted_iota(jnp.int32, logits.shape, 1)
    lane = lane_i.astype(F32)
    grp_of_lane = (lane_i >> 2).astype(F32)
    ninf = -jnp.inf
    far = float(LANES)
    is_g = (lane_i >= N_EXPERTS) & (lane_i < N_EXPERTS + N_GROUPS)
    glog = jnp.where(is_g, logits, ninf)
    gmax = jnp.max(glog, axis=1, keepdims=True)
    gidx = jnp.min(jnp.where(glog == gmax, lane - float(N_EXPERTS), far), axis=1, keepdims=True)
    group_p = 1.0 / jnp.sum(jnp.exp(glog - gmax), axis=1, keepdims=True)
    in_grp = (lane_i < N_EXPERTS) & (grp_of_lane == gidx)
    elog = jnp.where(in_grp, logits, ninf)
    v1 = jnp.max(elog, axis=1, keepdims=True)
    i1 = jnp.min(jnp.where(elog == v1, lane, far), axis=1, keepdims=True)
    elog2 = jnp.where(lane == i1, ninf, elog)
    v2 = jnp.max(elog2, axis=1, keepdims=True)
    i2 = jnp.min(jnp.where(elog2 == v2, lane, far), axis=1, keepdims=True)
    e = jnp.exp(v2 - v1)
    w1 = group_p / (1.0 + e)
    w2 = group_p * e / (1.0 + e)
    gate = jnp.where(lane == i1, w1, 0.0) + jnp.where(lane == i2, w2, 0.0)
    return gate + jnp.where(lane_i == GROUP_LANE, gidx, 0.0)


def _mix_out_kernel(x_ref, yh_ref, ya_ref, sh1_ref, sc1_ref, g1_ref, sh2_ref, sc2_ref,
                    wg_ref, wbh_ref, wba_ref, wo_ref, lng_ref, lnb_ref, wrh_ref, wrhl_ref, br_ref,
                    x1_ref, gate_ref, *t_refs):
    d = x_ref.shape[1]
    for r0 in range(0, x_ref.shape[0], SUB_ROWS):
        rows = slice(r0, r0 + SUB_ROWS)
        x = x_ref[rows, :]
        h = (_standardize(x) * (1.0 + sc1_ref[0]) + sh1_ref[0]).astype(BF16)
        g_hy = jax.nn.sigmoid(_dot(h, wg_ref[:, :d]))
        merged = g_hy * _dot(yh_ref[rows, :], wbh_ref[...])
        g_at = jax.nn.sigmoid(_dot(h, wg_ref[:, d:]))
        merged = merged + g_at * _dot(ya_ref[rows, :], wba_ref[...])
        mix = _dot(merged.astype(BF16), wo_ref[...])
        x1 = _standardize(DEEPNORM_ALPHA * x + g1_ref[0] * mix) * lng_ref[...] + lnb_ref[...]
        x1_ref[rows, :] = x1
        t = _standardize(x1) * (1.0 + sc2_ref[0]) + sh2_ref[0]
        gate_ref[rows, :] = _route(t, wrh_ref, wrhl_ref, br_ref)
        for t_ref, piece in zip(t_refs, _pack_pieces(t)):
            t_ref[rows, :] = piece


def _mix_out(x2d, yh, ya, mods, w_g, w_bh, w_ba, w_o, ln_g, ln_b, wr_hi, wr_hl, br, seq, tm):
    t, d = x2d.shape
    per_b = seq // tm
    row = lambda i: (i, 0)
    mod = lambda i: (i // per_b, 0, 0)
    mspec = pl.BlockSpec((1, 1, d), mod)
    sh1, sc1, g1, sh2, sc2 = mods
    return pl.pallas_call(
        _mix_out_kernel,
        grid=(t // tm,),
        in_specs=[pl.BlockSpec((tm, d), row),
                  pl.BlockSpec((tm, yh.shape[1]), row),
                  pl.BlockSpec((tm, ya.shape[1]), row),
                  mspec, mspec, mspec, mspec, mspec,
                  _const_spec(w_g.shape), _const_spec(w_bh.shape), _const_spec(w_ba.shape),
                  _const_spec(w_o.shape), _const_spec(ln_g.shape), _const_spec(ln_b.shape),
                  _const_spec(wr_hi.shape), _const_spec(wr_hl.shape), _const_spec(br.shape)],
        out_specs=[pl.BlockSpec((tm, d), row), pl.BlockSpec((tm, LANES), row)]
        + [pl.BlockSpec((tm, SC_ROW_WORDS), row)] * N_PIECES,
        out_shape=[jax.ShapeDtypeStruct((t, d), F32), jax.ShapeDtypeStruct((t, LANES), F32)]
        + [jax.ShapeDtypeStruct((t, SC_ROW_WORDS), jnp.uint32)] * N_PIECES,
        compiler_params=_params(("arbitrary",)),
        name="mix_out",
    )(x2d, yh, ya, sh1, sc1, g1, sh2, sc2, w_g, w_bh, w_ba, w_o, ln_g, ln_b, wr_hi, wr_hl, br)


def _slots_kernel(g_ref, upper_ref, lower_ref, dest_ref, meta_ref):
    gsel = g_ref[...]
    dest = jnp.zeros(gsel.shape, F32)
    base = jnp.zeros((1, 1), F32)
    chunk_start = lax.broadcasted_iota(jnp.int32, (1, LANES), 1).astype(F32) * float(MOE_CHUNK)
    owner = jnp.zeros((1, LANES), F32)
    for g in range(N_GROUPS):
        onehot = jnp.where(gsel == float(g), 1.0, 0.0)
        in_row = _dot(onehot.astype(BF16), upper_ref[...])
        row_tot = jnp.sum(onehot, axis=1, keepdims=True)
        rows_before = _dot(lower_ref[...], jnp.broadcast_to(row_tot, onehot.shape).astype(BF16))
        dest = dest + onehot * (base + rows_before + in_row)
        if g > 0:
            owner = owner + jnp.where(chunk_start >= base, 1.0, 0.0)
        n_g = jnp.sum(row_tot, axis=0, keepdims=True)
        base = base + jnp.floor((n_g + float(MOE_CHUNK - 1)) * (1.0 / MOE_CHUNK)) * float(MOE_CHUNK)
    dest_ref[...] = dest.astype(jnp.int32)
    row = lax.broadcasted_iota(jnp.int32, meta_ref.shape, 0)
    meta_ref[...] = jnp.where(row == 0, owner, base * (1.0 / MOE_CHUNK)).astype(jnp.int32)


def _slots(gsel):
    r = gsel.shape[0]
    upper = jnp.asarray(np.triu(np.ones((LANES, LANES), np.float32), 1)).astype(BF16)
    lower = jnp.asarray(np.tril(np.ones((r, r), np.float32), -1)).astype(BF16)
    full = lambda a: pl.BlockSpec(a.shape, lambda i: (0,) * a.ndim)
    return pl.pallas_call(
        _slots_kernel,
        grid=(1,),
        in_specs=[full(gsel), full(upper), full(lower)],
        out_specs=[pl.BlockSpec((r, LANES), lambda i: (0, 0)), pl.BlockSpec((SUBLANES, LANES), lambda i: (0, 0))],
        out_shape=[jax.ShapeDtypeStruct((r, LANES), jnp.int32), jax.ShapeDtypeStruct((SUBLANES, LANES), jnp.int32)],
        compiler_params=_params(("arbitrary",)),
        name="moe_slots",
    )(gsel, upper, lower)


def _experts_kernel(owner_ref, used_ref, *refs):
    n_p = N_PIECES
    t_refs, gate_ref = refs[:n_p], refs[n_p]
    wg_ref, wu_ref, wd_ref = refs[n_p + 1:n_p + 4]
    y_refs = refs[n_p + 4:]
    c = pl.program_id(0)

    @pl.when(c < used_ref[0])
    def _():
        t = _unpack_pieces([r[...] for r in t_refs]).astype(BF16)
        gate = gate_ref[...]
        lane = lax.broadcasted_iota(jnp.int32, gate.shape, 1)
        first = owner_ref[c] * EXPERTS_PER_GROUP
        parts = []
        for e in range(EXPERTS_PER_GROUP):
            a = _dot(t, wg_ref[e])
            u = _dot(t, wu_ref[e])
            ge = jnp.sum(jnp.where(lane == first + e, gate, 0.0), axis=1, keepdims=True)
            parts.append((a * jax.nn.sigmoid(a) * u * ge).astype(BF16))
        y = _dot(jnp.concatenate(parts, axis=1), wd_ref[0])
        for y_ref, piece in zip(y_refs, _pack_pieces(y)):
            y_ref[...] = piece

    @pl.when(c >= used_ref[0])
    def _():
        for y_ref in y_refs:
            y_ref[...] = jnp.zeros_like(y_ref)


def _experts(owner, used, t_pieces, gate_sorted, wg16, wu16, wd3):
    n_slots = gate_sorted.shape[0]
    row = lambda c, owner, used: (c, 0)
    by_owner = lambda c, owner, used: (owner[c], 0, 0)
    piece = pl.BlockSpec((MOE_CHUNK, SC_ROW_WORDS), row)
    group_of_experts = lambda w: pl.BlockSpec((EXPERTS_PER_GROUP,) + w.shape[1:], by_owner)
    return pl.pallas_call(
        _experts_kernel,
        grid_spec=pltpu.PrefetchScalarGridSpec(
            num_scalar_prefetch=2,
            grid=(n_slots // MOE_CHUNK,),
            in_specs=[piece] * len(t_pieces) + [pl.BlockSpec((MOE_CHUNK, LANES), row),
                                                group_of_experts(wg16), group_of_experts(wu16),
                                                pl.BlockSpec((1,) + wd3.shape[1:], by_owner)],
            out_specs=[piece] * len(t_pieces),
        ),
        out_shape=[jax.ShapeDtypeStruct((n_slots, SC_ROW_WORDS), jnp.uint32)] * len(t_pieces),
        compiler_params=_params(("arbitrary",)),
        name="moe_experts",
    )(owner, used, *t_pieces, gate_sorted, wg16, wu16, wd3)


def _final_kernel(x1_ref, g2_ref, lng_ref, lnb_ref, *refs):
    y_refs, o_ref = refs[:N_PIECES], refs[-1]
    y = _unpack_pieces([r[...] for r in y_refs])
    o_ref[...] = _standardize(DEEPNORM_ALPHA * x1_ref[...] + g2_ref[0] * y) * lng_ref[...] + lnb_ref[...]


def _final(x1, g2, ln_g, ln_b, y_pieces, seq, tm, row0, out_so_far):
    t, d = x1.shape
    per_b = seq // tm
    first = row0 // tm
    row = lambda i: (i + first, 0)
    in_specs = ([pl.BlockSpec((tm, d), row), pl.BlockSpec((1, 1, d), lambda i: ((i + first) // per_b, 0, 0)),
                 _const_spec(ln_g.shape), _const_spec(ln_b.shape)]
                + [pl.BlockSpec((tm, SC_ROW_WORDS), lambda i: (i, 0))] * len(y_pieces))
    args = [x1, g2, ln_g, ln_b, *y_pieces]
    aliases = {}
    if out_so_far is not None:
        in_specs.append(pl.BlockSpec(memory_space=pl.ANY))
        aliases = {len(args): 0}
        args.append(out_so_far)
    return pl.pallas_call(
        _final_kernel,
        grid=(y_pieces[0].shape[0] // tm,),
        in_specs=in_specs,
        out_specs=pl.BlockSpec((tm, d), row),
        out_shape=jax.ShapeDtypeStruct((t, d), F32),
        input_output_aliases=aliases,
        compiler_params=_params(("arbitrary",)),
        name="moe_final",
    )(*args)


def _sc_move_rows(tables, idx, n_out, scatter, name, row0=0):
    n = idx.shape[0]
    mesh = plsc.VectorSubcoreMesh(core_axis_name="c", subcore_axis_name="s")
    out_type = [jax.ShapeDtypeStruct((n_out, t.shape[1]), t.dtype) for t in tables]
    window = lambda i: (i, 0)
    src_window = lambda i: (i + row0 // SC_WINDOW, 0)
    index_win = pl.BlockSpec((1, SC_WINDOW), lambda i: (0, i))
    split = dict(core_axis_name=("c", "s"), dimension_semantics=(pltpu.PARALLEL,))

    @functools.partial(pl.kernel, out_type=out_type, mesh=mesh, scratch_types=[], name=name)
    def move(*refs):
        srcs, i_hbm, dsts = refs[:len(tables)], refs[len(tables)], refs[len(tables) + 1:]
        for src, dst in zip(srcs, dsts):
            rows_win = pl.BlockSpec((SC_WINDOW, src.shape[1]), window)
            if scatter:
                def body(x_vmem, i_vmem, dst=dst):
                    pltpu.sync_copy(x_vmem, dst.at[i_vmem.at[0]])

                pltpu.emit_pipeline(body, grid=(n // SC_WINDOW,),
                                    in_specs=[pl.BlockSpec((SC_WINDOW, src.shape[1]), src_window), index_win],
                                    out_specs=[], **split)(src, i_hbm)
            else:
                def body(i_vmem, o_vmem, src=src):
                    pltpu.sync_copy(src.at[i_vmem.at[0]], o_vmem)

                pltpu.emit_pipeline(body, grid=(n // SC_WINDOW,), in_specs=[index_win],
                                    out_specs=[rows_win], **split)(i_hbm, dst)

    return move(*tables, idx.reshape(1, n))


def _dft_tables(seq):
    n = 2 * seq
    n1_full = n // FFT_N2
    n1_data = seq // FFT_N2
    k1 = np.arange(KH, dtype=np.float64)[None, :, None]
    n2 = np.arange(FFT_N2, dtype=np.float64)[:, None, None]
    n1 = np.arange(n1_full, dtype=np.float64)[None, None, :]
    ang = 2.0 * np.pi * k1 * (FFT_N2 * n1 + n2) / n
    fa = np.zeros((FFT_N2, 2 * KP, n1_full))
    fa[:, :KH] = np.cos(ang)
    fa[:, KP:KP + KH] = -np.sin(ang)
    wgt = np.full((KH,), 2.0)
    wgt[0] = 1.0
    wgt[KH - 1] = 1.0
    fai = np.zeros((FFT_N2, n1_data, 2 * KP))
    angt = np.transpose(ang[:, :, :n1_data], (0, 2, 1))
    fai[:, :, :KH] = np.cos(angt) * wgt / n
    fai[:, :, KP:KP + KH] = -np.sin(angt) * wgt / n
    kk = np.arange(FFT_N2, dtype=np.float64)
    a2 = 2.0 * np.pi * np.outer(kk, kk) / FFT_N2
    fr, fi = np.cos(a2), -np.sin(a2)
    fb = np.block([[fr, -fi], [fi, fr]])
    fbi = np.block([[fr, fi], [-fi, fr]])
    f32 = lambda a: jnp.asarray(a.astype(np.float32))
    return f32(fa[:, :, :n1_data]), f32(fa), f32(fai), f32(fb), f32(fbi)


def _filter_features(seq, rows):
    f32 = np.float32
    t = np.linspace(0.0, 1.0, seq, dtype=f32)
    w = (f32(2.0 * math.pi) * np.arange(seq, dtype=f32) / f32(seq)).astype(f32)
    t2 = np.concatenate([t, t[::-1]])
    w2 = np.concatenate([w, w[::-1]])
    bands = np.linspace(1e-4, FILTER_BANDS - 1, FILTER_BANDS, dtype=f32)
    max_decay = math.log(DECAY_TARGET) / FAST_DECAY_PCT
    min_decay = math.log(DECAY_TARGET) / SLOW_DECAY_PCT
    deltas = jnp.linspace(min_decay, max_decay, D_HYENA, dtype=F32)

    def feats(k):
        tp = t2.reshape(-1, 2, rows // 2)[:, k].reshape(-1, 1)
        wp = w2.reshape(-1, 2, rows // 2)[:, k].reshape(-1, 1)
        pad = np.zeros((tp.shape[0], FILTER_ORDER - FILTER_EMB), f32)
        arg = (bands * wp).astype(f32)
        return np.concatenate([tp, np.cos(arg), -np.sin(arg), pad], axis=-1).astype(f32)

    zp = jnp.asarray(np.concatenate([feats(0), feats(1)], axis=-1))
    decay = jnp.exp(-jnp.asarray(t2)[:, None] * jnp.abs(deltas))
    return zp, decay


def _rope_tables(seq):
    f32 = np.float32
    rows = seq // GRID_W
    row = np.repeat(np.arange(rows, dtype=f32), GRID_W)
    col = np.tile(np.arange(GRID_W, dtype=f32), rows)
    half = HEAD_DIM // 2
    inv_freq = (f32(ROPE_BASE) ** (-np.arange(0, half, 2, dtype=f32) / f32(half))).astype(f32)
    ang = np.concatenate([row[:, None] * inv_freq, col[:, None] * inv_freq], axis=-1).astype(f32)
    cos, sin = np.cos(ang), np.sin(ang)
    c64 = np.concatenate([cos, cos], axis=-1)
    s64 = np.concatenate([-sin, sin], axis=-1)
    return (jnp.asarray(np.concatenate([c64, c64], axis=-1).astype(f32)),
            jnp.asarray(np.concatenate([s64, s64], axis=-1).astype(f32)))


def _head_perm(n_heads):
    idx = []
    for h in range(n_heads):
        base = h * HEAD_DIM
        idx += [base + 2 * j for j in range(HEAD_DIM // 2)]
        idx += [base + 2 * j + 1 for j in range(HEAD_DIM // 2)]
    return np.asarray(idx, dtype=np.int32)


def _dup_heads(w):
    parts = []
    for g in range(N_KV_HEADS):
        blk = w[:, g * HEAD_DIM:(g + 1) * HEAD_DIM]
        parts += [blk, blk]
    return jnp.concatenate(parts, axis=1)


def kernel(x, c, ctx, c_ctx, ada_w, ada_b, w_in, hy_conv_w, hy_conv_b, hy_w1, hy_b1, hy_w2, hy_b2, hy_w3, hy_b3, hy_w4, hy_freq, hy_bias, attn_sinks, w_branch_hy, w_branch_attn, w_out, ln1_g, ln1_b, w_group, b_group, w_router, b_router, w_gate_e, w_up_e, w_down_e, ln2_g, ln2_b):
    batch, seq, d = x.shape
    n_ctx = ctx.shape[1]
    assert d == D_MODEL and ada_w.shape[0] == DEPTH == 1
    assert 2 * seq == FFT_N2 * FFT_N2 and seq % TM_PROJ == 0
    l = 0

    w = w_in[l]
    s0, s1, s2, s3 = 3 * D_HYENA, 3 * D_HYENA + D_ATTN, 3 * D_HYENA + D_ATTN + D_KV, 3 * D_HYENA + D_ATTN + 2 * D_KV
    w_q = w[:, s0:s1][:, _head_perm(N_HEADS)]
    w_k = _dup_heads(w[:, s1:s2][:, _head_perm(N_KV_HEADS)])
    w_v = _dup_heads(w[:, s2:s3])
    w_c = jnp.concatenate([w[:, :s0], w_q, w_k, w_v], axis=1).astype(BF16)
    w_kv = jnp.concatenate([w_k, w_v], axis=1).astype(BF16)
    w_g = w[:, s3:].astype(BF16)
    w_bh = w_branch_hy[l].astype(BF16)
    w_ba = w_branch_attn[l].astype(BF16)
    w_o = w_out[l].astype(BF16)
    wr = jnp.zeros((d, LANES), F32)
    wr = wr.at[:, :N_EXPERTS].set(w_router[l]).at[:, N_EXPERTS:N_EXPERTS + N_GROUPS].set(w_group[l])
    wr_hi = wr.astype(BF16)
    wr_hl = jnp.concatenate([wr_hi, (wr - wr_hi.astype(F32)).astype(BF16)], axis=1)
    br = jnp.zeros((1, LANES), F32)
    br = br.at[0, :N_EXPERTS].set(b_router[l]).at[0, N_EXPERTS:N_EXPERTS + N_GROUPS].set(b_group[l])
    wg16 = w_gate_e[l].astype(BF16)
    wu16 = w_up_e[l].astype(BF16)
    wd3 = w_down_e[l].astype(BF16).reshape(N_GROUPS, EXPERTS_PER_GROUP * D_EXPERT, d)
    cw = jnp.concatenate([hy_conv_w[l][:, 0, :], hy_conv_b[l][None, :],
                          jnp.zeros((SUBLANES - SHORT_CONV - 1, 3 * D_HYENA), F32)], axis=0)
    row2 = lambda a: a.reshape(1, -1)
    pair = lambda a: jnp.concatenate([a, a], axis=-1)
    zero_o = jnp.zeros((FILTER_ORDER, FILTER_ORDER), F32)
    bdiag = lambda a: jnp.concatenate([jnp.concatenate([a, zero_o], axis=1),
                                       jnp.concatenate([zero_o, a], axis=1)], axis=0)
    w1p = bdiag(jnp.concatenate([hy_w1[l], jnp.zeros((FILTER_ORDER - FILTER_EMB, FILTER_ORDER), F32)], axis=0))
    w4h = jnp.transpose(hy_w4[l].reshape(FILTER_ORDER, 2, D_HYENA), (1, 0, 2))
    zero_w4 = jnp.zeros_like(w4h)
    w4s = jnp.stack([jnp.concatenate([w4h, zero_w4], axis=1),
                     jnp.concatenate([zero_w4, w4h], axis=1)], axis=1)

    fa, fa_full, fai, fb, fbi = _dft_tables(seq)
    fa, fa_full, fai, fb, fbi = (a.astype(BF16) for a in (fa, fa_full, fai, fb, fbi))
    zp, dec2 = _filter_features(seq, FILT_ROWS)
    cos_t, sin_t = _rope_tables(seq)

    cond = jnp.concatenate([c, c_ctx[None], jnp.zeros((SUBLANES - batch - 1, d), F32)], axis=0)
    mods = _adaln(cond, ada_w[l], ada_b[l])
    m6 = [mods[:, k * d:(k + 1) * d].reshape(SUBLANES, 1, d) for k in range(6)]
    sh1, sc1, g1, sh2, sc2, g2 = m6

    kc, vc = _ctx_kv(ctx.reshape(batch * n_ctx, d), sh1[batch], sc1[batch], w_kv, n_ctx)

    x2d = x.reshape(batch * seq, d)
    u, q, kd, vd = _in_proj(x2d, sh1, sc1, w_c, cos_t, sin_t, seq, TM_PROJ)
    h2u, ss = _filt_mlp(zp, w1p, pair(row2(hy_b1[l])), bdiag(hy_w2[l]), pair(row2(hy_b2[l])),
                        bdiag(hy_w3[l]), pair(row2(hy_b3[l])), w4s, pair(row2(hy_freq[l])), dec2, FILT_ROWS)
    hf = _filt_fft(h2u, ss, row2(hy_bias[l]), fa_full, fb)
    y_hy = _hyena(u.reshape(batch, seq, 3 * D_HYENA), cw, hf, fa, fai, fb, fbi)
    y_at = _attention(attn_sinks[l], q, kd, vd, kc, vc, batch, seq, TQ_ATTN)
    x1, gate, *t_pieces = _mix_out(x2d, y_hy.reshape(batch * seq, D_HYENA), y_at, (sh1, sc1, g1, sh2, sc2),
                                   w_g, w_bh, w_ba, w_o, row2(ln1_g[l]), row2(ln1_b[l]), wr_hi, wr_hl, br,
                                   seq, TM_PROJ)

    n_tok = batch * seq
    n_part = n_tok // MOE_PARTS
    n_slots = n_part + N_GROUPS * MOE_CHUNK
    gsel = gate[:, GROUP_LANE].reshape(MOE_PARTS, n_part // LANES, LANES)
    out = None
    for part in range(MOE_PARTS):
        row0 = part * n_part
        dest2d, meta = _slots(gsel[part])
        dest = dest2d.reshape(n_part)
        *t_sorted, gate_sorted = _sc_move_rows(t_pieces + [gate], dest, n_slots, True, "moe_sort", row0)
        y_sorted = _experts(meta[0], meta[1, :1], t_sorted, gate_sorted, wg16, wu16, wd3)
        y_pieces = _sc_move_rows(y_sorted, dest, n_part, False, "moe_unsort")
        out = _final(x1, g2, row2(ln2_g[l]), row2(ln2_b[l]), y_pieces, seq, TM_FINAL, row0, out)
    return out.reshape(batch, seq, d)
```

```python
import functools
import math

import numpy as np
import jax
import jax.numpy as jnp
from jax import lax
from jax.experimental import pallas as pl
from jax.experimental.pallas import tpu as pltpu
from jax.experimental.pallas import tpu_sc as plsc

F32 = jnp.float32
BF16 = jnp.bfloat16

D_MODEL = 1024
GRID_W = 64
D_HYENA = D_MODEL // 2
SHORT_CONV = 3
FILTER_BANDS = 16
FILTER_EMB = 1 + 2 * FILTER_BANDS
FILTER_ORDER = 64
DECAY_TARGET = 1e-2
FAST_DECAY_PCT = 0.3
SLOW_DECAY_PCT = 1.5
HEAD_DIM = 64
D_ATTN = D_MODEL // 2
N_HEADS = D_ATTN // HEAD_DIM
N_KV_HEADS = N_HEADS // 4
GQA_GROUP = N_HEADS // N_KV_HEADS
D_KV = N_KV_HEADS * HEAD_DIM
WINDOW = 128
BLOCK = 128
ROPE_BASE = 10000.0
NEG_INF = -1e30
N_GROUPS = 4
EXPERTS_PER_GROUP = 4
N_EXPERTS = N_GROUPS * EXPERTS_PER_GROUP
D_EXPERT = D_MODEL // 4
LN_EPS = 1e-5
DEPTH = 1
DEEPNORM_ALPHA = (2.0 * DEPTH) ** 0.25

LANES = 128
SUBLANES = 8
VMEM_LIMIT = 56 * 1024 * 1024

SUB_ROWS = 512
TM_PROJ = 1024
MOE_CHUNK = 512
TM_FINAL = 1024
GROUP_LANE = N_EXPERTS
SC_WINDOW = 128
SC_ROW_WORDS = 256
N_PIECES = D_MODEL // (2 * SC_ROW_WORDS)
TQ_ATTN = 512
FILT_ROWS = 2048

FFT_N2 = 128
KH = 65
KHP = 66
KP = 72
ZPITCH = 2 * FFT_N2 + SUBLANES
TPITCH = FFT_N2 + SUBLANES


def _dot(a, b):
    return jnp.dot(a, b, preferred_element_type=F32)


def _dot_nt(a, b):
    return lax.dot_general(a, b, (((1,), (1,)), ((), ())), preferred_element_type=F32)


def _split(a):
    hi = a.astype(BF16)
    lo = (a - hi.astype(F32)).astype(BF16)
    return hi, lo


def _pack_pieces(x):
    w = SC_ROW_WORDS
    pieces = []
    for p in range(x.shape[1] // (2 * w)):
        hi = lax.bitcast_convert_type(x[:, 2 * p * w:(2 * p + 1) * w].astype(BF16).astype(F32), jnp.uint32)
        lo = lax.bitcast_convert_type(x[:, (2 * p + 1) * w:(2 * p + 2) * w].astype(BF16).astype(F32), jnp.uint32)
        pieces.append(hi | (lo >> 16))
    return pieces


def _unpack_pieces(pieces):
    cols = []
    for word in pieces:
        cols.append(lax.bitcast_convert_type(word & jnp.uint32(0xFFFF0000), F32))
        cols.append(lax.bitcast_convert_type(word << 16, F32))
    return jnp.concatenate(cols, axis=1)


def _dot3(a, b):
    ah, al = _split(a)
    bh, bl = _split(b)
    return _dot(ah, bh) + _dot(al, bh) + _dot(ah, bl)


def _standardize(x):
    mu = jnp.mean(x, axis=-1, keepdims=True)
    xc = x - mu
    var = jnp.mean(xc * xc, axis=-1, keepdims=True)
    return xc * lax.rsqrt(var + LN_EPS)


def _params(sem, vmem=VMEM_LIMIT):
    return pltpu.CompilerParams(dimension_semantics=sem, vmem_limit_bytes=vmem)


def _const_spec(shape):
    nd = len(shape)
    return pl.BlockSpec(shape, lambda *_: (0,) * nd, pipeline_mode=pl.Buffered(1))


def _adaln_kernel(c_ref, w_ref, b_ref, o_ref):
    s = c_ref[...]
    s = s * jax.nn.sigmoid(s)
    o_ref[...] = _dot3(s, w_ref[...]) + b_ref[...]


def _adaln(cond, w, b):
    n, d = cond.shape
    cols = w.shape[1]
    bc = 1024
    return pl.pallas_call(
        _adaln_kernel,
        grid=(cols // bc,),
        in_specs=[pl.BlockSpec((n, d), lambda j: (0, 0)),
                  pl.BlockSpec((d, bc), lambda j: (0, j)),
                  pl.BlockSpec((1, bc), lambda j: (0, j))],
        out_specs=pl.BlockSpec((n, bc), lambda j: (0, j)),
        out_shape=jax.ShapeDtypeStruct((n, cols), F32),
        compiler_params=_params(("arbitrary",)),
        name="adaln",
    )(cond, w, b.reshape(1, cols))


def _ctx_kv_kernel(x_ref, sh_ref, sc_ref, w_ref, k_ref, v_ref):
    h = _standardize(x_ref[...]) * (1.0 + sc_ref[...]) + sh_ref[...]
    kv = _dot(h.astype(BF16), w_ref[...])
    half = k_ref.shape[1]
    k_ref[...] = kv[:, :half].astype(BF16)
    v_ref[...] = kv[:, half:].astype(BF16)


def _ctx_kv(ctx2d, sh, sc, w_kv, rows):
    n, d = ctx2d.shape
    half = w_kv.shape[1] // 2
    return pl.pallas_call(
        _ctx_kv_kernel,
        grid=(n // rows,),
        in_specs=[pl.BlockSpec((rows, d), lambda i: (i, 0)),
                  pl.BlockSpec((1, d), lambda i: (0, 0)),
                  pl.BlockSpec((1, d), lambda i: (0, 0)),
                  pl.BlockSpec(w_kv.shape, lambda i: (0, 0))],
        out_specs=[pl.BlockSpec((rows, half), lambda i: (i, 0)),
                   pl.BlockSpec((rows, half), lambda i: (i, 0))],
        out_shape=[jax.ShapeDtypeStruct((n, half), BF16)] * 2,
        compiler_params=_params(("arbitrary",)),
        name="ctx_kv",
    )(ctx2d, sh, sc, w_kv)


def _rope(x, cos_t, sin_t):
    width = x.shape[1]
    reps = width // LANES
    c = jnp.concatenate([cos_t] * reps, axis=1)
    s = jnp.concatenate([sin_t] * reps, axis=1)
    half = HEAD_DIM // 2
    lane = lax.broadcasted_iota(jnp.int32, x.shape, 1)
    first_half = (lane & (HEAD_DIM - 1)) < half
    partner = jnp.where(first_half, pltpu.roll(x, width - half, 1), pltpu.roll(x, half, 1))
    return x * c + partner * s


def _in_proj_kernel(x_ref, sh_ref, sc_ref, w_ref, cos_ref, sin_ref, u_ref, q_ref, k_ref, v_ref):
    n_u = u_ref.shape[1]
    n_q = q_ref.shape[1]
    n_k = k_ref.shape[1]
    for r0 in range(0, x_ref.shape[0], SUB_ROWS):
        rows = slice(r0, r0 + SUB_ROWS)
        h = (_standardize(x_ref[rows, :]) * (1.0 + sc_ref[0]) + sh_ref[0]).astype(BF16)
        u_ref[rows, :] = _dot(h, w_ref[:, :n_u]).astype(BF16)
        cos_t = cos_ref[rows, :]
        sin_t = sin_ref[rows, :]
        q = _dot(h, w_ref[:, n_u:n_u + n_q])
        q_ref[rows, :] = (_rope(q, cos_t, sin_t) * (HEAD_DIM ** -0.5)).astype(BF16)
        k = _dot(h, w_ref[:, n_u + n_q:n_u + n_q + n_k])
        k_ref[rows, :] = _rope(k, cos_t, sin_t).astype(BF16)
        v_ref[rows, :] = _dot(h, w_ref[:, n_u + n_q + n_k:]).astype(BF16)


def _in_proj(x2d, sh, sc, w_c, cos_t, sin_t, seq, tm):
    t, d = x2d.shape
    per_b = seq // tm
    n_u, n_q, n_k = 3 * D_HYENA, D_ATTN, 2 * D_KV
    row = lambda i: (i, 0)
    mod = lambda i: (i // per_b, 0, 0)
    pos = lambda i: (i % per_b, 0)
    return pl.pallas_call(
        _in_proj_kernel,
        grid=(t // tm,),
        in_specs=[pl.BlockSpec((tm, d), row),
                  pl.BlockSpec((1, 1, d), mod),
                  pl.BlockSpec((1, 1, d), mod),
                  _const_spec(w_c.shape),
                  pl.BlockSpec((tm, LANES), pos),
                  pl.BlockSpec((tm, LANES), pos)],
        out_specs=[pl.BlockSpec((tm, n_u), row),
                   pl.BlockSpec((tm, n_q), row),
                   pl.BlockSpec((tm, n_k), row),
                   pl.BlockSpec((tm, n_k), row)],
        out_shape=[jax.ShapeDtypeStruct((t, n_u), BF16),
                   jax.ShapeDtypeStruct((t, n_q), BF16),
                   jax.ShapeDtypeStruct((t, n_k), BF16),
                   jax.ShapeDtypeStruct((t, n_k), BF16)],
        compiler_params=_params(("arbitrary",)),
        name="in_proj",
    )(x2d, sh, sc, w_c, cos_t, sin_t)


def _filt_mlp_kernel(z_ref, w1_ref, b1_ref, w2_ref, b2_ref, w3_ref, b3_ref, w4_ref, fr_ref, t_ref, absd_ref,
                     h_ref, ss_ref):
    fr = fr_ref[...]
    a = jnp.sin(fr * (_dot3(z_ref[...], w1_ref[...]) + b1_ref[...]))
    a = jnp.sin(fr * (_dot3(a, w2_ref[...]) + b2_ref[...]))
    a = jnp.sin(fr * (_dot3(a, w3_ref[...]) + b3_ref[...]))
    half = a.shape[0]
    ss = jnp.zeros(ss_ref.shape, F32)
    for k in range(2):
        decay = jnp.exp(-t_ref[k * half:(k + 1) * half, :] * absd_ref[...])
        h = _dot3(a, w4_ref[0, k]) * decay
        h_ref[k * half:(k + 1) * half, :] = h
        ss = ss + jnp.sum(h * h, axis=0, keepdims=True)

    @pl.when(pl.program_id(0) == 0)
    def _():
        ss_ref[...] = jnp.zeros_like(ss_ref)

    ss_ref[...] += ss


def _filt_mlp(zp, w1p, b1, w2, b2, w3, b3, w4s, fr, t_col, absd, rows):
    n, c = t_col.shape[0], absd.shape[1]
    half_steps = (n // 2) // rows
    vec = lambda a: pl.BlockSpec(a.shape, lambda i: (0,) * a.ndim)
    return pl.pallas_call(
        _filt_mlp_kernel,
        grid=(n // rows,),
        in_specs=[pl.BlockSpec((rows // 2, zp.shape[1]), lambda i: (i, 0)),
                  vec(w1p), vec(b1), vec(w2), vec(b2), vec(w3), vec(b3),
                  pl.BlockSpec((1,) + w4s.shape[1:], lambda i: (i // half_steps, 0, 0, 0)),
                  vec(fr),
                  pl.BlockSpec((rows, 1), lambda i: (i, 0)),
                  vec(absd)],
        out_specs=[pl.BlockSpec((rows, c), lambda i: (i, 0)),
                   pl.BlockSpec((1, c), lambda i: (0, 0))],
        out_shape=[jax.ShapeDtypeStruct((n, c), F32), jax.ShapeDtypeStruct((1, c), F32)],
        compiler_params=_params(("arbitrary",)),
        name="filt_mlp",
    )(zp, w1p, b1, w2, b2, w3, b3, w4s, fr, t_col, absd)


def _filt_fft_kernel(h_ref, ss_ref, bias_ref, fa_ref, fb_ref, o_ref, zs_ref):
    scale = lax.rsqrt(ss_ref[...] + 1e-6)
    n1 = h_ref.shape[0] // FFT_N2
    row = lax.broadcasted_iota(jnp.int32, (2 * FFT_N2, LANES), 0)
    impulse = jnp.where(row < FFT_N2, bias_ref[...], 0.0)

    def stage_a(n2, carry):
        slab = h_ref[pl.ds(n2, n1, stride=FFT_N2), :]
        z = _dot(fa_ref[n2], slab.astype(BF16))
        zs_ref[pl.ds(n2, KP, stride=ZPITCH), :] = z[:KP]
        zs_ref[pl.ds(FFT_N2 + n2, KP, stride=ZPITCH), :] = z[KP:]
        return carry

    lax.fori_loop(0, FFT_N2, stage_a, 0, unroll=8)

    def stage_b(p, carry):
        b0 = pl.multiple_of(2 * p * ZPITCH, SUBLANES)
        b1 = pl.multiple_of(b0 + ZPITCH, SUBLANES)
        z = jnp.concatenate([zs_ref[pl.ds(b0, 2 * FFT_N2), :], zs_ref[pl.ds(b1, 2 * FFT_N2), :]], axis=1)
        x = _dot(fb_ref[...], z.astype(BF16))
        o_ref[0, 2 * p] = x[:, :LANES] * scale + impulse
        o_ref[0, 2 * p + 1] = x[:, LANES:] * scale + impulse
        return carry

    lax.fori_loop(0, KHP // 2, stage_b, 0, unroll=11)


def _filt_fft(h2u, ss, bias, fa_full, fb):
    n, c = h2u.shape
    nblk = c // LANES
    return pl.pallas_call(
        _filt_fft_kernel,
        grid=(nblk,),
        in_specs=[pl.BlockSpec((n, LANES), lambda j: (0, j), pipeline_mode=pl.Buffered(1)),
                  pl.BlockSpec((1, LANES), lambda j: (0, j)),
                  pl.BlockSpec((1, LANES), lambda j: (0, j)),
                  _const_spec(fa_full.shape),
                  _const_spec(fb.shape)],
        out_specs=pl.BlockSpec((1, KHP, 2 * FFT_N2, LANES), lambda j: (j, 0, 0, 0)),
        out_shape=jax.ShapeDtypeStruct((nblk, KHP, 2 * FFT_N2, LANES), F32),
        scratch_shapes=[pltpu.VMEM((KP * ZPITCH, LANES), F32)],
        compiler_params=_params(("arbitrary",)),
        name="filt_fft",
    )(h2u, ss, bias, fa_full, fb)


def _conv_slab(u_ref, cw_ref, j, n_slabs):
    r0 = pl.multiple_of(j * FFT_N2, FFT_N2)
    cur = u_ref[0, pl.ds(r0, FFT_N2), :].astype(F32)
    grp = 2 * SUBLANES
    pr0 = pl.multiple_of(jnp.maximum(j * FFT_N2 - grp, 0), grp)
    nr0 = pl.multiple_of(jnp.minimum((j + 1) * FFT_N2, (n_slabs - 1) * FFT_N2), grp)
    prev_row = u_ref[0, pl.ds(pr0, grp), :].astype(F32)[grp - 1:grp]
    next_row = u_ref[0, pl.ds(nr0, grp), :].astype(F32)[0:1]
    prev_row = jnp.where(j > 0, prev_row, 0.0)
    next_row = jnp.where(j < n_slabs - 1, next_row, 0.0)
    row = lax.broadcasted_iota(jnp.int32, cur.shape, 0)
    before = jnp.where(row == 0, prev_row, pltpu.roll(cur, 1, 0))
    after = jnp.where(row == FFT_N2 - 1, next_row, pltpu.roll(cur, FFT_N2 - 1, 0))
    w = cw_ref[...]
    return before * w[0:1] + cur * w[1:2] + after * w[2:3] + w[3:4]


def _hyena_kernel(x0_ref, x1_ref, v_ref, cw0_ref, cw1_ref, cwv_ref, hf_ref,
                  fa_ref, fai_ref, fb_ref, fbi_ref, o_ref, ts_ref, zs_ref):
    n_slabs = x0_ref.shape[1] // FFT_N2

    def gated_value(j):
        return _conv_slab(x1_ref, cw1_ref, j, n_slabs) * _conv_slab(v_ref, cwv_ref, j, n_slabs)

    def fill(j, carry):
        ts_ref[pl.ds(pl.multiple_of(j * TPITCH, SUBLANES), FFT_N2), :] = gated_value(j)
        return carry

    lax.fori_loop(0, n_slabs, fill, 0, unroll=2)

    def stage_a(n2, carry):
        sa = ts_ref[pl.ds(n2, n_slabs, stride=TPITCH), :].astype(BF16)
        sb = ts_ref[pl.ds(n2 + 1, n_slabs, stride=TPITCH), :].astype(BF16)
        zero = jnp.zeros_like(sa)
        rhs = jnp.concatenate([jnp.concatenate([sa, zero], axis=1), jnp.concatenate([zero, sb], axis=1)], axis=0)
        z = _dot(fa_ref[n2 // 2], rhs)
        for k in range(2):
            zk = z[:, k * LANES:(k + 1) * LANES]
            zs_ref[pl.ds(n2 + k, KP, stride=ZPITCH), :] = zk[:KP]
            zs_ref[pl.ds(FFT_N2 + n2 + k, KP, stride=ZPITCH), :] = zk[KP:]
        return carry

    lax.fori_loop(0, FFT_N2 // 2, lambda p, c: stage_a(2 * p, c), 0, unroll=8)

    def stage_b(p, carry):
        b0 = pl.multiple_of(2 * p * ZPITCH, SUBLANES)
        b1 = pl.multiple_of(b0 + ZPITCH, SUBLANES)
        z = jnp.concatenate([zs_ref[pl.ds(b0, 2 * FFT_N2), :], zs_ref[pl.ds(b1, 2 * FFT_N2), :]], axis=1)
        x = _dot(fb_ref[...], z.astype(BF16))
        h = jnp.concatenate([hf_ref[0, 2 * p], hf_ref[0, 2 * p + 1]], axis=1)
        xr, xi = x[:FFT_N2], x[FFT_N2:]
        hr, hi = h[:FFT_N2], h[FFT_N2:]
        prod = jnp.concatenate([xr * hr - xi * hi, xr * hi + xi * hr], axis=0)
        y = _dot(fbi_ref[...], prod.astype(BF16))
        zs_ref[pl.ds(b0, 2 * FFT_N2), :] = y[:, :LANES]
        zs_ref[pl.ds(b1, 2 * FFT_N2), :] = y[:, LANES:]
        return carry

    lax.fori_loop(0, KHP // 2, stage_b, 0, unroll=11)

    def stage_ai(n2, carry):
        yr = zs_ref[pl.ds(n2, KP, stride=ZPITCH), :]
        yi = zs_ref[pl.ds(FFT_N2 + n2, KP, stride=ZPITCH), :]
        y = jnp.concatenate([yr, yi], axis=0).astype(BF16)
        ts_ref[pl.ds(n2, n_slabs, stride=TPITCH), :] = _dot(fai_ref[n2], y)
        return carry

    lax.fori_loop(0, FFT_N2, stage_ai, 0, unroll=16)

    def finish(j, carry):
        conv = ts_ref[pl.ds(pl.multiple_of(j * TPITCH, SUBLANES), FFT_N2), :]
        y = _conv_slab(x0_ref, cw0_ref, j, n_slabs) * conv
        o_ref[0, pl.ds(pl.multiple_of(j * FFT_N2, FFT_N2), FFT_N2), :] = y.astype(BF16)
        return carry

    lax.fori_loop(0, n_slabs, finish, 0, unroll=2)


def _hyena(u, cw, hf, fa, fai, fb, fbi):
    b, seq, c3 = u.shape
    nblk = D_HYENA // LANES
    n_slabs = seq // FFT_N2
    stream = lambda k: pl.BlockSpec((1, seq, LANES), lambda j, i, k=k: (i, 0, k * nblk + j))
    cwspec = lambda k: pl.BlockSpec((SUBLANES, LANES), lambda j, i, k=k: (0, k * nblk + j))
    return pl.pallas_call(
        _hyena_kernel,
        grid=(nblk, b),
        in_specs=[stream(0), stream(1), stream(2), cwspec(0), cwspec(1), cwspec(2),
                  pl.BlockSpec((1, KHP, 2 * FFT_N2, LANES), lambda j, i: (j, 0, 0, 0),
                               pipeline_mode=pl.Buffered(1)),
                  _const_spec(fa.shape), _const_spec(fai.shape),
                  _const_spec(fb.shape), _const_spec(fbi.shape)],
        out_specs=pl.BlockSpec((1, seq, LANES), lambda j, i: (i, 0, j)),
        out_shape=jax.ShapeDtypeStruct((b, seq, D_HYENA), BF16),
        scratch_shapes=[pltpu.VMEM((n_slabs * TPITCH, LANES), F32),
                        pltpu.VMEM((KP * ZPITCH, LANES), F32)],
        compiler_params=_params(("arbitrary", "arbitrary")),
        name="hyena",
    )(u, u, u, cw, cw, cw, hf, fa, fai, fb, fbi)


def _attn_kernel(sink_ref, q_ref, kp_ref, km_ref, kn_ref, vp_ref, vm_ref, vn_ref, kc_ref, vc_ref,
                 o_ref, ka_ref, va_ref):
    i = pl.program_id(1)
    n_i = pl.num_programs(1)
    tq = q_ref.shape[0]
    nqb = tq // BLOCK
    ka_ref[0:BLOCK] = kp_ref[...]
    ka_ref[BLOCK:BLOCK + tq] = km_ref[...]
    ka_ref[BLOCK + tq:] = kn_ref[...]
    va_ref[0:BLOCK] = vp_ref[...]
    va_ref[BLOCK:BLOCK + tq] = vm_ref[...]
    va_ref[BLOCK + tq:] = vn_ref[...]

    qi = lax.broadcasted_iota(jnp.int32, (BLOCK, BLOCK), 0)
    kj = lax.broadcasted_iota(jnp.int32, (BLOCK, BLOCK), 1)
    lane = lax.broadcasted_iota(jnp.int32, (BLOCK, LANES), 1)
    low = lane < HEAD_DIM
    half = GQA_GROUP // 2
    rows2 = half * BLOCK
    hrow = lax.broadcasted_iota(jnp.int32, (GQA_GROUP * BLOCK, 1), 0) // BLOCK
    head_order = [hh for hh in range(GQA_GROUP) if hh % 2 == 0] + [hh for hh in range(GQA_GROUP) if hh % 2 == 1]
    one = jnp.ones((), BF16)

    def with_ones(v):
        lanes_low = lax.broadcasted_iota(jnp.int32, v.shape, 1) < HEAD_DIM
        return jnp.where(lanes_low, v, one), jnp.where(lanes_low, one, v)

    for g in range(N_KV_HEADS):
        gl = slice(g * LANES, (g + 1) * LANES)
        vc_even, vc_odd = with_ones(vc_ref[:, gl])
        sink = jnp.zeros((GQA_GROUP * BLOCK, 1), F32)
        for pos, hh in enumerate(head_order):
            sink = jnp.where(hrow == pos, sink_ref[g * GQA_GROUP + hh], sink)
        for j in range(nqb):
            prev_ok = kj >= qi
            next_ok = kj <= qi
            if j == 0:
                prev_ok = prev_ok & (i > 0)
            if j == nqb - 1:
                next_ok = next_ok & (i < n_i - 1)
            bias_p = jnp.concatenate([jnp.where(prev_ok, 0.0, NEG_INF).astype(F32)] * GQA_GROUP, axis=0)
            bias_n = jnp.concatenate([jnp.where(next_ok, 0.0, NEG_INF).astype(F32)] * GQA_GROUP, axis=0)
            qb = q_ref[j * BLOCK:(j + 1) * BLOCK, :]
            parts = []
            for hh in head_order:
                h = g * GQA_GROUP + hh
                qp = qb[:, (h // 2) * LANES:(h // 2 + 1) * LANES]
                parts.append(jnp.where(low if h % 2 == 0 else ~low, qp, jnp.zeros_like(qp)))
            qs = jnp.concatenate(parts, axis=0)
            kw = ka_ref[j * BLOCK:(j + 3) * BLOCK, gl]
            vw_even, vw_odd = with_ones(va_ref[j * BLOCK:(j + 3) * BLOCK, gl])
            s_w = _dot_nt(qs, kw)
            s_p = s_w[:, :BLOCK] + bias_p
            s_m = s_w[:, BLOCK:2 * BLOCK]
            s_n = s_w[:, 2 * BLOCK:] + bias_n
            s_c = _dot_nt(qs, kc_ref[:, gl])
            m = jnp.maximum(jnp.maximum(jnp.max(jnp.maximum(jnp.maximum(s_p, s_m), s_n), axis=1, keepdims=True),
                                        jnp.max(s_c, axis=1, keepdims=True)), sink)
            e_w = jnp.concatenate([jnp.exp(s_p - m), jnp.exp(s_m - m), jnp.exp(s_n - m)], axis=1).astype(BF16)
            e_c = jnp.exp(s_c - m).astype(BF16)
            e_sink = jnp.exp(sink - m)
            outs = []
            for par, (vw, vcx) in enumerate(((vw_even, vc_even), (vw_odd, vc_odd))):
                rs = slice(par * rows2, (par + 1) * rows2)
                acc = _dot(e_w[rs], vw) + _dot(e_c[rs], vcx)
                den = pltpu.roll(acc, HEAD_DIM, 1) + e_sink[rs]
                outs.append(acc / den)
            for pp in range(half):
                pair = jnp.where(low, outs[0][pp * BLOCK:(pp + 1) * BLOCK], outs[1][pp * BLOCK:(pp + 1) * BLOCK])
                col = (g * half + pp) * LANES
                o_ref[j * BLOCK:(j + 1) * BLOCK, col:col + LANES] = pair.astype(BF16)


def _attention(sinks, q, kd, vd, kc, vc, batch, seq, tq):
    t = q.shape[0]
    per_b = seq // tq
    nqb = tq // BLOCK
    nb = seq // BLOCK
    n_ctx = kc.shape[0] // batch
    main = lambda b, i: (b * per_b + i, 0)
    prev = lambda b, i: (b * nb + jnp.maximum(i * nqb - 1, 0), 0)
    nxt = lambda b, i: (b * nb + jnp.minimum(i * nqb + nqb, nb - 1), 0)
    kvw = kd.shape[1]
    return pl.pallas_call(
        _attn_kernel,
        grid=(batch, per_b),
        in_specs=[pl.BlockSpec(memory_space=pltpu.SMEM),
                  pl.BlockSpec((tq, D_ATTN), main),
                  pl.BlockSpec((BLOCK, kvw), prev), pl.BlockSpec((tq, kvw), main), pl.BlockSpec((BLOCK, kvw), nxt),
                  pl.BlockSpec((BLOCK, kvw), prev), pl.BlockSpec((tq, kvw), main), pl.BlockSpec((BLOCK, kvw), nxt),
                  pl.BlockSpec((n_ctx, kvw), lambda b, i: (b, 0)),
                  pl.BlockSpec((n_ctx, kvw), lambda b, i: (b, 0))],
        out_specs=pl.BlockSpec((tq, D_ATTN), main),
        out_shape=jax.ShapeDtypeStruct((t, D_ATTN), BF16),
        scratch_shapes=[pltpu.VMEM((tq + 2 * BLOCK, kvw), BF16),
                        pltpu.VMEM((tq + 2 * BLOCK, kvw), BF16)],
        compiler_params=_params(("arbitrary", "arbitrary")),
        name="attention",
    )(sinks, q, kd, kd, kd, vd, vd, vd, kc, vc)


def _route(t, wr_hi_ref, wr_hl_ref, br_ref):
    th, tl = _split(t)
    both = _dot(th, wr_hl_ref[...])
    logits = both[:, :LANES] + both[:, LANES:] + _dot(tl, wr_hi_ref[...]) + br_ref[...]
    lane_i = lax.broadcasted_iota(jnp.int32, logits.shape, 1)
    lane = lane_i.astype(F32)
    grp_of_lane = (lane_i >> 2).astype(F32)
    ninf = -jnp.inf
    far = float(LANES)
    is_g = (lane_i >= N_EXPERTS) & (lane_i < N_EXPERTS + N_GROUPS)
    glog = jnp.where(is_g, logits, ninf)
    gmax = jnp.max(glog, axis=1, keepdims=True)
    gidx = jnp.min(jnp.where(glog == gmax, lane - float(N_EXPERTS), far), axis=1, keepdims=True)
    group_p = 1.0 / jnp.sum(jnp.exp(glog - gmax), axis=1, keepdims=True)
    in_grp = (lane_i < N_EXPERTS) & (grp_of_lane == gidx)
    elog = jnp.where(in_grp, logits, ninf)
    v1 = jnp.max(elog, axis=1, keepdims=True)
    i1 = jnp.min(jnp.where(elog == v1, lane, far), axis=1, keepdims=True)
    elog2 = jnp.where(lane == i1, ninf, elog)
    v2 = jnp.max(elog2, axis=1, keepdims=True)
    i2 = jnp.min(jnp.where(elog2 == v2, lane, far), axis=1, keepdims=True)
    e = jnp.exp(v2 - v1)
    w1 = group_p / (1.0 + e)
    w2 = group_p * e / (1.0 + e)
    gate = jnp.where(lane == i1, w1, 0.0) + jnp.where(lane == i2, w2, 0.0)
    return gate + jnp.where(lane_i == GROUP_LANE, gidx, 0.0)


def _mix_out_kernel(x_ref, yh_ref, ya_ref, sh1_ref, sc1_ref, g1_ref, sh2_ref, sc2_ref,
                    wg_ref, wbh_ref, wba_ref, wo_ref, lng_ref, lnb_ref, wrh_ref, wrhl_ref, br_ref,
                    x1_ref, gate_ref, gsel_ref, *t_refs):
    d = x_ref.shape[1]
    for r0 in range(0, x_ref.shape[0], SUB_ROWS):
        rows = slice(r0, r0 + SUB_ROWS)
        x = x_ref[rows, :]
        h = (_standardize(x) * (1.0 + sc1_ref[0]) + sh1_ref[0]).astype(BF16)
        g_hy = jax.nn.sigmoid(_dot(h, wg_ref[:, :d]))
        merged = g_hy * _dot(yh_ref[rows, :], wbh_ref[...])
        g_at = jax.nn.sigmoid(_dot(h, wg_ref[:, d:]))
        merged = merged + g_at * _dot(ya_ref[rows, :], wba_ref[...])
        mix = _dot(merged.astype(BF16), wo_ref[...])
        x1 = _standardize(DEEPNORM_ALPHA * x + g1_ref[0] * mix) * lng_ref[...] + lnb_ref[...]
        x1_ref[rows, :] = x1
        t = _standardize(x1) * (1.0 + sc2_ref[0]) + sh2_ref[0]
        gate = _route(t, wrh_ref, wrhl_ref, br_ref)
        gate_ref[rows, :] = gate
        grp = gate.T[GROUP_LANE:GROUP_LANE + 1, :]
        gsel_ref[r0 // LANES:(r0 + SUB_ROWS) // LANES, :] = jnp.concatenate(
            [grp[:, k * LANES:(k + 1) * LANES] for k in range(SUB_ROWS // LANES)], axis=0)
        for t_ref, piece in zip(t_refs, _pack_pieces(t)):
            t_ref[rows, :] = piece


def _mix_out(x2d, yh, ya, mods, w_g, w_bh, w_ba, w_o, ln_g, ln_b, wr_hi, wr_hl, br, seq, tm):
    t, d = x2d.shape
    per_b = seq // tm
    row = lambda i: (i, 0)
    mod = lambda i: (i // per_b, 0, 0)
    mspec = pl.BlockSpec((1, 1, d), mod)
    sh1, sc1, g1, sh2, sc2 = mods
    return pl.pallas_call(
        _mix_out_kernel,
        grid=(t // tm,),
        in_specs=[pl.BlockSpec((tm, d), row),
                  pl.BlockSpec((tm, yh.shape[1]), row),
                  pl.BlockSpec((tm, ya.shape[1]), row),
                  mspec, mspec, mspec, mspec, mspec,
                  _const_spec(w_g.shape), _const_spec(w_bh.shape), _const_spec(w_ba.shape),
                  _const_spec(w_o.shape), _const_spec(ln_g.shape), _const_spec(ln_b.shape),
                  _const_spec(wr_hi.shape), _const_spec(wr_hl.shape), _const_spec(br.shape)],
        out_specs=[pl.BlockSpec((tm, d), row), pl.BlockSpec((tm, LANES), row),
                   pl.BlockSpec((tm // LANES, LANES), row)]
        + [pl.BlockSpec((tm, SC_ROW_WORDS), row)] * N_PIECES,
        out_shape=[jax.ShapeDtypeStruct((t, d), F32), jax.ShapeDtypeStruct((t, LANES), F32),
                   jax.ShapeDtypeStruct((t // LANES, LANES), F32)]
        + [jax.ShapeDtypeStruct((t, SC_ROW_WORDS), jnp.uint32)] * N_PIECES,
        compiler_params=_params(("arbitrary",)),
        name="mix_out",
    )(x2d, yh, ya, sh1, sc1, g1, sh2, sc2, w_g, w_bh, w_ba, w_o, ln_g, ln_b, wr_hi, wr_hl, br)


def _slots_kernel(g_ref, upper_ref, lower_ref, dest_ref, meta_ref):
    gsel = g_ref[...]
    dest = jnp.zeros(gsel.shape, F32)
    base = jnp.zeros((1, 1), F32)
    chunk_start = lax.broadcasted_iota(jnp.int32, (1, LANES), 1).astype(F32) * float(MOE_CHUNK)
    owner = jnp.zeros((1, LANES), F32)
    for g in range(N_GROUPS):
        onehot = jnp.where(gsel == float(g), 1.0, 0.0)
        in_row = _dot(onehot.astype(BF16), upper_ref[...])
        row_tot = jnp.sum(onehot, axis=1, keepdims=True)
        rows_before = _dot(lower_ref[...], jnp.broadcast_to(row_tot, onehot.shape).astype(BF16))
        dest = dest + onehot * (base + rows_before + in_row)
        if g > 0:
            owner = owner + jnp.where(chunk_start >= base, 1.0, 0.0)
        n_g = jnp.sum(row_tot, axis=0, keepdims=True)
        base = base + jnp.floor((n_g + float(MOE_CHUNK - 1)) * (1.0 / MOE_CHUNK)) * float(MOE_CHUNK)
    dest_ref[...] = dest.astype(jnp.int32)
    row = lax.broadcasted_iota(jnp.int32, meta_ref.shape, 0)
    meta_ref[...] = jnp.where(row == 0, owner, base * (1.0 / MOE_CHUNK)).astype(jnp.int32)


def _slots(gsel):
    r = gsel.shape[0]
    upper = jnp.asarray(np.triu(np.ones((LANES, LANES), np.float32), 1)).astype(BF16)
    lower = jnp.asarray(np.tril(np.ones((r, r), np.float32), -1)).astype(BF16)
    full = lambda a: pl.BlockSpec(a.shape, lambda i: (0,) * a.ndim)
    return pl.pallas_call(
        _slots_kernel,
        grid=(1,),
        in_specs=[full(gsel), full(upper), full(lower)],
        out_specs=[pl.BlockSpec((r, LANES), lambda i: (0, 0)), pl.BlockSpec((SUBLANES, LANES), lambda i: (0, 0))],
        out_shape=[jax.ShapeDtypeStruct((r, LANES), jnp.int32), jax.ShapeDtypeStruct((SUBLANES, LANES), jnp.int32)],
        compiler_params=_params(("arbitrary",)),
        name="moe_slots",
    )(gsel, upper, lower)


def _experts_kernel(owner_ref, used_ref, *refs):
    n_p = N_PIECES
    t_refs, gate_ref = refs[:n_p], refs[n_p]
    wg_ref, wu_ref, wd_ref = refs[n_p + 1:n_p + 4]
    y_refs = refs[n_p + 4:]
    c = pl.program_id(0)

    @pl.when(c < used_ref[0])
    def _():
        t = _unpack_pieces([r[...] for r in t_refs]).astype(BF16)
        gate = gate_ref[...]
        lane = lax.broadcasted_iota(jnp.int32, gate.shape, 1)
        first = owner_ref[c] * EXPERTS_PER_GROUP
        parts = []
        for e in range(EXPERTS_PER_GROUP):
            a = _dot(t, wg_ref[e])
            u = _dot(t, wu_ref[e])
            ge = jnp.sum(jnp.where(lane == first + e, gate, 0.0), axis=1, keepdims=True)
            parts.append((a * jax.nn.sigmoid(a) * u * ge).astype(BF16))
        y = _dot(jnp.concatenate(parts, axis=1), wd_ref[0])
        for y_ref, piece in zip(y_refs, _pack_pieces(y)):
            y_ref[...] = piece

    @pl.when(c >= used_ref[0])
    def _():
        for y_ref in y_refs:
            y_ref[...] = jnp.zeros_like(y_ref)


def _experts(owner, used, t_pieces, gate_sorted, wg16, wu16, wd3):
    n_slots = gate_sorted.shape[0]
    row = lambda c, owner, used: (c, 0)
    by_owner = lambda c, owner, used: (owner[c], 0, 0)
    piece = pl.BlockSpec((MOE_CHUNK, SC_ROW_WORDS), row)
    group_of_experts = lambda w: pl.BlockSpec((EXPERTS_PER_GROUP,) + w.shape[1:], by_owner)
    return pl.pallas_call(
        _experts_kernel,
        grid_spec=pltpu.PrefetchScalarGridSpec(
            num_scalar_prefetch=2,
            grid=(n_slots // MOE_CHUNK,),
            in_specs=[piece] * len(t_pieces) + [pl.BlockSpec((MOE_CHUNK, LANES), row),
                                                group_of_experts(wg16), group_of_experts(wu16),
                                                pl.BlockSpec((1,) + wd3.shape[1:], by_owner)],
            out_specs=[piece] * len(t_pieces),
        ),
        out_shape=[jax.ShapeDtypeStruct((n_slots, SC_ROW_WORDS), jnp.uint32)] * len(t_pieces),
        compiler_params=_params(("arbitrary",)),
        name="moe_experts",
    )(owner, used, *t_pieces, gate_sorted, wg16, wu16, wd3)


def _final_kernel(x1_ref, g2_ref, lng_ref, lnb_ref, *refs):
    y_refs, o_ref = refs[:-1], refs[-1]
    y = _unpack_pieces([r[...] for r in y_refs])
    o_ref[...] = _standardize(DEEPNORM_ALPHA * x1_ref[...] + g2_ref[0] * y) * lng_ref[...] + lnb_ref[...]


def _final(x1, g2, ln_g, ln_b, y_pieces, seq, tm):
    t, d = x1.shape
    per_b = seq // tm
    row = lambda i: (i, 0)
    return pl.pallas_call(
        _final_kernel,
        grid=(t // tm,),
        in_specs=[pl.BlockSpec((tm, d), row), pl.BlockSpec((1, 1, d), lambda i: (i // per_b, 0, 0)),
                  _const_spec(ln_g.shape), _const_spec(ln_b.shape)]
        + [pl.BlockSpec((tm, SC_ROW_WORDS), row)] * len(y_pieces),
        out_specs=pl.BlockSpec((tm, d), row),
        out_shape=jax.ShapeDtypeStruct((t, d), F32),
        compiler_params=_params(("arbitrary",)),
        name="moe_final",
    )(x1, g2, ln_g, ln_b, *y_pieces)


def _sc_move_rows(tables, idx, n_out, scatter, name):
    n = idx.shape[0]
    mesh = plsc.VectorSubcoreMesh(core_axis_name="c", subcore_axis_name="s")
    out_type = [jax.ShapeDtypeStruct((n_out, t.shape[1]), t.dtype) for t in tables]
    window = lambda i: (i, 0)
    index_win = pl.BlockSpec((1, SC_WINDOW), lambda i: (0, i))
    split = dict(core_axis_name=("c", "s"), dimension_semantics=(pltpu.PARALLEL,))

    @functools.partial(pl.kernel, out_type=out_type, mesh=mesh, scratch_types=[], name=name)
    def move(*refs):
        srcs, i_hbm, dsts = refs[:len(tables)], refs[len(tables)], refs[len(tables) + 1:]
        for src, dst in zip(srcs, dsts):
            rows_win = pl.BlockSpec((SC_WINDOW, src.shape[1]), window)
            if scatter:
                def body(x_vmem, i_vmem, dst=dst):
                    pltpu.sync_copy(x_vmem, dst.at[i_vmem.at[0]])

                pltpu.emit_pipeline(body, grid=(n // SC_WINDOW,), in_specs=[rows_win, index_win],
                                    out_specs=[], **split)(src, i_hbm)
            else:
                def body(i_vmem, o_vmem, src=src):
                    pltpu.sync_copy(src.at[i_vmem.at[0]], o_vmem)

                pltpu.emit_pipeline(body, grid=(n // SC_WINDOW,), in_specs=[index_win],
                                    out_specs=[rows_win], **split)(i_hbm, dst)

    return move(*tables, idx.reshape(1, n))


def _dft_tables(seq):
    n = 2 * seq
    n1_full = n // FFT_N2
    n1_data = seq // FFT_N2
    k1 = np.arange(KH, dtype=np.float64)[None, :, None]
    n2 = np.arange(FFT_N2, dtype=np.float64)[:, None, None]
    n1 = np.arange(n1_full, dtype=np.float64)[None, None, :]
    ang = 2.0 * np.pi * k1 * (FFT_N2 * n1 + n2) / n
    fa = np.zeros((FFT_N2, 2 * KP, n1_full))
    fa[:, :KH] = np.cos(ang)
    fa[:, KP:KP + KH] = -np.sin(ang)
    wgt = np.full((KH,), 2.0)
    wgt[0] = 1.0
    wgt[KH - 1] = 1.0
    fai = np.zeros((FFT_N2, n1_data, 2 * KP))
    angt = np.transpose(ang[:, :, :n1_data], (0, 2, 1))
    fai[:, :, :KH] = np.cos(angt) * wgt / n
    fai[:, :, KP:KP + KH] = -np.sin(angt) * wgt / n
    kk = np.arange(FFT_N2, dtype=np.float64)
    a2 = 2.0 * np.pi * np.outer(kk, kk) / FFT_N2
    fr, fi = np.cos(a2), -np.sin(a2)
    fb = np.block([[fr, -fi], [fi, fr]])
    fbi = np.block([[fr, fi], [-fi, fr]])
    f32 = lambda a: jnp.asarray(a.astype(np.float32))
    fa_data = fa[:, :, :n1_data]
    fa_pairs = np.concatenate([fa_data[0::2], fa_data[1::2]], axis=2)
    return f32(fa_pairs), f32(fa), f32(fai), f32(fb), f32(fbi)


def _filter_features(seq, rows):
    f32 = np.float32
    t = np.linspace(0.0, 1.0, seq, dtype=f32)
    w = (f32(2.0 * math.pi) * np.arange(seq, dtype=f32) / f32(seq)).astype(f32)
    t2 = np.concatenate([t, t[::-1]])
    w2 = np.concatenate([w, w[::-1]])
    bands = np.linspace(1e-4, FILTER_BANDS - 1, FILTER_BANDS, dtype=f32)
    max_decay = math.log(DECAY_TARGET) / FAST_DECAY_PCT
    min_decay = math.log(DECAY_TARGET) / SLOW_DECAY_PCT
    deltas = jnp.linspace(min_decay, max_decay, D_HYENA, dtype=F32)

    def feats(k):
        tp = t2.reshape(-1, 2, rows // 2)[:, k].reshape(-1, 1)
        wp = w2.reshape(-1, 2, rows // 2)[:, k].reshape(-1, 1)
        pad = np.zeros((tp.shape[0], FILTER_ORDER - FILTER_EMB), f32)
        arg = (bands * wp).astype(f32)
        return np.concatenate([tp, np.cos(arg), -np.sin(arg), pad], axis=-1).astype(f32)

    zp = jnp.asarray(np.concatenate([feats(0), feats(1)], axis=-1))
    return zp, jnp.asarray(t2)[:, None], jnp.abs(deltas)[None, :]


def _rope_tables(seq):
    f32 = np.float32
    rows = seq // GRID_W
    row = np.repeat(np.arange(rows, dtype=f32), GRID_W)
    col = np.tile(np.arange(GRID_W, dtype=f32), rows)
    half = HEAD_DIM // 2
    inv_freq = (f32(ROPE_BASE) ** (-np.arange(0, half, 2, dtype=f32) / f32(half))).astype(f32)
    ang = np.concatenate([row[:, None] * inv_freq, col[:, None] * inv_freq], axis=-1).astype(f32)
    cos, sin = np.cos(ang), np.sin(ang)
    c64 = np.concatenate([cos, cos], axis=-1)
    s64 = np.concatenate([-sin, sin], axis=-1)
    return (jnp.asarray(np.concatenate([c64, c64], axis=-1).astype(f32)),
            jnp.asarray(np.concatenate([s64, s64], axis=-1).astype(f32)))


def _head_perm(n_heads):
    idx = []
    for h in range(n_heads):
        base = h * HEAD_DIM
        idx += [base + 2 * j for j in range(HEAD_DIM // 2)]
        idx += [base + 2 * j + 1 for j in range(HEAD_DIM // 2)]
    return np.asarray(idx, dtype=np.int32)


def _dup_heads(w):
    parts = []
    for g in range(N_KV_HEADS):
        blk = w[:, g * HEAD_DIM:(g + 1) * HEAD_DIM]
        parts += [blk, blk]
    return jnp.concatenate(parts, axis=1)


def kernel(x, c, ctx, c_ctx, ada_w, ada_b, w_in, hy_conv_w, hy_conv_b, hy_w1, hy_b1, hy_w2, hy_b2, hy_w3, hy_b3, hy_w4, hy_freq, hy_bias, attn_sinks, w_branch_hy, w_branch_attn, w_out, ln1_g, ln1_b, w_group, b_group, w_router, b_router, w_gate_e, w_up_e, w_down_e, ln2_g, ln2_b):
    batch, seq, d = x.shape
    n_ctx = ctx.shape[1]
    assert d == D_MODEL and ada_w.shape[0] == DEPTH == 1
    assert 2 * seq == FFT_N2 * FFT_N2 and seq % TM_PROJ == 0
    l = 0

    w = w_in[l]
    s0, s1, s2, s3 = 3 * D_HYENA, 3 * D_HYENA + D_ATTN, 3 * D_HYENA + D_ATTN + D_KV, 3 * D_HYENA + D_ATTN + 2 * D_KV
    w_q = w[:, s0:s1][:, _head_perm(N_HEADS)]
    w_k = _dup_heads(w[:, s1:s2][:, _head_perm(N_KV_HEADS)])
    w_v = _dup_heads(w[:, s2:s3])
    w_c = jnp.concatenate([w[:, :s0], w_q, w_k, w_v], axis=1).astype(BF16)
    w_kv = jnp.concatenate([w_k, w_v], axis=1).astype(BF16)
    w_g = w[:, s3:].astype(BF16)
    w_bh = w_branch_hy[l].astype(BF16)
    w_ba = w_branch_attn[l].astype(BF16)
    w_o = w_out[l].astype(BF16)
    wr = jnp.zeros((d, LANES), F32)
    wr = wr.at[:, :N_EXPERTS].set(w_router[l]).at[:, N_EXPERTS:N_EXPERTS + N_GROUPS].set(w_group[l])
    wr_hi = wr.astype(BF16)
    wr_hl = jnp.concatenate([wr_hi, (wr - wr_hi.astype(F32)).astype(BF16)], axis=1)
    br = jnp.zeros((1, LANES), F32)
    br = br.at[0, :N_EXPERTS].set(b_router[l]).at[0, N_EXPERTS:N_EXPERTS + N_GROUPS].set(b_group[l])
    wg16 = w_gate_e[l].astype(BF16)
    wu16 = w_up_e[l].astype(BF16)
    wd3 = w_down_e[l].astype(BF16).reshape(N_GROUPS, EXPERTS_PER_GROUP * D_EXPERT, d)
    cw = jnp.concatenate([hy_conv_w[l][:, 0, :], hy_conv_b[l][None, :],
                          jnp.zeros((SUBLANES - SHORT_CONV - 1, 3 * D_HYENA), F32)], axis=0)
    row2 = lambda a: a.reshape(1, -1)
    pair = lambda a: jnp.concatenate([a, a], axis=-1)
    zero_o = jnp.zeros((FILTER_ORDER, FILTER_ORDER), F32)
    bdiag = lambda a: jnp.concatenate([jnp.concatenate([a, zero_o], axis=1),
                                       jnp.concatenate([zero_o, a], axis=1)], axis=0)
    w1p = bdiag(jnp.concatenate([hy_w1[l], jnp.zeros((FILTER_ORDER - FILTER_EMB, FILTER_ORDER), F32)], axis=0))
    w4h = jnp.transpose(hy_w4[l].reshape(FILTER_ORDER, 2, D_HYENA), (1, 0, 2))
    zero_w4 = jnp.zeros_like(w4h)
    w4s = jnp.stack([jnp.concatenate([w4h, zero_w4], axis=1),
                     jnp.concatenate([zero_w4, w4h], axis=1)], axis=1)

    fa, fa_full, fai, fb, fbi = _dft_tables(seq)
    fa, fa_full, fai, fb, fbi = (a.astype(BF16) for a in (fa, fa_full, fai, fb, fbi))
    zp, t_col, absd = _filter_features(seq, FILT_ROWS)
    cos_t, sin_t = _rope_tables(seq)

    cond = jnp.concatenate([c, c_ctx[None], jnp.zeros((SUBLANES - batch - 1, d), F32)], axis=0)
    mods = _adaln(cond, ada_w[l], ada_b[l])
    m6 = [mods[:, k * d:(k + 1) * d].reshape(SUBLANES, 1, d) for k in range(6)]
    sh1, sc1, g1, sh2, sc2, g2 = m6

    kc, vc = _ctx_kv(ctx.reshape(batch * n_ctx, d), sh1[batch], sc1[batch], w_kv, n_ctx)

    x2d = x.reshape(batch * seq, d)
    u, q, kd, vd = _in_proj(x2d, sh1, sc1, w_c, cos_t, sin_t, seq, TM_PROJ)
    h2u, ss = _filt_mlp(zp, w1p, pair(row2(hy_b1[l])), bdiag(hy_w2[l]), pair(row2(hy_b2[l])),
                        bdiag(hy_w3[l]), pair(row2(hy_b3[l])), w4s, pair(row2(hy_freq[l])), t_col, absd, FILT_ROWS)
    hf = _filt_fft(h2u, ss, row2(hy_bias[l]), fa_full, fb)
    y_hy = _hyena(u.reshape(batch, seq, 3 * D_HYENA), cw, hf, fa, fai, fb, fbi)
    y_at = _attention(attn_sinks[l], q, kd, vd, kc, vc, batch, seq, TQ_ATTN)
    x1, gate, gsel, *t_pieces = _mix_out(x2d, y_hy.reshape(batch * seq, D_HYENA), y_at, (sh1, sc1, g1, sh2, sc2),
                                   w_g, w_bh, w_ba, w_o, row2(ln1_g[l]), row2(ln1_b[l]), wr_hi, wr_hl, br,
                                   seq, TM_PROJ)

    n_tok = batch * seq
    n_slots = n_tok + N_GROUPS * MOE_CHUNK
    dest2d, meta = _slots(gsel)
    dest = dest2d.reshape(n_tok)
    *t_sorted, gate_sorted = _sc_move_rows(t_pieces + [gate], dest, n_slots, True, "moe_sort")
    y_sorted = _experts(meta[0], meta[1, :1], t_sorted, gate_sorted, wg16, wu16, wd3)
    y_pieces = _sc_move_rows(y_sorted, dest, n_tok, False, "moe_unsort")
    out = _final(x1, g2, row2(ln2_g[l]), row2(ln2_b[l]), y_pieces, seq, TM_FINAL)
    return out.reshape(batch, seq, d)
```

```python
import functools
import math

import numpy as np
import jax
import jax.numpy as jnp
from jax import lax
from jax.experimental import pallas as pl
from jax.experimental.pallas import tpu as pltpu
from jax.experimental.pallas import tpu_sc as plsc

F32 = jnp.float32
BF16 = jnp.bfloat16

D_MODEL = 1024
GRID_W = 64
D_HYENA = D_MODEL // 2
SHORT_CONV = 3
FILTER_BANDS = 16
FILTER_EMB = 1 + 2 * FILTER_BANDS
FILTER_ORDER = 64
DECAY_TARGET = 1e-2
FAST_DECAY_PCT = 0.3
SLOW_DECAY_PCT = 1.5
HEAD_DIM = 64
D_ATTN = D_MODEL // 2
N_HEADS = D_ATTN // HEAD_DIM
N_KV_HEADS = N_HEADS // 4
GQA_GROUP = N_HEADS // N_KV_HEADS
D_KV = N_KV_HEADS * HEAD_DIM
WINDOW = 128
BLOCK = 128
ROPE_BASE = 10000.0
NEG_INF = -1e30
N_GROUPS = 4
EXPERTS_PER_GROUP = 4
N_EXPERTS = N_GROUPS * EXPERTS_PER_GROUP
D_EXPERT = D_MODEL // 4
LN_EPS = 1e-5
DEPTH = 1
DEEPNORM_ALPHA = (2.0 * DEPTH) ** 0.25

LANES = 128
SUBLANES = 8
VMEM_LIMIT = 56 * 1024 * 1024

SUB_ROWS = 512
TM_PROJ = 1024
MOE_CHUNK = 512
TM_FINAL = 1024
GROUP_LANE = N_EXPERTS
SC_WINDOW = 128
SC_ROW_WORDS = 256
N_PIECES = D_MODEL // (2 * SC_ROW_WORDS)
TQ_ATTN = 1024
FILT_ROWS = 2048

FFT_N2 = 128
KH = 65
KHP = 66
KP = 72
ZPITCH = 2 * FFT_N2 + SUBLANES
TPITCH = FFT_N2 + SUBLANES


def _dot(a, b):
    return jnp.dot(a, b, preferred_element_type=F32)


def _dot_nt(a, b):
    return lax.dot_general(a, b, (((1,), (1,)), ((), ())), preferred_element_type=F32)


def _split(a):
    hi = a.astype(BF16)
    lo = (a - hi.astype(F32)).astype(BF16)
    return hi, lo


def _pack_pieces(x):
    w = SC_ROW_WORDS
    pieces = []
    for p in range(x.shape[1] // (2 * w)):
        hi = lax.bitcast_convert_type(x[:, 2 * p * w:(2 * p + 1) * w].astype(BF16).astype(F32), jnp.uint32)
        lo = lax.bitcast_convert_type(x[:, (2 * p + 1) * w:(2 * p + 2) * w].astype(BF16).astype(F32), jnp.uint32)
        pieces.append(hi | (lo >> 16))
    return pieces


def _unpack_pieces(pieces):
    cols = []
    for word in pieces:
        cols.append(lax.bitcast_convert_type(word & jnp.uint32(0xFFFF0000), F32))
        cols.append(lax.bitcast_convert_type(word << 16, F32))
    return jnp.concatenate(cols, axis=1)


def _dot3(a, b):
    ah, al = _split(a)
    bh, bl = _split(b)
    return _dot(ah, bh) + _dot(al, bh) + _dot(ah, bl)


def _standardize(x):
    mu = jnp.mean(x, axis=-1, keepdims=True)
    xc = x - mu
    var = jnp.mean(xc * xc, axis=-1, keepdims=True)
    return xc * lax.rsqrt(var + LN_EPS)


def _params(sem, vmem=VMEM_LIMIT):
    return pltpu.CompilerParams(dimension_semantics=sem, vmem_limit_bytes=vmem)


def _const_spec(shape):
    nd = len(shape)
    return pl.BlockSpec(shape, lambda *_: (0,) * nd, pipeline_mode=pl.Buffered(1))


def _adaln_kernel(c_ref, w_ref, b_ref, o_ref):
    s = c_ref[...]
    s = s * jax.nn.sigmoid(s)
    o_ref[...] = _dot3(s, w_ref[...]) + b_ref[...]


def _adaln(cond, w, b):
    n, d = cond.shape
    cols = w.shape[1]
    bc = 1024
    return pl.pallas_call(
        _adaln_kernel,
        grid=(cols // bc,),
        in_specs=[pl.BlockSpec((n, d), lambda j: (0, 0)),
                  pl.BlockSpec((d, bc), lambda j: (0, j)),
                  pl.BlockSpec((1, bc), lambda j: (0, j))],
        out_specs=pl.BlockSpec((n, bc), lambda j: (0, j)),
        out_shape=jax.ShapeDtypeStruct((n, cols), F32),
        compiler_params=_params(("arbitrary",)),
        name="adaln",
    )(cond, w, b.reshape(1, cols))


def _ctx_kv_kernel(x_ref, sh_ref, sc_ref, w_ref, k_ref, v_ref):
    h = _standardize(x_ref[...]) * (1.0 + sc_ref[...]) + sh_ref[...]
    kv = _dot(h.astype(BF16), w_ref[...])
    half = k_ref.shape[1]
    k_ref[...] = kv[:, :half].astype(BF16)
    v_ref[...] = kv[:, half:].astype(BF16)


def _ctx_kv(ctx2d, sh, sc, w_kv, rows):
    n, d = ctx2d.shape
    half = w_kv.shape[1] // 2
    return pl.pallas_call(
        _ctx_kv_kernel,
        grid=(n // rows,),
        in_specs=[pl.BlockSpec((rows, d), lambda i: (i, 0)),
                  pl.BlockSpec((1, d), lambda i: (0, 0)),
                  pl.BlockSpec((1, d), lambda i: (0, 0)),
                  pl.BlockSpec(w_kv.shape, lambda i: (0, 0))],
        out_specs=[pl.BlockSpec((rows, half), lambda i: (i, 0)),
                   pl.BlockSpec((rows, half), lambda i: (i, 0))],
        out_shape=[jax.ShapeDtypeStruct((n, half), BF16)] * 2,
        compiler_params=_params(("arbitrary",)),
        name="ctx_kv",
    )(ctx2d, sh, sc, w_kv)


def _rope(x, cos_t, sin_t):
    width = x.shape[1]
    reps = width // LANES
    c = jnp.concatenate([cos_t] * reps, axis=1)
    s = jnp.concatenate([sin_t] * reps, axis=1)
    half = HEAD_DIM // 2
    lane = lax.broadcasted_iota(jnp.int32, x.shape, 1)
    first_half = (lane & (HEAD_DIM - 1)) < half
    partner = jnp.where(first_half, pltpu.roll(x, width - half, 1), pltpu.roll(x, half, 1))
    return x * c + partner * s


def _in_proj_kernel(x_ref, sh_ref, sc_ref, w_ref, cos_ref, sin_ref, u_ref, q_ref, k_ref, v_ref, h_ref):
    n_u = u_ref.shape[1]
    n_q = q_ref.shape[1]
    n_k = k_ref.shape[1]
    for r0 in range(0, x_ref.shape[0], SUB_ROWS):
        rows = slice(r0, r0 + SUB_ROWS)
        h = (_standardize(x_ref[rows, :]) * (1.0 + sc_ref[0]) + sh_ref[0]).astype(BF16)
        h_ref[rows, :] = h
        u_ref[rows, :] = _dot(h, w_ref[:, :n_u]).astype(BF16)
        cos_t = cos_ref[rows, :]
        sin_t = sin_ref[rows, :]
        q = _dot(h, w_ref[:, n_u:n_u + n_q])
        q_ref[rows, :] = (_rope(q, cos_t, sin_t) * (HEAD_DIM ** -0.5)).astype(BF16)
        k = _dot(h, w_ref[:, n_u + n_q:n_u + n_q + n_k])
        k_ref[rows, :] = _rope(k, cos_t, sin_t).astype(BF16)
        v_ref[rows, :] = _dot(h, w_ref[:, n_u + n_q + n_k:]).astype(BF16)


def _in_proj(x2d, sh, sc, w_c, cos_t, sin_t, seq, tm):
    t, d = x2d.shape
    per_b = seq // tm
    n_u, n_q, n_k = 3 * D_HYENA, D_ATTN, 2 * D_KV
    row = lambda i: (i, 0)
    mod = lambda i: (i // per_b, 0, 0)
    pos = lambda i: (i % per_b, 0)
    return pl.pallas_call(
        _in_proj_kernel,
        grid=(t // tm,),
        in_specs=[pl.BlockSpec((tm, d), row),
                  pl.BlockSpec((1, 1, d), mod),
                  pl.BlockSpec((1, 1, d), mod),
                  _const_spec(w_c.shape),
                  pl.BlockSpec((tm, LANES), pos),
                  pl.BlockSpec((tm, LANES), pos)],
        out_specs=[pl.BlockSpec((tm, n_u), row),
                   pl.BlockSpec((tm, n_q), row),
                   pl.BlockSpec((tm, n_k), row),
                   pl.BlockSpec((tm, n_k), row),
                   pl.BlockSpec((tm, d), row)],
        out_shape=[jax.ShapeDtypeStruct((t, n_u), BF16),
                   jax.ShapeDtypeStruct((t, n_q), BF16),
                   jax.ShapeDtypeStruct((t, n_k), BF16),
                   jax.ShapeDtypeStruct((t, n_k), BF16),
                   jax.ShapeDtypeStruct((t, d), BF16)],
        compiler_params=_params(("arbitrary",)),
        name="in_proj",
    )(x2d, sh, sc, w_c, cos_t, sin_t)


def _filt_mlp_kernel(z_ref, w1_ref, b1_ref, w2_ref, b2_ref, w3_ref, b3_ref, w4_ref, fr_ref, t_ref, absd_ref,
                     h_ref, ss_ref):
    fr = fr_ref[...]
    a = jnp.sin(fr * (_dot3(z_ref[...], w1_ref[...]) + b1_ref[...]))
    a = jnp.sin(fr * (_dot3(a, w2_ref[...]) + b2_ref[...]))
    a = jnp.sin(fr * (_dot3(a, w3_ref[...]) + b3_ref[...]))
    half = a.shape[0]
    ss = jnp.zeros(ss_ref.shape, F32)
    for k in range(2):
        decay = jnp.exp(-t_ref[k * half:(k + 1) * half, :] * absd_ref[...])
        h = _dot3(a, w4_ref[0, k]) * decay
        h_ref[k * half:(k + 1) * half, :] = h
        ss = ss + jnp.sum(h * h, axis=0, keepdims=True)

    @pl.when(pl.program_id(0) == 0)
    def _():
        ss_ref[...] = jnp.zeros_like(ss_ref)

    ss_ref[...] += ss


def _filt_mlp(zp, w1p, b1, w2, b2, w3, b3, w4s, fr, t_col, absd, rows):
    n, c = t_col.shape[0], absd.shape[1]
    half_steps = (n // 2) // rows
    vec = lambda a: pl.BlockSpec(a.shape, lambda i: (0,) * a.ndim)
    return pl.pallas_call(
        _filt_mlp_kernel,
        grid=(n // rows,),
        in_specs=[pl.BlockSpec((rows // 2, zp.shape[1]), lambda i: (i, 0)),
                  vec(w1p), vec(b1), vec(w2), vec(b2), vec(w3), vec(b3),
                  pl.BlockSpec((1,) + w4s.shape[1:], lambda i: (i // half_steps, 0, 0, 0)),
                  vec(fr),
                  pl.BlockSpec((rows, 1), lambda i: (i, 0)),
                  vec(absd)],
        out_specs=[pl.BlockSpec((rows, c), lambda i: (i, 0)),
                   pl.BlockSpec((1, c), lambda i: (0, 0))],
        out_shape=[jax.ShapeDtypeStruct((n, c), F32), jax.ShapeDtypeStruct((1, c), F32)],
        compiler_params=_params(("arbitrary",)),
        name="filt_mlp",
    )(zp, w1p, b1, w2, b2, w3, b3, w4s, fr, t_col, absd)


def _filt_fft_kernel(h_ref, ss_ref, bias_ref, fa_ref, fb_ref, o_ref, zs_ref):
    scale = lax.rsqrt(ss_ref[...] + 1e-6)
    n1 = h_ref.shape[0] // FFT_N2
    row = lax.broadcasted_iota(jnp.int32, (2 * FFT_N2, LANES), 0)
    impulse = jnp.where(row < FFT_N2, bias_ref[...], 0.0)

    def stage_a(n2, carry):
        slab = h_ref[pl.ds(n2, n1, stride=FFT_N2), :]
        z = _dot(fa_ref[n2], slab.astype(BF16))
        zs_ref[pl.ds(n2, KP, stride=ZPITCH), :] = z[:KP]
        zs_ref[pl.ds(FFT_N2 + n2, KP, stride=ZPITCH), :] = z[KP:]
        return carry

    lax.fori_loop(0, FFT_N2, stage_a, 0, unroll=8)

    def stage_b(p, carry):
        b0 = pl.multiple_of(2 * p * ZPITCH, SUBLANES)
        b1 = pl.multiple_of(b0 + ZPITCH, SUBLANES)
        z = jnp.concatenate([zs_ref[pl.ds(b0, 2 * FFT_N2), :], zs_ref[pl.ds(b1, 2 * FFT_N2), :]], axis=1)
        x = _dot(fb_ref[...], z.astype(BF16))
        o_ref[0, 2 * p] = x[:, :LANES] * scale + impulse
        o_ref[0, 2 * p + 1] = x[:, LANES:] * scale + impulse
        return carry

    lax.fori_loop(0, KHP // 2, stage_b, 0, unroll=11)


def _filt_fft(h2u, ss, bias, fa_full, fb):
    n, c = h2u.shape
    nblk = c // LANES
    return pl.pallas_call(
        _filt_fft_kernel,
        grid=(nblk,),
        in_specs=[pl.BlockSpec((n, LANES), lambda j: (0, j), pipeline_mode=pl.Buffered(1)),
                  pl.BlockSpec((1, LANES), lambda j: (0, j)),
                  pl.BlockSpec((1, LANES), lambda j: (0, j)),
                  _const_spec(fa_full.shape),
                  _const_spec(fb.shape)],
        out_specs=pl.BlockSpec((1, KHP, 2 * FFT_N2, LANES), lambda j: (j, 0, 0, 0)),
        out_shape=jax.ShapeDtypeStruct((nblk, KHP, 2 * FFT_N2, LANES), F32),
        scratch_shapes=[pltpu.VMEM((KP * ZPITCH, LANES), F32)],
        compiler_params=_params(("arbitrary",)),
        name="filt_fft",
    )(h2u, ss, bias, fa_full, fb)


def _conv_slab(u_ref, cw_ref, j, n_slabs):
    r0 = pl.multiple_of(j * FFT_N2, FFT_N2)
    cur = u_ref[0, pl.ds(r0, FFT_N2), :].astype(F32)
    grp = 2 * SUBLANES
    pr0 = pl.multiple_of(jnp.maximum(j * FFT_N2 - grp, 0), grp)
    nr0 = pl.multiple_of(jnp.minimum((j + 1) * FFT_N2, (n_slabs - 1) * FFT_N2), grp)
    prev_row = u_ref[0, pl.ds(pr0, grp), :].astype(F32)[grp - 1:grp]
    next_row = u_ref[0, pl.ds(nr0, grp), :].astype(F32)[0:1]
    prev_row = jnp.where(j > 0, prev_row, 0.0)
    next_row = jnp.where(j < n_slabs - 1, next_row, 0.0)
    row = lax.broadcasted_iota(jnp.int32, cur.shape, 0)
    before = jnp.where(row == 0, prev_row, pltpu.roll(cur, 1, 0))
    after = jnp.where(row == FFT_N2 - 1, next_row, pltpu.roll(cur, FFT_N2 - 1, 0))
    w = cw_ref[...]
    return before * w[0:1] + cur * w[1:2] + after * w[2:3] + w[3:4]


def _hyena_kernel(x0_ref, x1_ref, v_ref, cw0_ref, cw1_ref, cwv_ref, hf_ref,
                  fa_ref, fai_ref, fb_ref, fbi_ref, o_ref, ts_ref, zs_ref):
    n_slabs = x0_ref.shape[1] // FFT_N2

    def gated_value(j):
        return _conv_slab(x1_ref, cw1_ref, j, n_slabs) * _conv_slab(v_ref, cwv_ref, j, n_slabs)

    def fill(j, carry):
        ts_ref[pl.ds(pl.multiple_of(j * TPITCH, SUBLANES), FFT_N2), :] = gated_value(j)
        return carry

    lax.fori_loop(0, n_slabs, fill, 0, unroll=2)

    def stage_a(n2, carry):
        sa = ts_ref[pl.ds(n2, n_slabs, stride=TPITCH), :].astype(BF16)
        sb = ts_ref[pl.ds(n2 + 1, n_slabs, stride=TPITCH), :].astype(BF16)
        zero = jnp.zeros_like(sa)
        rhs = jnp.concatenate([jnp.concatenate([sa, zero], axis=1), jnp.concatenate([zero, sb], axis=1)], axis=0)
        z = _dot(fa_ref[n2 // 2], rhs)
        for k in range(2):
            zk = z[:, k * LANES:(k + 1) * LANES]
            zs_ref[pl.ds(n2 + k, KP, stride=ZPITCH), :] = zk[:KP]
            zs_ref[pl.ds(FFT_N2 + n2 + k, KP, stride=ZPITCH), :] = zk[KP:]
        return carry

    lax.fori_loop(0, FFT_N2 // 2, lambda p, c: stage_a(2 * p, c), 0, unroll=8)

    def stage_b(p, carry):
        b0 = pl.multiple_of(2 * p * ZPITCH, SUBLANES)
        b1 = pl.multiple_of(b0 + ZPITCH, SUBLANES)
        z = jnp.concatenate([zs_ref[pl.ds(b0, 2 * FFT_N2), :], zs_ref[pl.ds(b1, 2 * FFT_N2), :]], axis=1)
        x = _dot(fb_ref[...], z.astype(BF16))
        h = jnp.concatenate([hf_ref[0, 2 * p], hf_ref[0, 2 * p + 1]], axis=1)
        xr, xi = x[:FFT_N2], x[FFT_N2:]
        hr, hi = h[:FFT_N2], h[FFT_N2:]
        prod = jnp.concatenate([xr * hr - xi * hi, xr * hi + xi * hr], axis=0)
        y = _dot(fbi_ref[...], prod.astype(BF16))
        zs_ref[pl.ds(b0, 2 * FFT_N2), :] = y[:, :LANES]
        zs_ref[pl.ds(b1, 2 * FFT_N2), :] = y[:, LANES:]
        return carry

    lax.fori_loop(0, KHP // 2, stage_b, 0, unroll=11)

    def stage_ai(n2, carry):
        yr = zs_ref[pl.ds(n2, KP, stride=ZPITCH), :]
        yi = zs_ref[pl.ds(FFT_N2 + n2, KP, stride=ZPITCH), :]
        y = jnp.concatenate([yr, yi], axis=0).astype(BF16)
        ts_ref[pl.ds(n2, n_slabs, stride=TPITCH), :] = _dot(fai_ref[n2], y)
        return carry

    lax.fori_loop(0, FFT_N2, stage_ai, 0, unroll=16)

    def finish(j, carry):
        conv = ts_ref[pl.ds(pl.multiple_of(j * TPITCH, SUBLANES), FFT_N2), :]
        y = _conv_slab(x0_ref, cw0_ref, j, n_slabs) * conv
        o_ref[0, pl.ds(pl.multiple_of(j * FFT_N2, FFT_N2), FFT_N2), :] = y.astype(BF16)
        return carry

    lax.fori_loop(0, n_slabs, finish, 0, unroll=2)


def _hyena(u, cw, hf, fa, fai, fb, fbi):
    b, seq, c3 = u.shape
    nblk = D_HYENA // LANES
    n_slabs = seq // FFT_N2
    stream = lambda k: pl.BlockSpec((1, seq, LANES), lambda j, i, k=k: (i, 0, k * nblk + j))
    cwspec = lambda k: pl.BlockSpec((SUBLANES, LANES), lambda j, i, k=k: (0, k * nblk + j))
    return pl.pallas_call(
        _hyena_kernel,
        grid=(nblk, b),
        in_specs=[stream(0), stream(1), stream(2), cwspec(0), cwspec(1), cwspec(2),
                  pl.BlockSpec((1, KHP, 2 * FFT_N2, LANES), lambda j, i: (j, 0, 0, 0),
                               pipeline_mode=pl.Buffered(1)),
                  _const_spec(fa.shape), _const_spec(fai.shape),
                  _const_spec(fb.shape), _const_spec(fbi.shape)],
        out_specs=pl.BlockSpec((1, seq, LANES), lambda j, i: (i, 0, j)),
        out_shape=jax.ShapeDtypeStruct((b, seq, D_HYENA), BF16),
        scratch_shapes=[pltpu.VMEM((n_slabs * TPITCH, LANES), F32),
                        pltpu.VMEM((KP * ZPITCH, LANES), F32)],
        compiler_params=_params(("arbitrary", "arbitrary")),
        name="hyena",
    )(u, u, u, cw, cw, cw, hf, fa, fai, fb, fbi)


def _attn_kernel(sink_ref, q_ref, kp_ref, km_ref, kn_ref, vp_ref, vm_ref, vn_ref, kc_ref, vc_ref,
                 o_ref, ka_ref, va_ref):
    i = pl.program_id(1)
    n_i = pl.num_programs(1)
    tq = q_ref.shape[0]
    nqb = tq // BLOCK
    ka_ref[0:BLOCK] = kp_ref[...]
    ka_ref[BLOCK:BLOCK + tq] = km_ref[...]
    ka_ref[BLOCK + tq:] = kn_ref[...]
    va_ref[0:BLOCK] = vp_ref[...]
    va_ref[BLOCK:BLOCK + tq] = vm_ref[...]
    va_ref[BLOCK + tq:] = vn_ref[...]

    qi = lax.broadcasted_iota(jnp.int32, (BLOCK, BLOCK), 0)
    kj = lax.broadcasted_iota(jnp.int32, (BLOCK, BLOCK), 1)
    lane = lax.broadcasted_iota(jnp.int32, (BLOCK, LANES), 1)
    low = lane < HEAD_DIM
    half = GQA_GROUP // 2
    rows2 = half * BLOCK
    hrow = lax.broadcasted_iota(jnp.int32, (GQA_GROUP * BLOCK, 1), 0) // BLOCK
    head_order = [hh for hh in range(GQA_GROUP) if hh % 2 == 0] + [hh for hh in range(GQA_GROUP) if hh % 2 == 1]
    one = jnp.ones((), BF16)

    def with_ones(v):
        lanes_low = lax.broadcasted_iota(jnp.int32, v.shape, 1) < HEAD_DIM
        return jnp.where(lanes_low, v, one), jnp.where(lanes_low, one, v)

    for g in range(N_KV_HEADS):
        gl = slice(g * LANES, (g + 1) * LANES)
        vc_even, vc_odd = with_ones(vc_ref[:, gl])
        sink = jnp.zeros((GQA_GROUP * BLOCK, 1), F32)
        for pos, hh in enumerate(head_order):
            sink = jnp.where(hrow == pos, sink_ref[g * GQA_GROUP + hh], sink)
        for j in range(nqb):
            prev_ok = kj >= qi
            next_ok = kj <= qi
            if j == 0:
                prev_ok = prev_ok & (i > 0)
            if j == nqb - 1:
                next_ok = next_ok & (i < n_i - 1)
            bias_p = jnp.concatenate([jnp.where(prev_ok, 0.0, NEG_INF).astype(F32)] * GQA_GROUP, axis=0)
            bias_n = jnp.concatenate([jnp.where(next_ok, 0.0, NEG_INF).astype(F32)] * GQA_GROUP, axis=0)
            qb = q_ref[j * BLOCK:(j + 1) * BLOCK, :]
            parts = []
            for hh in head_order:
                h = g * GQA_GROUP + hh
                qp = qb[:, (h // 2) * LANES:(h // 2 + 1) * LANES]
                parts.append(jnp.where(low if h % 2 == 0 else ~low, qp, jnp.zeros_like(qp)))
            qs = jnp.concatenate(parts, axis=0)
            kw = ka_ref[j * BLOCK:(j + 3) * BLOCK, gl]
            vw_even, vw_odd = with_ones(va_ref[j * BLOCK:(j + 3) * BLOCK, gl])
            s_w = _dot_nt(qs, kw)
            s_p = s_w[:, :BLOCK] + bias_p
            s_m = s_w[:, BLOCK:2 * BLOCK]
            s_n = s_w[:, 2 * BLOCK:] + bias_n
            s_c = _dot_nt(qs, kc_ref[:, gl])
            m = jnp.maximum(jnp.maximum(jnp.max(jnp.maximum(jnp.maximum(s_p, s_m), s_n), axis=1, keepdims=True),
                                        jnp.max(s_c, axis=1, keepdims=True)), sink)
            e_w = jnp.concatenate([jnp.exp(s_p - m), jnp.exp(s_m - m), jnp.exp(s_n - m)], axis=1).astype(BF16)
            e_c = jnp.exp(s_c - m).astype(BF16)
            e_sink = jnp.exp(sink - m)
            outs = []
            for par, (vw, vcx) in enumerate(((vw_even, vc_even), (vw_odd, vc_odd))):
                rs = slice(par * rows2, (par + 1) * rows2)
                acc = _dot(e_w[rs], vw) + _dot(e_c[rs], vcx)
                den = pltpu.roll(acc, HEAD_DIM, 1) + e_sink[rs]
                outs.append(acc / den)
            for pp in range(half):
                pair = jnp.where(low, outs[0][pp * BLOCK:(pp + 1) * BLOCK], outs[1][pp * BLOCK:(pp + 1) * BLOCK])
                col = (g * half + pp) * LANES
                o_ref[j * BLOCK:(j + 1) * BLOCK, col:col + LANES] = pair.astype(BF16)


def _attention(sinks, q, kd, vd, kc, vc, batch, seq, tq):
    t = q.shape[0]
    per_b = seq // tq
    nqb = tq // BLOCK
    nb = seq // BLOCK
    n_ctx = kc.shape[0] // batch
    main = lambda b, i: (b * per_b + i, 0)
    prev = lambda b, i: (b * nb + jnp.maximum(i * nqb - 1, 0), 0)
    nxt = lambda b, i: (b * nb + jnp.minimum(i * nqb + nqb, nb - 1), 0)
    kvw = kd.shape[1]
    return pl.pallas_call(
        _attn_kernel,
        grid=(batch, per_b),
        in_specs=[pl.BlockSpec(memory_space=pltpu.SMEM),
                  pl.BlockSpec((tq, D_ATTN), main),
                  pl.BlockSpec((BLOCK, kvw), prev), pl.BlockSpec((tq, kvw), main), pl.BlockSpec((BLOCK, kvw), nxt),
                  pl.BlockSpec((BLOCK, kvw), prev), pl.BlockSpec((tq, kvw), main), pl.BlockSpec((BLOCK, kvw), nxt),
                  pl.BlockSpec((n_ctx, kvw), lambda b, i: (b, 0)),
                  pl.BlockSpec((n_ctx, kvw), lambda b, i: (b, 0))],
        out_specs=pl.BlockSpec((tq, D_ATTN), main),
        out_shape=jax.ShapeDtypeStruct((t, D_ATTN), BF16),
        scratch_shapes=[pltpu.VMEM((tq + 2 * BLOCK, kvw), BF16),
                        pltpu.VMEM((tq + 2 * BLOCK, kvw), BF16)],
        compiler_params=_params(("arbitrary", "arbitrary")),
        name="attention",
    )(sinks, q, kd, kd, kd, vd, vd, vd, kc, vc)


def _route(t, wr_hi_ref, wr_hl_ref, br_ref):
    th, tl = _split(t)
    both = _dot(th, wr_hl_ref[...])
    logits = both[:, :LANES] + both[:, LANES:] + _dot(tl, wr_hi_ref[...]) + br_ref[...]
    lane_i = lax.broadcasted_iota(jnp.int32, logits.shape, 1)
    lane = lane_i.astype(F32)
    grp_of_lane = (lane_i >> 2).astype(F32)
    ninf = -jnp.inf
    far = float(LANES)
    is_g = (lane_i >= N_EXPERTS) & (lane_i < N_EXPERTS + N_GROUPS)
    glog = jnp.where(is_g, logits, ninf)
    gmax = jnp.max(glog, axis=1, keepdims=True)
    gidx = jnp.min(jnp.where(glog == gmax, lane - float(N_EXPERTS), far), axis=1, keepdims=True)
    group_p = 1.0 / jnp.sum(jnp.exp(glog - gmax), axis=1, keepdims=True)
    in_grp = (lane_i < N_EXPERTS) & (grp_of_lane == gidx)
    elog = jnp.where(in_grp, logits, ninf)
    v1 = jnp.max(elog, axis=1, keepdims=True)
    i1 = jnp.min(jnp.where(elog == v1, lane, far), axis=1, keepdims=True)
    elog2 = jnp.where(lane == i1, ninf, elog)
    v2 = jnp.max(elog2, axis=1, keepdims=True)
    i2 = jnp.min(jnp.where(elog2 == v2, lane, far), axis=1, keepdims=True)
    e = jnp.exp(v2 - v1)
    w1 = group_p / (1.0 + e)
    w2 = group_p * e / (1.0 + e)
    gate = jnp.where(lane == i1, w1, 0.0) + jnp.where(lane == i2, w2, 0.0)
    return gate + jnp.where(lane_i == GROUP_LANE, gidx, 0.0)


def _mix_out_kernel(x_ref, h_ref, yh_ref, ya_ref, g1_ref, sh2_ref, sc2_ref,
                    wg_ref, wbh_ref, wba_ref, wo_ref, lng_ref, lnb_ref, wrh_ref, wrhl_ref, br_ref,
                    x1_ref, gate_ref, gsel_ref, *t_refs):
    d = x_ref.shape[1]
    for r0 in range(0, x_ref.shape[0], SUB_ROWS):
        rows = slice(r0, r0 + SUB_ROWS)
        x = x_ref[rows, :]
        h = h_ref[rows, :]
        g_hy = jax.nn.sigmoid(_dot(h, wg_ref[:, :d]))
        merged = g_hy * _dot(yh_ref[rows, :], wbh_ref[...])
        g_at = jax.nn.sigmoid(_dot(h, wg_ref[:, d:]))
        merged = merged + g_at * _dot(ya_ref[rows, :], wba_ref[...])
        mix = _dot(merged.astype(BF16), wo_ref[...])
        x1 = _standardize(DEEPNORM_ALPHA * x + g1_ref[0] * mix) * lng_ref[...] + lnb_ref[...]
        x1_ref[rows, :] = x1
        t = _standardize(x1) * (1.0 + sc2_ref[0]) + sh2_ref[0]
        gate = _route(t, wrh_ref, wrhl_ref, br_ref)
        gate_ref[rows, :] = gate
        grp = gate.T[GROUP_LANE:GROUP_LANE + 1, :]
        gsel_ref[r0 // LANES:(r0 + SUB_ROWS) // LANES, :] = jnp.concatenate(
            [grp[:, k * LANES:(k + 1) * LANES] for k in range(SUB_ROWS // LANES)], axis=0)
        for t_ref, piece in zip(t_refs, _pack_pieces(t)):
            t_ref[rows, :] = piece


def _mix_out(x2d, h, yh, ya, mods, w_g, w_bh, w_ba, w_o, ln_g, ln_b, wr_hi, wr_hl, br, seq, tm):
    t, d = x2d.shape
    per_b = seq // tm
    row = lambda i: (i, 0)
    mod = lambda i: (i // per_b, 0, 0)
    mspec = pl.BlockSpec((1, 1, d), mod)
    g1, sh2, sc2 = mods
    return pl.pallas_call(
        _mix_out_kernel,
        grid=(t // tm,),
        in_specs=[pl.BlockSpec((tm, d), row),
                  pl.BlockSpec((tm, d), row),
                  pl.BlockSpec((tm, yh.shape[1]), row),
                  pl.BlockSpec((tm, ya.shape[1]), row),
                  mspec, mspec, mspec,
                  _const_spec(w_g.shape), _const_spec(w_bh.shape), _const_spec(w_ba.shape),
                  _const_spec(w_o.shape), _const_spec(ln_g.shape), _const_spec(ln_b.shape),
                  _const_spec(wr_hi.shape), _const_spec(wr_hl.shape), _const_spec(br.shape)],
        out_specs=[pl.BlockSpec((tm, d), row), pl.BlockSpec((tm, LANES), row),
                   pl.BlockSpec((tm // LANES, LANES), row)]
        + [pl.BlockSpec((tm, SC_ROW_WORDS), row)] * N_PIECES,
        out_shape=[jax.ShapeDtypeStruct((t, d), F32), jax.ShapeDtypeStruct((t, LANES), F32),
                   jax.ShapeDtypeStruct((t // LANES, LANES), F32)]
        + [jax.ShapeDtypeStruct((t, SC_ROW_WORDS), jnp.uint32)] * N_PIECES,
        compiler_params=_params(("arbitrary",)),
        name="mix_out",
    )(x2d, h, yh, ya, g1, sh2, sc2, w_g, w_bh, w_ba, w_o, ln_g, ln_b, wr_hi, wr_hl, br)


def _slots_kernel(g_ref, upper_ref, lower_ref, dest_ref, meta_ref):
    gsel = g_ref[...]
    dest = jnp.zeros(gsel.shape, F32)
    base = jnp.zeros((1, 1), F32)
    chunk_start = lax.broadcasted_iota(jnp.int32, (1, LANES), 1).astype(F32) * float(MOE_CHUNK)
    owner = jnp.zeros((1, LANES), F32)
    for g in range(N_GROUPS):
        onehot = jnp.where(gsel == float(g), 1.0, 0.0)
        in_row = _dot(onehot.astype(BF16), upper_ref[...])
        row_tot = jnp.sum(onehot, axis=1, keepdims=True)
        rows_before = _dot(lower_ref[...], jnp.broadcast_to(row_tot, onehot.shape).astype(BF16))
        dest = dest + onehot * (base + rows_before + in_row)
        if g > 0:
            owner = owner + jnp.where(chunk_start >= base, 1.0, 0.0)
        n_g = jnp.sum(row_tot, axis=0, keepdims=True)
        base = base + jnp.floor((n_g + float(MOE_CHUNK - 1)) * (1.0 / MOE_CHUNK)) * float(MOE_CHUNK)
    dest_ref[...] = dest.astype(jnp.int32)
    row = lax.broadcasted_iota(jnp.int32, meta_ref.shape, 0)
    meta_ref[...] = jnp.where(row == 0, owner, base * (1.0 / MOE_CHUNK)).astype(jnp.int32)


def _slots(gsel):
    r = gsel.shape[0]
    upper = jnp.asarray(np.triu(np.ones((LANES, LANES), np.float32), 1)).astype(BF16)
    lower = jnp.asarray(np.tril(np.ones((r, r), np.float32), -1)).astype(BF16)
    full = lambda a: pl.BlockSpec(a.shape, lambda i: (0,) * a.ndim)
    return pl.pallas_call(
        _slots_kernel,
        grid=(1,),
        in_specs=[full(gsel), full(upper), full(lower)],
        out_specs=[pl.BlockSpec((r, LANES), lambda i: (0, 0)), pl.BlockSpec((SUBLANES, LANES), lambda i: (0, 0))],
        out_shape=[jax.ShapeDtypeStruct((r, LANES), jnp.int32), jax.ShapeDtypeStruct((SUBLANES, LANES), jnp.int32)],
        compiler_params=_params(("arbitrary",)),
        name="moe_slots",
    )(gsel, upper, lower)


def _experts_kernel(owner_ref, used_ref, *refs):
    n_p = N_PIECES
    t_refs, gate_ref = refs[:n_p], refs[n_p]
    wg_ref, wu_ref, wd_ref = refs[n_p + 1:n_p + 4]
    y_refs = refs[n_p + 4:]
    c = pl.program_id(0)

    @pl.when(c < used_ref[0])
    def _():
        t = _unpack_pieces([r[...] for r in t_refs]).astype(BF16)
        gate = gate_ref[...]
        lane = lax.broadcasted_iota(jnp.int32, gate.shape, 1)
        first = owner_ref[c] * EXPERTS_PER_GROUP
        parts = []
        for e in range(EXPERTS_PER_GROUP):
            a = _dot(t, wg_ref[e])
            u = _dot(t, wu_ref[e])
            ge = jnp.sum(jnp.where(lane == first + e, gate, 0.0), axis=1, keepdims=True)
            parts.append((a * jax.nn.sigmoid(a) * u * ge).astype(BF16))
        y = _dot(jnp.concatenate(parts, axis=1), wd_ref[0])
        for y_ref, piece in zip(y_refs, _pack_pieces(y)):
            y_ref[...] = piece

    @pl.when(c >= used_ref[0])
    def _():
        for y_ref in y_refs:
            y_ref[...] = jnp.zeros_like(y_ref)


def _experts(owner, used, t_pieces, gate_sorted, wg16, wu16, wd3):
    n_slots = gate_sorted.shape[0]
    row = lambda c, owner, used: (c, 0)
    by_owner = lambda c, owner, used: (owner[c], 0, 0)
    piece = pl.BlockSpec((MOE_CHUNK, SC_ROW_WORDS), row)
    group_of_experts = lambda w: pl.BlockSpec((EXPERTS_PER_GROUP,) + w.shape[1:], by_owner)
    return pl.pallas_call(
        _experts_kernel,
        grid_spec=pltpu.PrefetchScalarGridSpec(
            num_scalar_prefetch=2,
            grid=(n_slots // MOE_CHUNK,),
            in_specs=[piece] * len(t_pieces) + [pl.BlockSpec((MOE_CHUNK, LANES), row),
                                                group_of_experts(wg16), group_of_experts(wu16),
                                                pl.BlockSpec((1,) + wd3.shape[1:], by_owner)],
            out_specs=[piece] * len(t_pieces),
        ),
        out_shape=[jax.ShapeDtypeStruct((n_slots, SC_ROW_WORDS), jnp.uint32)] * len(t_pieces),
        compiler_params=_params(("arbitrary",)),
        name="moe_experts",
    )(owner, used, *t_pieces, gate_sorted, wg16, wu16, wd3)


def _final_kernel(x1_ref, g2_ref, lng_ref, lnb_ref, *refs):
    y_refs, o_ref = refs[:-1], refs[-1]
    y = _unpack_pieces([r[...] for r in y_refs])
    o_ref[...] = _standardize(DEEPNORM_ALPHA * x1_ref[...] + g2_ref[0] * y) * lng_ref[...] + lnb_ref[...]


def _final(x1, g2, ln_g, ln_b, y_pieces, seq, tm):
    t, d = x1.shape
    per_b = seq // tm
    row = lambda i: (i, 0)
    return pl.pallas_call(
        _final_kernel,
        grid=(t // tm,),
        in_specs=[pl.BlockSpec((tm, d), row), pl.BlockSpec((1, 1, d), lambda i: (i // per_b, 0, 0)),
                  _const_spec(ln_g.shape), _const_spec(ln_b.shape)]
        + [pl.BlockSpec((tm, SC_ROW_WORDS), row)] * len(y_pieces),
        out_specs=pl.BlockSpec((tm, d), row),
        out_shape=jax.ShapeDtypeStruct((t, d), F32),
        compiler_params=_params(("arbitrary",)),
        name="moe_final",
    )(x1, g2, ln_g, ln_b, *y_pieces)


def _sc_move_rows(tables, idx, n_out, scatter, name):
    n = idx.shape[0]
    mesh = plsc.VectorSubcoreMesh(core_axis_name="c", subcore_axis_name="s")
    out_type = [jax.ShapeDtypeStruct((n_out, t.shape[1]), t.dtype) for t in tables]
    window = lambda i: (i, 0)
    index_win = pl.BlockSpec((1, SC_WINDOW), lambda i: (0, i))
    split = dict(core_axis_name=("c", "s"), dimension_semantics=(pltpu.PARALLEL,))

    @functools.partial(pl.kernel, out_type=out_type, mesh=mesh, scratch_types=[], name=name)
    def move(*refs):
        srcs, i_hbm, dsts = refs[:len(tables)], refs[len(tables)], refs[len(tables) + 1:]
        for src, dst in zip(srcs, dsts):
            rows_win = pl.BlockSpec((SC_WINDOW, src.shape[1]), window)
            if scatter:
                def body(x_vmem, i_vmem, dst=dst):
                    pltpu.sync_copy(x_vmem, dst.at[i_vmem.at[0]])

                pltpu.emit_pipeline(body, grid=(n // SC_WINDOW,), in_specs=[rows_win, index_win],
                                    out_specs=[], **split)(src, i_hbm)
            else:
                def body(i_vmem, o_vmem, src=src):
                    pltpu.sync_copy(src.at[i_vmem.at[0]], o_vmem)

                pltpu.emit_pipeline(body, grid=(n // SC_WINDOW,), in_specs=[index_win],
                                    out_specs=[rows_win], **split)(i_hbm, dst)

    return move(*tables, idx.reshape(1, n))


def _dft_tables(seq):
    n = 2 * seq
    n1_full = n // FFT_N2
    n1_data = seq // FFT_N2
    k1 = np.arange(KH, dtype=np.float64)[None, :, None]
    n2 = np.arange(FFT_N2, dtype=np.float64)[:, None, None]
    n1 = np.arange(n1_full, dtype=np.float64)[None, None, :]
    ang = 2.0 * np.pi * k1 * (FFT_N2 * n1 + n2) / n
    fa = np.zeros((FFT_N2, 2 * KP, n1_full))
    fa[:, :KH] = np.cos(ang)
    fa[:, KP:KP + KH] = -np.sin(ang)
    wgt = np.full((KH,), 2.0)
    wgt[0] = 1.0
    wgt[KH - 1] = 1.0
    fai = np.zeros((FFT_N2, n1_data, 2 * KP))
    angt = np.transpose(ang[:, :, :n1_data], (0, 2, 1))
    fai[:, :, :KH] = np.cos(angt) * wgt / n
    fai[:, :, KP:KP + KH] = -np.sin(angt) * wgt / n
    kk = np.arange(FFT_N2, dtype=np.float64)
    a2 = 2.0 * np.pi * np.outer(kk, kk) / FFT_N2
    fr, fi = np.cos(a2), -np.sin(a2)
    fb = np.block([[fr, -fi], [fi, fr]])
    fbi = np.block([[fr, fi], [-fi, fr]])
    f32 = lambda a: jnp.asarray(a.astype(np.float32))
    fa_data = fa[:, :, :n1_data]
    fa_pairs = np.concatenate([fa_data[0::2], fa_data[1::2]], axis=2)
    return f32(fa_pairs), f32(fa), f32(fai), f32(fb), f32(fbi)


def _filter_features(seq, rows):
    f32 = np.float32
    t = np.linspace(0.0, 1.0, seq, dtype=f32)
    w = (f32(2.0 * math.pi) * np.arange(seq, dtype=f32) / f32(seq)).astype(f32)
    t2 = np.concatenate([t, t[::-1]])
    w2 = np.concatenate([w, w[::-1]])
    bands = np.linspace(1e-4, FILTER_BANDS - 1, FILTER_BANDS, dtype=f32)
    max_decay = math.log(DECAY_TARGET) / FAST_DECAY_PCT
    min_decay = math.log(DECAY_TARGET) / SLOW_DECAY_PCT
    deltas = jnp.linspace(min_decay, max_decay, D_HYENA, dtype=F32)

    def feats(k):
        tp = t2.reshape(-1, 2, rows // 2)[:, k].reshape(-1, 1)
        wp = w2.reshape(-1, 2, rows // 2)[:, k].reshape(-1, 1)
        pad = np.zeros((tp.shape[0], FILTER_ORDER - FILTER_EMB), f32)
        arg = (bands * wp).astype(f32)
        return np.concatenate([tp, np.cos(arg), -np.sin(arg), pad], axis=-1).astype(f32)

    zp = jnp.asarray(np.concatenate([feats(0), feats(1)], axis=-1))
    return zp, jnp.asarray(t2)[:, None], jnp.abs(deltas)[None, :]


def _rope_tables(seq):
    f32 = np.float32
    rows = seq // GRID_W
    row = np.repeat(np.arange(rows, dtype=f32), GRID_W)
    col = np.tile(np.arange(GRID_W, dtype=f32), rows)
    half = HEAD_DIM // 2
    inv_freq = (f32(ROPE_BASE) ** (-np.arange(0, half, 2, dtype=f32) / f32(half))).astype(f32)
    ang = np.concatenate([row[:, None] * inv_freq, col[:, None] * inv_freq], axis=-1).astype(f32)
    cos, sin = np.cos(ang), np.sin(ang)
    c64 = np.concatenate([cos, cos], axis=-1)
    s64 = np.concatenate([-sin, sin], axis=-1)
    return (jnp.asarray(np.concatenate([c64, c64], axis=-1).astype(f32)),
            jnp.asarray(np.concatenate([s64, s64], axis=-1).astype(f32)))


def _head_perm(n_heads):
    idx = []
    for h in range(n_heads):
        base = h * HEAD_DIM
        idx += [base + 2 * j for j in range(HEAD_DIM // 2)]
        idx += [base + 2 * j + 1 for j in range(HEAD_DIM // 2)]
    return np.asarray(idx, dtype=np.int32)


def _dup_heads(w):
    parts = []
    for g in range(N_KV_HEADS):
        blk = w[:, g * HEAD_DIM:(g + 1) * HEAD_DIM]
        parts += [blk, blk]
    return jnp.concatenate(parts, axis=1)


def kernel(x, c, ctx, c_ctx, ada_w, ada_b, w_in, hy_conv_w, hy_conv_b, hy_w1, hy_b1, hy_w2, hy_b2, hy_w3, hy_b3, hy_w4, hy_freq, hy_bias, attn_sinks, w_branch_hy, w_branch_attn, w_out, ln1_g, ln1_b, w_group, b_group, w_router, b_router, w_gate_e, w_up_e, w_down_e, ln2_g, ln2_b):
    batch, seq, d = x.shape
    n_ctx = ctx.shape[1]
    assert d == D_MODEL and ada_w.shape[0] == DEPTH == 1
    assert 2 * seq == FFT_N2 * FFT_N2 and seq % TM_PROJ == 0
    l = 0

    w = w_in[l]
    s0, s1, s2, s3 = 3 * D_HYENA, 3 * D_HYENA + D_ATTN, 3 * D_HYENA + D_ATTN + D_KV, 3 * D_HYENA + D_ATTN + 2 * D_KV
    w_q = w[:, s0:s1][:, _head_perm(N_HEADS)]
    w_k = _dup_heads(w[:, s1:s2][:, _head_perm(N_KV_HEADS)])
    w_v = _dup_heads(w[:, s2:s3])
    w_c = jnp.concatenate([w[:, :s0], w_q, w_k, w_v], axis=1).astype(BF16)
    w_kv = jnp.concatenate([w_k, w_v], axis=1).astype(BF16)
    w_g = w[:, s3:].astype(BF16)
    w_bh = w_branch_hy[l].astype(BF16)
    w_ba = w_branch_attn[l].astype(BF16)
    w_o = w_out[l].astype(BF16)
    wr = jnp.zeros((d, LANES), F32)
    wr = wr.at[:, :N_EXPERTS].set(w_router[l]).at[:, N_EXPERTS:N_EXPERTS + N_GROUPS].set(w_group[l])
    wr_hi = wr.astype(BF16)
    wr_hl = jnp.concatenate([wr_hi, (wr - wr_hi.astype(F32)).astype(BF16)], axis=1)
    br = jnp.zeros((1, LANES), F32)
    br = br.at[0, :N_EXPERTS].set(b_router[l]).at[0, N_EXPERTS:N_EXPERTS + N_GROUPS].set(b_group[l])
    wg16 = w_gate_e[l].astype(BF16)
    wu16 = w_up_e[l].astype(BF16)
    wd3 = w_down_e[l].astype(BF16).reshape(N_GROUPS, EXPERTS_PER_GROUP * D_EXPERT, d)
    cw = jnp.concatenate([hy_conv_w[l][:, 0, :], hy_conv_b[l][None, :],
                          jnp.zeros((SUBLANES - SHORT_CONV - 1, 3 * D_HYENA), F32)], axis=0)
    row2 = lambda a: a.reshape(1, -1)
    pair = lambda a: jnp.concatenate([a, a], axis=-1)
    zero_o = jnp.zeros((FILTER_ORDER, FILTER_ORDER), F32)
    bdiag = lambda a: jnp.concatenate([jnp.concatenate([a, zero_o], axis=1),
                                       jnp.concatenate([zero_o, a], axis=1)], axis=0)
    w1p = bdiag(jnp.concatenate([hy_w1[l], jnp.zeros((FILTER_ORDER - FILTER_EMB, FILTER_ORDER), F32)], axis=0))
    w4h = jnp.transpose(hy_w4[l].reshape(FILTER_ORDER, 2, D_HYENA), (1, 0, 2))
    zero_w4 = jnp.zeros_like(w4h)
    w4s = jnp.stack([jnp.concatenate([w4h, zero_w4], axis=1),
                     jnp.concatenate([zero_w4, w4h], axis=1)], axis=1)

    fa, fa_full, fai, fb, fbi = _dft_tables(seq)
    fa, fa_full, fai, fb, fbi = (a.astype(BF16) for a in (fa, fa_full, fai, fb, fbi))
    zp, t_col, absd = _filter_features(seq, FILT_ROWS)
    cos_t, sin_t = _rope_tables(seq)

    cond = jnp.concatenate([c, c_ctx[None], jnp.zeros((SUBLANES - batch - 1, d), F32)], axis=0)
    mods = _adaln(cond, ada_w[l], ada_b[l])
    m6 = [mods[:, k * d:(k + 1) * d].reshape(SUBLANES, 1, d) for k in range(6)]
    sh1, sc1, g1, sh2, sc2, g2 = m6

    kc, vc = _ctx_kv(ctx.reshape(batch * n_ctx, d), sh1[batch], sc1[batch], w_kv, n_ctx)

    x2d = x.reshape(batch * seq, d)
    u, q, kd, vd, h_mod = _in_proj(x2d, sh1, sc1, w_c, cos_t, sin_t, seq, TM_PROJ)
    h2u, ss = _filt_mlp(zp, w1p, pair(row2(hy_b1[l])), bdiag(hy_w2[l]), pair(row2(hy_b2[l])),
                        bdiag(hy_w3[l]), pair(row2(hy_b3[l])), w4s, pair(row2(hy_freq[l])), t_col, absd, FILT_ROWS)
    hf = _filt_fft(h2u, ss, row2(hy_bias[l]), fa_full, fb)
    y_hy = _hyena(u.reshape(batch, seq, 3 * D_HYENA), cw, hf, fa, fai, fb, fbi)
    y_at = _attention(attn_sinks[l], q, kd, vd, kc, vc, batch, seq, TQ_ATTN)
    x1, gate, gsel, *t_pieces = _mix_out(x2d, h_mod, y_hy.reshape(batch * seq, D_HYENA), y_at, (g1, sh2, sc2),
                                         w_g, w_bh, w_ba, w_o, row2(ln1_g[l]), row2(ln1_b[l]), wr_hi, wr_hl, br,
                                         seq, TM_PROJ)

    n_tok = batch * seq
    n_slots = n_tok + N_GROUPS * MOE_CHUNK
    dest2d, meta = _slots(gsel)
    dest = dest2d.reshape(n_tok)
    *t_sorted, gate_sorted = _sc_move_rows(t_pieces + [gate], dest, n_slots, True, "moe_sort")
    y_sorted = _experts(meta[0], meta[1, :1], t_sorted, gate_sorted, wg16, wu16, wd3)
    y_pieces = _sc_move_rows(y_sorted, dest, n_tok, False, "moe_unsort")
    out = _final(x1, g2, row2(ln2_g[l]), row2(ln2_b[l]), y_pieces, seq, TM_FINAL)
    return out.reshape(batch, seq, d)
```

```python
import functools
import math

import numpy as np
import jax
import jax.numpy as jnp
from jax import lax
from jax.experimental import pallas as pl
from jax.experimental.pallas import tpu as pltpu
from jax.experimental.pallas import tpu_sc as plsc

F32 = jnp.float32
BF16 = jnp.bfloat16

D_MODEL = 1024
GRID_W = 64
D_HYENA = D_MODEL // 2
SHORT_CONV = 3
FILTER_BANDS = 16
FILTER_EMB = 1 + 2 * FILTER_BANDS
FILTER_ORDER = 64
DECAY_TARGET = 1e-2
FAST_DECAY_PCT = 0.3
SLOW_DECAY_PCT = 1.5
HEAD_DIM = 64
D_ATTN = D_MODEL // 2
N_HEADS = D_ATTN // HEAD_DIM
N_KV_HEADS = N_HEADS // 4
GQA_GROUP = N_HEADS // N_KV_HEADS
D_KV = N_KV_HEADS * HEAD_DIM
WINDOW = 128
BLOCK = 128
ROPE_BASE = 10000.0
NEG_INF = -1e30
N_GROUPS = 4
EXPERTS_PER_GROUP = 4
N_EXPERTS = N_GROUPS * EXPERTS_PER_GROUP
D_EXPERT = D_MODEL // 4
LN_EPS = 1e-5
DEPTH = 1
DEEPNORM_ALPHA = (2.0 * DEPTH) ** 0.25

LANES = 128
SUBLANES = 8
VMEM_LIMIT = 56 * 1024 * 1024

SUB_ROWS = 512
TM_PROJ = 1024
MOE_CHUNK = 512
TM_FINAL = 1024
GROUP_LANE = N_EXPERTS
SC_WINDOW = 128
SC_ROW_WORDS = 256
N_PIECES = D_MODEL // (2 * SC_ROW_WORDS)
TQ_ATTN = 1024
FILT_ROWS = 2048

FFT_N2 = 128
KH = 65
KHP = 66
KP = 72
ZPITCH = 2 * FFT_N2 + SUBLANES
TPITCH = FFT_N2 + SUBLANES


def _dot(a, b):
    return jnp.dot(a, b, preferred_element_type=F32)


def _dot_nt(a, b):
    return lax.dot_general(a, b, (((1,), (1,)), ((), ())), preferred_element_type=F32)


def _split(a):
    hi = a.astype(BF16)
    lo = (a - hi.astype(F32)).astype(BF16)
    return hi, lo


def _pack_pieces(x):
    w = SC_ROW_WORDS
    pieces = []
    for p in range(x.shape[1] // (2 * w)):
        hi = lax.bitcast_convert_type(x[:, 2 * p * w:(2 * p + 1) * w].astype(BF16).astype(F32), jnp.uint32)
        lo = lax.bitcast_convert_type(x[:, (2 * p + 1) * w:(2 * p + 2) * w].astype(BF16).astype(F32), jnp.uint32)
        pieces.append(hi | (lo >> 16))
    return pieces


def _unpack_pieces(pieces):
    cols = []
    for word in pieces:
        cols.append(lax.bitcast_convert_type(word & jnp.uint32(0xFFFF0000), F32))
        cols.append(lax.bitcast_convert_type(word << 16, F32))
    return jnp.concatenate(cols, axis=1)


def _dot3(a, b):
    ah, al = _split(a)
    bh, bl = _split(b)
    return _dot(ah, bh) + _dot(al, bh) + _dot(ah, bl)


def _standardize(x):
    mu = jnp.mean(x, axis=-1, keepdims=True)
    xc = x - mu
    var = jnp.mean(xc * xc, axis=-1, keepdims=True)
    return xc * lax.rsqrt(var + LN_EPS)


def _params(sem, vmem=VMEM_LIMIT):
    return pltpu.CompilerParams(dimension_semantics=sem, vmem_limit_bytes=vmem)


def _const_spec(shape):
    nd = len(shape)
    return pl.BlockSpec(shape, lambda *_: (0,) * nd, pipeline_mode=pl.Buffered(1))


def _adaln_kernel(c_ref, w_ref, b_ref, o_ref):
    s = c_ref[...]
    s = s * jax.nn.sigmoid(s)
    o_ref[...] = _dot3(s, w_ref[...]) + b_ref[...]


def _adaln(cond, w, b):
    n, d = cond.shape
    cols = w.shape[1]
    bc = 1024
    return pl.pallas_call(
        _adaln_kernel,
        grid=(cols // bc,),
        in_specs=[pl.BlockSpec((n, d), lambda j: (0, 0)),
                  pl.BlockSpec((d, bc), lambda j: (0, j)),
                  pl.BlockSpec((1, bc), lambda j: (0, j))],
        out_specs=pl.BlockSpec((n, bc), lambda j: (0, j)),
        out_shape=jax.ShapeDtypeStruct((n, cols), F32),
        compiler_params=_params(("arbitrary",)),
        name="adaln",
    )(cond, w, b.reshape(1, cols))


def _ctx_kv_kernel(x_ref, sh_ref, sc_ref, w_ref, k_ref, v_ref):
    h = _standardize(x_ref[...]) * (1.0 + sc_ref[...]) + sh_ref[...]
    kv = _dot(h.astype(BF16), w_ref[...])
    half = k_ref.shape[1]
    k_ref[...] = kv[:, :half].astype(BF16)
    v_ref[...] = kv[:, half:].astype(BF16)


def _ctx_kv(ctx2d, sh, sc, w_kv, rows):
    n, d = ctx2d.shape
    half = w_kv.shape[1] // 2
    return pl.pallas_call(
        _ctx_kv_kernel,
        grid=(n // rows,),
        in_specs=[pl.BlockSpec((rows, d), lambda i: (i, 0)),
                  pl.BlockSpec((1, d), lambda i: (0, 0)),
                  pl.BlockSpec((1, d), lambda i: (0, 0)),
                  pl.BlockSpec(w_kv.shape, lambda i: (0, 0))],
        out_specs=[pl.BlockSpec((rows, half), lambda i: (i, 0)),
                   pl.BlockSpec((rows, half), lambda i: (i, 0))],
        out_shape=[jax.ShapeDtypeStruct((n, half), BF16)] * 2,
        compiler_params=_params(("arbitrary",)),
        name="ctx_kv",
    )(ctx2d, sh, sc, w_kv)


def _rope(x, cos_t, sin_t):
    width = x.shape[1]
    reps = width // LANES
    c = jnp.concatenate([cos_t] * reps, axis=1)
    s = jnp.concatenate([sin_t] * reps, axis=1)
    half = HEAD_DIM // 2
    lane = lax.broadcasted_iota(jnp.int32, x.shape, 1)
    first_half = (lane & (HEAD_DIM - 1)) < half
    partner = jnp.where(first_half, pltpu.roll(x, width - half, 1), pltpu.roll(x, half, 1))
    return x * c + partner * s


def _in_proj_kernel(x_ref, sh_ref, sc_ref, w_ref, cos_ref, sin_ref, u_ref, q_ref, k_ref, v_ref, h_ref):
    n_u = u_ref.shape[1]
    n_q = q_ref.shape[1]
    n_k = k_ref.shape[1]
    for r0 in range(0, x_ref.shape[0], SUB_ROWS):
        rows = slice(r0, r0 + SUB_ROWS)
        h = (_standardize(x_ref[rows, :]) * (1.0 + sc_ref[0]) + sh_ref[0]).astype(BF16)
        h_ref[rows, :] = h
        u_ref[rows, :] = _dot(h, w_ref[:, :n_u]).astype(BF16)
        cos_t = cos_ref[rows, :]
        sin_t = sin_ref[rows, :]
        q = _dot(h, w_ref[:, n_u:n_u + n_q])
        q_ref[rows, :] = (_rope(q, cos_t, sin_t) * (HEAD_DIM ** -0.5)).astype(BF16)
        k = _dot(h, w_ref[:, n_u + n_q:n_u + n_q + n_k])
        k_ref[rows, :] = _rope(k, cos_t, sin_t).astype(BF16)
        v_ref[rows, :] = _dot(h, w_ref[:, n_u + n_q + n_k:]).astype(BF16)


def _in_proj(x2d, sh, sc, w_c, cos_t, sin_t, seq, tm):
    t, d = x2d.shape
    per_b = seq // tm
    n_u, n_q, n_k = 3 * D_HYENA, D_ATTN, 2 * D_KV
    row = lambda i: (i, 0)
    mod = lambda i: (i // per_b, 0, 0)
    pos = lambda i: (i % per_b, 0)
    return pl.pallas_call(
        _in_proj_kernel,
        grid=(t // tm,),
        in_specs=[pl.BlockSpec((tm, d), row),
                  pl.BlockSpec((1, 1, d), mod),
                  pl.BlockSpec((1, 1, d), mod),
                  _const_spec(w_c.shape),
                  pl.BlockSpec((tm, LANES), pos),
                  pl.BlockSpec((tm, LANES), pos)],
        out_specs=[pl.BlockSpec((tm, n_u), row),
                   pl.BlockSpec((tm, n_q), row),
                   pl.BlockSpec((tm, n_k), row),
                   pl.BlockSpec((tm, n_k), row),
                   pl.BlockSpec((tm, d), row)],
        out_shape=[jax.ShapeDtypeStruct((t, n_u), BF16),
                   jax.ShapeDtypeStruct((t, n_q), BF16),
                   jax.ShapeDtypeStruct((t, n_k), BF16),
                   jax.ShapeDtypeStruct((t, n_k), BF16),
                   jax.ShapeDtypeStruct((t, d), BF16)],
        compiler_params=_params(("arbitrary",)),
        name="in_proj",
    )(x2d, sh, sc, w_c, cos_t, sin_t)


def _filt_mlp_kernel(z_ref, w1_ref, b1_ref, w2_ref, b2_ref, w3_ref, b3_ref, w4_ref, fr_ref, t_ref, absd_ref,
                     h_ref, ss_ref):
    fr = fr_ref[...]
    a = jnp.sin(fr * (_dot3(z_ref[...], w1_ref[...]) + b1_ref[...]))
    a = jnp.sin(fr * (_dot3(a, w2_ref[...]) + b2_ref[...]))
    a = jnp.sin(fr * (_dot3(a, w3_ref[...]) + b3_ref[...]))
    half = a.shape[0]
    ss = jnp.zeros(ss_ref.shape, F32)
    for k in range(2):
        decay = jnp.exp(-t_ref[k * half:(k + 1) * half, :] * absd_ref[...])
        h = _dot3(a, w4_ref[0, k]) * decay
        ss = ss + jnp.sum(h * h, axis=0, keepdims=True)
        for s in range(half // FFT_N2):
            r0 = (k * (half // FFT_N2) + s) * TPITCH
            h_ref[r0:r0 + FFT_N2, :] = h[s * FFT_N2:(s + 1) * FFT_N2]
            h_ref[r0 + FFT_N2:r0 + TPITCH, :] = jnp.zeros((TPITCH - FFT_N2, h.shape[1]), F32)

    @pl.when(pl.program_id(0) == 0)
    def _():
        ss_ref[...] = jnp.zeros_like(ss_ref)

    ss_ref[...] += ss


def _filt_mlp(zp, w1p, b1, w2, b2, w3, b3, w4s, fr, t_col, absd, rows):
    n, c = t_col.shape[0], absd.shape[1]
    half_steps = (n // 2) // rows
    vec = lambda a: pl.BlockSpec(a.shape, lambda i: (0,) * a.ndim)
    return pl.pallas_call(
        _filt_mlp_kernel,
        grid=(n // rows,),
        in_specs=[pl.BlockSpec((rows // 2, zp.shape[1]), lambda i: (i, 0)),
                  vec(w1p), vec(b1), vec(w2), vec(b2), vec(w3), vec(b3),
                  pl.BlockSpec((1,) + w4s.shape[1:], lambda i: (i // half_steps, 0, 0, 0)),
                  vec(fr),
                  pl.BlockSpec((rows, 1), lambda i: (i, 0)),
                  vec(absd)],
        out_specs=[pl.BlockSpec((rows // FFT_N2 * TPITCH, c), lambda i: (i, 0)),
                   pl.BlockSpec((1, c), lambda i: (0, 0))],
        out_shape=[jax.ShapeDtypeStruct((n // FFT_N2 * TPITCH, c), F32), jax.ShapeDtypeStruct((1, c), F32)],
        compiler_params=_params(("arbitrary",)),
        name="filt_mlp",
    )(zp, w1p, b1, w2, b2, w3, b3, w4s, fr, t_col, absd)


def _filt_fft_kernel(h_ref, ss_ref, bias_ref, fa_ref, fb_ref, o_ref, zs_ref):
    scale = lax.rsqrt(ss_ref[...] + 1e-6)
    n1 = h_ref.shape[0] // TPITCH
    row = lax.broadcasted_iota(jnp.int32, (2 * FFT_N2, LANES), 0)
    impulse = jnp.where(row < FFT_N2, bias_ref[...], 0.0)

    def stage_a(n2, carry):
        sa = h_ref[pl.ds(n2, n1, stride=TPITCH), :].astype(BF16)
        sb = h_ref[pl.ds(n2 + 1, n1, stride=TPITCH), :].astype(BF16)
        zero = jnp.zeros_like(sa)
        rhs = jnp.concatenate([jnp.concatenate([sa, zero], axis=1), jnp.concatenate([zero, sb], axis=1)], axis=0)
        z = _dot(fa_ref[n2 // 2], rhs)
        for k in range(2):
            zk = z[:, k * LANES:(k + 1) * LANES]
            zs_ref[pl.ds(n2 + k, KP, stride=ZPITCH), :] = zk[:KP]
            zs_ref[pl.ds(FFT_N2 + n2 + k, KP, stride=ZPITCH), :] = zk[KP:]
        return carry

    lax.fori_loop(0, FFT_N2 // 2, lambda p, c: stage_a(2 * p, c), 0, unroll=4)

    def stage_b(p, carry):
        b0 = pl.multiple_of(2 * p * ZPITCH, SUBLANES)
        b1 = pl.multiple_of(b0 + ZPITCH, SUBLANES)
        z = jnp.concatenate([zs_ref[pl.ds(b0, 2 * FFT_N2), :], zs_ref[pl.ds(b1, 2 * FFT_N2), :]], axis=1)
        x = _dot(fb_ref[...], z.astype(BF16))
        o_ref[0, 2 * p] = x[:, :LANES] * scale + impulse
        o_ref[0, 2 * p + 1] = x[:, LANES:] * scale + impulse
        return carry

    lax.fori_loop(0, KHP // 2, stage_b, 0, unroll=11)


def _filt_fft(h2u, ss, bias, fa_full, fb):
    n, c = h2u.shape
    nblk = c // LANES
    return pl.pallas_call(
        _filt_fft_kernel,
        grid=(nblk,),
        in_specs=[pl.BlockSpec((n, LANES), lambda j: (0, j), pipeline_mode=pl.Buffered(1)),
                  pl.BlockSpec((1, LANES), lambda j: (0, j)),
                  pl.BlockSpec((1, LANES), lambda j: (0, j)),
                  _const_spec(fa_full.shape),
                  _const_spec(fb.shape)],
        out_specs=pl.BlockSpec((1, KHP, 2 * FFT_N2, LANES), lambda j: (j, 0, 0, 0)),
        out_shape=jax.ShapeDtypeStruct((nblk, KHP, 2 * FFT_N2, LANES), F32),
        scratch_shapes=[pltpu.VMEM((KP * ZPITCH, LANES), F32)],
        compiler_params=_params(("arbitrary",)),
        name="filt_fft",
    )(h2u, ss, bias, fa_full, fb)


def _conv_slab(u_ref, cw_ref, j, n_slabs):
    r0 = pl.multiple_of(j * FFT_N2, FFT_N2)
    cur = u_ref[0, pl.ds(r0, FFT_N2), :].astype(F32)
    grp = 2 * SUBLANES
    pr0 = pl.multiple_of(jnp.maximum(j * FFT_N2 - grp, 0), grp)
    nr0 = pl.multiple_of(jnp.minimum((j + 1) * FFT_N2, (n_slabs - 1) * FFT_N2), grp)
    prev_row = u_ref[0, pl.ds(pr0, grp), :].astype(F32)[grp - 1:grp]
    next_row = u_ref[0, pl.ds(nr0, grp), :].astype(F32)[0:1]
    prev_row = jnp.where(j > 0, prev_row, 0.0)
    next_row = jnp.where(j < n_slabs - 1, next_row, 0.0)
    row = lax.broadcasted_iota(jnp.int32, cur.shape, 0)
    before = jnp.where(row == 0, prev_row, pltpu.roll(cur, 1, 0))
    after = jnp.where(row == FFT_N2 - 1, next_row, pltpu.roll(cur, FFT_N2 - 1, 0))
    w = cw_ref[...]
    return before * w[0:1] + cur * w[1:2] + after * w[2:3] + w[3:4]


def _hyena_kernel(x0_ref, x1_ref, v_ref, cw0_ref, cw1_ref, cwv_ref, hf_ref,
                  fa_ref, fai_ref, fb_ref, fbi_ref, o_ref, ts_ref, zs_ref):
    n_slabs = x0_ref.shape[1] // FFT_N2

    def gated_value(j):
        return _conv_slab(x1_ref, cw1_ref, j, n_slabs) * _conv_slab(v_ref, cwv_ref, j, n_slabs)

    def fill(j, carry):
        ts_ref[pl.ds(pl.multiple_of(j * TPITCH, SUBLANES), FFT_N2), :] = gated_value(j)
        return carry

    lax.fori_loop(0, n_slabs, fill, 0, unroll=2)

    def stage_a(n2, carry):
        sa = ts_ref[pl.ds(n2, n_slabs, stride=TPITCH), :].astype(BF16)
        sb = ts_ref[pl.ds(n2 + 1, n_slabs, stride=TPITCH), :].astype(BF16)
        zero = jnp.zeros_like(sa)
        rhs = jnp.concatenate([jnp.concatenate([sa, zero], axis=1), jnp.concatenate([zero, sb], axis=1)], axis=0)
        z = _dot(fa_ref[n2 // 2], rhs)
        for k in range(2):
            zk = z[:, k * LANES:(k + 1) * LANES]
            zs_ref[pl.ds(n2 + k, KP, stride=ZPITCH), :] = zk[:KP]
            zs_ref[pl.ds(FFT_N2 + n2 + k, KP, stride=ZPITCH), :] = zk[KP:]
        return carry

    lax.fori_loop(0, FFT_N2 // 2, lambda p, c: stage_a(2 * p, c), 0, unroll=8)

    def stage_b(p, carry):
        b0 = pl.multiple_of(2 * p * ZPITCH, SUBLANES)
        b1 = pl.multiple_of(b0 + ZPITCH, SUBLANES)
        z = jnp.concatenate([zs_ref[pl.ds(b0, 2 * FFT_N2), :], zs_ref[pl.ds(b1, 2 * FFT_N2), :]], axis=1)
        x = _dot(fb_ref[...], z.astype(BF16))
        h = jnp.concatenate([hf_ref[0, 2 * p], hf_ref[0, 2 * p + 1]], axis=1)
        xr, xi = x[:FFT_N2], x[FFT_N2:]
        hr, hi = h[:FFT_N2], h[FFT_N2:]
        prod = jnp.concatenate([xr * hr - xi * hi, xr * hi + xi * hr], axis=0)
        y = _dot(fbi_ref[...], prod.astype(BF16))
        zs_ref[pl.ds(b0, 2 * FFT_N2), :] = y[:, :LANES]
        zs_ref[pl.ds(b1, 2 * FFT_N2), :] = y[:, LANES:]
        return carry

    lax.fori_loop(0, KHP // 2, stage_b, 0, unroll=11)

    def stage_ai(n2, carry):
        yr = zs_ref[pl.ds(n2, KP, stride=ZPITCH), :]
        yi = zs_ref[pl.ds(FFT_N2 + n2, KP, stride=ZPITCH), :]
        y = jnp.concatenate([yr, yi], axis=0).astype(BF16)
        ts_ref[pl.ds(n2, n_slabs, stride=TPITCH), :] = _dot(fai_ref[n2], y)
        return carry

    lax.fori_loop(0, FFT_N2, stage_ai, 0, unroll=16)

    def finish(j, carry):
        conv = ts_ref[pl.ds(pl.multiple_of(j * TPITCH, SUBLANES), FFT_N2), :]
        y = _conv_slab(x0_ref, cw0_ref, j, n_slabs) * conv
        o_ref[0, pl.ds(pl.multiple_of(j * FFT_N2, FFT_N2), FFT_N2), :] = y.astype(BF16)
        return carry

    lax.fori_loop(0, n_slabs, finish, 0, unroll=2)


def _hyena(u, cw, hf, fa, fai, fb, fbi):
    b, seq, c3 = u.shape
    nblk = D_HYENA // LANES
    n_slabs = seq // FFT_N2
    stream = lambda k: pl.BlockSpec((1, seq, LANES), lambda j, i, k=k: (i, 0, k * nblk + j))
    cwspec = lambda k: pl.BlockSpec((SUBLANES, LANES), lambda j, i, k=k: (0, k * nblk + j))
    return pl.pallas_call(
        _hyena_kernel,
        grid=(nblk, b),
        in_specs=[stream(0), stream(1), stream(2), cwspec(0), cwspec(1), cwspec(2),
                  pl.BlockSpec((1, KHP, 2 * FFT_N2, LANES), lambda j, i: (j, 0, 0, 0),
                               pipeline_mode=pl.Buffered(1)),
                  _const_spec(fa.shape), _const_spec(fai.shape),
                  _const_spec(fb.shape), _const_spec(fbi.shape)],
        out_specs=pl.BlockSpec((1, seq, LANES), lambda j, i: (i, 0, j)),
        out_shape=jax.ShapeDtypeStruct((b, seq, D_HYENA), BF16),
        scratch_shapes=[pltpu.VMEM((n_slabs * TPITCH, LANES), F32),
                        pltpu.VMEM((KP * ZPITCH, LANES), F32)],
        compiler_params=_params(("arbitrary", "arbitrary")),
        name="hyena",
    )(u, u, u, cw, cw, cw, hf, fa, fai, fb, fbi)


def _attn_kernel(sink_ref, q_ref, kp_ref, km_ref, kn_ref, vp_ref, vm_ref, vn_ref, kc_ref, vc_ref,
                 o_ref, ka_ref, va_ref):
    i = pl.program_id(1)
    n_i = pl.num_programs(1)
    tq = q_ref.shape[0]
    nqb = tq // BLOCK
    ka_ref[0:BLOCK] = kp_ref[...]
    ka_ref[BLOCK:BLOCK + tq] = km_ref[...]
    ka_ref[BLOCK + tq:] = kn_ref[...]
    va_ref[0:BLOCK] = vp_ref[...]
    va_ref[BLOCK:BLOCK + tq] = vm_ref[...]
    va_ref[BLOCK + tq:] = vn_ref[...]

    qi = lax.broadcasted_iota(jnp.int32, (BLOCK, BLOCK), 0)
    kj = lax.broadcasted_iota(jnp.int32, (BLOCK, BLOCK), 1)
    lane = lax.broadcasted_iota(jnp.int32, (BLOCK, LANES), 1)
    low = lane < HEAD_DIM
    half = GQA_GROUP // 2
    rows2 = half * BLOCK
    hrow = lax.broadcasted_iota(jnp.int32, (GQA_GROUP * BLOCK, 1), 0) // BLOCK
    head_order = [hh for hh in range(GQA_GROUP) if hh % 2 == 0] + [hh for hh in range(GQA_GROUP) if hh % 2 == 1]
    one = jnp.ones((), BF16)

    def with_ones(v):
        lanes_low = lax.broadcasted_iota(jnp.int32, v.shape, 1) < HEAD_DIM
        return jnp.where(lanes_low, v, one), jnp.where(lanes_low, one, v)

    for g in range(N_KV_HEADS):
        gl = slice(g * LANES, (g + 1) * LANES)
        vc_even, vc_odd = with_ones(vc_ref[:, gl])
        sink = jnp.zeros((GQA_GROUP * BLOCK, 1), F32)
        for pos, hh in enumerate(head_order):
            sink = jnp.where(hrow == pos, sink_ref[g * GQA_GROUP + hh], sink)
        for j in range(nqb):
            prev_ok = kj >= qi
            next_ok = kj <= qi
            if j == 0:
                prev_ok = prev_ok & (i > 0)
            if j == nqb - 1:
                next_ok = next_ok & (i < n_i - 1)
            bias_p = jnp.concatenate([jnp.where(prev_ok, 0.0, NEG_INF).astype(F32)] * GQA_GROUP, axis=0)
            bias_n = jnp.concatenate([jnp.where(next_ok, 0.0, NEG_INF).astype(F32)] * GQA_GROUP, axis=0)
            qb = q_ref[j * BLOCK:(j + 1) * BLOCK, :]
            parts = []
            for hh in head_order:
                h = g * GQA_GROUP + hh
                qp = qb[:, (h // 2) * LANES:(h // 2 + 1) * LANES]
                parts.append(jnp.where(low if h % 2 == 0 else ~low, qp, jnp.zeros_like(qp)))
            qs = jnp.concatenate(parts, axis=0)
            kw = ka_ref[j * BLOCK:(j + 3) * BLOCK, gl]
            vw_even, vw_odd = with_ones(va_ref[j * BLOCK:(j + 3) * BLOCK, gl])
            s_w = _dot_nt(qs, kw)
            s_p = s_w[:, :BLOCK] + bias_p
            s_m = s_w[:, BLOCK:2 * BLOCK]
            s_n = s_w[:, 2 * BLOCK:] + bias_n
            s_c = _dot_nt(qs, kc_ref[:, gl])
            m = jnp.maximum(jnp.maximum(jnp.max(jnp.maximum(jnp.maximum(s_p, s_m), s_n), axis=1, keepdims=True),
                                        jnp.max(s_c, axis=1, keepdims=True)), sink)
            e_w = jnp.concatenate([jnp.exp(s_p - m), jnp.exp(s_m - m), jnp.exp(s_n - m)], axis=1).astype(BF16)
            e_c = jnp.exp(s_c - m).astype(BF16)
            e_sink = jnp.exp(sink - m)
            outs = []
            for par, (vw, vcx) in enumerate(((vw_even, vc_even), (vw_odd, vc_odd))):
                rs = slice(par * rows2, (par + 1) * rows2)
                acc = _dot(e_w[rs], vw) + _dot(e_c[rs], vcx)
                den = pltpu.roll(acc, HEAD_DIM, 1) + e_sink[rs]
                outs.append(acc / den)
            for pp in range(half):
                pair = jnp.where(low, outs[0][pp * BLOCK:(pp + 1) * BLOCK], outs[1][pp * BLOCK:(pp + 1) * BLOCK])
                col = (g * half + pp) * LANES
                o_ref[j * BLOCK:(j + 1) * BLOCK, col:col + LANES] = pair.astype(BF16)


def _attention(sinks, q, kd, vd, kc, vc, batch, seq, tq):
    t = q.shape[0]
    per_b = seq // tq
    nqb = tq // BLOCK
    nb = seq // BLOCK
    n_ctx = kc.shape[0] // batch
    main = lambda b, i: (b * per_b + i, 0)
    prev = lambda b, i: (b * nb + jnp.maximum(i * nqb - 1, 0), 0)
    nxt = lambda b, i: (b * nb + jnp.minimum(i * nqb + nqb, nb - 1), 0)
    kvw = kd.shape[1]
    return pl.pallas_call(
        _attn_kernel,
        grid=(batch, per_b),
        in_specs=[pl.BlockSpec(memory_space=pltpu.SMEM),
                  pl.BlockSpec((tq, D_ATTN), main),
                  pl.BlockSpec((BLOCK, kvw), prev), pl.BlockSpec((tq, kvw), main), pl.BlockSpec((BLOCK, kvw), nxt),
                  pl.BlockSpec((BLOCK, kvw), prev), pl.BlockSpec((tq, kvw), main), pl.BlockSpec((BLOCK, kvw), nxt),
                  pl.BlockSpec((n_ctx, kvw), lambda b, i: (b, 0)),
                  pl.BlockSpec((n_ctx, kvw), lambda b, i: (b, 0))],
        out_specs=pl.BlockSpec((tq, D_ATTN), main),
        out_shape=jax.ShapeDtypeStruct((t, D_ATTN), BF16),
        scratch_shapes=[pltpu.VMEM((tq + 2 * BLOCK, kvw), BF16),
                        pltpu.VMEM((tq + 2 * BLOCK, kvw), BF16)],
        compiler_params=_params(("arbitrary", "arbitrary")),
        name="attention",
    )(sinks, q, kd, kd, kd, vd, vd, vd, kc, vc)


def _route(t, wr_hi_ref, wr_hl_ref, br_ref):
    th, tl = _split(t)
    both = _dot(th, wr_hl_ref[...])
    logits = both[:, :LANES] + both[:, LANES:] + _dot(tl, wr_hi_ref[...]) + br_ref[...]
    lane_i = lax.broadcasted_iota(jnp.int32, logits.shape, 1)
    lane = lane_i.astype(F32)
    grp_of_lane = (lane_i >> 2).astype(F32)
    ninf = -jnp.inf
    far = float(LANES)
    is_g = (lane_i >= N_EXPERTS) & (lane_i < N_EXPERTS + N_GROUPS)
    glog = jnp.where(is_g, logits, ninf)
    gmax = jnp.max(glog, axis=1, keepdims=True)
    gidx = jnp.min(jnp.where(glog == gmax, lane - float(N_EXPERTS), far), axis=1, keepdims=True)
    group_p = 1.0 / jnp.sum(jnp.exp(glog - gmax), axis=1, keepdims=True)
    in_grp = (lane_i < N_EXPERTS) & (grp_of_lane == gidx)
    elog = jnp.where(in_grp, logits, ninf)
    v1 = jnp.max(elog, axis=1, keepdims=True)
    i1 = jnp.min(jnp.where(elog == v1, lane, far), axis=1, keepdims=True)
    elog2 = jnp.where(lane == i1, ninf, elog)
    v2 = jnp.max(elog2, axis=1, keepdims=True)
    i2 = jnp.min(jnp.where(elog2 == v2, lane, far), axis=1, keepdims=True)
    e = jnp.exp(v2 - v1)
    w1 = group_p / (1.0 + e)
    w2 = group_p * e / (1.0 + e)
    gate = jnp.where(lane == i1, w1, 0.0) + jnp.where(lane == i2, w2, 0.0)
    return gate + jnp.where(lane_i == GROUP_LANE, gidx, 0.0)


def _mix_out_kernel(x_ref, h_ref, yh_ref, ya_ref, g1_ref, sh2_ref, sc2_ref,
                    wg_ref, wbh_ref, wba_ref, wo_ref, lng_ref, lnb_ref, wrh_ref, wrhl_ref, br_ref,
                    x1_ref, gate_ref, gsel_ref, *t_refs):
    d = x_ref.shape[1]
    for r0 in range(0, x_ref.shape[0], SUB_ROWS):
        rows = slice(r0, r0 + SUB_ROWS)
        x = x_ref[rows, :]
        h = h_ref[rows, :]
        g_hy = jax.nn.sigmoid(_dot(h, wg_ref[:, :d]))
        merged = g_hy * _dot(yh_ref[rows, :], wbh_ref[...])
        g_at = jax.nn.sigmoid(_dot(h, wg_ref[:, d:]))
        merged = merged + g_at * _dot(ya_ref[rows, :], wba_ref[...])
        mix = _dot(merged.astype(BF16), wo_ref[...])
        x1 = _standardize(DEEPNORM_ALPHA * x + g1_ref[0] * mix) * lng_ref[...] + lnb_ref[...]
        x1_ref[rows, :] = x1
        t = _standardize(x1) * (1.0 + sc2_ref[0]) + sh2_ref[0]
        gate = _route(t, wrh_ref, wrhl_ref, br_ref)
        gate_ref[rows, :] = gate
        grp = gate.T[GROUP_LANE:GROUP_LANE + 1, :]
        gsel_ref[r0 // LANES:(r0 + SUB_ROWS) // LANES, :] = jnp.concatenate(
            [grp[:, k * LANES:(k + 1) * LANES] for k in range(SUB_ROWS // LANES)], axis=0)
        for t_ref, piece in zip(t_refs, _pack_pieces(t)):
            t_ref[rows, :] = piece


def _mix_out(x2d, h, yh, ya, mods, w_g, w_bh, w_ba, w_o, ln_g, ln_b, wr_hi, wr_hl, br, seq, tm):
    t, d = x2d.shape
    per_b = seq // tm
    row = lambda i: (i, 0)
    mod = lambda i: (i // per_b, 0, 0)
    mspec = pl.BlockSpec((1, 1, d), mod)
    g1, sh2, sc2 = mods
    return pl.pallas_call(
        _mix_out_kernel,
        grid=(t // tm,),
        in_specs=[pl.BlockSpec((tm, d), row),
                  pl.BlockSpec((tm, d), row),
                  pl.BlockSpec((tm, yh.shape[1]), row),
                  pl.BlockSpec((tm, ya.shape[1]), row),
                  mspec, mspec, mspec,
                  _const_spec(w_g.shape), _const_spec(w_bh.shape), _const_spec(w_ba.shape),
                  _const_spec(w_o.shape), _const_spec(ln_g.shape), _const_spec(ln_b.shape),
                  _const_spec(wr_hi.shape), _const_spec(wr_hl.shape), _const_spec(br.shape)],
        out_specs=[pl.BlockSpec((tm, d), row), pl.BlockSpec((tm, LANES), row),
                   pl.BlockSpec((tm // LANES, LANES), row)]
        + [pl.BlockSpec((tm, SC_ROW_WORDS), row)] * N_PIECES,
        out_shape=[jax.ShapeDtypeStruct((t, d), F32), jax.ShapeDtypeStruct((t, LANES), F32),
                   jax.ShapeDtypeStruct((t // LANES, LANES), F32)]
        + [jax.ShapeDtypeStruct((t, SC_ROW_WORDS), jnp.uint32)] * N_PIECES,
        compiler_params=_params(("arbitrary",)),
        name="mix_out",
    )(x2d, h, yh, ya, g1, sh2, sc2, w_g, w_bh, w_ba, w_o, ln_g, ln_b, wr_hi, wr_hl, br)


def _slots_kernel(g_ref, upper_ref, lower_ref, dest_ref, meta_ref):
    gsel = g_ref[...]
    dest = jnp.zeros(gsel.shape, F32)
    base = jnp.zeros((1, 1), F32)
    chunk_start = lax.broadcasted_iota(jnp.int32, (1, LANES), 1).astype(F32) * float(MOE_CHUNK)
    owner = jnp.zeros((1, LANES), F32)
    for g in range(N_GROUPS):
        onehot = jnp.where(gsel == float(g), 1.0, 0.0)
        in_row = _dot(onehot.astype(BF16), upper_ref[...])
        row_tot = jnp.sum(onehot, axis=1, keepdims=True)
        rows_before = _dot(lower_ref[...], jnp.broadcast_to(row_tot, onehot.shape).astype(BF16))
        dest = dest + onehot * (base + rows_before + in_row)
        if g > 0:
            owner = owner + jnp.where(chunk_start >= base, 1.0, 0.0)
        n_g = jnp.sum(row_tot, axis=0, keepdims=True)
        base = base + jnp.floor((n_g + float(MOE_CHUNK - 1)) * (1.0 / MOE_CHUNK)) * float(MOE_CHUNK)
    dest_ref[...] = dest.astype(jnp.int32)
    row = lax.broadcasted_iota(jnp.int32, meta_ref.shape, 0)
    meta_ref[...] = jnp.where(row == 0, owner, base * (1.0 / MOE_CHUNK)).astype(jnp.int32)


def _slots(gsel):
    r = gsel.shape[0]
    upper = jnp.asarray(np.triu(np.ones((LANES, LANES), np.float32), 1)).astype(BF16)
    lower = jnp.asarray(np.tril(np.ones((r, r), np.float32), -1)).astype(BF16)
    full = lambda a: pl.BlockSpec(a.shape, lambda i: (0,) * a.ndim)
    return pl.pallas_call(
        _slots_kernel,
        grid=(1,),
        in_specs=[full(gsel), full(upper), full(lower)],
        out_specs=[pl.BlockSpec((r, LANES), lambda i: (0, 0)), pl.BlockSpec((SUBLANES, LANES), lambda i: (0, 0))],
        out_shape=[jax.ShapeDtypeStruct((r, LANES), jnp.int32), jax.ShapeDtypeStruct((SUBLANES, LANES), jnp.int32)],
        compiler_params=_params(("arbitrary",)),
        name="moe_slots",
    )(gsel, upper, lower)


def _experts_kernel(owner_ref, used_ref, *refs):
    n_p = N_PIECES
    t_refs, gate_ref = refs[:n_p], refs[n_p]
    wg_ref, wu_ref, wd_ref = refs[n_p + 1:n_p + 4]
    y_refs = refs[n_p + 4:]
    c = pl.program_id(0)

    @pl.when(c < used_ref[0])
    def _():
        t = _unpack_pieces([r[...] for r in t_refs]).astype(BF16)
        gate = gate_ref[...]
        lane = lax.broadcasted_iota(jnp.int32, gate.shape, 1)
        first = owner_ref[c] * EXPERTS_PER_GROUP
        parts = []
        for e in range(EXPERTS_PER_GROUP):
            a = _dot(t, wg_ref[e])
            u = _dot(t, wu_ref[e])
            ge = jnp.sum(jnp.where(lane == first + e, gate, 0.0), axis=1, keepdims=True)
            parts.append((a * jax.nn.sigmoid(a) * u * ge).astype(BF16))
        y = _dot(jnp.concatenate(parts, axis=1), wd_ref[0])
        for y_ref, piece in zip(y_refs, _pack_pieces(y)):
            y_ref[...] = piece

    @pl.when(c >= used_ref[0])
    def _():
        for y_ref in y_refs:
            y_ref[...] = jnp.zeros_like(y_ref)


def _experts(owner, used, t_pieces, gate_sorted, wg16, wu16, wd3):
    n_slots = gate_sorted.shape[0]
    row = lambda c, owner, used: (c, 0)
    by_owner = lambda c, owner, used: (owner[c], 0, 0)
    piece = pl.BlockSpec((MOE_CHUNK, SC_ROW_WORDS), row)
    group_of_experts = lambda w: pl.BlockSpec((EXPERTS_PER_GROUP,) + w.shape[1:], by_owner)
    return pl.pallas_call(
        _experts_kernel,
        grid_spec=pltpu.PrefetchScalarGridSpec(
            num_scalar_prefetch=2,
            grid=(n_slots // MOE_CHUNK,),
            in_specs=[piece] * len(t_pieces) + [pl.BlockSpec((MOE_CHUNK, LANES), row),
                                                group_of_experts(wg16), group_of_experts(wu16),
                                                pl.BlockSpec((1,) + wd3.shape[1:], by_owner)],
            out_specs=[piece] * len(t_pieces),
        ),
        out_shape=[jax.ShapeDtypeStruct((n_slots, SC_ROW_WORDS), jnp.uint32)] * len(t_pieces),
        compiler_params=_params(("arbitrary",)),
        name="moe_experts",
    )(owner, used, *t_pieces, gate_sorted, wg16, wu16, wd3)


def _final_kernel(x1_ref, g2_ref, lng_ref, lnb_ref, *refs):
    y_refs, o_ref = refs[:-1], refs[-1]
    y = _unpack_pieces([r[...] for r in y_refs])
    o_ref[...] = _standardize(DEEPNORM_ALPHA * x1_ref[...] + g2_ref[0] * y) * lng_ref[...] + lnb_ref[...]


def _final(x1, g2, ln_g, ln_b, y_pieces, seq, tm):
    t, d = x1.shape
    per_b = seq // tm
    row = lambda i: (i, 0)
    return pl.pallas_call(
        _final_kernel,
        grid=(t // tm,),
        in_specs=[pl.BlockSpec((tm, d), row), pl.BlockSpec((1, 1, d), lambda i: (i // per_b, 0, 0)),
                  _const_spec(ln_g.shape), _const_spec(ln_b.shape)]
        + [pl.BlockSpec((tm, SC_ROW_WORDS), row)] * len(y_pieces),
        out_specs=pl.BlockSpec((tm, d), row),
        out_shape=jax.ShapeDtypeStruct((t, d), F32),
        compiler_params=_params(("arbitrary",)),
        name="moe_final",
    )(x1, g2, ln_g, ln_b, *y_pieces)


def _sc_move_rows(tables, idx, n_out, scatter, name):
    n = idx.shape[0]
    mesh = plsc.VectorSubcoreMesh(core_axis_name="c", subcore_axis_name="s")
    out_type = [jax.ShapeDtypeStruct((n_out, t.shape[1]), t.dtype) for t in tables]
    window = lambda i: (i, 0)
    index_win = pl.BlockSpec((1, SC_WINDOW), lambda i: (0, i))
    split = dict(core_axis_name=("c", "s"), dimension_semantics=(pltpu.PARALLEL,))

    @functools.partial(pl.kernel, out_type=out_type, mesh=mesh, scratch_types=[], name=name)
    def move(*refs):
        srcs, i_hbm, dsts = refs[:len(tables)], refs[len(tables)], refs[len(tables) + 1:]
        for src, dst in zip(srcs, dsts):
            rows_win = pl.BlockSpec((SC_WINDOW, src.shape[1]), window)
            if scatter:
                def body(x_vmem, i_vmem, dst=dst):
                    pltpu.sync_copy(x_vmem, dst.at[i_vmem.at[0]])

                pltpu.emit_pipeline(body, grid=(n // SC_WINDOW,), in_specs=[rows_win, index_win],
                                    out_specs=[], **split)(src, i_hbm)
            else:
                def body(i_vmem, o_vmem, src=src):
                    pltpu.sync_copy(src.at[i_vmem.at[0]], o_vmem)

                pltpu.emit_pipeline(body, grid=(n // SC_WINDOW,), in_specs=[index_win],
                                    out_specs=[rows_win], **split)(i_hbm, dst)

    return move(*tables, idx.reshape(1, n))


def _dft_tables(seq):
    n = 2 * seq
    n1_full = n // FFT_N2
    n1_data = seq // FFT_N2
    k1 = np.arange(KH, dtype=np.float64)[None, :, None]
    n2 = np.arange(FFT_N2, dtype=np.float64)[:, None, None]
    n1 = np.arange(n1_full, dtype=np.float64)[None, None, :]
    ang = 2.0 * np.pi * k1 * (FFT_N2 * n1 + n2) / n
    fa = np.zeros((FFT_N2, 2 * KP, n1_full))
    fa[:, :KH] = np.cos(ang)
    fa[:, KP:KP + KH] = -np.sin(ang)
    wgt = np.full((KH,), 2.0)
    wgt[0] = 1.0
    wgt[KH - 1] = 1.0
    fai = np.zeros((FFT_N2, n1_data, 2 * KP))
    angt = np.transpose(ang[:, :, :n1_data], (0, 2, 1))
    fai[:, :, :KH] = np.cos(angt) * wgt / n
    fai[:, :, KP:KP + KH] = -np.sin(angt) * wgt / n
    kk = np.arange(FFT_N2, dtype=np.float64)
    a2 = 2.0 * np.pi * np.outer(kk, kk) / FFT_N2
    fr, fi = np.cos(a2), -np.sin(a2)
    fb = np.block([[fr, -fi], [fi, fr]])
    fbi = np.block([[fr, fi], [-fi, fr]])
    f32 = lambda a: jnp.asarray(a.astype(np.float32))
    pairs = lambda a: np.concatenate([a[0::2], a[1::2]], axis=2)
    return f32(pairs(fa[:, :, :n1_data])), f32(pairs(fa)), f32(fai), f32(fb), f32(fbi)


def _filter_features(seq, rows):
    f32 = np.float32
    t = np.linspace(0.0, 1.0, seq, dtype=f32)
    w = (f32(2.0 * math.pi) * np.arange(seq, dtype=f32) / f32(seq)).astype(f32)
    t2 = np.concatenate([t, t[::-1]])
    w2 = np.concatenate([w, w[::-1]])
    bands = np.linspace(1e-4, FILTER_BANDS - 1, FILTER_BANDS, dtype=f32)
    max_decay = math.log(DECAY_TARGET) / FAST_DECAY_PCT
    min_decay = math.log(DECAY_TARGET) / SLOW_DECAY_PCT
    deltas = jnp.linspace(min_decay, max_decay, D_HYENA, dtype=F32)

    def feats(k):
        tp = t2.reshape(-1, 2, rows // 2)[:, k].reshape(-1, 1)
        wp = w2.reshape(-1, 2, rows // 2)[:, k].reshape(-1, 1)
        pad = np.zeros((tp.shape[0], FILTER_ORDER - FILTER_EMB), f32)
        arg = (bands * wp).astype(f32)
        return np.concatenate([tp, np.cos(arg), -np.sin(arg), pad], axis=-1).astype(f32)

    zp = jnp.asarray(np.concatenate([feats(0), feats(1)], axis=-1))
    return zp, jnp.asarray(t2)[:, None], jnp.abs(deltas)[None, :]


def _rope_tables(seq):
    f32 = np.float32
    rows = seq // GRID_W
    row = np.repeat(np.arange(rows, dtype=f32), GRID_W)
    col = np.tile(np.arange(GRID_W, dtype=f32), rows)
    half = HEAD_DIM // 2
    inv_freq = (f32(ROPE_BASE) ** (-np.arange(0, half, 2, dtype=f32) / f32(half))).astype(f32)
    ang = np.concatenate([row[:, None] * inv_freq, col[:, None] * inv_freq], axis=-1).astype(f32)
    cos, sin = np.cos(ang), np.sin(ang)
    c64 = np.concatenate([cos, cos], axis=-1)
    s64 = np.concatenate([-sin, sin], axis=-1)
    return (jnp.asarray(np.concatenate([c64, c64], axis=-1).astype(f32)),
            jnp.asarray(np.concatenate([s64, s64], axis=-1).astype(f32)))


def _head_perm(n_heads):
    idx = []
    for h in range(n_heads):
        base = h * HEAD_DIM
        idx += [base + 2 * j for j in range(HEAD_DIM // 2)]
        idx += [base + 2 * j + 1 for j in range(HEAD_DIM // 2)]
    return np.asarray(idx, dtype=np.int32)


def _dup_heads(w):
    parts = []
    for g in range(N_KV_HEADS):
        blk = w[:, g * HEAD_DIM:(g + 1) * HEAD_DIM]
        parts += [blk, blk]
    return jnp.concatenate(parts, axis=1)


def kernel(x, c, ctx, c_ctx, ada_w, ada_b, w_in, hy_conv_w, hy_conv_b, hy_w1, hy_b1, hy_w2, hy_b2, hy_w3, hy_b3, hy_w4, hy_freq, hy_bias, attn_sinks, w_branch_hy, w_branch_attn, w_out, ln1_g, ln1_b, w_group, b_group, w_router, b_router, w_gate_e, w_up_e, w_down_e, ln2_g, ln2_b):
    batch, seq, d = x.shape
    n_ctx = ctx.shape[1]
    assert d == D_MODEL and ada_w.shape[0] == DEPTH == 1
    assert 2 * seq == FFT_N2 * FFT_N2 and seq % TM_PROJ == 0
    l = 0

    w = w_in[l]
    s0, s1, s2, s3 = 3 * D_HYENA, 3 * D_HYENA + D_ATTN, 3 * D_HYENA + D_ATTN + D_KV, 3 * D_HYENA + D_ATTN + 2 * D_KV
    w_q = w[:, s0:s1][:, _head_perm(N_HEADS)]
    w_k = _dup_heads(w[:, s1:s2][:, _head_perm(N_KV_HEADS)])
    w_v = _dup_heads(w[:, s2:s3])
    w_c = jnp.concatenate([w[:, :s0], w_q, w_k, w_v], axis=1).astype(BF16)
    w_kv = jnp.concatenate([w_k, w_v], axis=1).astype(BF16)
    w_g = w[:, s3:].astype(BF16)
    w_bh = w_branch_hy[l].astype(BF16)
    w_ba = w_branch_attn[l].astype(BF16)
    w_o = w_out[l].astype(BF16)
    wr = jnp.zeros((d, LANES), F32)
    wr = wr.at[:, :N_EXPERTS].set(w_router[l]).at[:, N_EXPERTS:N_EXPERTS + N_GROUPS].set(w_group[l])
    wr_hi = wr.astype(BF16)
    wr_hl = jnp.concatenate([wr_hi, (wr - wr_hi.astype(F32)).astype(BF16)], axis=1)
    br = jnp.zeros((1, LANES), F32)
    br = br.at[0, :N_EXPERTS].set(b_router[l]).at[0, N_EXPERTS:N_EXPERTS + N_GROUPS].set(b_group[l])
    wg16 = w_gate_e[l].astype(BF16)
    wu16 = w_up_e[l].astype(BF16)
    wd3 = w_down_e[l].astype(BF16).reshape(N_GROUPS, EXPERTS_PER_GROUP * D_EXPERT, d)
    cw = jnp.concatenate([hy_conv_w[l][:, 0, :], hy_conv_b[l][None, :],
                          jnp.zeros((SUBLANES - SHORT_CONV - 1, 3 * D_HYENA), F32)], axis=0)
    row2 = lambda a: a.reshape(1, -1)
    pair = lambda a: jnp.concatenate([a, a], axis=-1)
    zero_o = jnp.zeros((FILTER_ORDER, FILTER_ORDER), F32)
    bdiag = lambda a: jnp.concatenate([jnp.concatenate([a, zero_o], axis=1),
                                       jnp.concatenate([zero_o, a], axis=1)], axis=0)
    w1p = bdiag(jnp.concatenate([hy_w1[l], jnp.zeros((FILTER_ORDER - FILTER_EMB, FILTER_ORDER), F32)], axis=0))
    w4h = jnp.transpose(hy_w4[l].reshape(FILTER_ORDER, 2, D_HYENA), (1, 0, 2))
    zero_w4 = jnp.zeros_like(w4h)
    w4s = jnp.stack([jnp.concatenate([w4h, zero_w4], axis=1),
                     jnp.concatenate([zero_w4, w4h], axis=1)], axis=1)

    fa, fa_full, fai, fb, fbi = _dft_tables(seq)
    fa, fa_full, fai, fb, fbi = (a.astype(BF16) for a in (fa, fa_full, fai, fb, fbi))
    zp, t_col, absd = _filter_features(seq, FILT_ROWS)
    cos_t, sin_t = _rope_tables(seq)

    cond = jnp.concatenate([c, c_ctx[None], jnp.zeros((SUBLANES - batch - 1, d), F32)], axis=0)
    mods = _adaln(cond, ada_w[l], ada_b[l])
    m6 = [mods[:, k * d:(k + 1) * d].reshape(SUBLANES, 1, d) for k in range(6)]
    sh1, sc1, g1, sh2, sc2, g2 = m6

    kc, vc = _ctx_kv(ctx.reshape(batch * n_ctx, d), sh1[batch], sc1[batch], w_kv, n_ctx)

    x2d = x.reshape(batch * seq, d)
    u, q, kd, vd, h_mod = _in_proj(x2d, sh1, sc1, w_c, cos_t, sin_t, seq, TM_PROJ)
    h2u, ss = _filt_mlp(zp, w1p, pair(row2(hy_b1[l])), bdiag(hy_w2[l]), pair(row2(hy_b2[l])),
                        bdiag(hy_w3[l]), pair(row2(hy_b3[l])), w4s, pair(row2(hy_freq[l])), t_col, absd, FILT_ROWS)
    hf = _filt_fft(h2u, ss, row2(hy_bias[l]), fa_full, fb)
    y_hy = _hyena(u.reshape(batch, seq, 3 * D_HYENA), cw, hf, fa, fai, fb, fbi)
    y_at = _attention(attn_sinks[l], q, kd, vd, kc, vc, batch, seq, TQ_ATTN)
    x1, gate, gsel, *t_pieces = _mix_out(x2d, h_mod, y_hy.reshape(batch * seq, D_HYENA), y_at, (g1, sh2, sc2),
                                         w_g, w_bh, w_ba, w_o, row2(ln1_g[l]), row2(ln1_b[l]), wr_hi, wr_hl, br,
                                         seq, TM_PROJ)

    n_tok = batch * seq
    n_slots = n_tok + N_GROUPS * MOE_CHUNK
    dest2d, meta = _slots(gsel)
    dest = dest2d.reshape(n_tok)
    *t_sorted, gate_sorted = _sc_move_rows(t_pieces + [gate], dest, n_slots, True, "moe_sort")
    y_sorted = _experts(meta[0], meta[1, :1], t_sorted, gate_sorted, wg16, wu16, wd3)
    y_pieces = _sc_move_rows(y_sorted, dest, n_tok, False, "moe_unsort")
    out = _final(x1, g2, row2(ln2_g[l]), row2(ln2_b[l]), y_pieces, seq, TM_FINAL)
    return out.reshape(batch, seq, d)
```

```python
import functools
import math

import numpy as np
import jax
import jax.numpy as jnp
from jax import lax
from jax.experimental import pallas as pl
from jax.experimental.pallas import tpu as pltpu
from jax.experimental.pallas import tpu_sc as plsc

F32 = jnp.float32
BF16 = jnp.bfloat16

D_MODEL = 1024
GRID_W = 64
D_HYENA = D_MODEL // 2
SHORT_CONV = 3
FILTER_BANDS = 16
FILTER_EMB = 1 + 2 * FILTER_BANDS
FILTER_ORDER = 64
DECAY_TARGET = 1e-2
FAST_DECAY_PCT = 0.3
SLOW_DECAY_PCT = 1.5
HEAD_DIM = 64
D_ATTN = D_MODEL // 2
N_HEADS = D_ATTN // HEAD_DIM
N_KV_HEADS = N_HEADS // 4
GQA_GROUP = N_HEADS // N_KV_HEADS
D_KV = N_KV_HEADS * HEAD_DIM
WINDOW = 128
BLOCK = 128
ROPE_BASE = 10000.0
NEG_INF = -1e30
N_GROUPS = 4
EXPERTS_PER_GROUP = 4
N_EXPERTS = N_GROUPS * EXPERTS_PER_GROUP
D_EXPERT = D_MODEL // 4
LN_EPS = 1e-5
DEPTH = 1
DEEPNORM_ALPHA = (2.0 * DEPTH) ** 0.25

LANES = 128
SUBLANES = 8
VMEM_LIMIT = 56 * 1024 * 1024

SUB_ROWS = 512
TM_PROJ = 1024
MOE_CHUNK = 512
TM_FINAL = 1024
GROUP_LANE = N_EXPERTS
SC_WINDOW = 128
SC_ROW_WORDS = 256
N_PIECES = D_MODEL // (2 * SC_ROW_WORDS)
TQ_ATTN = 1024
FILT_ROWS = 2048

FFT_N2 = 128
KH = 65
KHP = 66
KP = 72
ZPITCH = 2 * FFT_N2 + SUBLANES
TPITCH = FFT_N2 + SUBLANES


def _dot(a, b):
    return jnp.dot(a, b, preferred_element_type=F32)


def _dot_nt(a, b):
    return lax.dot_general(a, b, (((1,), (1,)), ((), ())), preferred_element_type=F32)


def _split(a):
    hi = a.astype(BF16)
    lo = (a - hi.astype(F32)).astype(BF16)
    return hi, lo


def _pack_pieces(x):
    w = SC_ROW_WORDS
    pieces = []
    for p in range(x.shape[1] // (2 * w)):
        hi = lax.bitcast_convert_type(x[:, 2 * p * w:(2 * p + 1) * w].astype(BF16).astype(F32), jnp.uint32)
        lo = lax.bitcast_convert_type(x[:, (2 * p + 1) * w:(2 * p + 2) * w].astype(BF16).astype(F32), jnp.uint32)
        pieces.append(hi | (lo >> 16))
    return pieces


def _unpack_pieces(pieces):
    cols = []
    for word in pieces:
        cols.append(lax.bitcast_convert_type(word & jnp.uint32(0xFFFF0000), F32))
        cols.append(lax.bitcast_convert_type(word << 16, F32))
    return jnp.concatenate(cols, axis=1)


def _dot3(a, b):
    ah, al = _split(a)
    bh, bl = _split(b)
    return _dot(ah, bh) + _dot(al, bh) + _dot(ah, bl)


def _standardize(x):
    mu = jnp.mean(x, axis=-1, keepdims=True)
    xc = x - mu
    var = jnp.mean(xc * xc, axis=-1, keepdims=True)
    return xc * lax.rsqrt(var + LN_EPS)


def _params(sem, vmem=VMEM_LIMIT):
    return pltpu.CompilerParams(dimension_semantics=sem, vmem_limit_bytes=vmem)


def _const_spec(shape):
    nd = len(shape)
    return pl.BlockSpec(shape, lambda *_: (0,) * nd, pipeline_mode=pl.Buffered(1))


def _adaln_kernel(c_ref, w_ref, b_ref, o_ref):
    s = c_ref[...]
    s = s * jax.nn.sigmoid(s)
    o_ref[...] = _dot3(s, w_ref[...]) + b_ref[...]


def _adaln(cond, w, b):
    n, d = cond.shape
    cols = w.shape[1]
    bc = 1024
    return pl.pallas_call(
        _adaln_kernel,
        grid=(cols // bc,),
        in_specs=[pl.BlockSpec((n, d), lambda j: (0, 0)),
                  pl.BlockSpec((d, bc), lambda j: (0, j)),
                  pl.BlockSpec((1, bc), lambda j: (0, j))],
        out_specs=pl.BlockSpec((n, bc), lambda j: (0, j)),
        out_shape=jax.ShapeDtypeStruct((n, cols), F32),
        compiler_params=_params(("arbitrary",)),
        name="adaln",
    )(cond, w, b.reshape(1, cols))


def _ctx_kv_kernel(x_ref, sh_ref, sc_ref, w_ref, k_ref, v_ref):
    h = _standardize(x_ref[...]) * (1.0 + sc_ref[...]) + sh_ref[...]
    kv = _dot(h.astype(BF16), w_ref[...])
    half = k_ref.shape[1]
    k_ref[...] = kv[:, :half].astype(BF16)
    v_ref[...] = kv[:, half:].astype(BF16)


def _ctx_kv(ctx2d, sh, sc, w_kv, rows):
    n, d = ctx2d.shape
    half = w_kv.shape[1] // 2
    return pl.pallas_call(
        _ctx_kv_kernel,
        grid=(n // rows,),
        in_specs=[pl.BlockSpec((rows, d), lambda i: (i, 0)),
                  pl.BlockSpec((1, d), lambda i: (0, 0)),
                  pl.BlockSpec((1, d), lambda i: (0, 0)),
                  pl.BlockSpec(w_kv.shape, lambda i: (0, 0))],
        out_specs=[pl.BlockSpec((rows, half), lambda i: (i, 0)),
                   pl.BlockSpec((rows, half), lambda i: (i, 0))],
        out_shape=[jax.ShapeDtypeStruct((n, half), BF16)] * 2,
        compiler_params=_params(("arbitrary",)),
        name="ctx_kv",
    )(ctx2d, sh, sc, w_kv)


def _rope(x, cos_t, sin_t):
    width = x.shape[1]
    reps = width // LANES
    c = jnp.concatenate([cos_t] * reps, axis=1)
    s = jnp.concatenate([sin_t] * reps, axis=1)
    half = HEAD_DIM // 2
    lane = lax.broadcasted_iota(jnp.int32, x.shape, 1)
    first_half = (lane & (HEAD_DIM - 1)) < half
    partner = jnp.where(first_half, pltpu.roll(x, width - half, 1), pltpu.roll(x, half, 1))
    return x * c + partner * s


def _in_proj_kernel(x_ref, sh_ref, sc_ref, w_ref, cos_ref, sin_ref, u_ref, q_ref, k_ref, v_ref, h_ref):
    n_u = u_ref.shape[1]
    n_q = q_ref.shape[1]
    n_k = k_ref.shape[1]
    for r0 in range(0, x_ref.shape[0], SUB_ROWS):
        rows = slice(r0, r0 + SUB_ROWS)
        h = (_standardize(x_ref[rows, :]) * (1.0 + sc_ref[0]) + sh_ref[0]).astype(BF16)
        h_ref[rows, :] = h
        u_ref[rows, :] = _dot(h, w_ref[:, :n_u]).astype(BF16)
        cos_t = cos_ref[rows, :]
        sin_t = sin_ref[rows, :]
        q = _dot(h, w_ref[:, n_u:n_u + n_q])
        q_ref[rows, :] = (_rope(q, cos_t, sin_t) * (HEAD_DIM ** -0.5)).astype(BF16)
        k = _dot(h, w_ref[:, n_u + n_q:n_u + n_q + n_k])
        k_ref[rows, :] = _rope(k, cos_t, sin_t).astype(BF16)
        v_ref[rows, :] = _dot(h, w_ref[:, n_u + n_q + n_k:]).astype(BF16)


def _in_proj(x2d, sh, sc, w_c, cos_t, sin_t, seq, tm):
    t, d = x2d.shape
    per_b = seq // tm
    n_u, n_q, n_k = 3 * D_HYENA, D_ATTN, 2 * D_KV
    row = lambda i: (i, 0)
    mod = lambda i: (i // per_b, 0, 0)
    pos = lambda i: (i % per_b, 0)
    return pl.pallas_call(
        _in_proj_kernel,
        grid=(t // tm,),
        in_specs=[pl.BlockSpec((tm, d), row),
                  pl.BlockSpec((1, 1, d), mod),
                  pl.BlockSpec((1, 1, d), mod),
                  _const_spec(w_c.shape),
                  pl.BlockSpec((tm, LANES), pos),
                  pl.BlockSpec((tm, LANES), pos)],
        out_specs=[pl.BlockSpec((tm, n_u), row),
                   pl.BlockSpec((tm, n_q), row),
                   pl.BlockSpec((tm, n_k), row),
                   pl.BlockSpec((tm, n_k), row),
                   pl.BlockSpec((tm, d), row)],
        out_shape=[jax.ShapeDtypeStruct((t, n_u), BF16),
                   jax.ShapeDtypeStruct((t, n_q), BF16),
                   jax.ShapeDtypeStruct((t, n_k), BF16),
                   jax.ShapeDtypeStruct((t, n_k), BF16),
                   jax.ShapeDtypeStruct((t, d), BF16)],
        compiler_params=_params(("arbitrary",)),
        name="in_proj",
    )(x2d, sh, sc, w_c, cos_t, sin_t)


def _filt_mlp_kernel(z_ref, w1_ref, b1_ref, w2_ref, b2_ref, w3_ref, b3_ref, w4_ref, fr_ref, t_ref, absd_ref,
                     h_ref, ss_ref):
    fr = fr_ref[...]
    a = jnp.sin(fr * (_dot3(z_ref[...], w1_ref[...]) + b1_ref[...]))
    a = jnp.sin(fr * (_dot3(a, w2_ref[...]) + b2_ref[...]))
    a = jnp.sin(fr * (_dot3(a, w3_ref[...]) + b3_ref[...]))
    half = a.shape[0]
    ss = jnp.zeros(ss_ref.shape, F32)
    for k in range(2):
        decay = jnp.exp(-t_ref[k * half:(k + 1) * half, :] * absd_ref[...])
        h = _dot3(a, w4_ref[0, k]) * decay
        ss = ss + jnp.sum(h * h, axis=0, keepdims=True)
        for s in range(half // FFT_N2):
            r0 = (k * (half // FFT_N2) + s) * TPITCH
            h_ref[r0:r0 + FFT_N2, :] = h[s * FFT_N2:(s + 1) * FFT_N2]
            h_ref[r0 + FFT_N2:r0 + TPITCH, :] = jnp.zeros((TPITCH - FFT_N2, h.shape[1]), F32)

    @pl.when(pl.program_id(0) == 0)
    def _():
        ss_ref[...] = jnp.zeros_like(ss_ref)

    ss_ref[...] += ss


def _filt_mlp(zp, w1p, b1, w2, b2, w3, b3, w4s, fr, t_col, absd, rows):
    n, c = t_col.shape[0], absd.shape[1]
    half_steps = (n // 2) // rows
    vec = lambda a: pl.BlockSpec(a.shape, lambda i: (0,) * a.ndim)
    return pl.pallas_call(
        _filt_mlp_kernel,
        grid=(n // rows,),
        in_specs=[pl.BlockSpec((rows // 2, zp.shape[1]), lambda i: (i, 0)),
                  vec(w1p), vec(b1), vec(w2), vec(b2), vec(w3), vec(b3),
                  pl.BlockSpec((1,) + w4s.shape[1:], lambda i: (i // half_steps, 0, 0, 0)),
                  vec(fr),
                  pl.BlockSpec((rows, 1), lambda i: (i, 0)),
                  vec(absd)],
        out_specs=[pl.BlockSpec((rows // FFT_N2 * TPITCH, c), lambda i: (i, 0)),
                   pl.BlockSpec((1, c), lambda i: (0, 0))],
        out_shape=[jax.ShapeDtypeStruct((n // FFT_N2 * TPITCH, c), F32), jax.ShapeDtypeStruct((1, c), F32)],
        compiler_params=_params(("arbitrary",)),
        name="filt_mlp",
    )(zp, w1p, b1, w2, b2, w3, b3, w4s, fr, t_col, absd)


def _filt_fft_kernel(h_ref, ss_ref, bias_ref, fa_ref, fb_ref, o_ref, zs_ref):
    scale = lax.rsqrt(ss_ref[...] + 1e-6)
    n1 = h_ref.shape[0] // TPITCH
    row = lax.broadcasted_iota(jnp.int32, (2 * FFT_N2, LANES), 0)
    impulse = jnp.where(row < FFT_N2, bias_ref[...], 0.0)

    def stage_a(n2, carry):
        sa = h_ref[pl.ds(n2, n1, stride=TPITCH), :].astype(BF16)
        sb = h_ref[pl.ds(n2 + 1, n1, stride=TPITCH), :].astype(BF16)
        zero = jnp.zeros_like(sa)
        rhs = jnp.concatenate([jnp.concatenate([sa, zero], axis=1), jnp.concatenate([zero, sb], axis=1)], axis=0)
        z = _dot(fa_ref[n2 // 2], rhs)
        for k in range(2):
            zk = z[:, k * LANES:(k + 1) * LANES]
            zs_ref[pl.ds(n2 + k, KP, stride=ZPITCH), :] = zk[:KP]
            zs_ref[pl.ds(FFT_N2 + n2 + k, KP, stride=ZPITCH), :] = zk[KP:]
        return carry

    lax.fori_loop(0, FFT_N2 // 2, lambda p, c: stage_a(2 * p, c), 0, unroll=4)

    def stage_b(p, carry):
        b0 = pl.multiple_of(2 * p * ZPITCH, SUBLANES)
        b1 = pl.multiple_of(b0 + ZPITCH, SUBLANES)
        z = jnp.concatenate([zs_ref[pl.ds(b0, 2 * FFT_N2), :], zs_ref[pl.ds(b1, 2 * FFT_N2), :]], axis=1)
        x = _dot(fb_ref[...], z.astype(BF16))
        o_ref[0, 2 * p] = x[:, :LANES] * scale + impulse
        o_ref[0, 2 * p + 1] = x[:, LANES:] * scale + impulse
        return carry

    lax.fori_loop(0, KHP // 2, stage_b, 0, unroll=11)


def _filt_fft(h2u, ss, bias, fa_full, fb):
    n, c = h2u.shape
    nblk = c // LANES
    return pl.pallas_call(
        _filt_fft_kernel,
        grid=(nblk,),
        in_specs=[pl.BlockSpec((n, LANES), lambda j: (0, j), pipeline_mode=pl.Buffered(1)),
                  pl.BlockSpec((1, LANES), lambda j: (0, j)),
                  pl.BlockSpec((1, LANES), lambda j: (0, j)),
                  _const_spec(fa_full.shape),
                  _const_spec(fb.shape)],
        out_specs=pl.BlockSpec((1, KHP, 2 * FFT_N2, LANES), lambda j: (j, 0, 0, 0)),
        out_shape=jax.ShapeDtypeStruct((nblk, KHP, 2 * FFT_N2, LANES), F32),
        scratch_shapes=[pltpu.VMEM((KP * ZPITCH, LANES), F32)],
        compiler_params=_params(("arbitrary",)),
        name="filt_fft",
    )(h2u, ss, bias, fa_full, fb)


def _conv_slab(u_ref, cw_ref, j, n_slabs):
    r0 = pl.multiple_of(j * FFT_N2, FFT_N2)
    cur = u_ref[0, pl.ds(r0, FFT_N2), :].astype(F32)
    grp = 2 * SUBLANES
    pr0 = pl.multiple_of(jnp.maximum(j * FFT_N2 - grp, 0), grp)
    nr0 = pl.multiple_of(jnp.minimum((j + 1) * FFT_N2, (n_slabs - 1) * FFT_N2), grp)
    prev_row = u_ref[0, pl.ds(pr0, grp), :].astype(F32)[grp - 1:grp]
    next_row = u_ref[0, pl.ds(nr0, grp), :].astype(F32)[0:1]
    prev_row = jnp.where(j > 0, prev_row, 0.0)
    next_row = jnp.where(j < n_slabs - 1, next_row, 0.0)
    row = lax.broadcasted_iota(jnp.int32, cur.shape, 0)
    before = jnp.where(row == 0, prev_row, pltpu.roll(cur, 1, 0))
    after = jnp.where(row == FFT_N2 - 1, next_row, pltpu.roll(cur, FFT_N2 - 1, 0))
    w = cw_ref[...]
    return before * w[0:1] + cur * w[1:2] + after * w[2:3] + w[3:4]


def _hyena_kernel(x0_ref, x1_ref, v_ref, cw0_ref, cw1_ref, cwv_ref, hf_ref,
                  fa_ref, fai_ref, fb_ref, fbi_ref, o_ref, ts_ref, zs_ref):
    n_slabs = x0_ref.shape[1] // FFT_N2

    def gated_value(j):
        return _conv_slab(x1_ref, cw1_ref, j, n_slabs) * _conv_slab(v_ref, cwv_ref, j, n_slabs)

    def fill(j, carry):
        ts_ref[pl.ds(pl.multiple_of(j * TPITCH, SUBLANES), FFT_N2), :] = gated_value(j)
        return carry

    lax.fori_loop(0, n_slabs, fill, 0, unroll=2)

    def stage_a(n2, carry):
        sa = ts_ref[pl.ds(n2, n_slabs, stride=TPITCH), :].astype(BF16)
        sb = ts_ref[pl.ds(n2 + 1, n_slabs, stride=TPITCH), :].astype(BF16)
        zero = jnp.zeros_like(sa)
        rhs = jnp.concatenate([jnp.concatenate([sa, zero], axis=1), jnp.concatenate([zero, sb], axis=1)], axis=0)
        z = _dot(fa_ref[n2 // 2], rhs)
        for k in range(2):
            zk = z[:, k * LANES:(k + 1) * LANES]
            zs_ref[pl.ds(n2 + k, KP, stride=ZPITCH), :] = zk[:KP]
            zs_ref[pl.ds(FFT_N2 + n2 + k, KP, stride=ZPITCH), :] = zk[KP:]
        return carry

    lax.fori_loop(0, FFT_N2 // 2, lambda p, c: stage_a(2 * p, c), 0, unroll=8)

    def stage_b(p, carry):
        b0 = pl.multiple_of(2 * p * ZPITCH, SUBLANES)
        b1 = pl.multiple_of(b0 + ZPITCH, SUBLANES)
        z = jnp.concatenate([zs_ref[pl.ds(b0, 2 * FFT_N2), :], zs_ref[pl.ds(b1, 2 * FFT_N2), :]], axis=1)
        x = _dot(fb_ref[...], z.astype(BF16))
        h = jnp.concatenate([hf_ref[0, 2 * p], hf_ref[0, 2 * p + 1]], axis=1)
        xr, xi = x[:FFT_N2], x[FFT_N2:]
        hr, hi = h[:FFT_N2], h[FFT_N2:]
        prod = jnp.concatenate([xr * hr - xi * hi, xr * hi + xi * hr], axis=0)
        y = _dot(fbi_ref[...], prod.astype(BF16))
        zs_ref[pl.ds(b0, 2 * FFT_N2), :] = y[:, :LANES]
        zs_ref[pl.ds(b1, 2 * FFT_N2), :] = y[:, LANES:]
        return carry

    lax.fori_loop(0, KHP // 2, stage_b, 0, unroll=11)

    def stage_ai(n2, carry):
        yr = zs_ref[pl.ds(n2, KP, stride=ZPITCH), :]
        yi = zs_ref[pl.ds(FFT_N2 + n2, KP, stride=ZPITCH), :]
        y = jnp.concatenate([yr, yi], axis=0).astype(BF16)
        ts_ref[pl.ds(n2, n_slabs, stride=TPITCH), :] = _dot(fai_ref[n2], y)
        return carry

    lax.fori_loop(0, FFT_N2, stage_ai, 0, unroll=16)

    def finish(j, carry):
        conv = ts_ref[pl.ds(pl.multiple_of(j * TPITCH, SUBLANES), FFT_N2), :]
        y = _conv_slab(x0_ref, cw0_ref, j, n_slabs) * conv
        o_ref[0, pl.ds(pl.multiple_of(j * FFT_N2, FFT_N2), FFT_N2), :] = y.astype(BF16)
        return carry

    lax.fori_loop(0, n_slabs, finish, 0, unroll=2)


def _hyena(u, cw, hf, fa, fai, fb, fbi):
    b, seq, c3 = u.shape
    nblk = D_HYENA // LANES
    n_slabs = seq // FFT_N2
    stream = lambda k: pl.BlockSpec((1, seq, LANES), lambda j, i, k=k: (i, 0, k * nblk + j))
    cwspec = lambda k: pl.BlockSpec((SUBLANES, LANES), lambda j, i, k=k: (0, k * nblk + j))
    return pl.pallas_call(
        _hyena_kernel,
        grid=(nblk, b),
        in_specs=[stream(0), stream(1), stream(2), cwspec(0), cwspec(1), cwspec(2),
                  pl.BlockSpec((1, KHP, 2 * FFT_N2, LANES), lambda j, i: (j, 0, 0, 0),
                               pipeline_mode=pl.Buffered(1)),
                  _const_spec(fa.shape), _const_spec(fai.shape),
                  _const_spec(fb.shape), _const_spec(fbi.shape)],
        out_specs=pl.BlockSpec((1, seq, LANES), lambda j, i: (i, 0, j)),
        out_shape=jax.ShapeDtypeStruct((b, seq, D_HYENA), BF16),
        scratch_shapes=[pltpu.VMEM((n_slabs * TPITCH, LANES), F32),
                        pltpu.VMEM((KP * ZPITCH, LANES), F32)],
        compiler_params=_params(("arbitrary", "arbitrary")),
        name="hyena",
    )(u, u, u, cw, cw, cw, hf, fa, fai, fb, fbi)


def _attn_kernel(sink_ref, q_ref, kp_ref, km_ref, kn_ref, vp_ref, vm_ref, vn_ref, kc_ref, vc_ref,
                 o_ref, ka_ref, va_ref):
    i = pl.program_id(1)
    n_i = pl.num_programs(1)
    tq = q_ref.shape[0]
    nqb = tq // BLOCK
    ka_ref[0:BLOCK] = kp_ref[...]
    ka_ref[BLOCK:BLOCK + tq] = km_ref[...]
    ka_ref[BLOCK + tq:] = kn_ref[...]
    va_ref[0:BLOCK] = vp_ref[...]
    va_ref[BLOCK:BLOCK + tq] = vm_ref[...]
    va_ref[BLOCK + tq:] = vn_ref[...]

    qi = lax.broadcasted_iota(jnp.int32, (BLOCK, BLOCK), 0)
    kj = lax.broadcasted_iota(jnp.int32, (BLOCK, BLOCK), 1)
    lane = lax.broadcasted_iota(jnp.int32, (BLOCK, LANES), 1)
    low = lane < HEAD_DIM
    half = GQA_GROUP // 2
    rows2 = half * BLOCK
    hrow = lax.broadcasted_iota(jnp.int32, (GQA_GROUP * BLOCK, 1), 0) // BLOCK
    head_order = [hh for hh in range(GQA_GROUP) if hh % 2 == 0] + [hh for hh in range(GQA_GROUP) if hh % 2 == 1]
    one = jnp.ones((), BF16)

    def with_ones(v):
        lanes_low = lax.broadcasted_iota(jnp.int32, v.shape, 1) < HEAD_DIM
        return jnp.where(lanes_low, v, one), jnp.where(lanes_low, one, v)

    for g in range(N_KV_HEADS):
        gl = slice(g * LANES, (g + 1) * LANES)
        vc_even, vc_odd = with_ones(vc_ref[:, gl])
        sink = jnp.zeros((GQA_GROUP * BLOCK, 1), F32)
        for pos, hh in enumerate(head_order):
            sink = jnp.where(hrow == pos, sink_ref[g * GQA_GROUP + hh], sink)
        for j in range(nqb):
            prev_ok = kj >= qi
            next_ok = kj <= qi
            if j == 0:
                prev_ok = prev_ok & (i > 0)
            if j == nqb - 1:
                next_ok = next_ok & (i < n_i - 1)
            bias_p = jnp.concatenate([jnp.where(prev_ok, 0.0, NEG_INF).astype(F32)] * GQA_GROUP, axis=0)
            bias_n = jnp.concatenate([jnp.where(next_ok, 0.0, NEG_INF).astype(F32)] * GQA_GROUP, axis=0)
            qb = q_ref[j * BLOCK:(j + 1) * BLOCK, :]
            parts = []
            for hh in head_order:
                h = g * GQA_GROUP + hh
                qp = qb[:, (h // 2) * LANES:(h // 2 + 1) * LANES]
                parts.append(jnp.where(low if h % 2 == 0 else ~low, qp, jnp.zeros_like(qp)))
            qs = jnp.concatenate(parts, axis=0)
            kw = ka_ref[j * BLOCK:(j + 3) * BLOCK, gl]
            vw_even, vw_odd = with_ones(va_ref[j * BLOCK:(j + 3) * BLOCK, gl])
            s_w = _dot_nt(qs, kw)
            s_p = s_w[:, :BLOCK] + bias_p
            s_m = s_w[:, BLOCK:2 * BLOCK]
            s_n = s_w[:, 2 * BLOCK:] + bias_n
            s_c = _dot_nt(qs, kc_ref[:, gl])
            m = jnp.maximum(jnp.maximum(jnp.max(jnp.maximum(jnp.maximum(s_p, s_m), s_n), axis=1, keepdims=True),
                                        jnp.max(s_c, axis=1, keepdims=True)), sink)
            e_w = jnp.concatenate([jnp.exp(s_p - m), jnp.exp(s_m - m), jnp.exp(s_n - m)], axis=1).astype(BF16)
            e_c = jnp.exp(s_c - m).astype(BF16)
            e_sink = jnp.exp(sink - m)
            outs = []
            for par, (vw, vcx) in enumerate(((vw_even, vc_even), (vw_odd, vc_odd))):
                rs = slice(par * rows2, (par + 1) * rows2)
                acc = _dot(e_w[rs], vw) + _dot(e_c[rs], vcx)
                den = pltpu.roll(acc, HEAD_DIM, 1) + e_sink[rs]
                outs.append(acc / den)
            for pp in range(half):
                pair = jnp.where(low, outs[0][pp * BLOCK:(pp + 1) * BLOCK], outs[1][pp * BLOCK:(pp + 1) * BLOCK])
                col = (g * half + pp) * LANES
                o_ref[j * BLOCK:(j + 1) * BLOCK, col:col + LANES] = pair.astype(BF16)


def _attention(sinks, q, kd, vd, kc, vc, batch, seq, tq):
    t = q.shape[0]
    per_b = seq // tq
    nqb = tq // BLOCK
    nb = seq // BLOCK
    n_ctx = kc.shape[0] // batch
    main = lambda b, i: (b * per_b + i, 0)
    prev = lambda b, i: (b * nb + jnp.maximum(i * nqb - 1, 0), 0)
    nxt = lambda b, i: (b * nb + jnp.minimum(i * nqb + nqb, nb - 1), 0)
    kvw = kd.shape[1]
    return pl.pallas_call(
        _attn_kernel,
        grid=(batch, per_b),
        in_specs=[pl.BlockSpec(memory_space=pltpu.SMEM),
                  pl.BlockSpec((tq, D_ATTN), main),
                  pl.BlockSpec((BLOCK, kvw), prev), pl.BlockSpec((tq, kvw), main), pl.BlockSpec((BLOCK, kvw), nxt),
                  pl.BlockSpec((BLOCK, kvw), prev), pl.BlockSpec((tq, kvw), main), pl.BlockSpec((BLOCK, kvw), nxt),
                  pl.BlockSpec((n_ctx, kvw), lambda b, i: (b, 0)),
                  pl.BlockSpec((n_ctx, kvw), lambda b, i: (b, 0))],
        out_specs=pl.BlockSpec((tq, D_ATTN), main),
        out_shape=jax.ShapeDtypeStruct((t, D_ATTN), BF16),
        scratch_shapes=[pltpu.VMEM((tq + 2 * BLOCK, kvw), BF16),
                        pltpu.VMEM((tq + 2 * BLOCK, kvw), BF16)],
        compiler_params=_params(("arbitrary", "arbitrary")),
        name="attention",
    )(sinks, q, kd, kd, kd, vd, vd, vd, kc, vc)


def _route(t, wr_hi_ref, wr_hl_ref, br_ref):
    th, tl = _split(t)
    both = _dot(th, wr_hl_ref[...])
    logits = both[:, :LANES] + both[:, LANES:] + _dot(tl, wr_hi_ref[...]) + br_ref[...]
    lane_i = lax.broadcasted_iota(jnp.int32, logits.shape, 1)
    lane = lane_i.astype(F32)
    grp_of_lane = (lane_i >> 2).astype(F32)
    ninf = -jnp.inf
    far = float(LANES)
    is_g = (lane_i >= N_EXPERTS) & (lane_i < N_EXPERTS + N_GROUPS)
    glog = jnp.where(is_g, logits, ninf)
    gmax = jnp.max(glog, axis=1, keepdims=True)
    gidx = jnp.min(jnp.where(glog == gmax, lane - float(N_EXPERTS), far), axis=1, keepdims=True)
    group_p = 1.0 / jnp.sum(jnp.exp(glog - gmax), axis=1, keepdims=True)
    in_grp = (lane_i < N_EXPERTS) & (grp_of_lane == gidx)
    elog = jnp.where(in_grp, logits, ninf)
    v1 = jnp.max(elog, axis=1, keepdims=True)
    i1 = jnp.min(jnp.where(elog == v1, lane, far), axis=1, keepdims=True)
    elog2 = jnp.where(lane == i1, ninf, elog)
    v2 = jnp.max(elog2, axis=1, keepdims=True)
    i2 = jnp.min(jnp.where(elog2 == v2, lane, far), axis=1, keepdims=True)
    e = jnp.exp(v2 - v1)
    w1 = group_p / (1.0 + e)
    w2 = group_p * e / (1.0 + e)
    gate = jnp.where(lane == i1, w1, 0.0) + jnp.where(lane == i2, w2, 0.0)
    return gate + jnp.where(lane_i == GROUP_LANE, gidx, 0.0)


def _mix_out_kernel(x_ref, h_ref, yh_ref, ya_ref, g1_ref, sh2_ref, sc2_ref,
                    wg_ref, wbh_ref, wba_ref, wo_ref, lng_ref, lnb_ref, wrh_ref, wrhl_ref, br_ref,
                    x1_ref, gate_ref, gsel_ref, *t_refs):
    d = x_ref.shape[1]
    for r0 in range(0, x_ref.shape[0], SUB_ROWS):
        rows = slice(r0, r0 + SUB_ROWS)
        x = x_ref[rows, :]
        h = h_ref[rows, :]
        g_hy = jax.nn.sigmoid(_dot(h, wg_ref[:, :d]))
        merged = g_hy * _dot(yh_ref[rows, :], wbh_ref[...])
        g_at = jax.nn.sigmoid(_dot(h, wg_ref[:, d:]))
        merged = merged + g_at * _dot(ya_ref[rows, :], wba_ref[...])
        mix = _dot(merged.astype(BF16), wo_ref[...])
        x1 = _standardize(DEEPNORM_ALPHA * x + g1_ref[0] * mix) * lng_ref[...] + lnb_ref[...]
        x1_ref[rows, :] = x1
        t = _standardize(x1) * (1.0 + sc2_ref[0]) + sh2_ref[0]
        gate = _route(t, wrh_ref, wrhl_ref, br_ref)
        gate_ref[rows, :] = gate
        grp = gate.T[GROUP_LANE:GROUP_LANE + 1, :]
        gsel_ref[r0 // LANES:(r0 + SUB_ROWS) // LANES, :] = jnp.concatenate(
            [grp[:, k * LANES:(k + 1) * LANES] for k in range(SUB_ROWS // LANES)], axis=0)
        for t_ref, piece in zip(t_refs, _pack_pieces(t)):
            t_ref[rows, :] = piece


def _mix_out(x2d, h, yh, ya, mods, w_g, w_bh, w_ba, w_o, ln_g, ln_b, wr_hi, wr_hl, br, seq, tm):
    t, d = x2d.shape
    per_b = seq // tm
    row = lambda i: (i, 0)
    mod = lambda i: (i // per_b, 0, 0)
    mspec = pl.BlockSpec((1, 1, d), mod)
    g1, sh2, sc2 = mods
    return pl.pallas_call(
        _mix_out_kernel,
        grid=(t // tm,),
        in_specs=[pl.BlockSpec((tm, d), row),
                  pl.BlockSpec((tm, d), row),
                  pl.BlockSpec((tm, yh.shape[1]), row),
                  pl.BlockSpec((tm, ya.shape[1]), row),
                  mspec, mspec, mspec,
                  _const_spec(w_g.shape), _const_spec(w_bh.shape), _const_spec(w_ba.shape),
                  _const_spec(w_o.shape), _const_spec(ln_g.shape), _const_spec(ln_b.shape),
                  _const_spec(wr_hi.shape), _const_spec(wr_hl.shape), _const_spec(br.shape)],
        out_specs=[pl.BlockSpec((tm, d), row), pl.BlockSpec((tm, LANES), row),
                   pl.BlockSpec((tm // LANES, LANES), row)]
        + [pl.BlockSpec((tm, SC_ROW_WORDS), row)] * N_PIECES,
        out_shape=[jax.ShapeDtypeStruct((t, d), F32), jax.ShapeDtypeStruct((t, LANES), F32),
                   jax.ShapeDtypeStruct((t // LANES, LANES), F32)]
        + [jax.ShapeDtypeStruct((t, SC_ROW_WORDS), jnp.uint32)] * N_PIECES,
        compiler_params=_params(("arbitrary",)),
        name="mix_out",
    )(x2d, h, yh, ya, g1, sh2, sc2, w_g, w_bh, w_ba, w_o, ln_g, ln_b, wr_hi, wr_hl, br)


def _slots_kernel(g_ref, upper_ref, lower_ref, dest_ref, meta_ref):
    gsel = g_ref[...]
    dest = jnp.zeros(gsel.shape, F32)
    base = jnp.zeros((1, 1), F32)
    chunk_start = lax.broadcasted_iota(jnp.int32, (1, LANES), 1).astype(F32) * float(MOE_CHUNK)
    owner = jnp.zeros((1, LANES), F32)
    for g in range(N_GROUPS):
        onehot = jnp.where(gsel == float(g), 1.0, 0.0)
        in_row = _dot(onehot.astype(BF16), upper_ref[...])
        row_tot = jnp.sum(onehot, axis=1, keepdims=True)
        rows_before = _dot(lower_ref[...], jnp.broadcast_to(row_tot, onehot.shape).astype(BF16))
        dest = dest + onehot * (base + rows_before + in_row)
        if g > 0:
            owner = owner + jnp.where(chunk_start >= base, 1.0, 0.0)
        n_g = jnp.sum(row_tot, axis=0, keepdims=True)
        base = base + jnp.floor((n_g + float(MOE_CHUNK - 1)) * (1.0 / MOE_CHUNK)) * float(MOE_CHUNK)
    dest_ref[...] = dest.astype(jnp.int32)
    row = lax.broadcasted_iota(jnp.int32, meta_ref.shape, 0)
    meta_ref[...] = jnp.where(row == 0, owner, base * (1.0 / MOE_CHUNK)).astype(jnp.int32)


def _slots(gsel):
    r = gsel.shape[0]
    upper = jnp.asarray(np.triu(np.ones((LANES, LANES), np.float32), 1)).astype(BF16)
    lower = jnp.asarray(np.tril(np.ones((r, r), np.float32), -1)).astype(BF16)
    full = lambda a: pl.BlockSpec(a.shape, lambda i: (0,) * a.ndim)
    return pl.pallas_call(
        _slots_kernel,
        grid=(1,),
        in_specs=[full(gsel), full(upper), full(lower)],
        out_specs=[pl.BlockSpec((r, LANES), lambda i: (0, 0)), pl.BlockSpec((SUBLANES, LANES), lambda i: (0, 0))],
        out_shape=[jax.ShapeDtypeStruct((r, LANES), jnp.int32), jax.ShapeDtypeStruct((SUBLANES, LANES), jnp.int32)],
        compiler_params=_params(("arbitrary",)),
        name="moe_slots",
    )(gsel, upper, lower)


def _experts_kernel(owner_ref, used_ref, *refs):
    n_p = N_PIECES
    t_refs, gate_ref = refs[:n_p], refs[n_p]
    wg_ref, wu_ref, wd_ref = refs[n_p + 1:n_p + 4]
    y_refs = refs[n_p + 4:2 * n_p + 4]
    wg16_ref, wu16_ref, wd16_ref = refs[2 * n_p + 4:]
    c = pl.program_id(0)
    f = D_EXPERT

    @pl.when((c == 0) | (owner_ref[c] != owner_ref[jnp.maximum(c - 1, 0)]))
    def _():
        wg16_ref[...] = wg_ref[...].astype(BF16)
        wu16_ref[...] = wu_ref[...].astype(BF16)
        for e in range(EXPERTS_PER_GROUP):
            wd16_ref[e * f:(e + 1) * f, :] = wd_ref[e].astype(BF16)

    @pl.when(c < used_ref[0])
    def _():
        t = _unpack_pieces([r[...] for r in t_refs]).astype(BF16)
        gate = gate_ref[...]
        lane = lax.broadcasted_iota(jnp.int32, gate.shape, 1)
        first = owner_ref[c] * EXPERTS_PER_GROUP
        parts = []
        for e in range(EXPERTS_PER_GROUP):
            a = _dot(t, wg16_ref[e])
            u = _dot(t, wu16_ref[e])
            ge = jnp.sum(jnp.where(lane == first + e, gate, 0.0), axis=1, keepdims=True)
            parts.append((a * jax.nn.sigmoid(a) * u * ge).astype(BF16))
        y = _dot(jnp.concatenate(parts, axis=1), wd16_ref[...])
        for y_ref, piece in zip(y_refs, _pack_pieces(y)):
            y_ref[...] = piece

    @pl.when(c >= used_ref[0])
    def _():
        for y_ref in y_refs:
            y_ref[...] = jnp.zeros_like(y_ref)


def _experts(owner, used, t_pieces, gate_sorted, w_gate, w_up, w_down):
    n_slots = gate_sorted.shape[0]
    d, f = w_gate.shape[1:]
    row = lambda c, owner, used: (c, 0)
    by_owner = lambda c, owner, used: (owner[c], 0, 0)
    piece = pl.BlockSpec((MOE_CHUNK, SC_ROW_WORDS), row)
    group_of_experts = lambda w: pl.BlockSpec((EXPERTS_PER_GROUP,) + w.shape[1:], by_owner)
    return pl.pallas_call(
        _experts_kernel,
        grid_spec=pltpu.PrefetchScalarGridSpec(
            num_scalar_prefetch=2,
            grid=(n_slots // MOE_CHUNK,),
            in_specs=[piece] * len(t_pieces) + [pl.BlockSpec((MOE_CHUNK, LANES), row),
                                                group_of_experts(w_gate), group_of_experts(w_up),
                                                group_of_experts(w_down)],
            out_specs=[piece] * len(t_pieces),
            scratch_shapes=[pltpu.VMEM((EXPERTS_PER_GROUP, d, f), BF16),
                            pltpu.VMEM((EXPERTS_PER_GROUP, d, f), BF16),
                            pltpu.VMEM((EXPERTS_PER_GROUP * f, d), BF16)],
        ),
        out_shape=[jax.ShapeDtypeStruct((n_slots, SC_ROW_WORDS), jnp.uint32)] * len(t_pieces),
        compiler_params=_params(("arbitrary",)),
        name="moe_experts",
    )(owner, used, *t_pieces, gate_sorted, w_gate, w_up, w_down)


def _final_kernel(x1_ref, g2_ref, lng_ref, lnb_ref, *refs):
    y_refs, o_ref = refs[:-1], refs[-1]
    y = _unpack_pieces([r[...] for r in y_refs])
    o_ref[...] = _standardize(DEEPNORM_ALPHA * x1_ref[...] + g2_ref[0] * y) * lng_ref[...] + lnb_ref[...]


def _final(x1, g2, ln_g, ln_b, y_pieces, seq, tm):
    t, d = x1.shape
    per_b = seq // tm
    row = lambda i: (i, 0)
    return pl.pallas_call(
        _final_kernel,
        grid=(t // tm,),
        in_specs=[pl.BlockSpec((tm, d), row), pl.BlockSpec((1, 1, d), lambda i: (i // per_b, 0, 0)),
                  _const_spec(ln_g.shape), _const_spec(ln_b.shape)]
        + [pl.BlockSpec((tm, SC_ROW_WORDS), row)] * len(y_pieces),
        out_specs=pl.BlockSpec((tm, d), row),
        out_shape=jax.ShapeDtypeStruct((t, d), F32),
        compiler_params=_params(("arbitrary",)),
        name="moe_final",
    )(x1, g2, ln_g, ln_b, *y_pieces)


def _sc_move_rows(tables, idx, n_out, scatter, name):
    n = idx.shape[0]
    mesh = plsc.VectorSubcoreMesh(core_axis_name="c", subcore_axis_name="s")
    out_type = [jax.ShapeDtypeStruct((n_out, t.shape[1]), t.dtype) for t in tables]
    window = lambda i: (i, 0)
    index_win = pl.BlockSpec((1, SC_WINDOW), lambda i: (0, i))
    split = dict(core_axis_name=("c", "s"), dimension_semantics=(pltpu.PARALLEL,))

    @functools.partial(pl.kernel, out_type=out_type, mesh=mesh, scratch_types=[], name=name)
    def move(*refs):
        srcs, i_hbm, dsts = refs[:len(tables)], refs[len(tables)], refs[len(tables) + 1:]
        for src, dst in zip(srcs, dsts):
            rows_win = pl.BlockSpec((SC_WINDOW, src.shape[1]), window)
            if scatter:
                def body(x_vmem, i_vmem, dst=dst):
                    pltpu.sync_copy(x_vmem, dst.at[i_vmem.at[0]])

                pltpu.emit_pipeline(body, grid=(n // SC_WINDOW,), in_specs=[rows_win, index_win],
                                    out_specs=[], **split)(src, i_hbm)
            else:
                def body(i_vmem, o_vmem, src=src):
                    pltpu.sync_copy(src.at[i_vmem.at[0]], o_vmem)

                pltpu.emit_pipeline(body, grid=(n // SC_WINDOW,), in_specs=[index_win],
                                    out_specs=[rows_win], **split)(i_hbm, dst)

    return move(*tables, idx.reshape(1, n))


def _dft_tables(seq):
    n = 2 * seq
    n1_full = n // FFT_N2
    n1_data = seq // FFT_N2
    k1 = np.arange(KH, dtype=np.float64)[None, :, None]
    n2 = np.arange(FFT_N2, dtype=np.float64)[:, None, None]
    n1 = np.arange(n1_full, dtype=np.float64)[None, None, :]
    ang = 2.0 * np.pi * k1 * (FFT_N2 * n1 + n2) / n
    fa = np.zeros((FFT_N2, 2 * KP, n1_full))
    fa[:, :KH] = np.cos(ang)
    fa[:, KP:KP + KH] = -np.sin(ang)
    wgt = np.full((KH,), 2.0)
    wgt[0] = 1.0
    wgt[KH - 1] = 1.0
    fai = np.zeros((FFT_N2, n1_data, 2 * KP))
    angt = np.transpose(ang[:, :, :n1_data], (0, 2, 1))
    fai[:, :, :KH] = np.cos(angt) * wgt / n
    fai[:, :, KP:KP + KH] = -np.sin(angt) * wgt / n
    kk = np.arange(FFT_N2, dtype=np.float64)
    a2 = 2.0 * np.pi * np.outer(kk, kk) / FFT_N2
    fr, fi = np.cos(a2), -np.sin(a2)
    fb = np.block([[fr, -fi], [fi, fr]])
    fbi = np.block([[fr, fi], [-fi, fr]])
    f32 = lambda a: jnp.asarray(a.astype(np.float32))
    pairs = lambda a: np.concatenate([a[0::2], a[1::2]], axis=2)
    return f32(pairs(fa[:, :, :n1_data])), f32(pairs(fa)), f32(fai), f32(fb), f32(fbi)


def _filter_features(seq, rows):
    f32 = np.float32
    t = np.linspace(0.0, 1.0, seq, dtype=f32)
    w = (f32(2.0 * math.pi) * np.arange(seq, dtype=f32) / f32(seq)).astype(f32)
    t2 = np.concatenate([t, t[::-1]])
    w2 = np.concatenate([w, w[::-1]])
    bands = np.linspace(1e-4, FILTER_BANDS - 1, FILTER_BANDS, dtype=f32)
    max_decay = math.log(DECAY_TARGET) / FAST_DECAY_PCT
    min_decay = math.log(DECAY_TARGET) / SLOW_DECAY_PCT
    deltas = jnp.linspace(min_decay, max_decay, D_HYENA, dtype=F32)

    def feats(k):
        tp = t2.reshape(-1, 2, rows // 2)[:, k].reshape(-1, 1)
        wp = w2.reshape(-1, 2, rows // 2)[:, k].reshape(-1, 1)
        pad = np.zeros((tp.shape[0], FILTER_ORDER - FILTER_EMB), f32)
        arg = (bands * wp).astype(f32)
        return np.concatenate([tp, np.cos(arg), -np.sin(arg), pad], axis=-1).astype(f32)

    zp = jnp.asarray(np.concatenate([feats(0), feats(1)], axis=-1))
    return zp, jnp.asarray(t2)[:, None], jnp.abs(deltas)[None, :]


def _rope_tables(seq):
    f32 = np.float32
    rows = seq // GRID_W
    row = np.repeat(np.arange(rows, dtype=f32), GRID_W)
    col = np.tile(np.arange(GRID_W, dtype=f32), rows)
    half = HEAD_DIM // 2
    inv_freq = (f32(ROPE_BASE) ** (-np.arange(0, half, 2, dtype=f32) / f32(half))).astype(f32)
    ang = np.concatenate([row[:, None] * inv_freq, col[:, None] * inv_freq], axis=-1).astype(f32)
    cos, sin = np.cos(ang), np.sin(ang)
    c64 = np.concatenate([cos, cos], axis=-1)
    s64 = np.concatenate([-sin, sin], axis=-1)
    return (jnp.asarray(np.concatenate([c64, c64], axis=-1).astype(f32)),
            jnp.asarray(np.concatenate([s64, s64], axis=-1).astype(f32)))


def _head_perm(n_heads):
    idx = []
    for h in range(n_heads):
        base = h * HEAD_DIM
        idx += [base + 2 * j for j in range(HEAD_DIM // 2)]
        idx += [base + 2 * j + 1 for j in range(HEAD_DIM // 2)]
    return np.asarray(idx, dtype=np.int32)


def _dup_heads(w):
    parts = []
    for g in range(N_KV_HEADS):
        blk = w[:, g * HEAD_DIM:(g + 1) * HEAD_DIM]
        parts += [blk, blk]
    return jnp.concatenate(parts, axis=1)


def kernel(x, c, ctx, c_ctx, ada_w, ada_b, w_in, hy_conv_w, hy_conv_b, hy_w1, hy_b1, hy_w2, hy_b2, hy_w3, hy_b3, hy_w4, hy_freq, hy_bias, attn_sinks, w_branch_hy, w_branch_attn, w_out, ln1_g, ln1_b, w_group, b_group, w_router, b_router, w_gate_e, w_up_e, w_down_e, ln2_g, ln2_b):
    batch, seq, d = x.shape
    n_ctx = ctx.shape[1]
    assert d == D_MODEL and ada_w.shape[0] == DEPTH == 1
    assert 2 * seq == FFT_N2 * FFT_N2 and seq % TM_PROJ == 0
    l = 0

    w = w_in[l]
    s0, s1, s2, s3 = 3 * D_HYENA, 3 * D_HYENA + D_ATTN, 3 * D_HYENA + D_ATTN + D_KV, 3 * D_HYENA + D_ATTN + 2 * D_KV
    w_q = w[:, s0:s1][:, _head_perm(N_HEADS)]
    w_k = _dup_heads(w[:, s1:s2][:, _head_perm(N_KV_HEADS)])
    w_v = _dup_heads(w[:, s2:s3])
    w_c = jnp.concatenate([w[:, :s0], w_q, w_k, w_v], axis=1).astype(BF16)
    w_kv = jnp.concatenate([w_k, w_v], axis=1).astype(BF16)
    w_g = w[:, s3:].astype(BF16)
    w_bh = w_branch_hy[l].astype(BF16)
    w_ba = w_branch_attn[l].astype(BF16)
    w_o = w_out[l].astype(BF16)
    wr = jnp.zeros((d, LANES), F32)
    wr = wr.at[:, :N_EXPERTS].set(w_router[l]).at[:, N_EXPERTS:N_EXPERTS + N_GROUPS].set(w_group[l])
    wr_hi = wr.astype(BF16)
    wr_hl = jnp.concatenate([wr_hi, (wr - wr_hi.astype(F32)).astype(BF16)], axis=1)
    br = jnp.zeros((1, LANES), F32)
    br = br.at[0, :N_EXPERTS].set(b_router[l]).at[0, N_EXPERTS:N_EXPERTS + N_GROUPS].set(b_group[l])
    cw = jnp.concatenate([hy_conv_w[l][:, 0, :], hy_conv_b[l][None, :],
                          jnp.zeros((SUBLANES - SHORT_CONV - 1, 3 * D_HYENA), F32)], axis=0)
    row2 = lambda a: a.reshape(1, -1)
    pair = lambda a: jnp.concatenate([a, a], axis=-1)
    zero_o = jnp.zeros((FILTER_ORDER, FILTER_ORDER), F32)
    bdiag = lambda a: jnp.concatenate([jnp.concatenate([a, zero_o], axis=1),
                                       jnp.concatenate([zero_o, a], axis=1)], axis=0)
    w1p = bdiag(jnp.concatenate([hy_w1[l], jnp.zeros((FILTER_ORDER - FILTER_EMB, FILTER_ORDER), F32)], axis=0))
    w4h = jnp.transpose(hy_w4[l].reshape(FILTER_ORDER, 2, D_HYENA), (1, 0, 2))
    zero_w4 = jnp.zeros_like(w4h)
    w4s = jnp.stack([jnp.concatenate([w4h, zero_w4], axis=1),
                     jnp.concatenate([zero_w4, w4h], axis=1)], axis=1)

    fa, fa_full, fai, fb, fbi = _dft_tables(seq)
    fa, fa_full, fai, fb, fbi = (a.astype(BF16) for a in (fa, fa_full, fai, fb, fbi))
    zp, t_col, absd = _filter_features(seq, FILT_ROWS)
    cos_t, sin_t = _rope_tables(seq)

    cond = jnp.concatenate([c, c_ctx[None], jnp.zeros((SUBLANES - batch - 1, d), F32)], axis=0)
    mods = _adaln(cond, ada_w[l], ada_b[l])
    m6 = [mods[:, k * d:(k + 1) * d].reshape(SUBLANES, 1, d) for k in range(6)]
    sh1, sc1, g1, sh2, sc2, g2 = m6

    kc, vc = _ctx_kv(ctx.reshape(batch * n_ctx, d), sh1[batch], sc1[batch], w_kv, n_ctx)

    x2d = x.reshape(batch * seq, d)
    u, q, kd, vd, h_mod = _in_proj(x2d, sh1, sc1, w_c, cos_t, sin_t, seq, TM_PROJ)
    h2u, ss = _filt_mlp(zp, w1p, pair(row2(hy_b1[l])), bdiag(hy_w2[l]), pair(row2(hy_b2[l])),
                        bdiag(hy_w3[l]), pair(row2(hy_b3[l])), w4s, pair(row2(hy_freq[l])), t_col, absd, FILT_ROWS)
    hf = _filt_fft(h2u, ss, row2(hy_bias[l]), fa_full, fb)
    y_hy = _hyena(u.reshape(batch, seq, 3 * D_HYENA), cw, hf, fa, fai, fb, fbi)
    y_at = _attention(attn_sinks[l], q, kd, vd, kc, vc, batch, seq, TQ_ATTN)
    x1, gate, gsel, *t_pieces = _mix_out(x2d, h_mod, y_hy.reshape(batch * seq, D_HYENA), y_at, (g1, sh2, sc2),
                                         w_g, w_bh, w_ba, w_o, row2(ln1_g[l]), row2(ln1_b[l]), wr_hi, wr_hl, br,
                                         seq, TM_PROJ)

    n_tok = batch * seq
    n_slots = n_tok + N_GROUPS * MOE_CHUNK
    dest2d, meta = _slots(gsel)
    dest = dest2d.reshape(n_tok)
    *t_sorted, gate_sorted = _sc_move_rows(t_pieces + [gate], dest, n_slots, True, "moe_sort")
    y_sorted = _experts(meta[0], meta[1, :1], t_sorted, gate_sorted, w_gate_e[l], w_up_e[l], w_down_e[l])
    y_pieces = _sc_move_rows(y_sorted, dest, n_tok, False, "moe_unsort")
    out = _final(x1, g2, row2(ln2_g[l]), row2(ln2_b[l]), y_pieces, seq, TM_FINAL)
    return out.reshape(batch, seq, d)
```

```python
import functools
import math

import numpy as np
import jax
import jax.numpy as jnp
from jax import lax
from jax.experimental import pallas as pl
from jax.experimental.pallas import tpu as pltpu
from jax.experimental.pallas import tpu_sc as plsc

F32 = jnp.float32
BF16 = jnp.bfloat16

D_MODEL = 1024
GRID_W = 64
D_HYENA = D_MODEL // 2
SHORT_CONV = 3
FILTER_BANDS = 16
FILTER_EMB = 1 + 2 * FILTER_BANDS
FILTER_ORDER = 64
DECAY_TARGET = 1e-2
FAST_DECAY_PCT = 0.3
SLOW_DECAY_PCT = 1.5
HEAD_DIM = 64
D_ATTN = D_MODEL // 2
N_HEADS = D_ATTN // HEAD_DIM
N_KV_HEADS = N_HEADS // 4
GQA_GROUP = N_HEADS // N_KV_HEADS
D_KV = N_KV_HEADS * HEAD_DIM
WINDOW = 128
BLOCK = 128
ROPE_BASE = 10000.0
NEG_INF = -1e30
N_GROUPS = 4
EXPERTS_PER_GROUP = 4
N_EXPERTS = N_GROUPS * EXPERTS_PER_GROUP
D_EXPERT = D_MODEL // 4
LN_EPS = 1e-5
DEPTH = 1
DEEPNORM_ALPHA = (2.0 * DEPTH) ** 0.25

LANES = 128
SUBLANES = 8
VMEM_LIMIT = 56 * 1024 * 1024

SUB_ROWS = 512
TM_PROJ = 1024
MOE_CHUNK = 512
TM_FINAL = 1024
GROUP_LANE = N_EXPERTS
SC_WINDOW = 128
SC_ROW_WORDS = 256
N_PIECES = D_MODEL // (2 * SC_ROW_WORDS)
TQ_ATTN = 1024
FILT_ROWS = 2048

FFT_N2 = 128
KH = 65
KHP = 66
KP = 72
ZPITCH = 2 * FFT_N2 + SUBLANES
TPITCH = FFT_N2 + SUBLANES


def _dot(a, b):
    return jnp.dot(a, b, preferred_element_type=F32)


def _dot_nt(a, b):
    return lax.dot_general(a, b, (((1,), (1,)), ((), ())), preferred_element_type=F32)


def _split(a):
    hi = a.astype(BF16)
    lo = (a - hi.astype(F32)).astype(BF16)
    return hi, lo


def _pack_pieces(x):
    w = SC_ROW_WORDS
    pieces = []
    for p in range(x.shape[1] // (2 * w)):
        hi = lax.bitcast_convert_type(x[:, 2 * p * w:(2 * p + 1) * w].astype(BF16).astype(F32), jnp.uint32)
        lo = lax.bitcast_convert_type(x[:, (2 * p + 1) * w:(2 * p + 2) * w].astype(BF16).astype(F32), jnp.uint32)
        pieces.append(hi | (lo >> 16))
    return pieces


def _unpack_pieces(pieces):
    cols = []
    for word in pieces:
        cols.append(lax.bitcast_convert_type(word & jnp.uint32(0xFFFF0000), F32))
        cols.append(lax.bitcast_convert_type(word << 16, F32))
    return jnp.concatenate(cols, axis=1)


def _dot3(a, b):
    ah, al = _split(a)
    bh, bl = _split(b)
    return _dot(ah, bh) + _dot(al, bh) + _dot(ah, bl)


def _standardize(x):
    mu = jnp.mean(x, axis=-1, keepdims=True)
    xc = x - mu
    var = jnp.mean(xc * xc, axis=-1, keepdims=True)
    return xc * lax.rsqrt(var + LN_EPS)


def _params(sem, vmem=VMEM_LIMIT):
    return pltpu.CompilerParams(dimension_semantics=sem, vmem_limit_bytes=vmem)


def _const_spec(shape):
    nd = len(shape)
    return pl.BlockSpec(shape, lambda *_: (0,) * nd, pipeline_mode=pl.Buffered(1))


def _adaln_kernel(c_ref, w_ref, b_ref, o_ref):
    s = c_ref[...]
    s = s * jax.nn.sigmoid(s)
    o_ref[...] = _dot3(s, w_ref[...]) + b_ref[...]


def _adaln(cond, w, b):
    n, d = cond.shape
    cols = w.shape[1]
    bc = 1024
    return pl.pallas_call(
        _adaln_kernel,
        grid=(cols // bc,),
        in_specs=[pl.BlockSpec((n, d), lambda j: (0, 0)),
                  pl.BlockSpec((d, bc), lambda j: (0, j)),
                  pl.BlockSpec((1, bc), lambda j: (0, j))],
        out_specs=pl.BlockSpec((n, bc), lambda j: (0, j)),
        out_shape=jax.ShapeDtypeStruct((n, cols), F32),
        compiler_params=_params(("arbitrary",)),
        name="adaln",
    )(cond, w, b.reshape(1, cols))


def _ctx_kv_kernel(x_ref, sh_ref, sc_ref, w_ref, k_ref, v_ref):
    h = _standardize(x_ref[...]) * (1.0 + sc_ref[...]) + sh_ref[...]
    kv = _dot(h.astype(BF16), w_ref[...])
    half = k_ref.shape[1]
    k_ref[...] = kv[:, :half].astype(BF16)
    v_ref[...] = kv[:, half:].astype(BF16)


def _ctx_kv(ctx2d, sh, sc, w_kv, rows):
    n, d = ctx2d.shape
    half = w_kv.shape[1] // 2
    return pl.pallas_call(
        _ctx_kv_kernel,
        grid=(n // rows,),
        in_specs=[pl.BlockSpec((rows, d), lambda i: (i, 0)),
                  pl.BlockSpec((1, d), lambda i: (0, 0)),
                  pl.BlockSpec((1, d), lambda i: (0, 0)),
                  pl.BlockSpec(w_kv.shape, lambda i: (0, 0))],
        out_specs=[pl.BlockSpec((rows, half), lambda i: (i, 0)),
                   pl.BlockSpec((rows, half), lambda i: (i, 0))],
        out_shape=[jax.ShapeDtypeStruct((n, half), BF16)] * 2,
        compiler_params=_params(("arbitrary",)),
        name="ctx_kv",
    )(ctx2d, sh, sc, w_kv)


def _rope(x, cos_t, sin_t):
    width = x.shape[1]
    reps = width // LANES
    c = jnp.concatenate([cos_t] * reps, axis=1)
    s = jnp.concatenate([sin_t] * reps, axis=1)
    half = HEAD_DIM // 2
    lane = lax.broadcasted_iota(jnp.int32, x.shape, 1)
    first_half = (lane & (HEAD_DIM - 1)) < half
    partner = jnp.where(first_half, pltpu.roll(x, width - half, 1), pltpu.roll(x, half, 1))
    return x * c + partner * s


def _in_proj_kernel(x_ref, sh_ref, sc_ref, w_ref, cos_ref, sin_ref, u_ref, q_ref, k_ref, v_ref, h_ref):
    n_u = u_ref.shape[1]
    n_q = q_ref.shape[1]
    n_k = k_ref.shape[1]
    for r0 in range(0, x_ref.shape[0], SUB_ROWS):
        rows = slice(r0, r0 + SUB_ROWS)
        h = (_standardize(x_ref[rows, :]) * (1.0 + sc_ref[0]) + sh_ref[0]).astype(BF16)
        h_ref[rows, :] = h
        u_ref[rows, :] = _dot(h, w_ref[:, :n_u]).astype(BF16)
        cos_t = cos_ref[rows, :]
        sin_t = sin_ref[rows, :]
        q = _dot(h, w_ref[:, n_u:n_u + n_q])
        q_ref[rows, :] = (_rope(q, cos_t, sin_t) * (HEAD_DIM ** -0.5)).astype(BF16)
        k = _dot(h, w_ref[:, n_u + n_q:n_u + n_q + n_k])
        k_ref[rows, :] = _rope(k, cos_t, sin_t).astype(BF16)
        v_ref[rows, :] = _dot(h, w_ref[:, n_u + n_q + n_k:]).astype(BF16)


def _in_proj(x2d, sh, sc, w_c, cos_t, sin_t, seq, tm):
    t, d = x2d.shape
    per_b = seq // tm
    n_u, n_q, n_k = 3 * D_HYENA, D_ATTN, 2 * D_KV
    row = lambda i: (i, 0)
    mod = lambda i: (i // per_b, 0, 0)
    pos = lambda i: (i % per_b, 0)
    return pl.pallas_call(
        _in_proj_kernel,
        grid=(t // tm,),
        in_specs=[pl.BlockSpec((tm, d), row),
                  pl.BlockSpec((1, 1, d), mod),
                  pl.BlockSpec((1, 1, d), mod),
                  _const_spec(w_c.shape),
                  pl.BlockSpec((tm, LANES), pos),
                  pl.BlockSpec((tm, LANES), pos)],
        out_specs=[pl.BlockSpec((tm, n_u), row),
                   pl.BlockSpec((tm, n_q), row),
                   pl.BlockSpec((tm, n_k), row),
                   pl.BlockSpec((tm, n_k), row),
                   pl.BlockSpec((tm, d), row)],
        out_shape=[jax.ShapeDtypeStruct((t, n_u), BF16),
                   jax.ShapeDtypeStruct((t, n_q), BF16),
                   jax.ShapeDtypeStruct((t, n_k), BF16),
                   jax.ShapeDtypeStruct((t, n_k), BF16),
                   jax.ShapeDtypeStruct((t, d), BF16)],
        compiler_params=_params(("arbitrary",)),
        name="in_proj",
    )(x2d, sh, sc, w_c, cos_t, sin_t)


def _filt_mlp_kernel(z_ref, w1_ref, b1_ref, w2_ref, b2_ref, w3_ref, b3_ref, w4_ref, fr_ref, t_ref, absd_ref,
                     h_ref, ss_ref):
    fr = fr_ref[...]
    a = jnp.sin(fr * (_dot3(z_ref[...], w1_ref[...]) + b1_ref[...]))
    a = jnp.sin(fr * (_dot3(a, w2_ref[...]) + b2_ref[...]))
    a = jnp.sin(fr * (_dot3(a, w3_ref[...]) + b3_ref[...]))
    half = a.shape[0]
    ss = jnp.zeros(ss_ref.shape, F32)
    for k in range(2):
        decay = jnp.exp(-t_ref[k * half:(k + 1) * half, :] * absd_ref[...])
        h = _dot3(a, w4_ref[0, k]) * decay
        ss = ss + jnp.sum(h * h, axis=0, keepdims=True)
        for s in range(half // FFT_N2):
            r0 = (k * (half // FFT_N2) + s) * TPITCH
            h_ref[r0:r0 + FFT_N2, :] = h[s * FFT_N2:(s + 1) * FFT_N2]
            h_ref[r0 + FFT_N2:r0 + TPITCH, :] = jnp.zeros((TPITCH - FFT_N2, h.shape[1]), F32)

    @pl.when(pl.program_id(0) == 0)
    def _():
        ss_ref[...] = jnp.zeros_like(ss_ref)

    ss_ref[...] += ss


def _filt_mlp(zp, w1p, b1, w2, b2, w3, b3, w4s, fr, t_col, absd, rows):
    n, c = t_col.shape[0], absd.shape[1]
    half_steps = (n // 2) // rows
    vec = lambda a: pl.BlockSpec(a.shape, lambda i: (0,) * a.ndim)
    return pl.pallas_call(
        _filt_mlp_kernel,
        grid=(n // rows,),
        in_specs=[pl.BlockSpec((rows // 2, zp.shape[1]), lambda i: (i, 0)),
                  vec(w1p), vec(b1), vec(w2), vec(b2), vec(w3), vec(b3),
                  pl.BlockSpec((1,) + w4s.shape[1:], lambda i: (i // half_steps, 0, 0, 0)),
                  vec(fr),
                  pl.BlockSpec((rows, 1), lambda i: (i, 0)),
                  vec(absd)],
        out_specs=[pl.BlockSpec((rows // FFT_N2 * TPITCH, c), lambda i: (i, 0)),
                   pl.BlockSpec((1, c), lambda i: (0, 0))],
        out_shape=[jax.ShapeDtypeStruct((n // FFT_N2 * TPITCH, c), F32), jax.ShapeDtypeStruct((1, c), F32)],
        compiler_params=_params(("arbitrary",)),
        name="filt_mlp",
    )(zp, w1p, b1, w2, b2, w3, b3, w4s, fr, t_col, absd)


def _filt_fft_kernel(h_ref, ss_ref, bias_ref, fa_ref, fb_ref, o_ref, zs_ref):
    scale = lax.rsqrt(ss_ref[...] + 1e-6)
    n1 = h_ref.shape[0] // TPITCH
    row = lax.broadcasted_iota(jnp.int32, (2 * FFT_N2, LANES), 0)
    impulse = jnp.where(row < FFT_N2, bias_ref[...], 0.0)

    def stage_a(n2, carry):
        sa = h_ref[pl.ds(n2, n1, stride=TPITCH), :].astype(BF16)
        sb = h_ref[pl.ds(n2 + 1, n1, stride=TPITCH), :].astype(BF16)
        zero = jnp.zeros_like(sa)
        rhs = jnp.concatenate([jnp.concatenate([sa, zero], axis=1), jnp.concatenate([zero, sb], axis=1)], axis=0)
        z = _dot(fa_ref[n2 // 2], rhs)
        for k in range(2):
            zk = z[:, k * LANES:(k + 1) * LANES]
            zs_ref[pl.ds(n2 + k, KP, stride=ZPITCH), :] = zk[:KP]
            zs_ref[pl.ds(FFT_N2 + n2 + k, KP, stride=ZPITCH), :] = zk[KP:]
        return carry

    lax.fori_loop(0, FFT_N2 // 2, lambda p, c: stage_a(2 * p, c), 0, unroll=4)

    def stage_b(p, carry):
        b0 = pl.multiple_of(2 * p * ZPITCH, SUBLANES)
        b1 = pl.multiple_of(b0 + ZPITCH, SUBLANES)
        z = jnp.concatenate([zs_ref[pl.ds(b0, 2 * FFT_N2), :], zs_ref[pl.ds(b1, 2 * FFT_N2), :]], axis=1)
        x = _dot(fb_ref[...], z.astype(BF16))
        o_ref[0, 2 * p] = x[:, :LANES] * scale + impulse
        o_ref[0, 2 * p + 1] = x[:, LANES:] * scale + impulse
        return carry

    lax.fori_loop(0, KHP // 2, stage_b, 0, unroll=True)


def _filt_fft(h2u, ss, bias, fa_full, fb):
    n, c = h2u.shape
    nblk = c // LANES
    return pl.pallas_call(
        _filt_fft_kernel,
        grid=(nblk,),
        in_specs=[pl.BlockSpec((n, LANES), lambda j: (0, j), pipeline_mode=pl.Buffered(1)),
                  pl.BlockSpec((1, LANES), lambda j: (0, j)),
                  pl.BlockSpec((1, LANES), lambda j: (0, j)),
                  _const_spec(fa_full.shape),
                  _const_spec(fb.shape)],
        out_specs=pl.BlockSpec((1, KHP, 2 * FFT_N2, LANES), lambda j: (j, 0, 0, 0)),
        out_shape=jax.ShapeDtypeStruct((nblk, KHP, 2 * FFT_N2, LANES), F32),
        scratch_shapes=[pltpu.VMEM((KP * ZPITCH, LANES), F32)],
        compiler_params=_params(("arbitrary",)),
        name="filt_fft",
    )(h2u, ss, bias, fa_full, fb)


def _conv_slab(u_ref, cw_ref, j, n_slabs):
    r0 = pl.multiple_of(j * FFT_N2, FFT_N2)
    cur = u_ref[0, pl.ds(r0, FFT_N2), :].astype(F32)
    grp = 2 * SUBLANES
    pr0 = pl.multiple_of(jnp.maximum(j * FFT_N2 - grp, 0), grp)
    nr0 = pl.multiple_of(jnp.minimum((j + 1) * FFT_N2, (n_slabs - 1) * FFT_N2), grp)
    prev_row = u_ref[0, pl.ds(pr0, grp), :].astype(F32)[grp - 1:grp]
    next_row = u_ref[0, pl.ds(nr0, grp), :].astype(F32)[0:1]
    prev_row = jnp.where(j > 0, prev_row, 0.0)
    next_row = jnp.where(j < n_slabs - 1, next_row, 0.0)
    row = lax.broadcasted_iota(jnp.int32, cur.shape, 0)
    before = jnp.where(row == 0, prev_row, pltpu.roll(cur, 1, 0))
    after = jnp.where(row == FFT_N2 - 1, next_row, pltpu.roll(cur, FFT_N2 - 1, 0))
    w = cw_ref[...]
    return before * w[0:1] + cur * w[1:2] + after * w[2:3] + w[3:4]


def _hyena_kernel(x0_ref, x1_ref, v_ref, cw0_ref, cw1_ref, cwv_ref, hf_ref,
                  fa_ref, fai_ref, fb_ref, fbi_ref, o_ref, ts_ref, zs_ref):
    n_slabs = x0_ref.shape[1] // FFT_N2

    def gated_value(j):
        return _conv_slab(x1_ref, cw1_ref, j, n_slabs) * _conv_slab(v_ref, cwv_ref, j, n_slabs)

    def fill(j, carry):
        ts_ref[pl.ds(pl.multiple_of(j * TPITCH, SUBLANES), FFT_N2), :] = gated_value(j)
        return carry

    lax.fori_loop(0, n_slabs, fill, 0, unroll=4)

    def stage_a(n2, carry):
        sa = ts_ref[pl.ds(n2, n_slabs, stride=TPITCH), :].astype(BF16)
        sb = ts_ref[pl.ds(n2 + 1, n_slabs, stride=TPITCH), :].astype(BF16)
        zero = jnp.zeros_like(sa)
        rhs = jnp.concatenate([jnp.concatenate([sa, zero], axis=1), jnp.concatenate([zero, sb], axis=1)], axis=0)
        z = _dot(fa_ref[n2 // 2], rhs)
        for k in range(2):
            zk = z[:, k * LANES:(k + 1) * LANES]
            zs_ref[pl.ds(n2 + k, KP, stride=ZPITCH), :] = zk[:KP]
            zs_ref[pl.ds(FFT_N2 + n2 + k, KP, stride=ZPITCH), :] = zk[KP:]
        return carry

    lax.fori_loop(0, FFT_N2 // 2, lambda p, c: stage_a(2 * p, c), 0, unroll=16)

    def stage_b(p, carry):
        b0 = pl.multiple_of(2 * p * ZPITCH, SUBLANES)
        b1 = pl.multiple_of(b0 + ZPITCH, SUBLANES)
        z = jnp.concatenate([zs_ref[pl.ds(b0, 2 * FFT_N2), :], zs_ref[pl.ds(b1, 2 * FFT_N2), :]], axis=1)
        x = _dot(fb_ref[...], z.astype(BF16))
        h = jnp.concatenate([hf_ref[0, 2 * p], hf_ref[0, 2 * p + 1]], axis=1)
        xr, xi = x[:FFT_N2], x[FFT_N2:]
        hr, hi = h[:FFT_N2], h[FFT_N2:]
        prod = jnp.concatenate([xr * hr - xi * hi, xr * hi + xi * hr], axis=0)
        y = _dot(fbi_ref[...], prod.astype(BF16))
        zs_ref[pl.ds(b0, 2 * FFT_N2), :] = y[:, :LANES]
        zs_ref[pl.ds(b1, 2 * FFT_N2), :] = y[:, LANES:]
        return carry

    lax.fori_loop(0, KHP // 2, stage_b, 0, unroll=True)

    def stage_ai(n2, carry):
        yr = zs_ref[pl.ds(n2, KP, stride=ZPITCH), :]
        yi = zs_ref[pl.ds(FFT_N2 + n2, KP, stride=ZPITCH), :]
        y = jnp.concatenate([yr, yi], axis=0).astype(BF16)
        ts_ref[pl.ds(n2, n_slabs, stride=TPITCH), :] = _dot(fai_ref[n2], y)
        return carry

    lax.fori_loop(0, FFT_N2, stage_ai, 0, unroll=32)

    def finish(j, carry):
        conv = ts_ref[pl.ds(pl.multiple_of(j * TPITCH, SUBLANES), FFT_N2), :]
        y = _conv_slab(x0_ref, cw0_ref, j, n_slabs) * conv
        o_ref[0, pl.ds(pl.multiple_of(j * FFT_N2, FFT_N2), FFT_N2), :] = y.astype(BF16)
        return carry

    lax.fori_loop(0, n_slabs, finish, 0, unroll=2)


def _hyena(u, cw, hf, fa, fai, fb, fbi):
    b, seq, c3 = u.shape
    nblk = D_HYENA // LANES
    n_slabs = seq // FFT_N2
    stream = lambda k: pl.BlockSpec((1, seq, LANES), lambda j, i, k=k: (i, 0, k * nblk + j))
    cwspec = lambda k: pl.BlockSpec((SUBLANES, LANES), lambda j, i, k=k: (0, k * nblk + j))
    return pl.pallas_call(
        _hyena_kernel,
        grid=(nblk, b),
        in_specs=[stream(0), stream(1), stream(2), cwspec(0), cwspec(1), cwspec(2),
                  pl.BlockSpec((1, KHP, 2 * FFT_N2, LANES), lambda j, i: (j, 0, 0, 0),
                               pipeline_mode=pl.Buffered(1)),
                  _const_spec(fa.shape), _const_spec(fai.shape),
                  _const_spec(fb.shape), _const_spec(fbi.shape)],
        out_specs=pl.BlockSpec((1, seq, LANES), lambda j, i: (i, 0, j)),
        out_shape=jax.ShapeDtypeStruct((b, seq, D_HYENA), BF16),
        scratch_shapes=[pltpu.VMEM((n_slabs * TPITCH, LANES), F32),
                        pltpu.VMEM((KP * ZPITCH, LANES), F32)],
        compiler_params=_params(("arbitrary", "arbitrary")),
        name="hyena",
    )(u, u, u, cw, cw, cw, hf, fa, fai, fb, fbi)


def _attn_kernel(sink_ref, q_ref, kp_ref, km_ref, kn_ref, vp_ref, vm_ref, vn_ref, kc_ref, vc_ref,
                 o_ref, ka_ref, va_ref):
    i = pl.program_id(1)
    n_i = pl.num_programs(1)
    tq = q_ref.shape[0]
    nqb = tq // BLOCK
    ka_ref[0:BLOCK] = kp_ref[...]
    ka_ref[BLOCK:BLOCK + tq] = km_ref[...]
    ka_ref[BLOCK + tq:] = kn_ref[...]
    va_ref[0:BLOCK] = vp_ref[...]
    va_ref[BLOCK:BLOCK + tq] = vm_ref[...]
    va_ref[BLOCK + tq:] = vn_ref[...]

    qi = lax.broadcasted_iota(jnp.int32, (BLOCK, BLOCK), 0)
    kj = lax.broadcasted_iota(jnp.int32, (BLOCK, BLOCK), 1)
    lane = lax.broadcasted_iota(jnp.int32, (BLOCK, LANES), 1)
    low = lane < HEAD_DIM
    half = GQA_GROUP // 2
    rows2 = half * BLOCK
    hrow = lax.broadcasted_iota(jnp.int32, (GQA_GROUP * BLOCK, 1), 0) // BLOCK
    head_order = [hh for hh in range(GQA_GROUP) if hh % 2 == 0] + [hh for hh in range(GQA_GROUP) if hh % 2 == 1]
    one = jnp.ones((), BF16)

    def with_ones(v):
        lanes_low = lax.broadcasted_iota(jnp.int32, v.shape, 1) < HEAD_DIM
        return jnp.where(lanes_low, v, one), jnp.where(lanes_low, one, v)

    for g in range(N_KV_HEADS):
        gl = slice(g * LANES, (g + 1) * LANES)
        vc_even, vc_odd = with_ones(vc_ref[:, gl])
        sink = jnp.zeros((GQA_GROUP * BLOCK, 1), F32)
        for pos, hh in enumerate(head_order):
            sink = jnp.where(hrow == pos, sink_ref[g * GQA_GROUP + hh], sink)
        for j in range(nqb):
            prev_ok = kj >= qi
            next_ok = kj <= qi
            if j == 0:
                prev_ok = prev_ok & (i > 0)
            if j == nqb - 1:
                next_ok = next_ok & (i < n_i - 1)
            bias_p = jnp.concatenate([jnp.where(prev_ok, 0.0, NEG_INF).astype(F32)] * GQA_GROUP, axis=0)
            bias_n = jnp.concatenate([jnp.where(next_ok, 0.0, NEG_INF).astype(F32)] * GQA_GROUP, axis=0)
            qb = q_ref[j * BLOCK:(j + 1) * BLOCK, :]
            parts = []
            for hh in head_order:
                h = g * GQA_GROUP + hh
                qp = qb[:, (h // 2) * LANES:(h // 2 + 1) * LANES]
                parts.append(jnp.where(low if h % 2 == 0 else ~low, qp, jnp.zeros_like(qp)))
            qs = jnp.concatenate(parts, axis=0)
            kw = ka_ref[j * BLOCK:(j + 3) * BLOCK, gl]
            vw_even, vw_odd = with_ones(va_ref[j * BLOCK:(j + 3) * BLOCK, gl])
            s_w = _dot_nt(qs, kw)
            s_p = s_w[:, :BLOCK] + bias_p
            s_m = s_w[:, BLOCK:2 * BLOCK]
            s_n = s_w[:, 2 * BLOCK:] + bias_n
            s_c = _dot_nt(qs, kc_ref[:, gl])
            m = jnp.maximum(jnp.maximum(jnp.max(jnp.maximum(jnp.maximum(s_p, s_m), s_n), axis=1, keepdims=True),
                                        jnp.max(s_c, axis=1, keepdims=True)), sink)
            e_w = jnp.concatenate([jnp.exp(s_p - m), jnp.exp(s_m - m), jnp.exp(s_n - m)], axis=1).astype(BF16)
            e_c = jnp.exp(s_c - m).astype(BF16)
            e_sink = jnp.exp(sink - m)
            outs = []
            for par, (vw, vcx) in enumerate(((vw_even, vc_even), (vw_odd, vc_odd))):
                rs = slice(par * rows2, (par + 1) * rows2)
                acc = _dot(e_w[rs], vw) + _dot(e_c[rs], vcx)
                den = pltpu.roll(acc, HEAD_DIM, 1) + e_sink[rs]
                outs.append(acc / den)
            for pp in range(half):
                pair = jnp.where(low, outs[0][pp * BLOCK:(pp + 1) * BLOCK], outs[1][pp * BLOCK:(pp + 1) * BLOCK])
                col = (g * half + pp) * LANES
                o_ref[j * BLOCK:(j + 1) * BLOCK, col:col + LANES] = pair.astype(BF16)


def _attention(sinks, q, kd, vd, kc, vc, batch, seq, tq):
    t = q.shape[0]
    per_b = seq // tq
    nqb = tq // BLOCK
    nb = seq // BLOCK
    n_ctx = kc.shape[0] // batch
    main = lambda b, i: (b * per_b + i, 0)
    prev = lambda b, i: (b * nb + jnp.maximum(i * nqb - 1, 0), 0)
    nxt = lambda b, i: (b * nb + jnp.minimum(i * nqb + nqb, nb - 1), 0)
    kvw = kd.shape[1]
    return pl.pallas_call(
        _attn_kernel,
        grid=(batch, per_b),
        in_specs=[pl.BlockSpec(memory_space=pltpu.SMEM),
                  pl.BlockSpec((tq, D_ATTN), main),
                  pl.BlockSpec((BLOCK, kvw), prev), pl.BlockSpec((tq, kvw), main), pl.BlockSpec((BLOCK, kvw), nxt),
                  pl.BlockSpec((BLOCK, kvw), prev), pl.BlockSpec((tq, kvw), main), pl.BlockSpec((BLOCK, kvw), nxt),
                  pl.BlockSpec((n_ctx, kvw), lambda b, i: (b, 0)),
                  pl.BlockSpec((n_ctx, kvw), lambda b, i: (b, 0))],
        out_specs=pl.BlockSpec((tq, D_ATTN), main),
        out_shape=jax.ShapeDtypeStruct((t, D_ATTN), BF16),
        scratch_shapes=[pltpu.VMEM((tq + 2 * BLOCK, kvw), BF16),
                        pltpu.VMEM((tq + 2 * BLOCK, kvw), BF16)],
        compiler_params=_params(("arbitrary", "arbitrary")),
        name="attention",
    )(sinks, q, kd, kd, kd, vd, vd, vd, kc, vc)


def _route(t, wr_hi_ref, wr_hl_ref, br_ref):
    th, tl = _split(t)
    both = _dot(th, wr_hl_ref[...])
    logits = both[:, :LANES] + both[:, LANES:] + _dot(tl, wr_hi_ref[...]) + br_ref[...]
    lane_i = lax.broadcasted_iota(jnp.int32, logits.shape, 1)
    lane = lane_i.astype(F32)
    grp_of_lane = (lane_i >> 2).astype(F32)
    ninf = -jnp.inf
    far = float(LANES)
    is_g = (lane_i >= N_EXPERTS) & (lane_i < N_EXPERTS + N_GROUPS)
    glog = jnp.where(is_g, logits, ninf)
    gmax = jnp.max(glog, axis=1, keepdims=True)
    gidx = jnp.min(jnp.where(glog == gmax, lane - float(N_EXPERTS), far), axis=1, keepdims=True)
    group_p = 1.0 / jnp.sum(jnp.exp(glog - gmax), axis=1, keepdims=True)
    in_grp = (lane_i < N_EXPERTS) & (grp_of_lane == gidx)
    elog = jnp.where(in_grp, logits, ninf)
    v1 = jnp.max(elog, axis=1, keepdims=True)
    i1 = jnp.min(jnp.where(elog == v1, lane, far), axis=1, keepdims=True)
    elog2 = jnp.where(lane == i1, ninf, elog)
    v2 = jnp.max(elog2, axis=1, keepdims=True)
    i2 = jnp.min(jnp.where(elog2 == v2, lane, far), axis=1, keepdims=True)
    e = jnp.exp(v2 - v1)
    w1 = group_p / (1.0 + e)
    w2 = group_p * e / (1.0 + e)
    gate = jnp.where(lane == i1, w1, 0.0) + jnp.where(lane == i2, w2, 0.0)
    return gate + jnp.where(lane_i == GROUP_LANE, gidx, 0.0)


def _mix_out_kernel(x_ref, h_ref, yh_ref, ya_ref, g1_ref, sh2_ref, sc2_ref,
                    wg_ref, wbh_ref, wba_ref, wo_ref, lng_ref, lnb_ref, wrh_ref, wrhl_ref, br_ref,
                    x1_ref, gate_ref, gsel_ref, *t_refs):
    d = x_ref.shape[1]
    for r0 in range(0, x_ref.shape[0], SUB_ROWS):
        rows = slice(r0, r0 + SUB_ROWS)
        x = x_ref[rows, :]
        h = h_ref[rows, :]
        g_hy = jax.nn.sigmoid(_dot(h, wg_ref[:, :d]))
        merged = g_hy * _dot(yh_ref[rows, :], wbh_ref[...])
        g_at = jax.nn.sigmoid(_dot(h, wg_ref[:, d:]))
        merged = merged + g_at * _dot(ya_ref[rows, :], wba_ref[...])
        mix = _dot(merged.astype(BF16), wo_ref[...])
        x1 = _standardize(DEEPNORM_ALPHA * x + g1_ref[0] * mix) * lng_ref[...] + lnb_ref[...]
        x1_ref[rows, :] = x1
        t = _standardize(x1) * (1.0 + sc2_ref[0]) + sh2_ref[0]
        gate = _route(t, wrh_ref, wrhl_ref, br_ref)
        gate_ref[rows, :] = gate
        grp = gate.T[GROUP_LANE:GROUP_LANE + 1, :]
        gsel_ref[r0 // LANES:(r0 + SUB_ROWS) // LANES, :] = jnp.concatenate(
            [grp[:, k * LANES:(k + 1) * LANES] for k in range(SUB_ROWS // LANES)], axis=0)
        for t_ref, piece in zip(t_refs, _pack_pieces(t)):
            t_ref[rows, :] = piece


def _mix_out(x2d, h, yh, ya, mods, w_g, w_bh, w_ba, w_o, ln_g, ln_b, wr_hi, wr_hl, br, seq, tm):
    t, d = x2d.shape
    per_b = seq // tm
    row = lambda i: (i, 0)
    mod = lambda i: (i // per_b, 0, 0)
    mspec = pl.BlockSpec((1, 1, d), mod)
    g1, sh2, sc2 = mods
    return pl.pallas_call(
        _mix_out_kernel,
        grid=(t // tm,),
        in_specs=[pl.BlockSpec((tm, d), row),
                  pl.BlockSpec((tm, d), row),
                  pl.BlockSpec((tm, yh.shape[1]), row),
                  pl.BlockSpec((tm, ya.shape[1]), row),
                  mspec, mspec, mspec,
                  _const_spec(w_g.shape), _const_spec(w_bh.shape), _const_spec(w_ba.shape),
                  _const_spec(w_o.shape), _const_spec(ln_g.shape), _const_spec(ln_b.shape),
                  _const_spec(wr_hi.shape), _const_spec(wr_hl.shape), _const_spec(br.shape)],
        out_specs=[pl.BlockSpec((tm, d), row), pl.BlockSpec((tm, LANES), row),
                   pl.BlockSpec((tm // LANES, LANES), row)]
        + [pl.BlockSpec((tm, SC_ROW_WORDS), row)] * N_PIECES,
        out_shape=[jax.ShapeDtypeStruct((t, d), F32), jax.ShapeDtypeStruct((t, LANES), F32),
                   jax.ShapeDtypeStruct((t // LANES, LANES), F32)]
        + [jax.ShapeDtypeStruct((t, SC_ROW_WORDS), jnp.uint32)] * N_PIECES,
        compiler_params=_params(("arbitrary",)),
        name="mix_out",
    )(x2d, h, yh, ya, g1, sh2, sc2, w_g, w_bh, w_ba, w_o, ln_g, ln_b, wr_hi, wr_hl, br)


def _slots_kernel(g_ref, upper_ref, lower_ref, dest_ref, meta_ref):
    gsel = g_ref[...]
    dest = jnp.zeros(gsel.shape, F32)
    base = jnp.zeros((1, 1), F32)
    chunk_start = lax.broadcasted_iota(jnp.int32, (1, LANES), 1).astype(F32) * float(MOE_CHUNK)
    owner = jnp.zeros((1, LANES), F32)
    for g in range(N_GROUPS):
        onehot = jnp.where(gsel == float(g), 1.0, 0.0)
        in_row = _dot(onehot.astype(BF16), upper_ref[...])
        row_tot = jnp.sum(onehot, axis=1, keepdims=True)
        rows_before = _dot(lower_ref[...], jnp.broadcast_to(row_tot, onehot.shape).astype(BF16))
        dest = dest + onehot * (base + rows_before + in_row)
        if g > 0:
            owner = owner + jnp.where(chunk_start >= base, 1.0, 0.0)
        n_g = jnp.sum(row_tot, axis=0, keepdims=True)
        base = base + jnp.floor((n_g + float(MOE_CHUNK - 1)) * (1.0 / MOE_CHUNK)) * float(MOE_CHUNK)
    dest_ref[...] = dest.astype(jnp.int32)
    row = lax.broadcasted_iota(jnp.int32, meta_ref.shape, 0)
    meta_ref[...] = jnp.where(row == 0, owner, base * (1.0 / MOE_CHUNK)).astype(jnp.int32)


def _slots(gsel):
    r = gsel.shape[0]
    upper = jnp.asarray(np.triu(np.ones((LANES, LANES), np.float32), 1)).astype(BF16)
    lower = jnp.asarray(np.tril(np.ones((r, r), np.float32), -1)).astype(BF16)
    full = lambda a: pl.BlockSpec(a.shape, lambda i: (0,) * a.ndim)
    return pl.pallas_call(
        _slots_kernel,
        grid=(1,),
        in_specs=[full(gsel), full(upper), full(lower)],
        out_specs=[pl.BlockSpec((r, LANES), lambda i: (0, 0)), pl.BlockSpec((SUBLANES, LANES), lambda i: (0, 0))],
        out_shape=[jax.ShapeDtypeStruct((r, LANES), jnp.int32), jax.ShapeDtypeStruct((SUBLANES, LANES), jnp.int32)],
        compiler_params=_params(("arbitrary",)),
        name="moe_slots",
    )(gsel, upper, lower)


def _experts_kernel(owner_ref, used_ref, *refs):
    n_p = N_PIECES
    t_refs, gate_ref = refs[:n_p], refs[n_p]
    wg_ref, wu_ref, wd_ref = refs[n_p + 1:n_p + 4]
    y_refs = refs[n_p + 4:2 * n_p + 4]
    wg16_ref, wu16_ref, wd16_ref = refs[2 * n_p + 4:]
    c = pl.program_id(0)
    f = D_EXPERT

    @pl.when((c == 0) | (owner_ref[c] != owner_ref[jnp.maximum(c - 1, 0)]))
    def _():
        wg16_ref[...] = wg_ref[...].astype(BF16)
        wu16_ref[...] = wu_ref[...].astype(BF16)
        for e in range(EXPERTS_PER_GROUP):
            wd16_ref[e * f:(e + 1) * f, :] = wd_ref[e].astype(BF16)

    @pl.when(c < used_ref[0])
    def _():
        t = _unpack_pieces([r[...] for r in t_refs]).astype(BF16)
        gate = gate_ref[...]
        lane = lax.broadcasted_iota(jnp.int32, gate.shape, 1)
        first = owner_ref[c] * EXPERTS_PER_GROUP
        parts = []
        for e in range(EXPERTS_PER_GROUP):
            a = _dot(t, wg16_ref[e])
            u = _dot(t, wu16_ref[e])
            ge = jnp.sum(jnp.where(lane == first + e, gate, 0.0), axis=1, keepdims=True)
            parts.append((a * jax.nn.sigmoid(a) * u * ge).astype(BF16))
        y = _dot(jnp.concatenate(parts, axis=1), wd16_ref[...])
        for y_ref, piece in zip(y_refs, _pack_pieces(y)):
            y_ref[...] = piece

    @pl.when(c >= used_ref[0])
    def _():
        for y_ref in y_refs:
            y_ref[...] = jnp.zeros_like(y_ref)


def _experts(owner, used, t_pieces, gate_sorted, w_gate, w_up, w_down):
    n_slots = gate_sorted.shape[0]
    d, f = w_gate.shape[1:]
    row = lambda c, owner, used: (c, 0)
    by_owner = lambda c, owner, used: (owner[c], 0, 0)
    piece = pl.BlockSpec((MOE_CHUNK, SC_ROW_WORDS), row)
    group_of_experts = lambda w: pl.BlockSpec((EXPERTS_PER_GROUP,) + w.shape[1:], by_owner)
    return pl.pallas_call(
        _experts_kernel,
        grid_spec=pltpu.PrefetchScalarGridSpec(
            num_scalar_prefetch=2,
            grid=(n_slots // MOE_CHUNK,),
            in_specs=[piece] * len(t_pieces) + [pl.BlockSpec((MOE_CHUNK, LANES), row),
                                                group_of_experts(w_gate), group_of_experts(w_up),
                                                group_of_experts(w_down)],
            out_specs=[piece] * len(t_pieces),
            scratch_shapes=[pltpu.VMEM((EXPERTS_PER_GROUP, d, f), BF16),
                            pltpu.VMEM((EXPERTS_PER_GROUP, d, f), BF16),
                            pltpu.VMEM((EXPERTS_PER_GROUP * f, d), BF16)],
        ),
        out_shape=[jax.ShapeDtypeStruct((n_slots, SC_ROW_WORDS), jnp.uint32)] * len(t_pieces),
        compiler_params=_params(("arbitrary",)),
        name="moe_experts",
    )(owner, used, *t_pieces, gate_sorted, w_gate, w_up, w_down)


def _final_kernel(x1_ref, g2_ref, lng_ref, lnb_ref, *refs):
    y_refs, o_ref = refs[:-1], refs[-1]
    y = _unpack_pieces([r[...] for r in y_refs])
    o_ref[...] = _standardize(DEEPNORM_ALPHA * x1_ref[...] + g2_ref[0] * y) * lng_ref[...] + lnb_ref[...]


def _final(x1, g2, ln_g, ln_b, y_pieces, seq, tm):
    t, d = x1.shape
    per_b = seq // tm
    row = lambda i: (i, 0)
    return pl.pallas_call(
        _final_kernel,
        grid=(t // tm,),
        in_specs=[pl.BlockSpec((tm, d), row), pl.BlockSpec((1, 1, d), lambda i: (i // per_b, 0, 0)),
                  _const_spec(ln_g.shape), _const_spec(ln_b.shape)]
        + [pl.BlockSpec((tm, SC_ROW_WORDS), row)] * len(y_pieces),
        out_specs=pl.BlockSpec((tm, d), row),
        out_shape=jax.ShapeDtypeStruct((t, d), F32),
        compiler_params=_params(("arbitrary",)),
        name="moe_final",
    )(x1, g2, ln_g, ln_b, *y_pieces)


def _sc_move_rows(tables, idx, n_out, scatter, name):
    n = idx.shape[0]
    mesh = plsc.VectorSubcoreMesh(core_axis_name="c", subcore_axis_name="s")
    out_type = [jax.ShapeDtypeStruct((n_out, t.shape[1]), t.dtype) for t in tables]
    window = lambda i: (i, 0)
    index_win = pl.BlockSpec((1, SC_WINDOW), lambda i: (0, i))
    split = dict(core_axis_name=("c", "s"), dimension_semantics=(pltpu.PARALLEL,))

    @functools.partial(pl.kernel, out_type=out_type, mesh=mesh, scratch_types=[], name=name)
    def move(*refs):
        srcs, i_hbm, dsts = refs[:len(tables)], refs[len(tables)], refs[len(tables) + 1:]
        for src, dst in zip(srcs, dsts):
            rows_win = pl.BlockSpec((SC_WINDOW, src.shape[1]), window)
            if scatter:
                def body(x_vmem, i_vmem, dst=dst):
                    pltpu.sync_copy(x_vmem, dst.at[i_vmem.at[0]])

                pltpu.emit_pipeline(body, grid=(n // SC_WINDOW,), in_specs=[rows_win, index_win],
                                    out_specs=[], **split)(src, i_hbm)
            else:
                def body(i_vmem, o_vmem, src=src):
                    pltpu.sync_copy(src.at[i_vmem.at[0]], o_vmem)

                pltpu.emit_pipeline(body, grid=(n // SC_WINDOW,), in_specs=[index_win],
                                    out_specs=[rows_win], **split)(i_hbm, dst)

    return move(*tables, idx.reshape(1, n))


def _dft_tables(seq):
    n = 2 * seq
    n1_full = n // FFT_N2
    n1_data = seq // FFT_N2
    k1 = np.arange(KH, dtype=np.float64)[None, :, None]
    n2 = np.arange(FFT_N2, dtype=np.float64)[:, None, None]
    n1 = np.arange(n1_full, dtype=np.float64)[None, None, :]
    ang = 2.0 * np.pi * k1 * (FFT_N2 * n1 + n2) / n
    fa = np.zeros((FFT_N2, 2 * KP, n1_full))
    fa[:, :KH] = np.cos(ang)
    fa[:, KP:KP + KH] = -np.sin(ang)
    wgt = np.full((KH,), 2.0)
    wgt[0] = 1.0
    wgt[KH - 1] = 1.0
    fai = np.zeros((FFT_N2, n1_data, 2 * KP))
    angt = np.transpose(ang[:, :, :n1_data], (0, 2, 1))
    fai[:, :, :KH] = np.cos(angt) * wgt / n
    fai[:, :, KP:KP + KH] = -np.sin(angt) * wgt / n
    kk = np.arange(FFT_N2, dtype=np.float64)
    a2 = 2.0 * np.pi * np.outer(kk, kk) / FFT_N2
    fr, fi = np.cos(a2), -np.sin(a2)
    fb = np.block([[fr, -fi], [fi, fr]])
    fbi = np.block([[fr, fi], [-fi, fr]])
    f32 = lambda a: jnp.asarray(a.astype(np.float32))
    pairs = lambda a: np.concatenate([a[0::2], a[1::2]], axis=2)
    return f32(pairs(fa[:, :, :n1_data])), f32(pairs(fa)), f32(fai), f32(fb), f32(fbi)


def _filter_features(seq, rows):
    f32 = np.float32
    t = np.linspace(0.0, 1.0, seq, dtype=f32)
    w = (f32(2.0 * math.pi) * np.arange(seq, dtype=f32) / f32(seq)).astype(f32)
    t2 = np.concatenate([t, t[::-1]])
    w2 = np.concatenate([w, w[::-1]])
    bands = np.linspace(1e-4, FILTER_BANDS - 1, FILTER_BANDS, dtype=f32)
    max_decay = math.log(DECAY_TARGET) / FAST_DECAY_PCT
    min_decay = math.log(DECAY_TARGET) / SLOW_DECAY_PCT
    deltas = jnp.linspace(min_decay, max_decay, D_HYENA, dtype=F32)

    def feats(k):
        tp = t2.reshape(-1, 2, rows // 2)[:, k].reshape(-1, 1)
        wp = w2.reshape(-1, 2, rows // 2)[:, k].reshape(-1, 1)
        pad = np.zeros((tp.shape[0], FILTER_ORDER - FILTER_EMB), f32)
        arg = (bands * wp).astype(f32)
        return np.concatenate([tp, np.cos(arg), -np.sin(arg), pad], axis=-1).astype(f32)

    zp = jnp.asarray(np.concatenate([feats(0), feats(1)], axis=-1))
    return zp, jnp.asarray(t2)[:, None], jnp.abs(deltas)[None, :]


def _rope_tables(seq):
    f32 = np.float32
    rows = seq // GRID_W
    row = np.repeat(np.arange(rows, dtype=f32), GRID_W)
    col = np.tile(np.arange(GRID_W, dtype=f32), rows)
    half = HEAD_DIM // 2
    inv_freq = (f32(ROPE_BASE) ** (-np.arange(0, half, 2, dtype=f32) / f32(half))).astype(f32)
    ang = np.concatenate([row[:, None] * inv_freq, col[:, None] * inv_freq], axis=-1).astype(f32)
    cos, sin = np.cos(ang), np.sin(ang)
    c64 = np.concatenate([cos, cos], axis=-1)
    s64 = np.concatenate([-sin, sin], axis=-1)
    return (jnp.asarray(np.concatenate([c64, c64], axis=-1).astype(f32)),
            jnp.asarray(np.concatenate([s64, s64], axis=-1).astype(f32)))


def _head_perm(n_heads):
    idx = []
    for h in range(n_heads):
        base = h * HEAD_DIM
        idx += [base + 2 * j for j in range(HEAD_DIM // 2)]
        idx += [base + 2 * j + 1 for j in range(HEAD_DIM // 2)]
    return np.asarray(idx, dtype=np.int32)


def _dup_heads(w):
    parts = []
    for g in range(N_KV_HEADS):
        blk = w[:, g * HEAD_DIM:(g + 1) * HEAD_DIM]
        parts += [blk, blk]
    return jnp.concatenate(parts, axis=1)


def kernel(x, c, ctx, c_ctx, ada_w, ada_b, w_in, hy_conv_w, hy_conv_b, hy_w1, hy_b1, hy_w2, hy_b2, hy_w3, hy_b3, hy_w4, hy_freq, hy_bias, attn_sinks, w_branch_hy, w_branch_attn, w_out, ln1_g, ln1_b, w_group, b_group, w_router, b_router, w_gate_e, w_up_e, w_down_e, ln2_g, ln2_b):
    batch, seq, d = x.shape
    n_ctx = ctx.shape[1]
    assert d == D_MODEL and ada_w.shape[0] == DEPTH == 1
    assert 2 * seq == FFT_N2 * FFT_N2 and seq % TM_PROJ == 0
    l = 0

    w = w_in[l]
    s0, s1, s2, s3 = 3 * D_HYENA, 3 * D_HYENA + D_ATTN, 3 * D_HYENA + D_ATTN + D_KV, 3 * D_HYENA + D_ATTN + 2 * D_KV
    w_q = w[:, s0:s1][:, _head_perm(N_HEADS)]
    w_k = _dup_heads(w[:, s1:s2][:, _head_perm(N_KV_HEADS)])
    w_v = _dup_heads(w[:, s2:s3])
    w_c = jnp.concatenate([w[:, :s0], w_q, w_k, w_v], axis=1).astype(BF16)
    w_kv = jnp.concatenate([w_k, w_v], axis=1).astype(BF16)
    w_g = w[:, s3:].astype(BF16)
    w_bh = w_branch_hy[l].astype(BF16)
    w_ba = w_branch_attn[l].astype(BF16)
    w_o = w_out[l].astype(BF16)
    pad_lanes = LANES - N_EXPERTS - N_GROUPS
    wr = jnp.concatenate([w_router[l], w_group[l], jnp.zeros((d, pad_lanes), F32)], axis=1)
    wr_hi = wr.astype(BF16)
    wr_hl = jnp.concatenate([wr_hi, (wr - wr_hi.astype(F32)).astype(BF16)], axis=1)
    br = jnp.concatenate([b_router[l], b_group[l], jnp.zeros((pad_lanes,), F32)])[None, :]
    cw = jnp.concatenate([hy_conv_w[l][:, 0, :], hy_conv_b[l][None, :],
                          jnp.zeros((SUBLANES - SHORT_CONV - 1, 3 * D_HYENA), F32)], axis=0)
    row2 = lambda a: a.reshape(1, -1)
    pair = lambda a: jnp.concatenate([a, a], axis=-1)
    zero_o = jnp.zeros((FILTER_ORDER, FILTER_ORDER), F32)
    bdiag = lambda a: jnp.concatenate([jnp.concatenate([a, zero_o], axis=1),
                                       jnp.concatenate([zero_o, a], axis=1)], axis=0)
    w1p = bdiag(jnp.concatenate([hy_w1[l], jnp.zeros((FILTER_ORDER - FILTER_EMB, FILTER_ORDER), F32)], axis=0))
    w4h = jnp.transpose(hy_w4[l].reshape(FILTER_ORDER, 2, D_HYENA), (1, 0, 2))
    zero_w4 = jnp.zeros_like(w4h)
    w4s = jnp.stack([jnp.concatenate([w4h, zero_w4], axis=1),
                     jnp.concatenate([zero_w4, w4h], axis=1)], axis=1)

    fa, fa_full, fai, fb, fbi = _dft_tables(seq)
    fa, fa_full, fai, fb, fbi = (a.astype(BF16) for a in (fa, fa_full, fai, fb, fbi))
    zp, t_col, absd = _filter_features(seq, FILT_ROWS)
    cos_t, sin_t = _rope_tables(seq)

    cond = jnp.concatenate([c, c_ctx[None], jnp.zeros((SUBLANES - batch - 1, d), F32)], axis=0)
    mods = _adaln(cond, ada_w[l], ada_b[l])
    m6 = [mods[:, k * d:(k + 1) * d].reshape(SUBLANES, 1, d) for k in range(6)]
    sh1, sc1, g1, sh2, sc2, g2 = m6

    kc, vc = _ctx_kv(ctx.reshape(batch * n_ctx, d), sh1[batch], sc1[batch], w_kv, n_ctx)

    x2d = x.reshape(batch * seq, d)
    u, q, kd, vd, h_mod = _in_proj(x2d, sh1, sc1, w_c, cos_t, sin_t, seq, TM_PROJ)
    h2u, ss = _filt_mlp(zp, w1p, pair(row2(hy_b1[l])), bdiag(hy_w2[l]), pair(row2(hy_b2[l])),
                        bdiag(hy_w3[l]), pair(row2(hy_b3[l])), w4s, pair(row2(hy_freq[l])), t_col, absd, FILT_ROWS)
    hf = _filt_fft(h2u, ss, row2(hy_bias[l]), fa_full, fb)
    y_hy = _hyena(u.reshape(batch, seq, 3 * D_HYENA), cw, hf, fa, fai, fb, fbi)
    y_at = _attention(attn_sinks[l], q, kd, vd, kc, vc, batch, seq, TQ_ATTN)
    x1, gate, gsel, *t_pieces = _mix_out(x2d, h_mod, y_hy.reshape(batch * seq, D_HYENA), y_at, (g1, sh2, sc2),
                                         w_g, w_bh, w_ba, w_o, row2(ln1_g[l]), row2(ln1_b[l]), wr_hi, wr_hl, br,
                                         seq, TM_PROJ)

    n_tok = batch * seq
    n_slots = n_tok + N_GROUPS * MOE_CHUNK
    dest2d, meta = _slots(gsel)
    dest = dest2d.reshape(n_tok)
    *t_sorted, gate_sorted = _sc_move_rows(t_pieces + [gate], dest, n_slots, True, "moe_sort")
    y_sorted = _experts(meta[0], meta[1, :1], t_sorted, gate_sorted, w_gate_e[l], w_up_e[l], w_down_e[l])
    y_pieces = _sc_move_rows(y_sorted, dest, n_tok, False, "moe_unsort")
    out = _final(x1, g2, row2(ln2_g[l]), row2(ln2_b[l]), y_pieces, seq, TM_FINAL)
    return out.reshape(batch, seq, d)
```

```python
import functools
import math

import numpy as np
import jax
import jax.numpy as jnp
from jax import lax
from jax.experimental import pallas as pl
from jax.experimental.pallas import tpu as pltpu
from jax.experimental.pallas import tpu_sc as plsc

F32 = jnp.float32
BF16 = jnp.bfloat16

D_MODEL = 1024
GRID_W = 64
D_HYENA = D_MODEL // 2
SHORT_CONV = 3
FILTER_BANDS = 16
FILTER_EMB = 1 + 2 * FILTER_BANDS
FILTER_ORDER = 64
DECAY_TARGET = 1e-2
FAST_DECAY_PCT = 0.3
SLOW_DECAY_PCT = 1.5
HEAD_DIM = 64
D_ATTN = D_MODEL // 2
N_HEADS = D_ATTN // HEAD_DIM
N_KV_HEADS = N_HEADS // 4
GQA_GROUP = N_HEADS // N_KV_HEADS
D_KV = N_KV_HEADS * HEAD_DIM
WINDOW = 128
BLOCK = 128
ROPE_BASE = 10000.0
NEG_INF = -1e30
N_GROUPS = 4
EXPERTS_PER_GROUP = 4
N_EXPERTS = N_GROUPS * EXPERTS_PER_GROUP
D_EXPERT = D_MODEL // 4
LN_EPS = 1e-5
DEPTH = 1
DEEPNORM_ALPHA = (2.0 * DEPTH) ** 0.25

LANES = 128
SUBLANES = 8
VMEM_LIMIT = 56 * 1024 * 1024

SUB_ROWS = 512
TM_PROJ = 1024
MOE_CHUNK = 512
TM_FINAL = 1024
GROUP_LANE = N_EXPERTS
SC_WINDOW = 128
SC_ROW_WORDS = 256
N_PIECES = D_MODEL // (2 * SC_ROW_WORDS)
TQ_ATTN = 1024
FILT_ROWS = 2048

FFT_N2 = 128
KH = 65
KHP = 66
KP = 72
ZPITCH = 2 * FFT_N2 + SUBLANES
TPITCH = FFT_N2 + SUBLANES


def _dot(a, b):
    return jnp.dot(a, b, preferred_element_type=F32)


def _dot_nt(a, b):
    return lax.dot_general(a, b, (((1,), (1,)), ((), ())), preferred_element_type=F32)


def _split(a):
    hi = a.astype(BF16)
    lo = (a - hi.astype(F32)).astype(BF16)
    return hi, lo


def _pack_pieces(x):
    w = SC_ROW_WORDS
    pieces = []
    for p in range(x.shape[1] // (2 * w)):
        hi = lax.bitcast_convert_type(x[:, 2 * p * w:(2 * p + 1) * w].astype(BF16).astype(F32), jnp.uint32)
        lo = lax.bitcast_convert_type(x[:, (2 * p + 1) * w:(2 * p + 2) * w].astype(BF16).astype(F32), jnp.uint32)
        pieces.append(hi | (lo >> 16))
    return pieces


def _unpack_pieces(pieces):
    cols = []
    for word in pieces:
        cols.append(lax.bitcast_convert_type(word & jnp.uint32(0xFFFF0000), F32))
        cols.append(lax.bitcast_convert_type(word << 16, F32))
    return jnp.concatenate(cols, axis=1)


def _dot3(a, b):
    ah, al = _split(a)
    bh, bl = _split(b)
    return _dot(ah, bh) + _dot(al, bh) + _dot(ah, bl)


def _standardize(x):
    mu = jnp.mean(x, axis=-1, keepdims=True)
    xc = x - mu
    var = jnp.mean(xc * xc, axis=-1, keepdims=True)
    return xc * lax.rsqrt(var + LN_EPS)


def _params(sem, vmem=VMEM_LIMIT):
    return pltpu.CompilerParams(dimension_semantics=sem, vmem_limit_bytes=vmem)


def _const_spec(shape):
    nd = len(shape)
    return pl.BlockSpec(shape, lambda *_: (0,) * nd, pipeline_mode=pl.Buffered(1))


def _adaln_kernel(c_ref, w_ref, b_ref, o_ref):
    s = c_ref[...]
    s = s * jax.nn.sigmoid(s)
    o_ref[...] = _dot3(s, w_ref[...]) + b_ref[...]


def _adaln(cond, w, b):
    n, d = cond.shape
    cols = w.shape[1]
    bc = 1024
    return pl.pallas_call(
        _adaln_kernel,
        grid=(cols // bc,),
        in_specs=[pl.BlockSpec((n, d), lambda j: (0, 0)),
                  pl.BlockSpec((d, bc), lambda j: (0, j)),
                  pl.BlockSpec((1, bc), lambda j: (0, j))],
        out_specs=pl.BlockSpec((n, bc), lambda j: (0, j)),
        out_shape=jax.ShapeDtypeStruct((n, cols), F32),
        compiler_params=_params(("arbitrary",)),
        name="adaln",
    )(cond, w, b.reshape(1, cols))


def _ctx_kv_kernel(x_ref, sh_ref, sc_ref, w_ref, k_ref, v_ref):
    h = _standardize(x_ref[...]) * (1.0 + sc_ref[...]) + sh_ref[...]
    kv = _dot(h.astype(BF16), w_ref[...])
    half = k_ref.shape[1]
    k_ref[...] = kv[:, :half].astype(BF16)
    v_ref[...] = kv[:, half:].astype(BF16)


def _ctx_kv(ctx2d, sh, sc, w_kv, rows):
    n, d = ctx2d.shape
    half = w_kv.shape[1] // 2
    return pl.pallas_call(
        _ctx_kv_kernel,
        grid=(n // rows,),
        in_specs=[pl.BlockSpec((rows, d), lambda i: (i, 0)),
                  pl.BlockSpec((1, d), lambda i: (0, 0)),
                  pl.BlockSpec((1, d), lambda i: (0, 0)),
                  pl.BlockSpec(w_kv.shape, lambda i: (0, 0))],
        out_specs=[pl.BlockSpec((rows, half), lambda i: (i, 0)),
                   pl.BlockSpec((rows, half), lambda i: (i, 0))],
        out_shape=[jax.ShapeDtypeStruct((n, half), BF16)] * 2,
        compiler_params=_params(("arbitrary",)),
        name="ctx_kv",
    )(ctx2d, sh, sc, w_kv)


def _rope(x, cos_t, sin_t):
    width = x.shape[1]
    reps = width // LANES
    c = jnp.concatenate([cos_t] * reps, axis=1)
    s = jnp.concatenate([sin_t] * reps, axis=1)
    half = HEAD_DIM // 2
    lane = lax.broadcasted_iota(jnp.int32, x.shape, 1)
    first_half = (lane & (HEAD_DIM - 1)) < half
    partner = jnp.where(first_half, pltpu.roll(x, width - half, 1), pltpu.roll(x, half, 1))
    return x * c + partner * s


def _in_proj_kernel(x_ref, sh_ref, sc_ref, w_ref, cos_ref, sin_ref, u_ref, q_ref, k_ref, v_ref, h_ref):
    n_u = u_ref.shape[1]
    n_q = q_ref.shape[1]
    n_k = k_ref.shape[1]
    for r0 in range(0, x_ref.shape[0], SUB_ROWS):
        rows = slice(r0, r0 + SUB_ROWS)
        h = (_standardize(x_ref[rows, :]) * (1.0 + sc_ref[0]) + sh_ref[0]).astype(BF16)
        h_ref[rows, :] = h
        u_ref[rows, :] = _dot(h, w_ref[:, :n_u]).astype(BF16)
        cos_t = cos_ref[rows, :]
        sin_t = sin_ref[rows, :]
        q = _dot(h, w_ref[:, n_u:n_u + n_q])
        q_ref[rows, :] = (_rope(q, cos_t, sin_t) * (HEAD_DIM ** -0.5)).astype(BF16)
        k = _dot(h, w_ref[:, n_u + n_q:n_u + n_q + n_k])
        k_ref[rows, :] = _rope(k, cos_t, sin_t).astype(BF16)
        v_ref[rows, :] = _dot(h, w_ref[:, n_u + n_q + n_k:]).astype(BF16)


def _in_proj(x2d, sh, sc, w_c, cos_t, sin_t, seq, tm):
    t, d = x2d.shape
    per_b = seq // tm
    n_u, n_q, n_k = 3 * D_HYENA, D_ATTN, 2 * D_KV
    row = lambda i: (i, 0)
    mod = lambda i: (i // per_b, 0, 0)
    pos = lambda i: (i % per_b, 0)
    return pl.pallas_call(
        _in_proj_kernel,
        grid=(t // tm,),
        in_specs=[pl.BlockSpec((tm, d), row),
                  pl.BlockSpec((1, 1, d), mod),
                  pl.BlockSpec((1, 1, d), mod),
                  _const_spec(w_c.shape),
                  pl.BlockSpec((tm, LANES), pos),
                  pl.BlockSpec((tm, LANES), pos)],
        out_specs=[pl.BlockSpec((tm, n_u), row),
                   pl.BlockSpec((tm, n_q), row),
                   pl.BlockSpec((tm, n_k), row),
                   pl.BlockSpec((tm, n_k), row),
                   pl.BlockSpec((tm, d), row)],
        out_shape=[jax.ShapeDtypeStruct((t, n_u), BF16),
                   jax.ShapeDtypeStruct((t, n_q), BF16),
                   jax.ShapeDtypeStruct((t, n_k), BF16),
                   jax.ShapeDtypeStruct((t, n_k), BF16),
                   jax.ShapeDtypeStruct((t, d), BF16)],
        compiler_params=_params(("arbitrary",)),
        name="in_proj",
    )(x2d, sh, sc, w_c, cos_t, sin_t)


def _filt_mlp_kernel(z_ref, w1_ref, b1_ref, w2_ref, b2_ref, w3_ref, b3_ref, w4_ref, fr_ref, t_ref, absd_ref,
                     h_ref, ss_ref):
    fr = fr_ref[...]
    a = jnp.sin(fr * (_dot3(z_ref[...], w1_ref[...]) + b1_ref[...]))
    a = jnp.sin(fr * (_dot3(a, w2_ref[...]) + b2_ref[...]))
    a = jnp.sin(fr * (_dot3(a, w3_ref[...]) + b3_ref[...]))
    half = a.shape[0]
    ss = jnp.zeros(ss_ref.shape, F32)
    for k in range(2):
        decay = jnp.exp(-t_ref[k * half:(k + 1) * half, :] * absd_ref[...])
        h = _dot3(a, w4_ref[0, k]) * decay
        ss = ss + jnp.sum(h * h, axis=0, keepdims=True)
        for s in range(half // FFT_N2):
            r0 = (k * (half // FFT_N2) + s) * TPITCH
            h_ref[r0:r0 + FFT_N2, :] = h[s * FFT_N2:(s + 1) * FFT_N2]
            h_ref[r0 + FFT_N2:r0 + TPITCH, :] = jnp.zeros((TPITCH - FFT_N2, h.shape[1]), F32)

    @pl.when(pl.program_id(0) == 0)
    def _():
        ss_ref[...] = jnp.zeros_like(ss_ref)

    ss_ref[...] += ss


def _filt_mlp(zp, w1p, b1, w2, b2, w3, b3, w4s, fr, t_col, absd, rows):
    n, c = t_col.shape[0], absd.shape[1]
    half_steps = (n // 2) // rows
    vec = lambda a: pl.BlockSpec(a.shape, lambda i: (0,) * a.ndim)
    return pl.pallas_call(
        _filt_mlp_kernel,
        grid=(n // rows,),
        in_specs=[pl.BlockSpec((rows // 2, zp.shape[1]), lambda i: (i, 0)),
                  vec(w1p), vec(b1), vec(w2), vec(b2), vec(w3), vec(b3),
                  pl.BlockSpec((1,) + w4s.shape[1:], lambda i: (i // half_steps, 0, 0, 0)),
                  vec(fr),
                  pl.BlockSpec((rows, 1), lambda i: (i, 0)),
                  vec(absd)],
        out_specs=[pl.BlockSpec((rows // FFT_N2 * TPITCH, c), lambda i: (i, 0)),
                   pl.BlockSpec((1, c), lambda i: (0, 0))],
        out_shape=[jax.ShapeDtypeStruct((n // FFT_N2 * TPITCH, c), F32), jax.ShapeDtypeStruct((1, c), F32)],
        compiler_params=_params(("arbitrary",)),
        name="filt_mlp",
    )(zp, w1p, b1, w2, b2, w3, b3, w4s, fr, t_col, absd)


def _filt_fft_kernel(h_ref, ss_ref, bias_ref, fa_ref, fb_ref, o_ref, zs_ref):
    scale = lax.rsqrt(ss_ref[...] + 1e-6)
    n1 = h_ref.shape[0] // TPITCH
    row = lax.broadcasted_iota(jnp.int32, (2 * FFT_N2, LANES), 0)
    impulse = jnp.where(row < FFT_N2, bias_ref[...], 0.0)

    def stage_a(n2, carry):
        sa = h_ref[pl.ds(n2, n1, stride=TPITCH), :].astype(BF16)
        sb = h_ref[pl.ds(n2 + 1, n1, stride=TPITCH), :].astype(BF16)
        zero = jnp.zeros_like(sa)
        rhs = jnp.concatenate([jnp.concatenate([sa, zero], axis=1), jnp.concatenate([zero, sb], axis=1)], axis=0)
        z = _dot(fa_ref[n2 // 2], rhs)
        for k in range(2):
            zk = z[:, k * LANES:(k + 1) * LANES]
            zs_ref[pl.ds(n2 + k, KP, stride=ZPITCH), :] = zk[:KP]
            zs_ref[pl.ds(FFT_N2 + n2 + k, KP, stride=ZPITCH), :] = zk[KP:]
        return carry

    lax.fori_loop(0, FFT_N2 // 2, lambda p, c: stage_a(2 * p, c), 0, unroll=16)

    def stage_b(p, carry):
        b0 = pl.multiple_of(2 * p * ZPITCH, SUBLANES)
        b1 = pl.multiple_of(b0 + ZPITCH, SUBLANES)
        z = jnp.concatenate([zs_ref[pl.ds(b0, 2 * FFT_N2), :], zs_ref[pl.ds(b1, 2 * FFT_N2), :]], axis=1)
        x = _dot(fb_ref[...], z.astype(BF16))
        o_ref[0, 2 * p] = x[:, :LANES] * scale + impulse
        o_ref[0, 2 * p + 1] = x[:, LANES:] * scale + impulse
        return carry

    lax.fori_loop(0, KHP // 2, stage_b, 0, unroll=True)


def _filt_fft(h2u, ss, bias, fa_full, fb):
    n, c = h2u.shape
    nblk = c // LANES
    return pl.pallas_call(
        _filt_fft_kernel,
        grid=(nblk,),
        in_specs=[pl.BlockSpec((n, LANES), lambda j: (0, j), pipeline_mode=pl.Buffered(1)),
                  pl.BlockSpec((1, LANES), lambda j: (0, j)),
                  pl.BlockSpec((1, LANES), lambda j: (0, j)),
                  _const_spec(fa_full.shape),
                  _const_spec(fb.shape)],
        out_specs=pl.BlockSpec((1, KHP, 2 * FFT_N2, LANES), lambda j: (j, 0, 0, 0)),
        out_shape=jax.ShapeDtypeStruct((nblk, KHP, 2 * FFT_N2, LANES), F32),
        scratch_shapes=[pltpu.VMEM((KP * ZPITCH, LANES), F32)],
        compiler_params=_params(("arbitrary",)),
        name="filt_fft",
    )(h2u, ss, bias, fa_full, fb)


def _conv_slab(u_ref, cw_ref, cs_ref, j, n_slabs):
    r0 = pl.multiple_of(j * FFT_N2, FFT_N2)
    cur = u_ref[0, pl.ds(r0, FFT_N2), :].astype(F32)
    grp = 2 * SUBLANES
    pr0 = pl.multiple_of(jnp.maximum(j * FFT_N2 - grp, 0), grp)
    nr0 = pl.multiple_of(jnp.minimum((j + 1) * FFT_N2, (n_slabs - 1) * FFT_N2), grp)
    prev_row = u_ref[0, pl.ds(pr0, grp), :].astype(F32)[grp - 1:grp]
    next_row = u_ref[0, pl.ds(nr0, grp), :].astype(F32)[0:1]
    m = SUBLANES
    cs_ref[m - 1:m, :] = jnp.where(j > 0, prev_row, 0.0)
    cs_ref[m:m + FFT_N2, :] = cur
    cs_ref[m + FFT_N2:m + FFT_N2 + 1, :] = jnp.where(j < n_slabs - 1, next_row, 0.0)
    before = cs_ref[m - 1:m - 1 + FFT_N2, :]
    after = cs_ref[m + 1:m + 1 + FFT_N2, :]
    w = cw_ref[...]
    return before * w[0:1] + cur * w[1:2] + after * w[2:3] + w[3:4]


def _hyena_kernel(x0_ref, x1_ref, v_ref, cw0_ref, cw1_ref, cwv_ref, hf_ref,
                  fa_ref, fai_ref, fb_ref, fbi_ref, o_ref, ts_ref, zs_ref, cs0_ref, cs1_ref, csv_ref):
    n_slabs = x0_ref.shape[1] // FFT_N2

    def gated_value(j):
        return _conv_slab(x1_ref, cw1_ref, cs1_ref, j, n_slabs) * _conv_slab(v_ref, cwv_ref, csv_ref, j, n_slabs)

    def fill(j, carry):
        ts_ref[pl.ds(pl.multiple_of(j * TPITCH, SUBLANES), FFT_N2), :] = gated_value(j)
        return carry

    lax.fori_loop(0, n_slabs, fill, 0, unroll=2)

    def stage_a(n2, carry):
        sa = ts_ref[pl.ds(n2, n_slabs, stride=TPITCH), :].astype(BF16)
        sb = ts_ref[pl.ds(n2 + 1, n_slabs, stride=TPITCH), :].astype(BF16)
        zero = jnp.zeros_like(sa)
        rhs = jnp.concatenate([jnp.concatenate([sa, zero], axis=1), jnp.concatenate([zero, sb], axis=1)], axis=0)
        z = _dot(fa_ref[n2 // 2], rhs)
        for k in range(2):
            zk = z[:, k * LANES:(k + 1) * LANES]
            zs_ref[pl.ds(n2 + k, KP, stride=ZPITCH), :] = zk[:KP]
            zs_ref[pl.ds(FFT_N2 + n2 + k, KP, stride=ZPITCH), :] = zk[KP:]
        return carry

    lax.fori_loop(0, FFT_N2 // 2, lambda p, c: stage_a(2 * p, c), 0, unroll=32)

    def stage_b(p, carry):
        b0 = pl.multiple_of(2 * p * ZPITCH, SUBLANES)
        b1 = pl.multiple_of(b0 + ZPITCH, SUBLANES)
        z = jnp.concatenate([zs_ref[pl.ds(b0, 2 * FFT_N2), :], zs_ref[pl.ds(b1, 2 * FFT_N2), :]], axis=1)
        x = _dot(fb_ref[...], z.astype(BF16))
        h = jnp.concatenate([hf_ref[0, 2 * p], hf_ref[0, 2 * p + 1]], axis=1)
        xr, xi = x[:FFT_N2], x[FFT_N2:]
        hr, hi = h[:FFT_N2], h[FFT_N2:]
        prod = jnp.concatenate([xr * hr - xi * hi, xr * hi + xi * hr], axis=0)
        y = _dot(fbi_ref[...], prod.astype(BF16))
        zs_ref[pl.ds(b0, 2 * FFT_N2), :] = y[:, :LANES]
        zs_ref[pl.ds(b1, 2 * FFT_N2), :] = y[:, LANES:]
        return carry

    lax.fori_loop(0, KHP // 2, stage_b, 0, unroll=True)

    def stage_ai(n2, carry):
        yr = zs_ref[pl.ds(n2, KP, stride=ZPITCH), :]
        yi = zs_ref[pl.ds(FFT_N2 + n2, KP, stride=ZPITCH), :]
        y = jnp.concatenate([yr, yi], axis=0).astype(BF16)
        ts_ref[pl.ds(n2, n_slabs, stride=TPITCH), :] = _dot(fai_ref[n2], y)
        return carry

    lax.fori_loop(0, FFT_N2, stage_ai, 0, unroll=64)

    def finish(j, carry):
        conv = ts_ref[pl.ds(pl.multiple_of(j * TPITCH, SUBLANES), FFT_N2), :]
        y = _conv_slab(x0_ref, cw0_ref, cs0_ref, j, n_slabs) * conv
        o_ref[0, pl.ds(pl.multiple_of(j * FFT_N2, FFT_N2), FFT_N2), :] = y.astype(BF16)
        return carry

    lax.fori_loop(0, n_slabs, finish, 0, unroll=2)


def _hyena(u, cw, hf, fa, fai, fb, fbi):
    b, seq, c3 = u.shape
    nblk = D_HYENA // LANES
    n_slabs = seq // FFT_N2
    stream = lambda k: pl.BlockSpec((1, seq, LANES), lambda j, i, k=k: (i, 0, k * nblk + j))
    cwspec = lambda k: pl.BlockSpec((SUBLANES, LANES), lambda j, i, k=k: (0, k * nblk + j))
    return pl.pallas_call(
        _hyena_kernel,
        grid=(nblk, b),
        in_specs=[stream(0), stream(1), stream(2), cwspec(0), cwspec(1), cwspec(2),
                  pl.BlockSpec((1, KHP, 2 * FFT_N2, LANES), lambda j, i: (j, 0, 0, 0),
                               pipeline_mode=pl.Buffered(1)),
                  _const_spec(fa.shape), _const_spec(fai.shape),
                  _const_spec(fb.shape), _const_spec(fbi.shape)],
        out_specs=pl.BlockSpec((1, seq, LANES), lambda j, i: (i, 0, j)),
        out_shape=jax.ShapeDtypeStruct((b, seq, D_HYENA), BF16),
        scratch_shapes=[pltpu.VMEM((n_slabs * TPITCH, LANES), F32),
                        pltpu.VMEM((KP * ZPITCH, LANES), F32)]
        + [pltpu.VMEM((FFT_N2 + 2 * SUBLANES, LANES), F32)] * 3,
        compiler_params=_params(("arbitrary", "arbitrary")),
        name="hyena",
    )(u, u, u, cw, cw, cw, hf, fa, fai, fb, fbi)


def _attn_kernel(sink_ref, q_ref, kp_ref, km_ref, kn_ref, vp_ref, vm_ref, vn_ref, kc_ref, vc_ref,
                 o_ref, ka_ref, va_ref):
    i = pl.program_id(1)
    n_i = pl.num_programs(1)
    tq = q_ref.shape[0]
    nqb = tq // BLOCK
    ka_ref[0:BLOCK] = kp_ref[...]
    ka_ref[BLOCK:BLOCK + tq] = km_ref[...]
    ka_ref[BLOCK + tq:] = kn_ref[...]
    va_ref[0:BLOCK] = vp_ref[...]
    va_ref[BLOCK:BLOCK + tq] = vm_ref[...]
    va_ref[BLOCK + tq:] = vn_ref[...]

    qi = lax.broadcasted_iota(jnp.int32, (BLOCK, BLOCK), 0)
    kj = lax.broadcasted_iota(jnp.int32, (BLOCK, BLOCK), 1)
    lane = lax.broadcasted_iota(jnp.int32, (BLOCK, LANES), 1)
    low = lane < HEAD_DIM
    half = GQA_GROUP // 2
    rows2 = half * BLOCK
    hrow = lax.broadcasted_iota(jnp.int32, (GQA_GROUP * BLOCK, 1), 0) // BLOCK
    head_order = [hh for hh in range(GQA_GROUP) if hh % 2 == 0] + [hh for hh in range(GQA_GROUP) if hh % 2 == 1]
    one = jnp.ones((), BF16)

    def with_ones(v):
        lanes_low = lax.broadcasted_iota(jnp.int32, v.shape, 1) < HEAD_DIM
        return jnp.where(lanes_low, v, one), jnp.where(lanes_low, one, v)

    for g in range(N_KV_HEADS):
        gl = slice(g * LANES, (g + 1) * LANES)
        vc_even, vc_odd = with_ones(vc_ref[:, gl])
        sink = jnp.zeros((GQA_GROUP * BLOCK, 1), F32)
        for pos, hh in enumerate(head_order):
            sink = jnp.where(hrow == pos, sink_ref[g * GQA_GROUP + hh], sink)
        for j in range(nqb):
            prev_ok = kj >= qi
            next_ok = kj <= qi
            if j == 0:
                prev_ok = prev_ok & (i > 0)
            if j == nqb - 1:
                next_ok = next_ok & (i < n_i - 1)
            bias_p = jnp.concatenate([jnp.where(prev_ok, 0.0, NEG_INF).astype(F32)] * GQA_GROUP, axis=0)
            bias_n = jnp.concatenate([jnp.where(next_ok, 0.0, NEG_INF).astype(F32)] * GQA_GROUP, axis=0)
            qb = q_ref[j * BLOCK:(j + 1) * BLOCK, :]
            parts = []
            for hh in head_order:
                h = g * GQA_GROUP + hh
                qp = qb[:, (h // 2) * LANES:(h // 2 + 1) * LANES]
                parts.append(jnp.where(low if h % 2 == 0 else ~low, qp, jnp.zeros_like(qp)))
            qs = jnp.concatenate(parts, axis=0)
            kw = ka_ref[j * BLOCK:(j + 3) * BLOCK, gl]
            vw_even, vw_odd = with_ones(va_ref[j * BLOCK:(j + 3) * BLOCK, gl])
            s_w = _dot_nt(qs, kw)
            s_p = s_w[:, :BLOCK] + bias_p
            s_m = s_w[:, BLOCK:2 * BLOCK]
            s_n = s_w[:, 2 * BLOCK:] + bias_n
            s_c = _dot_nt(qs, kc_ref[:, gl])
            m = jnp.maximum(jnp.maximum(jnp.max(jnp.maximum(jnp.maximum(s_p, s_m), s_n), axis=1, keepdims=True),
                                        jnp.max(s_c, axis=1, keepdims=True)), sink)
            e_w = jnp.concatenate([jnp.exp(s_p - m), jnp.exp(s_m - m), jnp.exp(s_n - m)], axis=1).astype(BF16)
            e_c = jnp.exp(s_c - m).astype(BF16)
            e_sink = jnp.exp(sink - m)
            outs = []
            for par, (vw, vcx) in enumerate(((vw_even, vc_even), (vw_odd, vc_odd))):
                rs = slice(par * rows2, (par + 1) * rows2)
                acc = _dot(e_w[rs], vw) + _dot(e_c[rs], vcx)
                den = pltpu.roll(acc, HEAD_DIM, 1) + e_sink[rs]
                outs.append(acc / den)
            for pp in range(half):
                pair = jnp.where(low, outs[0][pp * BLOCK:(pp + 1) * BLOCK], outs[1][pp * BLOCK:(pp + 1) * BLOCK])
                col = (g * half + pp) * LANES
                o_ref[j * BLOCK:(j + 1) * BLOCK, col:col + LANES] = pair.astype(BF16)


def _attention(sinks, q, kd, vd, kc, vc, batch, seq, tq):
    t = q.shape[0]
    per_b = seq // tq
    nqb = tq // BLOCK
    nb = seq // BLOCK
    n_ctx = kc.shape[0] // batch
    main = lambda b, i: (b * per_b + i, 0)
    prev = lambda b, i: (b * nb + jnp.maximum(i * nqb - 1, 0), 0)
    nxt = lambda b, i: (b * nb + jnp.minimum(i * nqb + nqb, nb - 1), 0)
    kvw = kd.shape[1]
    return pl.pallas_call(
        _attn_kernel,
        grid=(batch, per_b),
        in_specs=[pl.BlockSpec(memory_space=pltpu.SMEM),
                  pl.BlockSpec((tq, D_ATTN), main),
                  pl.BlockSpec((BLOCK, kvw), prev), pl.BlockSpec((tq, kvw), main), pl.BlockSpec((BLOCK, kvw), nxt),
                  pl.BlockSpec((BLOCK, kvw), prev), pl.BlockSpec((tq, kvw), main), pl.BlockSpec((BLOCK, kvw), nxt),
                  pl.BlockSpec((n_ctx, kvw), lambda b, i: (b, 0)),
                  pl.BlockSpec((n_ctx, kvw), lambda b, i: (b, 0))],
        out_specs=pl.BlockSpec((tq, D_ATTN), main),
        out_shape=jax.ShapeDtypeStruct((t, D_ATTN), BF16),
        scratch_shapes=[pltpu.VMEM((tq + 2 * BLOCK, kvw), BF16),
                        pltpu.VMEM((tq + 2 * BLOCK, kvw), BF16)],
        compiler_params=_params(("arbitrary", "arbitrary")),
        name="attention",
    )(sinks, q, kd, kd, kd, vd, vd, vd, kc, vc)


def _route(t, wr_hi_ref, wr_hl_ref, br_ref):
    th, tl = _split(t)
    both = _dot(th, wr_hl_ref[...])
    logits = both[:, :LANES] + both[:, LANES:] + _dot(tl, wr_hi_ref[...]) + br_ref[...]
    lane_i = lax.broadcasted_iota(jnp.int32, logits.shape, 1)
    lane = lane_i.astype(F32)
    grp_of_lane = (lane_i >> 2).astype(F32)
    ninf = -jnp.inf
    far = float(LANES)
    is_g = (lane_i >= N_EXPERTS) & (lane_i < N_EXPERTS + N_GROUPS)
    glog = jnp.where(is_g, logits, ninf)
    gmax = jnp.max(glog, axis=1, keepdims=True)
    gidx = jnp.min(jnp.where(glog == gmax, lane - float(N_EXPERTS), far), axis=1, keepdims=True)
    group_p = 1.0 / jnp.sum(jnp.exp(glog - gmax), axis=1, keepdims=True)
    in_grp = (lane_i < N_EXPERTS) & (grp_of_lane == gidx)
    elog = jnp.where(in_grp, logits, ninf)
    v1 = jnp.max(elog, axis=1, keepdims=True)
    i1 = jnp.min(jnp.where(elog == v1, lane, far), axis=1, keepdims=True)
    elog2 = jnp.where(lane == i1, ninf, elog)
    v2 = jnp.max(elog2, axis=1, keepdims=True)
    i2 = jnp.min(jnp.where(elog2 == v2, lane, far), axis=1, keepdims=True)
    e = jnp.exp(v2 - v1)
    w1 = group_p / (1.0 + e)
    w2 = group_p * e / (1.0 + e)
    gate = jnp.where(lane == i1, w1, 0.0) + jnp.where(lane == i2, w2, 0.0)
    return gate + jnp.where(lane_i == GROUP_LANE, gidx, 0.0)


def _mix_out_kernel(x_ref, h_ref, yh_ref, ya_ref, g1_ref, sh2_ref, sc2_ref,
                    wg_ref, wbh_ref, wba_ref, wo_ref, lng_ref, lnb_ref, wrh_ref, wrhl_ref, br_ref,
                    x1_ref, gate_ref, gsel_ref, *t_refs):
    d = x_ref.shape[1]
    for r0 in range(0, x_ref.shape[0], SUB_ROWS):
        rows = slice(r0, r0 + SUB_ROWS)
        x = x_ref[rows, :]
        h = h_ref[rows, :]
        g_hy = jax.nn.sigmoid(_dot(h, wg_ref[:, :d]))
        merged = g_hy * _dot(yh_ref[rows, :], wbh_ref[...])
        g_at = jax.nn.sigmoid(_dot(h, wg_ref[:, d:]))
        merged = merged + g_at * _dot(ya_ref[rows, :], wba_ref[...])
        mix = _dot(merged.astype(BF16), wo_ref[...])
        x1 = _standardize(DEEPNORM_ALPHA * x + g1_ref[0] * mix) * lng_ref[...] + lnb_ref[...]
        x1_ref[rows, :] = x1
        t = _standardize(x1) * (1.0 + sc2_ref[0]) + sh2_ref[0]
        gate = _route(t, wrh_ref, wrhl_ref, br_ref)
        gate_ref[rows, :] = gate
        grp = gate.T[GROUP_LANE:GROUP_LANE + 1, :]
        gsel_ref[r0 // LANES:(r0 + SUB_ROWS) // LANES, :] = jnp.concatenate(
            [grp[:, k * LANES:(k + 1) * LANES] for k in range(SUB_ROWS // LANES)], axis=0)
        for t_ref, piece in zip(t_refs, _pack_pieces(t)):
            t_ref[rows, :] = piece


def _mix_out(x2d, h, yh, ya, mods, w_g, w_bh, w_ba, w_o, ln_g, ln_b, wr_hi, wr_hl, br, seq, tm):
    t, d = x2d.shape
    per_b = seq // tm
    row = lambda i: (i, 0)
    mod = lambda i: (i // per_b, 0, 0)
    mspec = pl.BlockSpec((1, 1, d), mod)
    g1, sh2, sc2 = mods
    return pl.pallas_call(
        _mix_out_kernel,
        grid=(t // tm,),
        in_specs=[pl.BlockSpec((tm, d), row),
                  pl.BlockSpec((tm, d), row),
                  pl.BlockSpec((tm, yh.shape[1]), row),
                  pl.BlockSpec((tm, ya.shape[1]), row),
                  mspec, mspec, mspec,
                  _const_spec(w_g.shape), _const_spec(w_bh.shape), _const_spec(w_ba.shape),
                  _const_spec(w_o.shape), _const_spec(ln_g.shape), _const_spec(ln_b.shape),
                  _const_spec(wr_hi.shape), _const_spec(wr_hl.shape), _const_spec(br.shape)],
        out_specs=[pl.BlockSpec((tm, d), row), pl.BlockSpec((tm, LANES), row),
                   pl.BlockSpec((tm // LANES, LANES), row)]
        + [pl.BlockSpec((tm, SC_ROW_WORDS), row)] * N_PIECES,
        out_shape=[jax.ShapeDtypeStruct((t, d), F32), jax.ShapeDtypeStruct((t, LANES), F32),
                   jax.ShapeDtypeStruct((t // LANES, LANES), F32)]
        + [jax.ShapeDtypeStruct((t, SC_ROW_WORDS), jnp.uint32)] * N_PIECES,
        compiler_params=_params(("arbitrary",)),
        name="mix_out",
    )(x2d, h, yh, ya, g1, sh2, sc2, w_g, w_bh, w_ba, w_o, ln_g, ln_b, wr_hi, wr_hl, br)


def _slots_kernel(g_ref, upper_ref, lower_ref, dest_ref, meta_ref):
    gsel = g_ref[...]
    dest = jnp.zeros(gsel.shape, F32)
    base = jnp.zeros((1, 1), F32)
    chunk_start = lax.broadcasted_iota(jnp.int32, (1, LANES), 1).astype(F32) * float(MOE_CHUNK)
    owner = jnp.zeros((1, LANES), F32)
    for g in range(N_GROUPS):
        onehot = jnp.where(gsel == float(g), 1.0, 0.0)
        in_row = _dot(onehot.astype(BF16), upper_ref[...])
        row_tot = jnp.sum(onehot, axis=1, keepdims=True)
        rows_before = _dot(lower_ref[...], jnp.broadcast_to(row_tot, onehot.shape).astype(BF16))
        dest = dest + onehot * (base + rows_before + in_row)
        if g > 0:
            owner = owner + jnp.where(chunk_start >= base, 1.0, 0.0)
        n_g = jnp.sum(row_tot, axis=0, keepdims=True)
        base = base + jnp.floor((n_g + float(MOE_CHUNK - 1)) * (1.0 / MOE_CHUNK)) * float(MOE_CHUNK)
    dest_ref[...] = dest.astype(jnp.int32)
    row = lax.broadcasted_iota(jnp.int32, meta_ref.shape, 0)
    meta_ref[...] = jnp.where(row == 0, owner, base * (1.0 / MOE_CHUNK)).astype(jnp.int32)


def _slots(gsel):
    r = gsel.shape[0]
    upper = jnp.asarray(np.triu(np.ones((LANES, LANES), np.float32), 1)).astype(BF16)
    lower = jnp.asarray(np.tril(np.ones((r, r), np.float32), -1)).astype(BF16)
    full = lambda a: pl.BlockSpec(a.shape, lambda i: (0,) * a.ndim)
    return pl.pallas_call(
        _slots_kernel,
        grid=(1,),
        in_specs=[full(gsel), full(upper), full(lower)],
        out_specs=[pl.BlockSpec((r, LANES), lambda i: (0, 0)), pl.BlockSpec((SUBLANES, LANES), lambda i: (0, 0))],
        out_shape=[jax.ShapeDtypeStruct((r, LANES), jnp.int32), jax.ShapeDtypeStruct((SUBLANES, LANES), jnp.int32)],
        compiler_params=_params(("arbitrary",)),
        name="moe_slots",
    )(gsel, upper, lower)


def _experts_kernel(owner_ref, used_ref, *refs):
    n_p = N_PIECES
    t_refs, gate_ref = refs[:n_p], refs[n_p]
    wg_ref, wu_ref, wd_ref = refs[n_p + 1:n_p + 4]
    y_refs = refs[n_p + 4:2 * n_p + 4]
    wg16_ref, wu16_ref, wd16_ref = refs[2 * n_p + 4:]
    c = pl.program_id(0)
    f = D_EXPERT

    @pl.when((c == 0) | (owner_ref[c] != owner_ref[jnp.maximum(c - 1, 0)]))
    def _():
        wg16_ref[...] = wg_ref[...].astype(BF16)
        wu16_ref[...] = wu_ref[...].astype(BF16)
        for e in range(EXPERTS_PER_GROUP):
            wd16_ref[e * f:(e + 1) * f, :] = wd_ref[e].astype(BF16)

    @pl.when(c < used_ref[0])
    def _():
        t = _unpack_pieces([r[...] for r in t_refs]).astype(BF16)
        gate = gate_ref[...]
        lane = lax.broadcasted_iota(jnp.int32, gate.shape, 1)
        first = owner_ref[c] * EXPERTS_PER_GROUP
        parts = []
        for e in range(EXPERTS_PER_GROUP):
            a = _dot(t, wg16_ref[e])
            u = _dot(t, wu16_ref[e])
            ge = jnp.sum(jnp.where(lane == first + e, gate, 0.0), axis=1, keepdims=True)
            parts.append((a * jax.nn.sigmoid(a) * u * ge).astype(BF16))
        y = _dot(jnp.concatenate(parts, axis=1), wd16_ref[...])
        for y_ref, piece in zip(y_refs, _pack_pieces(y)):
            y_ref[...] = piece

    @pl.when(c >= used_ref[0])
    def _():
        for y_ref in y_refs:
            y_ref[...] = jnp.zeros_like(y_ref)


def _experts(owner, used, t_pieces, gate_sorted, w_gate, w_up, w_down):
    n_slots = gate_sorted.shape[0]
    d, f = w_gate.shape[1:]
    row = lambda c, owner, used: (c, 0)
    by_owner = lambda c, owner, used: (owner[c], 0, 0)
    piece = pl.BlockSpec((MOE_CHUNK, SC_ROW_WORDS), row)
    group_of_experts = lambda w: pl.BlockSpec((EXPERTS_PER_GROUP,) + w.shape[1:], by_owner)
    return pl.pallas_call(
        _experts_kernel,
        grid_spec=pltpu.PrefetchScalarGridSpec(
            num_scalar_prefetch=2,
            grid=(n_slots // MOE_CHUNK,),
            in_specs=[piece] * len(t_pieces) + [pl.BlockSpec((MOE_CHUNK, LANES), row),
                                                group_of_experts(w_gate), group_of_experts(w_up),
                                                group_of_experts(w_down)],
            out_specs=[piece] * len(t_pieces),
            scratch_shapes=[pltpu.VMEM((EXPERTS_PER_GROUP, d, f), BF16),
                            pltpu.VMEM((EXPERTS_PER_GROUP, d, f), BF16),
                            pltpu.VMEM((EXPERTS_PER_GROUP * f, d), BF16)],
        ),
        out_shape=[jax.ShapeDtypeStruct((n_slots, SC_ROW_WORDS), jnp.uint32)] * len(t_pieces),
        compiler_params=_params(("arbitrary",)),
        name="moe_experts",
    )(owner, used, *t_pieces, gate_sorted, w_gate, w_up, w_down)


def _final_kernel(x1_ref, g2_ref, lng_ref, lnb_ref, *refs):
    y_refs, o_ref = refs[:-1], refs[-1]
    y = _unpack_pieces([r[...] for r in y_refs])
    o_ref[...] = _standardize(DEEPNORM_ALPHA * x1_ref[...] + g2_ref[0] * y) * lng_ref[...] + lnb_ref[...]


def _final(x1, g2, ln_g, ln_b, y_pieces, seq, tm):
    t, d = x1.shape
    per_b = seq // tm
    row = lambda i: (i, 0)
    return pl.pallas_call(
        _final_kernel,
        grid=(t // tm,),
        in_specs=[pl.BlockSpec((tm, d), row), pl.BlockSpec((1, 1, d), lambda i: (i // per_b, 0, 0)),
                  _const_spec(ln_g.shape), _const_spec(ln_b.shape)]
        + [pl.BlockSpec((tm, SC_ROW_WORDS), row)] * len(y_pieces),
        out_specs=pl.BlockSpec((tm, d), row),
        out_shape=jax.ShapeDtypeStruct((t, d), F32),
        compiler_params=_params(("arbitrary",)),
        name="moe_final",
    )(x1, g2, ln_g, ln_b, *y_pieces)


def _sc_move_rows(tables, idx, n_out, scatter, name):
    n = idx.shape[0]
    mesh = plsc.VectorSubcoreMesh(core_axis_name="c", subcore_axis_name="s")
    out_type = [jax.ShapeDtypeStruct((n_out, t.shape[1]), t.dtype) for t in tables]
    window = lambda i: (i, 0)
    index_win = pl.BlockSpec((1, SC_WINDOW), lambda i: (0, i))
    split = dict(core_axis_name=("c", "s"), dimension_semantics=(pltpu.PARALLEL,))

    @functools.partial(pl.kernel, out_type=out_type, mesh=mesh, scratch_types=[], name=name)
    def move(*refs):
        srcs, i_hbm, dsts = refs[:len(tables)], refs[len(tables)], refs[len(tables) + 1:]
        for src, dst in zip(srcs, dsts):
            rows_win = pl.BlockSpec((SC_WINDOW, src.shape[1]), window)
            if scatter:
                def body(x_vmem, i_vmem, dst=dst):
                    pltpu.sync_copy(x_vmem, dst.at[i_vmem.at[0]])

                pltpu.emit_pipeline(body, grid=(n // SC_WINDOW,), in_specs=[rows_win, index_win],
                                    out_specs=[], **split)(src, i_hbm)
            else:
                def body(i_vmem, o_vmem, src=src):
                    pltpu.sync_copy(src.at[i_vmem.at[0]], o_vmem)

                pltpu.emit_pipeline(body, grid=(n // SC_WINDOW,), in_specs=[index_win],
                                    out_specs=[rows_win], **split)(i_hbm, dst)

    return move(*tables, idx.reshape(1, n))


def _dft_tables(seq):
    n = 2 * seq
    n1_full = n // FFT_N2
    n1_data = seq // FFT_N2
    k1 = np.arange(KH, dtype=np.float64)[None, :, None]
    n2 = np.arange(FFT_N2, dtype=np.float64)[:, None, None]
    n1 = np.arange(n1_full, dtype=np.float64)[None, None, :]
    ang = 2.0 * np.pi * k1 * (FFT_N2 * n1 + n2) / n
    fa = np.zeros((FFT_N2, 2 * KP, n1_full))
    fa[:, :KH] = np.cos(ang)
    fa[:, KP:KP + KH] = -np.sin(ang)
    wgt = np.full((KH,), 2.0)
    wgt[0] = 1.0
    wgt[KH - 1] = 1.0
    fai = np.zeros((FFT_N2, n1_data, 2 * KP))
    angt = np.transpose(ang[:, :, :n1_data], (0, 2, 1))
    fai[:, :, :KH] = np.cos(angt) * wgt / n
    fai[:, :, KP:KP + KH] = -np.sin(angt) * wgt / n
    kk = np.arange(FFT_N2, dtype=np.float64)
    a2 = 2.0 * np.pi * np.outer(kk, kk) / FFT_N2
    fr, fi = np.cos(a2), -np.sin(a2)
    fb = np.block([[fr, -fi], [fi, fr]])
    fbi = np.block([[fr, fi], [-fi, fr]])
    f32 = lambda a: jnp.asarray(a.astype(np.float32))
    pairs = lambda a: np.concatenate([a[0::2], a[1::2]], axis=2)
    return f32(pairs(fa[:, :, :n1_data])), f32(pairs(fa)), f32(fai), f32(fb), f32(fbi)


def _filter_features(seq, rows):
    f32 = np.float32
    t = np.linspace(0.0, 1.0, seq, dtype=f32)
    w = (f32(2.0 * math.pi) * np.arange(seq, dtype=f32) / f32(seq)).astype(f32)
    t2 = np.concatenate([t, t[::-1]])
    w2 = np.concatenate([w, w[::-1]])
    bands = np.linspace(1e-4, FILTER_BANDS - 1, FILTER_BANDS, dtype=f32)
    max_decay = math.log(DECAY_TARGET) / FAST_DECAY_PCT
    min_decay = math.log(DECAY_TARGET) / SLOW_DECAY_PCT
    deltas = jnp.linspace(min_decay, max_decay, D_HYENA, dtype=F32)

    def feats(k):
        tp = t2.reshape(-1, 2, rows // 2)[:, k].reshape(-1, 1)
        wp = w2.reshape(-1, 2, rows // 2)[:, k].reshape(-1, 1)
        pad = np.zeros((tp.shape[0], FILTER_ORDER - FILTER_EMB), f32)
        arg = (bands * wp).astype(f32)
        return np.concatenate([tp, np.cos(arg), -np.sin(arg), pad], axis=-1).astype(f32)

    zp = jnp.asarray(np.concatenate([feats(0), feats(1)], axis=-1))
    return zp, jnp.asarray(t2)[:, None], jnp.abs(deltas)[None, :]


def _rope_tables(seq):
    f32 = np.float32
    rows = seq // GRID_W
    row = np.repeat(np.arange(rows, dtype=f32), GRID_W)
    col = np.tile(np.arange(GRID_W, dtype=f32), rows)
    half = HEAD_DIM // 2
    inv_freq = (f32(ROPE_BASE) ** (-np.arange(0, half, 2, dtype=f32) / f32(half))).astype(f32)
    ang = np.concatenate([row[:, None] * inv_freq, col[:, None] * inv_freq], axis=-1).astype(f32)
    cos, sin = np.cos(ang), np.sin(ang)
    c64 = np.concatenate([cos, cos], axis=-1)
    s64 = np.concatenate([-sin, sin], axis=-1)
    return (jnp.asarray(np.concatenate([c64, c64], axis=-1).astype(f32)),
            jnp.asarray(np.concatenate([s64, s64], axis=-1).astype(f32)))


def _head_perm(n_heads):
    idx = []
    for h in range(n_heads):
        base = h * HEAD_DIM
        idx += [base + 2 * j for j in range(HEAD_DIM // 2)]
        idx += [base + 2 * j + 1 for j in range(HEAD_DIM // 2)]
    return np.asarray(idx, dtype=np.int32)


def _dup_heads(w):
    parts = []
    for g in range(N_KV_HEADS):
        blk = w[:, g * HEAD_DIM:(g + 1) * HEAD_DIM]
        parts += [blk, blk]
    return jnp.concatenate(parts, axis=1)


def kernel(x, c, ctx, c_ctx, ada_w, ada_b, w_in, hy_conv_w, hy_conv_b, hy_w1, hy_b1, hy_w2, hy_b2, hy_w3, hy_b3, hy_w4, hy_freq, hy_bias, attn_sinks, w_branch_hy, w_branch_attn, w_out, ln1_g, ln1_b, w_group, b_group, w_router, b_router, w_gate_e, w_up_e, w_down_e, ln2_g, ln2_b):
    batch, seq, d = x.shape
    n_ctx = ctx.shape[1]
    assert d == D_MODEL and ada_w.shape[0] == DEPTH == 1
    assert 2 * seq == FFT_N2 * FFT_N2 and seq % TM_PROJ == 0
    l = 0

    w = w_in[l]
    s0, s1, s2, s3 = 3 * D_HYENA, 3 * D_HYENA + D_ATTN, 3 * D_HYENA + D_ATTN + D_KV, 3 * D_HYENA + D_ATTN + 2 * D_KV
    w_q = w[:, s0:s1][:, _head_perm(N_HEADS)]
    w_k = _dup_heads(w[:, s1:s2][:, _head_perm(N_KV_HEADS)])
    w_v = _dup_heads(w[:, s2:s3])
    w_c = jnp.concatenate([w[:, :s0], w_q, w_k, w_v], axis=1).astype(BF16)
    w_kv = jnp.concatenate([w_k, w_v], axis=1).astype(BF16)
    w_g = w[:, s3:].astype(BF16)
    w_bh = w_branch_hy[l].astype(BF16)
    w_ba = w_branch_attn[l].astype(BF16)
    w_o = w_out[l].astype(BF16)
    pad_lanes = LANES - N_EXPERTS - N_GROUPS
    wr = jnp.concatenate([w_router[l], w_group[l], jnp.zeros((d, pad_lanes), F32)], axis=1)
    wr_hi = wr.astype(BF16)
    wr_hl = jnp.concatenate([wr_hi, (wr - wr_hi.astype(F32)).astype(BF16)], axis=1)
    br = jnp.concatenate([b_router[l], b_group[l], jnp.zeros((pad_lanes,), F32)])[None, :]
    cw = jnp.concatenate([hy_conv_w[l][:, 0, :], hy_conv_b[l][None, :],
                          jnp.zeros((SUBLANES - SHORT_CONV - 1, 3 * D_HYENA), F32)], axis=0)
    row2 = lambda a: a.reshape(1, -1)
    pair = lambda a: jnp.concatenate([a, a], axis=-1)
    zero_o = jnp.zeros((FILTER_ORDER, FILTER_ORDER), F32)
    bdiag = lambda a: jnp.concatenate([jnp.concatenate([a, zero_o], axis=1),
                                       jnp.concatenate([zero_o, a], axis=1)], axis=0)
    w1p = bdiag(jnp.concatenate([hy_w1[l], jnp.zeros((FILTER_ORDER - FILTER_EMB, FILTER_ORDER), F32)], axis=0))
    w4h = jnp.transpose(hy_w4[l].reshape(FILTER_ORDER, 2, D_HYENA), (1, 0, 2))
    zero_w4 = jnp.zeros_like(w4h)
    w4s = jnp.stack([jnp.concatenate([w4h, zero_w4], axis=1),
                     jnp.concatenate([zero_w4, w4h], axis=1)], axis=1)

    fa, fa_full, fai, fb, fbi = _dft_tables(seq)
    fa, fa_full, fai, fb, fbi = (a.astype(BF16) for a in (fa, fa_full, fai, fb, fbi))
    zp, t_col, absd = _filter_features(seq, FILT_ROWS)
    cos_t, sin_t = _rope_tables(seq)

    cond = jnp.concatenate([c, c_ctx[None], jnp.zeros((SUBLANES - batch - 1, d), F32)], axis=0)
    mods = _adaln(cond, ada_w[l], ada_b[l])
    m6 = [mods[:, k * d:(k + 1) * d].reshape(SUBLANES, 1, d) for k in range(6)]
    sh1, sc1, g1, sh2, sc2, g2 = m6

    kc, vc = _ctx_kv(ctx.reshape(batch * n_ctx, d), sh1[batch], sc1[batch], w_kv, n_ctx)

    x2d = x.reshape(batch * seq, d)
    u, q, kd, vd, h_mod = _in_proj(x2d, sh1, sc1, w_c, cos_t, sin_t, seq, TM_PROJ)
    h2u, ss = _filt_mlp(zp, w1p, pair(row2(hy_b1[l])), bdiag(hy_w2[l]), pair(row2(hy_b2[l])),
                        bdiag(hy_w3[l]), pair(row2(hy_b3[l])), w4s, pair(row2(hy_freq[l])), t_col, absd, FILT_ROWS)
    hf = _filt_fft(h2u, ss, row2(hy_bias[l]), fa_full, fb)
    y_hy = _hyena(u.reshape(batch, seq, 3 * D_HYENA), cw, hf, fa, fai, fb, fbi)
    y_at = _attention(attn_sinks[l], q, kd, vd, kc, vc, batch, seq, TQ_ATTN)
    x1, gate, gsel, *t_pieces = _mix_out(x2d, h_mod, y_hy.reshape(batch * seq, D_HYENA), y_at, (g1, sh2, sc2),
                                         w_g, w_bh, w_ba, w_o, row2(ln1_g[l]), row2(ln1_b[l]), wr_hi, wr_hl, br,
                                         seq, TM_PROJ)

    n_tok = batch * seq
    n_slots = n_tok + N_GROUPS * MOE_CHUNK
    dest2d, meta = _slots(gsel)
    dest = dest2d.reshape(n_tok)
    *t_sorted, gate_sorted = _sc_move_rows(t_pieces + [gate], dest, n_slots, True, "moe_sort")
    y_sorted = _experts(meta[0], meta[1, :1], t_sorted, gate_sorted, w_gate_e[l], w_up_e[l], w_down_e[l])
    y_pieces = _sc_move_rows(y_sorted, dest, n_tok, False, "moe_unsort")
    out = _final(x1, g2, row2(ln2_g[l]), row2(ln2_b[l]), y_pieces, seq, TM_FINAL)
    return out.reshape(batch, seq, d)
```

```python
import functools
import math

import numpy as np
import jax
import jax.numpy as jnp
from jax import lax
from jax.experimental import pallas as pl
from jax.experimental.pallas import tpu as pltpu
from jax.experimental.pallas import tpu_sc as plsc

F32 = jnp.float32
BF16 = jnp.bfloat16

D_MODEL = 1024
GRID_W = 64
D_HYENA = D_MODEL // 2
SHORT_CONV = 3
FILTER_BANDS = 16
FILTER_EMB = 1 + 2 * FILTER_BANDS
FILTER_ORDER = 64
DECAY_TARGET = 1e-2
FAST_DECAY_PCT = 0.3
SLOW_DECAY_PCT = 1.5
HEAD_DIM = 64
D_ATTN = D_MODEL // 2
N_HEADS = D_ATTN // HEAD_DIM
N_KV_HEADS = N_HEADS // 4
GQA_GROUP = N_HEADS // N_KV_HEADS
D_KV = N_KV_HEADS * HEAD_DIM
WINDOW = 128
BLOCK = 128
ROPE_BASE = 10000.0
NEG_INF = -1e30
N_GROUPS = 4
EXPERTS_PER_GROUP = 4
N_EXPERTS = N_GROUPS * EXPERTS_PER_GROUP
D_EXPERT = D_MODEL // 4
LN_EPS = 1e-5
DEPTH = 1
DEEPNORM_ALPHA = (2.0 * DEPTH) ** 0.25

LANES = 128
SUBLANES = 8
VMEM_LIMIT = 56 * 1024 * 1024

SUB_ROWS = 512
TM_PROJ = 1024
MOE_CHUNK = 512
TM_FINAL = 1024
GROUP_LANE = N_EXPERTS
SC_WINDOW = 128
SC_ROW_WORDS = 256
N_PIECES = D_MODEL // (2 * SC_ROW_WORDS)
TQ_ATTN = 1024
FILT_ROWS = 2048

FFT_N2 = 128
KH = 65
KHP = 66
KP = 72
ZPITCH = 2 * FFT_N2 + SUBLANES
TPITCH = FFT_N2 + SUBLANES


def _dot(a, b):
    return jnp.dot(a, b, preferred_element_type=F32)


def _dot_nt(a, b):
    return lax.dot_general(a, b, (((1,), (1,)), ((), ())), preferred_element_type=F32)


def _split(a):
    hi = a.astype(BF16)
    lo = (a - hi.astype(F32)).astype(BF16)
    return hi, lo


def _pack_pieces(x):
    w = SC_ROW_WORDS
    pieces = []
    for p in range(x.shape[1] // (2 * w)):
        hi = lax.bitcast_convert_type(x[:, 2 * p * w:(2 * p + 1) * w].astype(BF16).astype(F32), jnp.uint32)
        lo = lax.bitcast_convert_type(x[:, (2 * p + 1) * w:(2 * p + 2) * w].astype(BF16).astype(F32), jnp.uint32)
        pieces.append(hi | (lo >> 16))
    return pieces


def _unpack_pieces(pieces):
    cols = []
    for word in pieces:
        cols.append(lax.bitcast_convert_type(word & jnp.uint32(0xFFFF0000), F32))
        cols.append(lax.bitcast_convert_type(word << 16, F32))
    return jnp.concatenate(cols, axis=1)


def _dot3(a, b):
    ah, al = _split(a)
    bh, bl = _split(b)
    return _dot(ah, bh) + _dot(al, bh) + _dot(ah, bl)


def _standardize(x):
    mu = jnp.mean(x, axis=-1, keepdims=True)
    xc = x - mu
    var = jnp.mean(xc * xc, axis=-1, keepdims=True)
    return xc * lax.rsqrt(var + LN_EPS)


def _params(sem, vmem=VMEM_LIMIT):
    return pltpu.CompilerParams(dimension_semantics=sem, vmem_limit_bytes=vmem)


def _const_spec(shape):
    nd = len(shape)
    return pl.BlockSpec(shape, lambda *_: (0,) * nd, pipeline_mode=pl.Buffered(1))


def _adaln_kernel(c_ref, w_ref, b_ref, o_ref):
    s = c_ref[...]
    s = s * jax.nn.sigmoid(s)
    o_ref[...] = _dot3(s, w_ref[...]) + b_ref[...]


def _adaln(cond, w, b):
    n, d = cond.shape
    cols = w.shape[1]
    bc = 1024
    return pl.pallas_call(
        _adaln_kernel,
        grid=(cols // bc,),
        in_specs=[pl.BlockSpec((n, d), lambda j: (0, 0)),
                  pl.BlockSpec((d, bc), lambda j: (0, j)),
                  pl.BlockSpec((1, bc), lambda j: (0, j))],
        out_specs=pl.BlockSpec((n, bc), lambda j: (0, j)),
        out_shape=jax.ShapeDtypeStruct((n, cols), F32),
        compiler_params=_params(("arbitrary",)),
        name="adaln",
    )(cond, w, b.reshape(1, cols))


def _ctx_kv_kernel(x_ref, sh_ref, sc_ref, w_ref, k_ref, v_ref):
    h = _standardize(x_ref[...]) * (1.0 + sc_ref[...]) + sh_ref[...]
    kv = _dot(h.astype(BF16), w_ref[...])
    half = k_ref.shape[1]
    k_ref[...] = kv[:, :half].astype(BF16)
    v_ref[...] = kv[:, half:].astype(BF16)


def _ctx_kv(ctx2d, sh, sc, w_kv, rows):
    n, d = ctx2d.shape
    half = w_kv.shape[1] // 2
    return pl.pallas_call(
        _ctx_kv_kernel,
        grid=(n // rows,),
        in_specs=[pl.BlockSpec((rows, d), lambda i: (i, 0)),
                  pl.BlockSpec((1, d), lambda i: (0, 0)),
                  pl.BlockSpec((1, d), lambda i: (0, 0)),
                  pl.BlockSpec(w_kv.shape, lambda i: (0, 0))],
        out_specs=[pl.BlockSpec((rows, half), lambda i: (i, 0)),
                   pl.BlockSpec((rows, half), lambda i: (i, 0))],
        out_shape=[jax.ShapeDtypeStruct((n, half), BF16)] * 2,
        compiler_params=_params(("arbitrary",)),
        name="ctx_kv",
    )(ctx2d, sh, sc, w_kv)


def _rope(x, cos_t, sin_t):
    width = x.shape[1]
    reps = width // LANES
    c = jnp.concatenate([cos_t] * reps, axis=1)
    s = jnp.concatenate([sin_t] * reps, axis=1)
    half = HEAD_DIM // 2
    lane = lax.broadcasted_iota(jnp.int32, x.shape, 1)
    first_half = (lane & (HEAD_DIM - 1)) < half
    partner = jnp.where(first_half, pltpu.roll(x, width - half, 1), pltpu.roll(x, half, 1))
    return x * c + partner * s


def _in_proj_kernel(per_b, x_ref, xp_ref, xn_ref, sh_ref, sc_ref, w_ref, cw_ref, cos_ref, sin_ref,
                    x0_ref, vx_ref, q_ref, k_ref, v_ref, h_ref, us_ref):
    n_u = cw_ref.shape[1]
    n_q = q_ref.shape[1]
    n_k = k_ref.shape[1]
    tm = x_ref.shape[0]
    m = SUBLANES
    modulate = lambda x: (_standardize(x) * (1.0 + sc_ref[0]) + sh_ref[0]).astype(BF16)
    i = pl.program_id(0)
    n_blk = n_u // LANES

    def stage(row0, val):
        for b in range(n_blk):
            us_ref[b, row0:row0 + val.shape[0], :] = val[:, b * LANES:(b + 1) * LANES]

    halo = _dot(modulate(jnp.concatenate([xp_ref[0], xn_ref[0]], axis=0)), w_ref[:, :n_u])
    stage(0, jnp.where(i % per_b == 0, 0.0, halo[:m]))
    stage(m + tm, jnp.where(i % per_b == per_b - 1, 0.0, halo[m:]))
    for r0 in range(0, tm, SUB_ROWS):
        rows = slice(r0, r0 + SUB_ROWS)
        h = modulate(x_ref[rows, :])
        h_ref[rows, :] = h
        stage(m + r0, _dot(h, w_ref[:, :n_u]))
        cos_t = cos_ref[rows, :]
        sin_t = sin_ref[rows, :]
        q = _dot(h, w_ref[:, n_u:n_u + n_q])
        q_ref[rows, :] = (_rope(q, cos_t, sin_t) * (HEAD_DIM ** -0.5)).astype(BF16)
        k = _dot(h, w_ref[:, n_u + n_q:n_u + n_q + n_k])
        k_ref[rows, :] = _rope(k, cos_t, sin_t).astype(BF16)
        v_ref[rows, :] = _dot(h, w_ref[:, n_u + n_q + n_k:]).astype(BF16)
    cw = cw_ref[...]
    c = n_u // 3
    def conv(r0, c0):
        cols = slice(c0, c0 + LANES)
        b = c0 // LANES
        return (us_ref[b, m - 1 + r0:m - 1 + r0 + SUB_ROWS, :] * cw[0:1, cols]
                + us_ref[b, m + r0:m + r0 + SUB_ROWS, :] * cw[1:2, cols]
                + us_ref[b, m + 1 + r0:m + 1 + r0 + SUB_ROWS, :] * cw[2:3, cols] + cw[3:4, cols])

    for r0 in range(0, tm, SUB_ROWS):
        rows = slice(r0, r0 + SUB_ROWS)
        for c0 in range(0, c, LANES):
            x0_ref[rows, c0:c0 + LANES] = conv(r0, c0).astype(BF16)
            vx_ref[rows, c0:c0 + LANES] = (conv(r0, c + c0) * conv(r0, 2 * c + c0)).astype(BF16)


def _in_proj(x2d, sh, sc, w_c, cw, cos_t, sin_t, seq, tm):
    t, d = x2d.shape
    per_b = seq // tm
    n_u, n_q, n_k = 3 * D_HYENA, D_ATTN, 2 * D_KV
    row = lambda i: (i, 0)
    mod = lambda i: (i // per_b, 0, 0)
    pos = lambda i: (i % per_b, 0)
    x8 = x2d.reshape(t // SUBLANES, SUBLANES, d)
    g = tm // SUBLANES
    halo = lambda f: pl.BlockSpec((1, SUBLANES, d), f)
    return pl.pallas_call(
        functools.partial(_in_proj_kernel, per_b),
        grid=(t // tm,),
        in_specs=[pl.BlockSpec((tm, d), row),
                  halo(lambda i: (jnp.maximum(i * g - 1, 0), 0, 0)),
                  halo(lambda i: (jnp.minimum((i + 1) * g, t // SUBLANES - 1), 0, 0)),
                  pl.BlockSpec((1, 1, d), mod),
                  pl.BlockSpec((1, 1, d), mod),
                  _const_spec(w_c.shape),
                  _const_spec(cw.shape),
                  pl.BlockSpec((tm, LANES), pos),
                  pl.BlockSpec((tm, LANES), pos)],
        out_specs=[pl.BlockSpec((tm, D_HYENA), row),
                   pl.BlockSpec((tm, D_HYENA), row),
                   pl.BlockSpec((tm, n_q), row),
                   pl.BlockSpec((tm, n_k), row),
                   pl.BlockSpec((tm, n_k), row),
                   pl.BlockSpec((tm, d), row)],
        out_shape=[jax.ShapeDtypeStruct((t, D_HYENA), BF16),
                   jax.ShapeDtypeStruct((t, D_HYENA), BF16),
                   jax.ShapeDtypeStruct((t, n_q), BF16),
                   jax.ShapeDtypeStruct((t, n_k), BF16),
                   jax.ShapeDtypeStruct((t, n_k), BF16),
                   jax.ShapeDtypeStruct((t, d), BF16)],
        scratch_shapes=[pltpu.VMEM((n_u // LANES, tm + 2 * SUBLANES, LANES), F32)],
        compiler_params=_params(("arbitrary",)),
        name="in_proj",
    )(x2d, x8, x8, sh, sc, w_c, cw, cos_t, sin_t)


def _filt_mlp_kernel(z_ref, w1_ref, b1_ref, w2_ref, b2_ref, w3_ref, b3_ref, w4_ref, fr_ref, t_ref, absd_ref,
                     h_ref, ss_ref):
    fr = fr_ref[...]
    a = jnp.sin(fr * (_dot3(z_ref[...], w1_ref[...]) + b1_ref[...]))
    a = jnp.sin(fr * (_dot3(a, w2_ref[...]) + b2_ref[...]))
    a = jnp.sin(fr * (_dot3(a, w3_ref[...]) + b3_ref[...]))
    half = a.shape[0]
    ss = jnp.zeros(ss_ref.shape, F32)
    for k in range(2):
        decay = jnp.exp(-t_ref[k * half:(k + 1) * half, :] * absd_ref[...])
        h = _dot3(a, w4_ref[0, k]) * decay
        ss = ss + jnp.sum(h * h, axis=0, keepdims=True)
        for s in range(half // FFT_N2):
            r0 = (k * (half // FFT_N2) + s) * TPITCH
            h_ref[r0:r0 + FFT_N2, :] = h[s * FFT_N2:(s + 1) * FFT_N2]
            h_ref[r0 + FFT_N2:r0 + TPITCH, :] = jnp.zeros((TPITCH - FFT_N2, h.shape[1]), F32)

    @pl.when(pl.program_id(0) == 0)
    def _():
        ss_ref[...] = jnp.zeros_like(ss_ref)

    ss_ref[...] += ss


def _filt_mlp(zp, w1p, b1, w2, b2, w3, b3, w4s, fr, t_col, absd, rows):
    n, c = t_col.shape[0], absd.shape[1]
    half_steps = (n // 2) // rows
    vec = lambda a: pl.BlockSpec(a.shape, lambda i: (0,) * a.ndim)
    return pl.pallas_call(
        _filt_mlp_kernel,
        grid=(n // rows,),
        in_specs=[pl.BlockSpec((rows // 2, zp.shape[1]), lambda i: (i, 0)),
                  vec(w1p), vec(b1), vec(w2), vec(b2), vec(w3), vec(b3),
                  pl.BlockSpec((1,) + w4s.shape[1:], lambda i: (i // half_steps, 0, 0, 0)),
                  vec(fr),
                  pl.BlockSpec((rows, 1), lambda i: (i, 0)),
                  vec(absd)],
        out_specs=[pl.BlockSpec((rows // FFT_N2 * TPITCH, c), lambda i: (i, 0)),
                   pl.BlockSpec((1, c), lambda i: (0, 0))],
        out_shape=[jax.ShapeDtypeStruct((n // FFT_N2 * TPITCH, c), F32), jax.ShapeDtypeStruct((1, c), F32)],
        compiler_params=_params(("arbitrary",)),
        name="filt_mlp",
    )(zp, w1p, b1, w2, b2, w3, b3, w4s, fr, t_col, absd)


def _filt_fft_kernel(h_ref, ss_ref, bias_ref, fa_ref, fb_ref, o_ref, zs_ref):
    scale = lax.rsqrt(ss_ref[...] + 1e-6)
    n1 = h_ref.shape[0] // TPITCH
    row = lax.broadcasted_iota(jnp.int32, (2 * FFT_N2, LANES), 0)
    impulse = jnp.where(row < FFT_N2, bias_ref[...], 0.0)

    def stage_a(n2, carry):
        sa = h_ref[pl.ds(n2, n1, stride=TPITCH), :].astype(BF16)
        sb = h_ref[pl.ds(n2 + 1, n1, stride=TPITCH), :].astype(BF16)
        zero = jnp.zeros_like(sa)
        rhs = jnp.concatenate([jnp.concatenate([sa, zero], axis=1), jnp.concatenate([zero, sb], axis=1)], axis=0)
        z = _dot(fa_ref[n2 // 2], rhs)
        for k in range(2):
            zk = z[:, k * LANES:(k + 1) * LANES]
            zs_ref[pl.ds(n2 + k, KP, stride=ZPITCH), :] = zk[:KP]
            zs_ref[pl.ds(FFT_N2 + n2 + k, KP, stride=ZPITCH), :] = zk[KP:]
        return carry

    lax.fori_loop(0, FFT_N2 // 2, lambda p, c: stage_a(2 * p, c), 0, unroll=16)

    def stage_b(p, carry):
        b0 = pl.multiple_of(2 * p * ZPITCH, SUBLANES)
        b1 = pl.multiple_of(b0 + ZPITCH, SUBLANES)
        z = jnp.concatenate([zs_ref[pl.ds(b0, 2 * FFT_N2), :], zs_ref[pl.ds(b1, 2 * FFT_N2), :]], axis=1)
        x = _dot(fb_ref[...], z.astype(BF16))
        o_ref[0, 2 * p] = x[:, :LANES] * scale + impulse
        o_ref[0, 2 * p + 1] = x[:, LANES:] * scale + impulse
        return carry

    lax.fori_loop(0, KHP // 2, stage_b, 0, unroll=True)


def _filt_fft(h2u, ss, bias, fa_full, fb):
    n, c = h2u.shape
    nblk = c // LANES
    return pl.pallas_call(
        _filt_fft_kernel,
        grid=(nblk,),
        in_specs=[pl.BlockSpec((n, LANES), lambda j: (0, j), pipeline_mode=pl.Buffered(1)),
                  pl.BlockSpec((1, LANES), lambda j: (0, j)),
                  pl.BlockSpec((1, LANES), lambda j: (0, j)),
                  _const_spec(fa_full.shape),
                  _const_spec(fb.shape)],
        out_specs=pl.BlockSpec((1, KHP, 2 * FFT_N2, LANES), lambda j: (j, 0, 0, 0)),
        out_shape=jax.ShapeDtypeStruct((nblk, KHP, 2 * FFT_N2, LANES), F32),
        scratch_shapes=[pltpu.VMEM((KP * ZPITCH, LANES), F32)],
        compiler_params=_params(("arbitrary",)),
        name="filt_fft",
    )(h2u, ss, bias, fa_full, fb)


def _hyena_kernel(x0_ref, vx_ref, hf_ref, fa_ref, fai_ref, fb_ref, fbi_ref, o_ref, ts_ref, zs_ref):
    n_slabs = x0_ref.shape[1] // FFT_N2

    def fill(j, carry):
        rows = pl.ds(pl.multiple_of(j * FFT_N2, FFT_N2), FFT_N2)
        ts_ref[pl.ds(pl.multiple_of(j * TPITCH, SUBLANES), FFT_N2), :] = vx_ref[0, rows, :].astype(F32)
        return carry

    lax.fori_loop(0, n_slabs, fill, 0, unroll=4)

    def stage_a(n2, carry):
        sa = ts_ref[pl.ds(n2, n_slabs, stride=TPITCH), :].astype(BF16)
        sb = ts_ref[pl.ds(n2 + 1, n_slabs, stride=TPITCH), :].astype(BF16)
        zero = jnp.zeros_like(sa)
        rhs = jnp.concatenate([jnp.concatenate([sa, zero], axis=1), jnp.concatenate([zero, sb], axis=1)], axis=0)
        z = _dot(fa_ref[n2 // 2], rhs)
        for k in range(2):
            zk = z[:, k * LANES:(k + 1) * LANES]
            zs_ref[pl.ds(n2 + k, KP, stride=ZPITCH), :] = zk[:KP]
            zs_ref[pl.ds(FFT_N2 + n2 + k, KP, stride=ZPITCH), :] = zk[KP:]
        return carry

    lax.fori_loop(0, FFT_N2 // 2, lambda p, c: stage_a(2 * p, c), 0, unroll=32)

    def stage_b(p, carry):
        b0 = pl.multiple_of(2 * p * ZPITCH, SUBLANES)
        b1 = pl.multiple_of(b0 + ZPITCH, SUBLANES)
        z = jnp.concatenate([zs_ref[pl.ds(b0, 2 * FFT_N2), :], zs_ref[pl.ds(b1, 2 * FFT_N2), :]], axis=1)
        x = _dot(fb_ref[...], z.astype(BF16))
        h = jnp.concatenate([hf_ref[0, 2 * p], hf_ref[0, 2 * p + 1]], axis=1)
        xr, xi = x[:FFT_N2], x[FFT_N2:]
        hr, hi = h[:FFT_N2], h[FFT_N2:]
        prod = jnp.concatenate([xr * hr - xi * hi, xr * hi + xi * hr], axis=0)
        y = _dot(fbi_ref[...], prod.astype(BF16))
        zs_ref[pl.ds(b0, 2 * FFT_N2), :] = y[:, :LANES]
        zs_ref[pl.ds(b1, 2 * FFT_N2), :] = y[:, LANES:]
        return carry

    lax.fori_loop(0, KHP // 2, stage_b, 0, unroll=True)

    def stage_ai(n2, carry):
        yr = zs_ref[pl.ds(n2, KP, stride=ZPITCH), :]
        yi = zs_ref[pl.ds(FFT_N2 + n2, KP, stride=ZPITCH), :]
        y = jnp.concatenate([yr, yi], axis=0).astype(BF16)
        ts_ref[pl.ds(n2, n_slabs, stride=TPITCH), :] = _dot(fai_ref[n2], y)
        return carry

    lax.fori_loop(0, FFT_N2, stage_ai, 0, unroll=64)

    def finish(j, carry):
        conv = ts_ref[pl.ds(pl.multiple_of(j * TPITCH, SUBLANES), FFT_N2), :]
        rows = pl.ds(pl.multiple_of(j * FFT_N2, FFT_N2), FFT_N2)
        o_ref[0, rows, :] = (x0_ref[0, rows, :].astype(F32) * conv).astype(BF16)
        return carry

    lax.fori_loop(0, n_slabs, finish, 0, unroll=4)


def _hyena(x0c, vx, hf, fa, fai, fb, fbi):
    b, seq, _ = x0c.shape
    nblk = D_HYENA // LANES
    n_slabs = seq // FFT_N2
    stream = pl.BlockSpec((1, seq, LANES), lambda j, i: (i, 0, j))
    return pl.pallas_call(
        _hyena_kernel,
        grid=(nblk, b),
        in_specs=[stream, stream,
                  pl.BlockSpec((1, KHP, 2 * FFT_N2, LANES), lambda j, i: (j, 0, 0, 0),
                               pipeline_mode=pl.Buffered(1)),
                  _const_spec(fa.shape), _const_spec(fai.shape),
                  _const_spec(fb.shape), _const_spec(fbi.shape)],
        out_specs=pl.BlockSpec((1, seq, LANES), lambda j, i: (i, 0, j)),
        out_shape=jax.ShapeDtypeStruct((b, seq, D_HYENA), BF16),
        scratch_shapes=[pltpu.VMEM((n_slabs * TPITCH, LANES), F32),
                        pltpu.VMEM((KP * ZPITCH, LANES), F32)],
        compiler_params=_params(("arbitrary", "arbitrary")),
        name="hyena",
    )(x0c, vx, hf, fa, fai, fb, fbi)


def _attn_kernel(sink_ref, q_ref, kp_ref, km_ref, kn_ref, vp_ref, vm_ref, vn_ref, kc_ref, vc_ref,
                 o_ref, ka_ref, va_ref):
    i = pl.program_id(1)
    n_i = pl.num_programs(1)
    tq = q_ref.shape[0]
    nqb = tq // BLOCK
    ka_ref[0:BLOCK] = kp_ref[...]
    ka_ref[BLOCK:BLOCK + tq] = km_ref[...]
    ka_ref[BLOCK + tq:] = kn_ref[...]
    va_ref[0:BLOCK] = vp_ref[...]
    va_ref[BLOCK:BLOCK + tq] = vm_ref[...]
    va_ref[BLOCK + tq:] = vn_ref[...]

    qi = lax.broadcasted_iota(jnp.int32, (BLOCK, BLOCK), 0)
    kj = lax.broadcasted_iota(jnp.int32, (BLOCK, BLOCK), 1)
    lane = lax.broadcasted_iota(jnp.int32, (BLOCK, LANES), 1)
    low = lane < HEAD_DIM
    half = GQA_GROUP // 2
    rows2 = half * BLOCK
    hrow = lax.broadcasted_iota(jnp.int32, (GQA_GROUP * BLOCK, 1), 0) // BLOCK
    head_order = [hh for hh in range(GQA_GROUP) if hh % 2 == 0] + [hh for hh in range(GQA_GROUP) if hh % 2 == 1]
    one = jnp.ones((), BF16)

    def with_ones(v):
        lanes_low = lax.broadcasted_iota(jnp.int32, v.shape, 1) < HEAD_DIM
        return jnp.where(lanes_low, v, one), jnp.where(lanes_low, one, v)

    for g in range(N_KV_HEADS):
        gl = slice(g * LANES, (g + 1) * LANES)
        vc_even, vc_odd = with_ones(vc_ref[:, gl])
        sink = jnp.zeros((GQA_GROUP * BLOCK, 1), F32)
        for pos, hh in enumerate(head_order):
            sink = jnp.where(hrow == pos, sink_ref[g * GQA_GROUP + hh], sink)
        for j in range(nqb):
            prev_ok = kj >= qi
            next_ok = kj <= qi
            if j == 0:
                prev_ok = prev_ok & (i > 0)
            if j == nqb - 1:
                next_ok = next_ok & (i < n_i - 1)
            bias_p = jnp.concatenate([jnp.where(prev_ok, 0.0, NEG_INF).astype(F32)] * GQA_GROUP, axis=0)
            bias_n = jnp.concatenate([jnp.where(next_ok, 0.0, NEG_INF).astype(F32)] * GQA_GROUP, axis=0)
            qb = q_ref[j * BLOCK:(j + 1) * BLOCK, :]
            parts = []
            for hh in head_order:
                h = g * GQA_GROUP + hh
                qp = qb[:, (h // 2) * LANES:(h // 2 + 1) * LANES]
                parts.append(jnp.where(low if h % 2 == 0 else ~low, qp, jnp.zeros_like(qp)))
            qs = jnp.concatenate(parts, axis=0)
            kw = ka_ref[j * BLOCK:(j + 3) * BLOCK, gl]
            vw_even, vw_odd = with_ones(va_ref[j * BLOCK:(j + 3) * BLOCK, gl])
            s_w = _dot_nt(qs, kw)
            s_p = s_w[:, :BLOCK] + bias_p
            s_m = s_w[:, BLOCK:2 * BLOCK]
            s_n = s_w[:, 2 * BLOCK:] + bias_n
            s_c = _dot_nt(qs, kc_ref[:, gl])
            m = jnp.maximum(jnp.maximum(jnp.max(jnp.maximum(jnp.maximum(s_p, s_m), s_n), axis=1, keepdims=True),
                                        jnp.max(s_c, axis=1, keepdims=True)), sink)
            e_w = jnp.concatenate([jnp.exp(s_p - m), jnp.exp(s_m - m), jnp.exp(s_n - m)], axis=1).astype(BF16)
            e_c = jnp.exp(s_c - m).astype(BF16)
            e_sink = jnp.exp(sink - m)
            outs = []
            for par, (vw, vcx) in enumerate(((vw_even, vc_even), (vw_odd, vc_odd))):
                rs = slice(par * rows2, (par + 1) * rows2)
                acc = _dot(e_w[rs], vw) + _dot(e_c[rs], vcx)
                den = pltpu.roll(acc, HEAD_DIM, 1) + e_sink[rs]
                outs.append(acc / den)
            for pp in range(half):
                pair = jnp.where(low, outs[0][pp * BLOCK:(pp + 1) * BLOCK], outs[1][pp * BLOCK:(pp + 1) * BLOCK])
                col = (g * half + pp) * LANES
                o_ref[j * BLOCK:(j + 1) * BLOCK, col:col + LANES] = pair.astype(BF16)


def _attention(sinks, q, kd, vd, kc, vc, batch, seq, tq):
    t = q.shape[0]
    per_b = seq // tq
    nqb = tq // BLOCK
    nb = seq // BLOCK
    n_ctx = kc.shape[0] // batch
    main = lambda b, i: (b * per_b + i, 0)
    prev = lambda b, i: (b * nb + jnp.maximum(i * nqb - 1, 0), 0)
    nxt = lambda b, i: (b * nb + jnp.minimum(i * nqb + nqb, nb - 1), 0)
    kvw = kd.shape[1]
    return pl.pallas_call(
        _attn_kernel,
        grid=(batch, per_b),
        in_specs=[pl.BlockSpec(memory_space=pltpu.SMEM),
                  pl.BlockSpec((tq, D_ATTN), main),
                  pl.BlockSpec((BLOCK, kvw), prev), pl.BlockSpec((tq, kvw), main), pl.BlockSpec((BLOCK, kvw), nxt),
                  pl.BlockSpec((BLOCK, kvw), prev), pl.BlockSpec((tq, kvw), main), pl.BlockSpec((BLOCK, kvw), nxt),
                  pl.BlockSpec((n_ctx, kvw), lambda b, i: (b, 0)),
                  pl.BlockSpec((n_ctx, kvw), lambda b, i: (b, 0))],
        out_specs=pl.BlockSpec((tq, D_ATTN), main),
        out_shape=jax.ShapeDtypeStruct((t, D_ATTN), BF16),
        scratch_shapes=[pltpu.VMEM((tq + 2 * BLOCK, kvw), BF16),
                        pltpu.VMEM((tq + 2 * BLOCK, kvw), BF16)],
        compiler_params=_params(("arbitrary", "arbitrary")),
        name="attention",
    )(sinks, q, kd, kd, kd, vd, vd, vd, kc, vc)


def _route(t, wr_hi_ref, wr_hl_ref, br_ref):
    th, tl = _split(t)
    both = _dot(th, wr_hl_ref[...])
    logits = both[:, :LANES] + both[:, LANES:] + _dot(tl, wr_hi_ref[...]) + br_ref[...]
    lane_i = lax.broadcasted_iota(jnp.int32, logits.shape, 1)
    lane = lane_i.astype(F32)
    grp_of_lane = (lane_i >> 2).astype(F32)
    ninf = -jnp.inf
    far = float(LANES)
    is_g = (lane_i >= N_EXPERTS) & (lane_i < N_EXPERTS + N_GROUPS)
    glog = jnp.where(is_g, logits, ninf)
    gmax = jnp.max(glog, axis=1, keepdims=True)
    gidx = jnp.min(jnp.where(glog == gmax, lane - float(N_EXPERTS), far), axis=1, keepdims=True)
    group_p = 1.0 / jnp.sum(jnp.exp(glog - gmax), axis=1, keepdims=True)
    in_grp = (lane_i < N_EXPERTS) & (grp_of_lane == gidx)
    elog = jnp.where(in_grp, logits, ninf)
    v1 = jnp.max(elog, axis=1, keepdims=True)
    i1 = jnp.min(jnp.where(elog == v1, lane, far), axis=1, keepdims=True)
    elog2 = jnp.where(lane == i1, ninf, elog)
    v2 = jnp.max(elog2, axis=1, keepdims=True)
    i2 = jnp.min(jnp.where(elog2 == v2, lane, far), axis=1, keepdims=True)
    e = jnp.exp(v2 - v1)
    w1 = group_p / (1.0 + e)
    w2 = group_p * e / (1.0 + e)
    gate = jnp.where(lane == i1, w1, 0.0) + jnp.where(lane == i2, w2, 0.0)
    return gate + jnp.where(lane_i == GROUP_LANE, gidx, 0.0)


def _mix_out_kernel(x_ref, h_ref, yh_ref, ya_ref, g1_ref, sh2_ref, sc2_ref,
                    wg_ref, wbh_ref, wba_ref, wo_ref, lng_ref, lnb_ref, wrh_ref, wrhl_ref, br_ref,
                    x1_ref, gate_ref, gsel_ref, *t_refs):
    d = x_ref.shape[1]
    for r0 in range(0, x_ref.shape[0], SUB_ROWS):
        rows = slice(r0, r0 + SUB_ROWS)
        x = x_ref[rows, :]
        h = h_ref[rows, :]
        g_hy = jax.nn.sigmoid(_dot(h, wg_ref[:, :d]))
        merged = g_hy * _dot(yh_ref[rows, :], wbh_ref[...])
        g_at = jax.nn.sigmoid(_dot(h, wg_ref[:, d:]))
        merged = merged + g_at * _dot(ya_ref[rows, :], wba_ref[...])
        mix = _dot(merged.astype(BF16), wo_ref[...])
        x1 = _standardize(DEEPNORM_ALPHA * x + g1_ref[0] * mix) * lng_ref[...] + lnb_ref[...]
        x1_ref[rows, :] = x1
        t = _standardize(x1) * (1.0 + sc2_ref[0]) + sh2_ref[0]
        gate = _route(t, wrh_ref, wrhl_ref, br_ref)
        gate_ref[rows, :] = gate
        grp = gate.T[GROUP_LANE:GROUP_LANE + 1, :]
        gsel_ref[r0 // LANES:(r0 + SUB_ROWS) // LANES, :] = jnp.concatenate(
            [grp[:, k * LANES:(k + 1) * LANES] for k in range(SUB_ROWS // LANES)], axis=0)
        for t_ref, piece in zip(t_refs, _pack_pieces(t)):
            t_ref[rows, :] = piece


def _mix_out(x2d, h, yh, ya, mods, w_g, w_bh, w_ba, w_o, ln_g, ln_b, wr_hi, wr_hl, br, seq, tm):
    t, d = x2d.shape
    per_b = seq // tm
    row = lambda i: (i, 0)
    mod = lambda i: (i // per_b, 0, 0)
    mspec = pl.BlockSpec((1, 1, d), mod)
    g1, sh2, sc2 = mods
    return pl.pallas_call(
        _mix_out_kernel,
        grid=(t // tm,),
        in_specs=[pl.BlockSpec((tm, d), row),
                  pl.BlockSpec((tm, d), row),
                  pl.BlockSpec((tm, yh.shape[1]), row),
                  pl.BlockSpec((tm, ya.shape[1]), row),
                  mspec, mspec, mspec,
                  _const_spec(w_g.shape), _const_spec(w_bh.shape), _const_spec(w_ba.shape),
                  _const_spec(w_o.shape), _const_spec(ln_g.shape), _const_spec(ln_b.shape),
                  _const_spec(wr_hi.shape), _const_spec(wr_hl.shape), _const_spec(br.shape)],
        out_specs=[pl.BlockSpec((tm, d), row), pl.BlockSpec((tm, LANES), row),
                   pl.BlockSpec((tm // LANES, LANES), row)]
        + [pl.BlockSpec((tm, SC_ROW_WORDS), row)] * N_PIECES,
        out_shape=[jax.ShapeDtypeStruct((t, d), F32), jax.ShapeDtypeStruct((t, LANES), F32),
                   jax.ShapeDtypeStruct((t // LANES, LANES), F32)]
        + [jax.ShapeDtypeStruct((t, SC_ROW_WORDS), jnp.uint32)] * N_PIECES,
        compiler_params=_params(("arbitrary",)),
        name="mix_out",
    )(x2d, h, yh, ya, g1, sh2, sc2, w_g, w_bh, w_ba, w_o, ln_g, ln_b, wr_hi, wr_hl, br)


def _slots_kernel(g_ref, upper_ref, lower_ref, dest_ref, meta_ref):
    gsel = g_ref[...]
    dest = jnp.zeros(gsel.shape, F32)
    base = jnp.zeros((1, 1), F32)
    chunk_start = lax.broadcasted_iota(jnp.int32, (1, LANES), 1).astype(F32) * float(MOE_CHUNK)
    owner = jnp.zeros((1, LANES), F32)
    for g in range(N_GROUPS):
        onehot = jnp.where(gsel == float(g), 1.0, 0.0)
        in_row = _dot(onehot.astype(BF16), upper_ref[...])
        row_tot = jnp.sum(onehot, axis=1, keepdims=True)
        rows_before = _dot(lower_ref[...], jnp.broadcast_to(row_tot, onehot.shape).astype(BF16))
        dest = dest + onehot * (base + rows_before + in_row)
        if g > 0:
            owner = owner + jnp.where(chunk_start >= base, 1.0, 0.0)
        n_g = jnp.sum(row_tot, axis=0, keepdims=True)
        base = base + jnp.floor((n_g + float(MOE_CHUNK - 1)) * (1.0 / MOE_CHUNK)) * float(MOE_CHUNK)
    dest_ref[...] = dest.astype(jnp.int32)
    row = lax.broadcasted_iota(jnp.int32, meta_ref.shape, 0)
    meta_ref[...] = jnp.where(row == 0, owner, base * (1.0 / MOE_CHUNK)).astype(jnp.int32)


def _slots(gsel):
    r = gsel.shape[0]
    upper = jnp.asarray(np.triu(np.ones((LANES, LANES), np.float32), 1)).astype(BF16)
    lower = jnp.asarray(np.tril(np.ones((r, r), np.float32), -1)).astype(BF16)
    full = lambda a: pl.BlockSpec(a.shape, lambda i: (0,) * a.ndim)
    return pl.pallas_call(
        _slots_kernel,
        grid=(1,),
        in_specs=[full(gsel), full(upper), full(lower)],
        out_specs=[pl.BlockSpec((r, LANES), lambda i: (0, 0)), pl.BlockSpec((SUBLANES, LANES), lambda i: (0, 0))],
        out_shape=[jax.ShapeDtypeStruct((r, LANES), jnp.int32), jax.ShapeDtypeStruct((SUBLANES, LANES), jnp.int32)],
        compiler_params=_params(("arbitrary",)),
        name="moe_slots",
    )(gsel, upper, lower)


def _experts_kernel(owner_ref, used_ref, *refs):
    n_p = N_PIECES
    t_refs, gate_ref = refs[:n_p], refs[n_p]
    wg_ref, wu_ref, wd_ref = refs[n_p + 1:n_p + 4]
    y_refs = refs[n_p + 4:2 * n_p + 4]
    wg16_ref, wu16_ref, wd16_ref = refs[2 * n_p + 4:]
    c = pl.program_id(0)
    f = D_EXPERT

    @pl.when((c == 0) | (owner_ref[c] != owner_ref[jnp.maximum(c - 1, 0)]))
    def _():
        wg16_ref[...] = wg_ref[...].astype(BF16)
        wu16_ref[...] = wu_ref[...].astype(BF16)
        for e in range(EXPERTS_PER_GROUP):
            wd16_ref[e * f:(e + 1) * f, :] = wd_ref[e].astype(BF16)

    @pl.when(c < used_ref[0])
    def _():
        t = _unpack_pieces([r[...] for r in t_refs]).astype(BF16)
        gate = gate_ref[...]
        lane = lax.broadcasted_iota(jnp.int32, gate.shape, 1)
        first = owner_ref[c] * EXPERTS_PER_GROUP
        parts = []
        for e in range(EXPERTS_PER_GROUP):
            a = _dot(t, wg16_ref[e])
            u = _dot(t, wu16_ref[e])
            ge = jnp.sum(jnp.where(lane == first + e, gate, 0.0), axis=1, keepdims=True)
            parts.append((a * jax.nn.sigmoid(a) * u * ge).astype(BF16))
        y = _dot(jnp.concatenate(parts, axis=1), wd16_ref[...])
        for y_ref, piece in zip(y_refs, _pack_pieces(y)):
            y_ref[...] = piece

    @pl.when(c >= used_ref[0])
    def _():
        for y_ref in y_refs:
            y_ref[...] = jnp.zeros_like(y_ref)


def _experts(owner, used, t_pieces, gate_sorted, w_gate, w_up, w_down):
    n_slots = gate_sorted.shape[0]
    d, f = w_gate.shape[1:]
    row = lambda c, owner, used: (c, 0)
    by_owner = lambda c, owner, used: (owner[c], 0, 0)
    piece = pl.BlockSpec((MOE_CHUNK, SC_ROW_WORDS), row)
    group_of_experts = lambda w: pl.BlockSpec((EXPERTS_PER_GROUP,) + w.shape[1:], by_owner)
    return pl.pallas_call(
        _experts_kernel,
        grid_spec=pltpu.PrefetchScalarGridSpec(
            num_scalar_prefetch=2,
            grid=(n_slots // MOE_CHUNK,),
            in_specs=[piece] * len(t_pieces) + [pl.BlockSpec((MOE_CHUNK, LANES), row),
                                                group_of_experts(w_gate), group_of_experts(w_up),
                                                group_of_experts(w_down)],
            out_specs=[piece] * len(t_pieces),
            scratch_shapes=[pltpu.VMEM((EXPERTS_PER_GROUP, d, f), BF16),
                            pltpu.VMEM((EXPERTS_PER_GROUP, d, f), BF16),
                            pltpu.VMEM((EXPERTS_PER_GROUP * f, d), BF16)],
        ),
        out_shape=[jax.ShapeDtypeStruct((n_slots, SC_ROW_WORDS), jnp.uint32)] * len(t_pieces),
        compiler_params=_params(("arbitrary",)),
        name="moe_experts",
    )(owner, used, *t_pieces, gate_sorted, w_gate, w_up, w_down)


def _final_kernel(x1_ref, g2_ref, lng_ref, lnb_ref, *refs):
    y_refs, o_ref = refs[:-1], refs[-1]
    y = _unpack_pieces([r[...] for r in y_refs])
    o_ref[...] = _standardize(DEEPNORM_ALPHA * x1_ref[...] + g2_ref[0] * y) * lng_ref[...] + lnb_ref[...]


def _final(x1, g2, ln_g, ln_b, y_pieces, seq, tm):
    t, d = x1.shape
    per_b = seq // tm
    row = lambda i: (i, 0)
    return pl.pallas_call(
        _final_kernel,
        grid=(t // tm,),
        in_specs=[pl.BlockSpec((tm, d), row), pl.BlockSpec((1, 1, d), lambda i: (i // per_b, 0, 0)),
                  _const_spec(ln_g.shape), _const_spec(ln_b.shape)]
        + [pl.BlockSpec((tm, SC_ROW_WORDS), row)] * len(y_pieces),
        out_specs=pl.BlockSpec((tm, d), row),
        out_shape=jax.ShapeDtypeStruct((t, d), F32),
        compiler_params=_params(("arbitrary",)),
        name="moe_final",
    )(x1, g2, ln_g, ln_b, *y_pieces)


def _sc_move_rows(tables, idx, n_out, scatter, name):
    n = idx.shape[0]
    mesh = plsc.VectorSubcoreMesh(core_axis_name="c", subcore_axis_name="s")
    out_type = [jax.ShapeDtypeStruct((n_out, t.shape[1]), t.dtype) for t in tables]
    window = lambda i: (i, 0)
    index_win = pl.BlockSpec((1, SC_WINDOW), lambda i: (0, i))
    split = dict(core_axis_name=("c", "s"), dimension_semantics=(pltpu.PARALLEL,))

    @functools.partial(pl.kernel, out_type=out_type, mesh=mesh, scratch_types=[], name=name)
    def move(*refs):
        srcs, i_hbm, dsts = refs[:len(tables)], refs[len(tables)], refs[len(tables) + 1:]
        for src, dst in zip(srcs, dsts):
            rows_win = pl.BlockSpec((SC_WINDOW, src.shape[1]), window)
            if scatter:
                def body(x_vmem, i_vmem, dst=dst):
                    pltpu.sync_copy(x_vmem, dst.at[i_vmem.at[0]])

                pltpu.emit_pipeline(body, grid=(n // SC_WINDOW,), in_specs=[rows_win, index_win],
                                    out_specs=[], **split)(src, i_hbm)
            else:
                def body(i_vmem, o_vmem, src=src):
                    pltpu.sync_copy(src.at[i_vmem.at[0]], o_vmem)

                pltpu.emit_pipeline(body, grid=(n // SC_WINDOW,), in_specs=[index_win],
                                    out_specs=[rows_win], **split)(i_hbm, dst)

    return move(*tables, idx.reshape(1, n))


def _dft_tables(seq):
    n = 2 * seq
    n1_full = n // FFT_N2
    n1_data = seq // FFT_N2
    k1 = np.arange(KH, dtype=np.float64)[None, :, None]
    n2 = np.arange(FFT_N2, dtype=np.float64)[:, None, None]
    n1 = np.arange(n1_full, dtype=np.float64)[None, None, :]
    ang = 2.0 * np.pi * k1 * (FFT_N2 * n1 + n2) / n
    fa = np.zeros((FFT_N2, 2 * KP, n1_full))
    fa[:, :KH] = np.cos(ang)
    fa[:, KP:KP + KH] = -np.sin(ang)
    wgt = np.full((KH,), 2.0)
    wgt[0] = 1.0
    wgt[KH - 1] = 1.0
    fai = np.zeros((FFT_N2, n1_data, 2 * KP))
    angt = np.transpose(ang[:, :, :n1_data], (0, 2, 1))
    fai[:, :, :KH] = np.cos(angt) * wgt / n
    fai[:, :, KP:KP + KH] = -np.sin(angt) * wgt / n
    kk = np.arange(FFT_N2, dtype=np.float64)
    a2 = 2.0 * np.pi * np.outer(kk, kk) / FFT_N2
    fr, fi = np.cos(a2), -np.sin(a2)
    fb = np.block([[fr, -fi], [fi, fr]])
    fbi = np.block([[fr, fi], [-fi, fr]])
    f32 = lambda a: jnp.asarray(a.astype(np.float32))
    pairs = lambda a: np.concatenate([a[0::2], a[1::2]], axis=2)
    return f32(pairs(fa[:, :, :n1_data])), f32(pairs(fa)), f32(fai), f32(fb), f32(fbi)


def _filter_features(seq, rows):
    f32 = np.float32
    t = np.linspace(0.0, 1.0, seq, dtype=f32)
    w = (f32(2.0 * math.pi) * np.arange(seq, dtype=f32) / f32(seq)).astype(f32)
    t2 = np.concatenate([t, t[::-1]])
    w2 = np.concatenate([w, w[::-1]])
    bands = np.linspace(1e-4, FILTER_BANDS - 1, FILTER_BANDS, dtype=f32)
    max_decay = math.log(DECAY_TARGET) / FAST_DECAY_PCT
    min_decay = math.log(DECAY_TARGET) / SLOW_DECAY_PCT
    deltas = jnp.linspace(min_decay, max_decay, D_HYENA, dtype=F32)

    def feats(k):
        tp = t2.reshape(-1, 2, rows // 2)[:, k].reshape(-1, 1)
        wp = w2.reshape(-1, 2, rows // 2)[:, k].reshape(-1, 1)
        pad = np.zeros((tp.shape[0], FILTER_ORDER - FILTER_EMB), f32)
        arg = (bands * wp).astype(f32)
        return np.concatenate([tp, np.cos(arg), -np.sin(arg), pad], axis=-1).astype(f32)

    zp = jnp.asarray(np.concatenate([feats(0), feats(1)], axis=-1))
    return zp, jnp.asarray(t2)[:, None], jnp.abs(deltas)[None, :]


def _rope_tables(seq):
    f32 = np.float32
    rows = seq // GRID_W
    row = np.repeat(np.arange(rows, dtype=f32), GRID_W)
    col = np.tile(np.arange(GRID_W, dtype=f32), rows)
    half = HEAD_DIM // 2
    inv_freq = (f32(ROPE_BASE) ** (-np.arange(0, half, 2, dtype=f32) / f32(half))).astype(f32)
    ang = np.concatenate([row[:, None] * inv_freq, col[:, None] * inv_freq], axis=-1).astype(f32)
    cos, sin = np.cos(ang), np.sin(ang)
    c64 = np.concatenate([cos, cos], axis=-1)
    s64 = np.concatenate([-sin, sin], axis=-1)
    return (jnp.asarray(np.concatenate([c64, c64], axis=-1).astype(f32)),
            jnp.asarray(np.concatenate([s64, s64], axis=-1).astype(f32)))


def _head_perm(n_heads):
    idx = []
    for h in range(n_heads):
        base = h * HEAD_DIM
        idx += [base + 2 * j for j in range(HEAD_DIM // 2)]
        idx += [base + 2 * j + 1 for j in range(HEAD_DIM // 2)]
    return np.asarray(idx, dtype=np.int32)


def _dup_heads(w):
    parts = []
    for g in range(N_KV_HEADS):
        blk = w[:, g * HEAD_DIM:(g + 1) * HEAD_DIM]
        parts += [blk, blk]
    return jnp.concatenate(parts, axis=1)


def kernel(x, c, ctx, c_ctx, ada_w, ada_b, w_in, hy_conv_w, hy_conv_b, hy_w1, hy_b1, hy_w2, hy_b2, hy_w3, hy_b3, hy_w4, hy_freq, hy_bias, attn_sinks, w_branch_hy, w_branch_attn, w_out, ln1_g, ln1_b, w_group, b_group, w_router, b_router, w_gate_e, w_up_e, w_down_e, ln2_g, ln2_b):
    batch, seq, d = x.shape
    n_ctx = ctx.shape[1]
    assert d == D_MODEL and ada_w.shape[0] == DEPTH == 1
    assert 2 * seq == FFT_N2 * FFT_N2 and seq % TM_PROJ == 0
    l = 0

    w = w_in[l]
    s0, s1, s2, s3 = 3 * D_HYENA, 3 * D_HYENA + D_ATTN, 3 * D_HYENA + D_ATTN + D_KV, 3 * D_HYENA + D_ATTN + 2 * D_KV
    w_q = w[:, s0:s1][:, _head_perm(N_HEADS)]
    w_k = _dup_heads(w[:, s1:s2][:, _head_perm(N_KV_HEADS)])
    w_v = _dup_heads(w[:, s2:s3])
    w_c = jnp.concatenate([w[:, :s0], w_q, w_k, w_v], axis=1).astype(BF16)
    w_kv = jnp.concatenate([w_k, w_v], axis=1).astype(BF16)
    w_g = w[:, s3:].astype(BF16)
    w_bh = w_branch_hy[l].astype(BF16)
    w_ba = w_branch_attn[l].astype(BF16)
    w_o = w_out[l].astype(BF16)
    pad_lanes = LANES - N_EXPERTS - N_GROUPS
    wr = jnp.concatenate([w_router[l], w_group[l], jnp.zeros((d, pad_lanes), F32)], axis=1)
    wr_hi = wr.astype(BF16)
    wr_hl = jnp.concatenate([wr_hi, (wr - wr_hi.astype(F32)).astype(BF16)], axis=1)
    br = jnp.concatenate([b_router[l], b_group[l], jnp.zeros((pad_lanes,), F32)])[None, :]
    cw = jnp.concatenate([hy_conv_w[l][:, 0, :], hy_conv_b[l][None, :],
                          jnp.zeros((SUBLANES - SHORT_CONV - 1, 3 * D_HYENA), F32)], axis=0)
    row2 = lambda a: a.reshape(1, -1)
    pair = lambda a: jnp.concatenate([a, a], axis=-1)
    zero_o = jnp.zeros((FILTER_ORDER, FILTER_ORDER), F32)
    bdiag = lambda a: jnp.concatenate([jnp.concatenate([a, zero_o], axis=1),
                                       jnp.concatenate([zero_o, a], axis=1)], axis=0)
    w1p = bdiag(jnp.concatenate([hy_w1[l], jnp.zeros((FILTER_ORDER - FILTER_EMB, FILTER_ORDER), F32)], axis=0))
    w4h = jnp.transpose(hy_w4[l].reshape(FILTER_ORDER, 2, D_HYENA), (1, 0, 2))
    zero_w4 = jnp.zeros_like(w4h)
    w4s = jnp.stack([jnp.concatenate([w4h, zero_w4], axis=1),
                     jnp.concatenate([zero_w4, w4h], axis=1)], axis=1)

    fa, fa_full, fai, fb, fbi = _dft_tables(seq)
    fa, fa_full, fai, fb, fbi = (a.astype(BF16) for a in (fa, fa_full, fai, fb, fbi))
    zp, t_col, absd = _filter_features(seq, FILT_ROWS)
    cos_t, sin_t = _rope_tables(seq)

    cond = jnp.concatenate([c, c_ctx[None], jnp.zeros((SUBLANES - batch - 1, d), F32)], axis=0)
    mods = _adaln(cond, ada_w[l], ada_b[l])
    m6 = [mods[:, k * d:(k + 1) * d].reshape(SUBLANES, 1, d) for k in range(6)]
    sh1, sc1, g1, sh2, sc2, g2 = m6

    kc, vc = _ctx_kv(ctx.reshape(batch * n_ctx, d), sh1[batch], sc1[batch], w_kv, n_ctx)

    x2d = x.reshape(batch * seq, d)
    x0c, vx, q, kd, vd, h_mod = _in_proj(x2d, sh1, sc1, w_c, cw, cos_t, sin_t, seq, TM_PROJ)
    h2u, ss = _filt_mlp(zp, w1p, pair(row2(hy_b1[l])), bdiag(hy_w2[l]), pair(row2(hy_b2[l])),
                        bdiag(hy_w3[l]), pair(row2(hy_b3[l])), w4s, pair(row2(hy_freq[l])), t_col, absd, FILT_ROWS)
    hf = _filt_fft(h2u, ss, row2(hy_bias[l]), fa_full, fb)
    y_hy = _hyena(x0c.reshape(batch, seq, D_HYENA), vx.reshape(batch, seq, D_HYENA), hf, fa, fai, fb, fbi)
    y_at = _attention(attn_sinks[l], q, kd, vd, kc, vc, batch, seq, TQ_ATTN)
    x1, gate, gsel, *t_pieces = _mix_out(x2d, h_mod, y_hy.reshape(batch * seq, D_HYENA), y_at, (g1, sh2, sc2),
                                         w_g, w_bh, w_ba, w_o, row2(ln1_g[l]), row2(ln1_b[l]), wr_hi, wr_hl, br,
                                         seq, TM_PROJ)

    n_tok = batch * seq
    n_slots = n_tok + N_GROUPS * MOE_CHUNK
    dest2d, meta = _slots(gsel)
    dest = dest2d.reshape(n_tok)
    *t_sorted, gate_sorted = _sc_move_rows(t_pieces + [gate], dest, n_slots, True, "moe_sort")
    y_sorted = _experts(meta[0], meta[1, :1], t_sorted, gate_sorted, w_gate_e[l], w_up_e[l], w_down_e[l])
    y_pieces = _sc_move_rows(y_sorted, dest, n_tok, False, "moe_unsort")
    out = _final(x1, g2, row2(ln2_g[l]), row2(ln2_b[l]), y_pieces, seq, TM_FINAL)
    return out.reshape(batch, seq, d)
```

```python
import functools
import math

import numpy as np
import jax
import jax.numpy as jnp
from jax import lax
from jax.experimental import pallas as pl
from jax.experimental.pallas import tpu as pltpu
from jax.experimental.pallas import tpu_sc as plsc

F32 = jnp.float32
BF16 = jnp.bfloat16

D_MODEL = 1024
GRID_W = 64
D_HYENA = D_MODEL // 2
SHORT_CONV = 3
FILTER_BANDS = 16
FILTER_EMB = 1 + 2 * FILTER_BANDS
FILTER_ORDER = 64
DECAY_TARGET = 1e-2
FAST_DECAY_PCT = 0.3
SLOW_DECAY_PCT = 1.5
HEAD_DIM = 64
D_ATTN = D_MODEL // 2
N_HEADS = D_ATTN // HEAD_DIM
N_KV_HEADS = N_HEADS // 4
GQA_GROUP = N_HEADS // N_KV_HEADS
D_KV = N_KV_HEADS * HEAD_DIM
WINDOW = 128
BLOCK = 128
ROPE_BASE = 10000.0
NEG_INF = -1e30
N_GROUPS = 4
EXPERTS_PER_GROUP = 4
N_EXPERTS = N_GROUPS * EXPERTS_PER_GROUP
D_EXPERT = D_MODEL // 4
LN_EPS = 1e-5
DEPTH = 1
DEEPNORM_ALPHA = (2.0 * DEPTH) ** 0.25

LANES = 128
SUBLANES = 8
VMEM_LIMIT = 56 * 1024 * 1024

SUB_ROWS = 512
TM_PROJ = 1024
MOE_CHUNK = 512
TM_FINAL = 1024
GROUP_LANE = N_EXPERTS
SC_WINDOW = 128
SC_ROW_WORDS = 256
N_PIECES = D_MODEL // (2 * SC_ROW_WORDS)
TQ_ATTN = 1024
FILT_ROWS = 2048

FFT_N2 = 128
KH = 65
KHP = 66
KP = 72
ZPITCH = 2 * FFT_N2 + SUBLANES
TPITCH = FFT_N2 + SUBLANES


def _dot(a, b):
    return jnp.dot(a, b, preferred_element_type=F32)


def _dot_nt(a, b):
    return lax.dot_general(a, b, (((1,), (1,)), ((), ())), preferred_element_type=F32)


def _split(a):
    hi = a.astype(BF16)
    lo = (a - hi.astype(F32)).astype(BF16)
    return hi, lo


def _pack_pieces(x):
    w = SC_ROW_WORDS
    pieces = []
    for p in range(x.shape[1] // (2 * w)):
        hi = lax.bitcast_convert_type(x[:, 2 * p * w:(2 * p + 1) * w].astype(BF16).astype(F32), jnp.uint32)
        lo = lax.bitcast_convert_type(x[:, (2 * p + 1) * w:(2 * p + 2) * w].astype(BF16).astype(F32), jnp.uint32)
        pieces.append(hi | (lo >> 16))
    return pieces


def _unpack_pieces(pieces):
    cols = []
    for word in pieces:
        cols.append(lax.bitcast_convert_type(word & jnp.uint32(0xFFFF0000), F32))
        cols.append(lax.bitcast_convert_type(word << 16, F32))
    return jnp.concatenate(cols, axis=1)


def _dot3(a, b):
    ah, al = _split(a)
    bh, bl = _split(b)
    return _dot(ah, bh) + _dot(al, bh) + _dot(ah, bl)


def _standardize(x):
    mu = jnp.mean(x, axis=-1, keepdims=True)
    xc = x - mu
    var = jnp.mean(xc * xc, axis=-1, keepdims=True)
    return xc * lax.rsqrt(var + LN_EPS)


def _params(sem, vmem=VMEM_LIMIT):
    return pltpu.CompilerParams(dimension_semantics=sem, vmem_limit_bytes=vmem)


def _const_spec(shape):
    nd = len(shape)
    return pl.BlockSpec(shape, lambda *_: (0,) * nd, pipeline_mode=pl.Buffered(1))


def _adaln_kernel(c_ref, w_ref, b_ref, o_ref):
    s = c_ref[...]
    s = s * jax.nn.sigmoid(s)
    o_ref[...] = _dot3(s, w_ref[...]) + b_ref[...]


def _adaln(cond, w, b):
    n, d = cond.shape
    cols = w.shape[1]
    bc = 1024
    return pl.pallas_call(
        _adaln_kernel,
        grid=(cols // bc,),
        in_specs=[pl.BlockSpec((n, d), lambda j: (0, 0)),
                  pl.BlockSpec((d, bc), lambda j: (0, j)),
                  pl.BlockSpec((1, bc), lambda j: (0, j))],
        out_specs=pl.BlockSpec((n, bc), lambda j: (0, j)),
        out_shape=jax.ShapeDtypeStruct((n, cols), F32),
        compiler_params=_params(("arbitrary",)),
        name="adaln",
    )(cond, w, b.reshape(1, cols))


def _ctx_kv_kernel(x_ref, sh_ref, sc_ref, w_ref, k_ref, v_ref):
    h = _standardize(x_ref[...]) * (1.0 + sc_ref[...]) + sh_ref[...]
    kv = _dot(h.astype(BF16), w_ref[...])
    half = k_ref.shape[1]
    k_ref[...] = kv[:, :half].astype(BF16)
    v_ref[...] = kv[:, half:].astype(BF16)


def _ctx_kv(ctx2d, sh, sc, w_kv, rows):
    n, d = ctx2d.shape
    half = w_kv.shape[1] // 2
    return pl.pallas_call(
        _ctx_kv_kernel,
        grid=(n // rows,),
        in_specs=[pl.BlockSpec((rows, d), lambda i: (i, 0)),
                  pl.BlockSpec((1, d), lambda i: (0, 0)),
                  pl.BlockSpec((1, d), lambda i: (0, 0)),
                  pl.BlockSpec(w_kv.shape, lambda i: (0, 0))],
        out_specs=[pl.BlockSpec((rows, half), lambda i: (i, 0)),
                   pl.BlockSpec((rows, half), lambda i: (i, 0))],
        out_shape=[jax.ShapeDtypeStruct((n, half), BF16)] * 2,
        compiler_params=_params(("arbitrary",)),
        name="ctx_kv",
    )(ctx2d, sh, sc, w_kv)


def _rope(x, cos_t, sin_t):
    width = x.shape[1]
    reps = width // LANES
    c = jnp.concatenate([cos_t] * reps, axis=1)
    s = jnp.concatenate([sin_t] * reps, axis=1)
    half = HEAD_DIM // 2
    lane = lax.broadcasted_iota(jnp.int32, x.shape, 1)
    first_half = (lane & (HEAD_DIM - 1)) < half
    partner = jnp.where(first_half, pltpu.roll(x, width - half, 1), pltpu.roll(x, half, 1))
    return x * c + partner * s


def _in_proj_kernel(per_b, x_ref, xp_ref, xn_ref, sh_ref, sc_ref, w_ref, cw_ref, cos_ref, sin_ref,
                    x0_ref, vx_ref, q_ref, k_ref, v_ref, h_ref, us_ref):
    n_u = cw_ref.shape[1]
    n_q = q_ref.shape[1]
    n_k = k_ref.shape[1]
    tm = x_ref.shape[0]
    m = SUBLANES
    modulate = lambda x: (_standardize(x) * (1.0 + sc_ref[0]) + sh_ref[0]).astype(BF16)
    i = pl.program_id(0)
    n_blk = n_u // LANES

    def stage(row0, val):
        for b in range(n_blk):
            us_ref[b, row0:row0 + val.shape[0], :] = val[:, b * LANES:(b + 1) * LANES]

    h_halo = modulate(jnp.concatenate([xp_ref[0], xn_ref[0]], axis=0))
    for r0 in range(0, tm, SUB_ROWS):
        rows = slice(r0, r0 + SUB_ROWS)
        h = modulate(x_ref[rows, :])
        h_ref[rows, :] = h
        if r0 == 0:
            u = _dot(jnp.concatenate([h, h_halo], axis=0), w_ref[:, :n_u])
            halo = u[SUB_ROWS:]
            stage(0, jnp.where(i % per_b == 0, 0.0, halo[:m]))
            stage(m + tm, jnp.where(i % per_b == per_b - 1, 0.0, halo[m:]))
            stage(m, u[:SUB_ROWS])
        else:
            stage(m + r0, _dot(h, w_ref[:, :n_u]))
        cos_t = cos_ref[rows, :]
        sin_t = sin_ref[rows, :]
        q = _dot(h, w_ref[:, n_u:n_u + n_q])
        q_ref[rows, :] = (_rope(q, cos_t, sin_t) * (HEAD_DIM ** -0.5)).astype(BF16)
        k = _dot(h, w_ref[:, n_u + n_q:n_u + n_q + n_k])
        k_ref[rows, :] = _rope(k, cos_t, sin_t).astype(BF16)
        v_ref[rows, :] = _dot(h, w_ref[:, n_u + n_q + n_k:]).astype(BF16)
    cw = cw_ref[...]
    c = n_u // 3
    def conv(r0, c0):
        cols = slice(c0, c0 + LANES)
        b = c0 // LANES
        return (us_ref[b, m - 1 + r0:m - 1 + r0 + SUB_ROWS, :] * cw[0:1, cols]
                + us_ref[b, m + r0:m + r0 + SUB_ROWS, :] * cw[1:2, cols]
                + us_ref[b, m + 1 + r0:m + 1 + r0 + SUB_ROWS, :] * cw[2:3, cols] + cw[3:4, cols])

    for r0 in range(0, tm, SUB_ROWS):
        rows = slice(r0, r0 + SUB_ROWS)
        for c0 in range(0, c, LANES):
            x0_ref[rows, c0:c0 + LANES] = conv(r0, c0).astype(BF16)
            vx_ref[rows, c0:c0 + LANES] = (conv(r0, c + c0) * conv(r0, 2 * c + c0)).astype(BF16)


def _in_proj(x2d, sh, sc, w_c, cw, cos_t, sin_t, seq, tm):
    t, d = x2d.shape
    per_b = seq // tm
    n_u, n_q, n_k = 3 * D_HYENA, D_ATTN, 2 * D_KV
    row = lambda i: (i, 0)
    mod = lambda i: (i // per_b, 0, 0)
    pos = lambda i: (i % per_b, 0)
    x8 = x2d.reshape(t // SUBLANES, SUBLANES, d)
    g = tm // SUBLANES
    halo = lambda f: pl.BlockSpec((1, SUBLANES, d), f)
    return pl.pallas_call(
        functools.partial(_in_proj_kernel, per_b),
        grid=(t // tm,),
        in_specs=[pl.BlockSpec((tm, d), row),
                  halo(lambda i: (jnp.maximum(i * g - 1, 0), 0, 0)),
                  halo(lambda i: (jnp.minimum((i + 1) * g, t // SUBLANES - 1), 0, 0)),
                  pl.BlockSpec((1, 1, d), mod),
                  pl.BlockSpec((1, 1, d), mod),
                  _const_spec(w_c.shape),
                  _const_spec(cw.shape),
                  pl.BlockSpec((tm, LANES), pos),
                  pl.BlockSpec((tm, LANES), pos)],
        out_specs=[pl.BlockSpec((tm, D_HYENA), row),
                   pl.BlockSpec((tm, D_HYENA), row),
                   pl.BlockSpec((tm, n_q), row),
                   pl.BlockSpec((tm, n_k), row),
                   pl.BlockSpec((tm, n_k), row),
                   pl.BlockSpec((tm, d), row)],
        out_shape=[jax.ShapeDtypeStruct((t, D_HYENA), BF16),
                   jax.ShapeDtypeStruct((t, D_HYENA), BF16),
                   jax.ShapeDtypeStruct((t, n_q), BF16),
                   jax.ShapeDtypeStruct((t, n_k), BF16),
                   jax.ShapeDtypeStruct((t, n_k), BF16),
                   jax.ShapeDtypeStruct((t, d), BF16)],
        scratch_shapes=[pltpu.VMEM((n_u // LANES, tm + 2 * SUBLANES, LANES), F32)],
        compiler_params=_params(("arbitrary",)),
        name="in_proj",
    )(x2d, x8, x8, sh, sc, w_c, cw, cos_t, sin_t)


def _filt_mlp_kernel(z_ref, w1_ref, b1_ref, w2_ref, b2_ref, w3_ref, b3_ref, w4_ref, fr_ref, t_ref, absd_ref,
                     h_ref, ss_ref):
    fr = fr_ref[...]
    a = jnp.sin(fr * (_dot3(z_ref[...], w1_ref[...]) + b1_ref[...]))
    a = jnp.sin(fr * (_dot3(a, w2_ref[...]) + b2_ref[...]))
    a = jnp.sin(fr * (_dot3(a, w3_ref[...]) + b3_ref[...]))
    half = a.shape[0]
    ss = jnp.zeros(ss_ref.shape, F32)
    for k in range(2):
        decay = jnp.exp(-t_ref[k * half:(k + 1) * half, :] * absd_ref[...])
        h = _dot3(a, w4_ref[0, k]) * decay
        ss = ss + jnp.sum(h * h, axis=0, keepdims=True)
        for s in range(half // FFT_N2):
            r0 = (k * (half // FFT_N2) + s) * TPITCH
            h_ref[r0:r0 + FFT_N2, :] = h[s * FFT_N2:(s + 1) * FFT_N2]
            h_ref[r0 + FFT_N2:r0 + TPITCH, :] = jnp.zeros((TPITCH - FFT_N2, h.shape[1]), F32)

    @pl.when(pl.program_id(0) == 0)
    def _():
        ss_ref[...] = jnp.zeros_like(ss_ref)

    ss_ref[...] += ss


def _filt_mlp(zp, w1p, b1, w2, b2, w3, b3, w4s, fr, t_col, absd, rows):
    n, c = t_col.shape[0], absd.shape[1]
    half_steps = (n // 2) // rows
    vec = lambda a: pl.BlockSpec(a.shape, lambda i: (0,) * a.ndim)
    return pl.pallas_call(
        _filt_mlp_kernel,
        grid=(n // rows,),
        in_specs=[pl.BlockSpec((rows // 2, zp.shape[1]), lambda i: (i, 0)),
                  vec(w1p), vec(b1), vec(w2), vec(b2), vec(w3), vec(b3),
                  pl.BlockSpec((1,) + w4s.shape[1:], lambda i: (i // half_steps, 0, 0, 0)),
                  vec(fr),
                  pl.BlockSpec((rows, 1), lambda i: (i, 0)),
                  vec(absd)],
        out_specs=[pl.BlockSpec((rows // FFT_N2 * TPITCH, c), lambda i: (i, 0)),
                   pl.BlockSpec((1, c), lambda i: (0, 0))],
        out_shape=[jax.ShapeDtypeStruct((n // FFT_N2 * TPITCH, c), F32), jax.ShapeDtypeStruct((1, c), F32)],
        compiler_params=_params(("arbitrary",)),
        name="filt_mlp",
    )(zp, w1p, b1, w2, b2, w3, b3, w4s, fr, t_col, absd)


def _filt_fft_kernel(h_ref, ss_ref, bias_ref, fa_ref, fb_ref, o_ref, zs_ref):
    scale = lax.rsqrt(ss_ref[...] + 1e-6)
    n1 = h_ref.shape[0] // TPITCH
    row = lax.broadcasted_iota(jnp.int32, (2 * FFT_N2, LANES), 0)
    impulse = jnp.where(row < FFT_N2, bias_ref[...], 0.0)

    def stage_a(n2, carry):
        sa = h_ref[pl.ds(n2, n1, stride=TPITCH), :].astype(BF16)
        sb = h_ref[pl.ds(n2 + 1, n1, stride=TPITCH), :].astype(BF16)
        zero = jnp.zeros_like(sa)
        rhs = jnp.concatenate([jnp.concatenate([sa, zero], axis=1), jnp.concatenate([zero, sb], axis=1)], axis=0)
        z = _dot(fa_ref[n2 // 2], rhs)
        for k in range(2):
            zk = z[:, k * LANES:(k + 1) * LANES]
            zs_ref[pl.ds(n2 + k, KP, stride=ZPITCH), :] = zk[:KP]
            zs_ref[pl.ds(FFT_N2 + n2 + k, KP, stride=ZPITCH), :] = zk[KP:]
        return carry

    lax.fori_loop(0, FFT_N2 // 2, lambda p, c: stage_a(2 * p, c), 0, unroll=16)

    def stage_b(p, carry):
        b0 = pl.multiple_of(2 * p * ZPITCH, SUBLANES)
        b1 = pl.multiple_of(b0 + ZPITCH, SUBLANES)
        z = jnp.concatenate([zs_ref[pl.ds(b0, 2 * FFT_N2), :], zs_ref[pl.ds(b1, 2 * FFT_N2), :]], axis=1)
        x = _dot(fb_ref[...], z.astype(BF16))
        o_ref[0, 2 * p] = x[:, :LANES] * scale + impulse
        o_ref[0, 2 * p + 1] = x[:, LANES:] * scale + impulse
        return carry

    lax.fori_loop(0, KHP // 2, stage_b, 0, unroll=True)


def _filt_fft(h2u, ss, bias, fa_full, fb):
    n, c = h2u.shape
    nblk = c // LANES
    return pl.pallas_call(
        _filt_fft_kernel,
        grid=(nblk,),
        in_specs=[pl.BlockSpec((n, LANES), lambda j: (0, j), pipeline_mode=pl.Buffered(1)),
                  pl.BlockSpec((1, LANES), lambda j: (0, j)),
                  pl.BlockSpec((1, LANES), lambda j: (0, j)),
                  _const_spec(fa_full.shape),
                  _const_spec(fb.shape)],
        out_specs=pl.BlockSpec((1, KHP, 2 * FFT_N2, LANES), lambda j: (j, 0, 0, 0)),
        out_shape=jax.ShapeDtypeStruct((nblk, KHP, 2 * FFT_N2, LANES), F32),
        scratch_shapes=[pltpu.VMEM((KP * ZPITCH, LANES), F32)],
        compiler_params=_params(("arbitrary",)),
        name="filt_fft",
    )(h2u, ss, bias, fa_full, fb)


def _hyena_kernel(x0_ref, vx_ref, hf_ref, fa_ref, fai_ref, fb_ref, fbi_ref, o_ref, ts_ref, zs_ref):
    n_slabs = x0_ref.shape[1] // FFT_N2

    def fill(j, carry):
        rows = pl.ds(pl.multiple_of(j * FFT_N2, FFT_N2), FFT_N2)
        ts_ref[pl.ds(pl.multiple_of(j * TPITCH, SUBLANES), FFT_N2), :] = vx_ref[0, rows, :].astype(F32)
        return carry

    lax.fori_loop(0, n_slabs, fill, 0, unroll=4)

    def stage_a(n2, carry):
        sa = ts_ref[pl.ds(n2, n_slabs, stride=TPITCH), :].astype(BF16)
        sb = ts_ref[pl.ds(n2 + 1, n_slabs, stride=TPITCH), :].astype(BF16)
        zero = jnp.zeros_like(sa)
        rhs = jnp.concatenate([jnp.concatenate([sa, zero], axis=1), jnp.concatenate([zero, sb], axis=1)], axis=0)
        z = _dot(fa_ref[n2 // 2], rhs)
        for k in range(2):
            zk = z[:, k * LANES:(k + 1) * LANES]
            zs_ref[pl.ds(n2 + k, KP, stride=ZPITCH), :] = zk[:KP]
            zs_ref[pl.ds(FFT_N2 + n2 + k, KP, stride=ZPITCH), :] = zk[KP:]
        return carry

    lax.fori_loop(0, FFT_N2 // 2, lambda p, c: stage_a(2 * p, c), 0, unroll=32)

    def stage_b(p, carry):
        b0 = pl.multiple_of(2 * p * ZPITCH, SUBLANES)
        b1 = pl.multiple_of(b0 + ZPITCH, SUBLANES)
        z = jnp.concatenate([zs_ref[pl.ds(b0, 2 * FFT_N2), :], zs_ref[pl.ds(b1, 2 * FFT_N2), :]], axis=1)
        x = _dot(fb_ref[...], z.astype(BF16))
        h = jnp.concatenate([hf_ref[0, 2 * p], hf_ref[0, 2 * p + 1]], axis=1)
        xr, xi = x[:FFT_N2], x[FFT_N2:]
        hr, hi = h[:FFT_N2], h[FFT_N2:]
        prod = jnp.concatenate([xr * hr - xi * hi, xr * hi + xi * hr], axis=0)
        y = _dot(fbi_ref[...], prod.astype(BF16))
        zs_ref[pl.ds(b0, 2 * FFT_N2), :] = y[:, :LANES]
        zs_ref[pl.ds(b1, 2 * FFT_N2), :] = y[:, LANES:]
        return carry

    lax.fori_loop(0, KHP // 2, stage_b, 0, unroll=True)

    def stage_ai(n2, carry):
        yr = zs_ref[pl.ds(n2, KP, stride=ZPITCH), :]
        yi = zs_ref[pl.ds(FFT_N2 + n2, KP, stride=ZPITCH), :]
        y = jnp.concatenate([yr, yi], axis=0).astype(BF16)
        ts_ref[pl.ds(n2, n_slabs, stride=TPITCH), :] = _dot(fai_ref[n2], y)
        return carry

    lax.fori_loop(0, FFT_N2, stage_ai, 0, unroll=64)

    def finish(j, carry):
        conv = ts_ref[pl.ds(pl.multiple_of(j * TPITCH, SUBLANES), FFT_N2), :]
        rows = pl.ds(pl.multiple_of(j * FFT_N2, FFT_N2), FFT_N2)
        o_ref[0, rows, :] = (x0_ref[0, rows, :].astype(F32) * conv).astype(BF16)
        return carry

    lax.fori_loop(0, n_slabs, finish, 0, unroll=4)


def _hyena(x0c, vx, hf, fa, fai, fb, fbi):
    b, seq, _ = x0c.shape
    nblk = D_HYENA // LANES
    n_slabs = seq // FFT_N2
    stream = pl.BlockSpec((1, seq, LANES), lambda j, i: (i, 0, j))
    return pl.pallas_call(
        _hyena_kernel,
        grid=(nblk, b),
        in_specs=[stream, stream,
                  pl.BlockSpec((1, KHP, 2 * FFT_N2, LANES), lambda j, i: (j, 0, 0, 0),
                               pipeline_mode=pl.Buffered(1)),
                  _const_spec(fa.shape), _const_spec(fai.shape),
                  _const_spec(fb.shape), _const_spec(fbi.shape)],
        out_specs=pl.BlockSpec((1, seq, LANES), lambda j, i: (i, 0, j)),
        out_shape=jax.ShapeDtypeStruct((b, seq, D_HYENA), BF16),
        scratch_shapes=[pltpu.VMEM((n_slabs * TPITCH, LANES), F32),
                        pltpu.VMEM((KP * ZPITCH, LANES), F32)],
        compiler_params=_params(("arbitrary", "arbitrary")),
        name="hyena",
    )(x0c, vx, hf, fa, fai, fb, fbi)


def _attn_kernel(sink_ref, q_ref, kp_ref, km_ref, kn_ref, vp_ref, vm_ref, vn_ref, kc_ref, vc_ref,
                 o_ref, ka_ref, va_ref):
    i = pl.program_id(1)
    n_i = pl.num_programs(1)
    tq = q_ref.shape[0]
    nqb = tq // BLOCK
    ka_ref[0:BLOCK] = kp_ref[...]
    ka_ref[BLOCK:BLOCK + tq] = km_ref[...]
    ka_ref[BLOCK + tq:] = kn_ref[...]
    va_ref[0:BLOCK] = vp_ref[...]
    va_ref[BLOCK:BLOCK + tq] = vm_ref[...]
    va_ref[BLOCK + tq:] = vn_ref[...]

    qi = lax.broadcasted_iota(jnp.int32, (BLOCK, BLOCK), 0)
    kj = lax.broadcasted_iota(jnp.int32, (BLOCK, BLOCK), 1)
    lane = lax.broadcasted_iota(jnp.int32, (BLOCK, LANES), 1)
    low = lane < HEAD_DIM
    half = GQA_GROUP // 2
    rows2 = half * BLOCK
    hrow = lax.broadcasted_iota(jnp.int32, (GQA_GROUP * BLOCK, 1), 0) // BLOCK
    head_order = [hh for hh in range(GQA_GROUP) if hh % 2 == 0] + [hh for hh in range(GQA_GROUP) if hh % 2 == 1]
    one = jnp.ones((), BF16)

    def with_ones(v):
        lanes_low = lax.broadcasted_iota(jnp.int32, v.shape, 1) < HEAD_DIM
        return jnp.where(lanes_low, v, one), jnp.where(lanes_low, one, v)

    for g in range(N_KV_HEADS):
        gl = slice(g * LANES, (g + 1) * LANES)
        vc_even, vc_odd = with_ones(vc_ref[:, gl])
        sink = jnp.zeros((GQA_GROUP * BLOCK, 1), F32)
        for pos, hh in enumerate(head_order):
            sink = jnp.where(hrow == pos, sink_ref[g * GQA_GROUP + hh], sink)
        for j in range(nqb):
            prev_ok = kj >= qi
            next_ok = kj <= qi
            if j == 0:
                prev_ok = prev_ok & (i > 0)
            if j == nqb - 1:
                next_ok = next_ok & (i < n_i - 1)
            bias_p = jnp.concatenate([jnp.where(prev_ok, 0.0, NEG_INF).astype(F32)] * GQA_GROUP, axis=0)
            bias_n = jnp.concatenate([jnp.where(next_ok, 0.0, NEG_INF).astype(F32)] * GQA_GROUP, axis=0)
            qb = q_ref[j * BLOCK:(j + 1) * BLOCK, :]
            parts = []
            for hh in head_order:
                h = g * GQA_GROUP + hh
                qp = qb[:, (h // 2) * LANES:(h // 2 + 1) * LANES]
                parts.append(jnp.where(low if h % 2 == 0 else ~low, qp, jnp.zeros_like(qp)))
            qs = jnp.concatenate(parts, axis=0)
            kw = ka_ref[j * BLOCK:(j + 3) * BLOCK, gl]
            vw_even, vw_odd = with_ones(va_ref[j * BLOCK:(j + 3) * BLOCK, gl])
            s_w = _dot_nt(qs, kw)
            s_p = s_w[:, :BLOCK] + bias_p
            s_m = s_w[:, BLOCK:2 * BLOCK]
            s_n = s_w[:, 2 * BLOCK:] + bias_n
            s_c = _dot_nt(qs, kc_ref[:, gl])
            m = jnp.maximum(jnp.maximum(jnp.max(jnp.maximum(jnp.maximum(s_p, s_m), s_n), axis=1, keepdims=True),
                                        jnp.max(s_c, axis=1, keepdims=True)), sink)
            e_w = jnp.concatenate([jnp.exp(s_p - m), jnp.exp(s_m - m), jnp.exp(s_n - m)], axis=1).astype(BF16)
            e_c = jnp.exp(s_c - m).astype(BF16)
            e_sink = jnp.exp(sink - m)
            outs = []
            for par, (vw, vcx) in enumerate(((vw_even, vc_even), (vw_odd, vc_odd))):
                rs = slice(par * rows2, (par + 1) * rows2)
                acc = _dot(e_w[rs], vw) + _dot(e_c[rs], vcx)
                den = pltpu.roll(acc, HEAD_DIM, 1) + e_sink[rs]
                outs.append(acc / den)
            for pp in range(half):
                pair = jnp.where(low, outs[0][pp * BLOCK:(pp + 1) * BLOCK], outs[1][pp * BLOCK:(pp + 1) * BLOCK])
                col = (g * half + pp) * LANES
                o_ref[j * BLOCK:(j + 1) * BLOCK, col:col + LANES] = pair.astype(BF16)


def _attention(sinks, q, kd, vd, kc, vc, batch, seq, tq):
    t = q.shape[0]
    per_b = seq // tq
    nqb = tq // BLOCK
    nb = seq // BLOCK
    n_ctx = kc.shape[0] // batch
    main = lambda b, i: (b * per_b + i, 0)
    prev = lambda b, i: (b * nb + jnp.maximum(i * nqb - 1, 0), 0)
    nxt = lambda b, i: (b * nb + jnp.minimum(i * nqb + nqb, nb - 1), 0)
    kvw = kd.shape[1]
    return pl.pallas_call(
        _attn_kernel,
        grid=(batch, per_b),
        in_specs=[pl.BlockSpec(memory_space=pltpu.SMEM),
                  pl.BlockSpec((tq, D_ATTN), main),
                  pl.BlockSpec((BLOCK, kvw), prev), pl.BlockSpec((tq, kvw), main), pl.BlockSpec((BLOCK, kvw), nxt),
                  pl.BlockSpec((BLOCK, kvw), prev), pl.BlockSpec((tq, kvw), main), pl.BlockSpec((BLOCK, kvw), nxt),
                  pl.BlockSpec((n_ctx, kvw), lambda b, i: (b, 0)),
                  pl.BlockSpec((n_ctx, kvw), lambda b, i: (b, 0))],
        out_specs=pl.BlockSpec((tq, D_ATTN), main),
        out_shape=jax.ShapeDtypeStruct((t, D_ATTN), BF16),
        scratch_shapes=[pltpu.VMEM((tq + 2 * BLOCK, kvw), BF16),
                        pltpu.VMEM((tq + 2 * BLOCK, kvw), BF16)],
        compiler_params=_params(("arbitrary", "arbitrary")),
        name="attention",
    )(sinks, q, kd, kd, kd, vd, vd, vd, kc, vc)


def _route(t, wr_hi_ref, wr_hl_ref, br_ref):
    th, tl = _split(t)
    both = _dot(th, wr_hl_ref[...])
    logits = both[:, :LANES] + both[:, LANES:] + _dot(tl, wr_hi_ref[...]) + br_ref[...]
    lane_i = lax.broadcasted_iota(jnp.int32, logits.shape, 1)
    lane = lane_i.astype(F32)
    grp_of_lane = (lane_i >> 2).astype(F32)
    ninf = -jnp.inf
    far = float(LANES)
    is_g = (lane_i >= N_EXPERTS) & (lane_i < N_EXPERTS + N_GROUPS)
    glog = jnp.where(is_g, logits, ninf)
    gmax = jnp.max(glog, axis=1, keepdims=True)
    gidx = jnp.min(jnp.where(glog == gmax, lane - float(N_EXPERTS), far), axis=1, keepdims=True)
    group_p = 1.0 / jnp.sum(jnp.exp(glog - gmax), axis=1, keepdims=True)
    in_grp = (lane_i < N_EXPERTS) & (grp_of_lane == gidx)
    elog = jnp.where(in_grp, logits, ninf)
    v1 = jnp.max(elog, axis=1, keepdims=True)
    i1 = jnp.min(jnp.where(elog == v1, lane, far), axis=1, keepdims=True)
    elog2 = jnp.where(lane == i1, ninf, elog)
    v2 = jnp.max(elog2, axis=1, keepdims=True)
    i2 = jnp.min(jnp.where(elog2 == v2, lane, far), axis=1, keepdims=True)
    e = jnp.exp(v2 - v1)
    w1 = group_p / (1.0 + e)
    w2 = group_p * e / (1.0 + e)
    gate = jnp.where(lane == i1, w1, 0.0) + jnp.where(lane == i2, w2, 0.0)
    return gate + jnp.where(lane_i == GROUP_LANE, gidx, 0.0)


def _mix_out_kernel(x_ref, h_ref, yh_ref, ya_ref, g1_ref, sh2_ref, sc2_ref,
                    wg_ref, wbh_ref, wba_ref, wo_ref, lng_ref, lnb_ref, wrh_ref, wrhl_ref, br_ref,
                    x1_ref, gate_ref, gsel_ref, *t_refs):
    d = x_ref.shape[1]
    for r0 in range(0, x_ref.shape[0], SUB_ROWS):
        rows = slice(r0, r0 + SUB_ROWS)
        x = x_ref[rows, :]
        h = h_ref[rows, :]
        g_hy = jax.nn.sigmoid(_dot(h, wg_ref[:, :d]))
        merged = g_hy * _dot(yh_ref[rows, :], wbh_ref[...])
        g_at = jax.nn.sigmoid(_dot(h, wg_ref[:, d:]))
        merged = merged + g_at * _dot(ya_ref[rows, :], wba_ref[...])
        mix = _dot(merged.astype(BF16), wo_ref[...])
        x1 = _standardize(DEEPNORM_ALPHA * x + g1_ref[0] * mix) * lng_ref[...] + lnb_ref[...]
        x1_ref[rows, :] = x1
        t = _standardize(x1) * (1.0 + sc2_ref[0]) + sh2_ref[0]
        gate = _route(t, wrh_ref, wrhl_ref, br_ref)
        gate_ref[rows, :] = gate
        grp = gate.T[GROUP_LANE:GROUP_LANE + 1, :]
        gsel_ref[r0 // LANES:(r0 + SUB_ROWS) // LANES, :] = jnp.concatenate(
            [grp[:, k * LANES:(k + 1) * LANES] for k in range(SUB_ROWS // LANES)], axis=0)
        for t_ref, piece in zip(t_refs, _pack_pieces(t)):
            t_ref[rows, :] = piece


def _mix_out(x2d, h, yh, ya, mods, w_g, w_bh, w_ba, w_o, ln_g, ln_b, wr_hi, wr_hl, br, seq, tm):
    t, d = x2d.shape
    per_b = seq // tm
    row = lambda i: (i, 0)
    mod = lambda i: (i // per_b, 0, 0)
    mspec = pl.BlockSpec((1, 1, d), mod)
    g1, sh2, sc2 = mods
    return pl.pallas_call(
        _mix_out_kernel,
        grid=(t // tm,),
        in_specs=[pl.BlockSpec((tm, d), row),
                  pl.BlockSpec((tm, d), row),
                  pl.BlockSpec((tm, yh.shape[1]), row),
                  pl.BlockSpec((tm, ya.shape[1]), row),
                  mspec, mspec, mspec,
                  _const_spec(w_g.shape), _const_spec(w_bh.shape), _const_spec(w_ba.shape),
                  _const_spec(w_o.shape), _const_spec(ln_g.shape), _const_spec(ln_b.shape),
                  _const_spec(wr_hi.shape), _const_spec(wr_hl.shape), _const_spec(br.shape)],
        out_specs=[pl.BlockSpec((tm, d), row), pl.BlockSpec((tm, LANES), row),
                   pl.BlockSpec((tm // LANES, LANES), row)]
        + [pl.BlockSpec((tm, SC_ROW_WORDS), row)] * N_PIECES,
        out_shape=[jax.ShapeDtypeStruct((t, d), F32), jax.ShapeDtypeStruct((t, LANES), F32),
                   jax.ShapeDtypeStruct((t // LANES, LANES), F32)]
        + [jax.ShapeDtypeStruct((t, SC_ROW_WORDS), jnp.uint32)] * N_PIECES,
        compiler_params=_params(("arbitrary",)),
        name="mix_out",
    )(x2d, h, yh, ya, g1, sh2, sc2, w_g, w_bh, w_ba, w_o, ln_g, ln_b, wr_hi, wr_hl, br)


def _slots_kernel(g_ref, upper_ref, lower_ref, dest_ref, meta_ref):
    gsel = g_ref[...]
    dest = jnp.zeros(gsel.shape, F32)
    base = jnp.zeros((1, 1), F32)
    chunk_start = lax.broadcasted_iota(jnp.int32, (1, LANES), 1).astype(F32) * float(MOE_CHUNK)
    owner = jnp.zeros((1, LANES), F32)
    for g in range(N_GROUPS):
        onehot = jnp.where(gsel == float(g), 1.0, 0.0)
        in_row = _dot(onehot.astype(BF16), upper_ref[...])
        row_tot = jnp.sum(onehot, axis=1, keepdims=True)
        rows_before = _dot(lower_ref[...], jnp.broadcast_to(row_tot, onehot.shape).astype(BF16))
        dest = dest + onehot * (base + rows_before + in_row)
        if g > 0:
            owner = owner + jnp.where(chunk_start >= base, 1.0, 0.0)
        n_g = jnp.sum(row_tot, axis=0, keepdims=True)
        base = base + jnp.floor((n_g + float(MOE_CHUNK - 1)) * (1.0 / MOE_CHUNK)) * float(MOE_CHUNK)
    dest_ref[...] = dest.astype(jnp.int32)
    row = lax.broadcasted_iota(jnp.int32, meta_ref.shape, 0)
    meta_ref[...] = jnp.where(row == 0, owner, base * (1.0 / MOE_CHUNK)).astype(jnp.int32)


def _slots(gsel):
    r = gsel.shape[0]
    upper = jnp.asarray(np.triu(np.ones((LANES, LANES), np.float32), 1)).astype(BF16)
    lower = jnp.asarray(np.tril(np.ones((r, r), np.float32), -1)).astype(BF16)
    full = lambda a: pl.BlockSpec(a.shape, lambda i: (0,) * a.ndim)
    return pl.pallas_call(
        _slots_kernel,
        grid=(1,),
        in_specs=[full(gsel), full(upper), full(lower)],
        out_specs=[pl.BlockSpec((r, LANES), lambda i: (0, 0)), pl.BlockSpec((SUBLANES, LANES), lambda i: (0, 0))],
        out_shape=[jax.ShapeDtypeStruct((r, LANES), jnp.int32), jax.ShapeDtypeStruct((SUBLANES, LANES), jnp.int32)],
        compiler_params=_params(("arbitrary",)),
        name="moe_slots",
    )(gsel, upper, lower)


def _experts_kernel(owner_ref, used_ref, *refs):
    n_p = N_PIECES
    t_refs, gate_ref = refs[:n_p], refs[n_p]
    wg_ref, wu_ref, wd_ref = refs[n_p + 1:n_p + 4]
    y_refs = refs[n_p + 4:2 * n_p + 4]
    wg16_ref, wu16_ref, wd16_ref = refs[2 * n_p + 4:]
    c = pl.program_id(0)
    f = D_EXPERT

    @pl.when((c == 0) | (owner_ref[c] != owner_ref[jnp.maximum(c - 1, 0)]))
    def _():
        wg16_ref[...] = wg_ref[...].astype(BF16)
        wu16_ref[...] = wu_ref[...].astype(BF16)
        for e in range(EXPERTS_PER_GROUP):
            wd16_ref[e * f:(e + 1) * f, :] = wd_ref[e].astype(BF16)

    @pl.when(c < used_ref[0])
    def _():
        t = _unpack_pieces([r[...] for r in t_refs]).astype(BF16)
        gate = gate_ref[...]
        lane = lax.broadcasted_iota(jnp.int32, gate.shape, 1)
        first = owner_ref[c] * EXPERTS_PER_GROUP
        parts = []
        for e in range(EXPERTS_PER_GROUP):
            a = _dot(t, wg16_ref[e])
            u = _dot(t, wu16_ref[e])
            ge = jnp.sum(jnp.where(lane == first + e, gate, 0.0), axis=1, keepdims=True)
            parts.append((a * jax.nn.sigmoid(a) * u * ge).astype(BF16))
        y = _dot(jnp.concatenate(parts, axis=1), wd16_ref[...])
        for y_ref, piece in zip(y_refs, _pack_pieces(y)):
            y_ref[...] = piece

    @pl.when(c >= used_ref[0])
    def _():
        for y_ref in y_refs:
            y_ref[...] = jnp.zeros_like(y_ref)


def _experts(owner, used, t_pieces, gate_sorted, w_gate, w_up, w_down):
    n_slots = gate_sorted.shape[0]
    d, f = w_gate.shape[1:]
    row = lambda c, owner, used: (c, 0)
    by_owner = lambda c, owner, used: (owner[c], 0, 0)
    piece = pl.BlockSpec((MOE_CHUNK, SC_ROW_WORDS), row)
    group_of_experts = lambda w: pl.BlockSpec((EXPERTS_PER_GROUP,) + w.shape[1:], by_owner)
    return pl.pallas_call(
        _experts_kernel,
        grid_spec=pltpu.PrefetchScalarGridSpec(
            num_scalar_prefetch=2,
            grid=(n_slots // MOE_CHUNK,),
            in_specs=[piece] * len(t_pieces) + [pl.BlockSpec((MOE_CHUNK, LANES), row),
                                                group_of_experts(w_gate), group_of_experts(w_up),
                                                group_of_experts(w_down)],
            out_specs=[piece] * len(t_pieces),
            scratch_shapes=[pltpu.VMEM((EXPERTS_PER_GROUP, d, f), BF16),
                            pltpu.VMEM((EXPERTS_PER_GROUP, d, f), BF16),
                            pltpu.VMEM((EXPERTS_PER_GROUP * f, d), BF16)],
        ),
        out_shape=[jax.ShapeDtypeStruct((n_slots, SC_ROW_WORDS), jnp.uint32)] * len(t_pieces),
        compiler_params=_params(("arbitrary",)),
        name="moe_experts",
    )(owner, used, *t_pieces, gate_sorted, w_gate, w_up, w_down)


def _final_kernel(x1_ref, g2_ref, lng_ref, lnb_ref, *refs):
    y_refs, o_ref = refs[:-1], refs[-1]
    y = _unpack_pieces([r[...] for r in y_refs])
    o_ref[...] = _standardize(DEEPNORM_ALPHA * x1_ref[...] + g2_ref[0] * y) * lng_ref[...] + lnb_ref[...]


def _final(x1, g2, ln_g, ln_b, y_pieces, seq, tm):
    t, d = x1.shape
    per_b = seq // tm
    row = lambda i: (i, 0)
    return pl.pallas_call(
        _final_kernel,
        grid=(t // tm,),
        in_specs=[pl.BlockSpec((tm, d), row), pl.BlockSpec((1, 1, d), lambda i: (i // per_b, 0, 0)),
                  _const_spec(ln_g.shape), _const_spec(ln_b.shape)]
        + [pl.BlockSpec((tm, SC_ROW_WORDS), row)] * len(y_pieces),
        out_specs=pl.BlockSpec((tm, d), row),
        out_shape=jax.ShapeDtypeStruct((t, d), F32),
        compiler_params=_params(("arbitrary",)),
        name="moe_final",
    )(x1, g2, ln_g, ln_b, *y_pieces)


def _sc_move_rows(tables, idx, n_out, scatter, name):
    n = idx.shape[0]
    mesh = plsc.VectorSubcoreMesh(core_axis_name="c", subcore_axis_name="s")
    out_type = [jax.ShapeDtypeStruct((n_out, t.shape[1]), t.dtype) for t in tables]
    window = lambda i: (i, 0)
    index_win = pl.BlockSpec((1, SC_WINDOW), lambda i: (0, i))
    split = dict(core_axis_name=("c", "s"), dimension_semantics=(pltpu.PARALLEL,))

    @functools.partial(pl.kernel, out_type=out_type, mesh=mesh, scratch_types=[], name=name)
    def move(*refs):
        srcs, i_hbm, dsts = refs[:len(tables)], refs[len(tables)], refs[len(tables) + 1:]
        for src, dst in zip(srcs, dsts):
            rows_win = pl.BlockSpec((SC_WINDOW, src.shape[1]), window)
            if scatter:
                def body(x_vmem, i_vmem, dst=dst):
                    pltpu.sync_copy(x_vmem, dst.at[i_vmem.at[0]])

                pltpu.emit_pipeline(body, grid=(n // SC_WINDOW,), in_specs=[rows_win, index_win],
                                    out_specs=[], **split)(src, i_hbm)
            else:
                def body(i_vmem, o_vmem, src=src):
                    pltpu.sync_copy(src.at[i_vmem.at[0]], o_vmem)

                pltpu.emit_pipeline(body, grid=(n // SC_WINDOW,), in_specs=[index_win],
                                    out_specs=[rows_win], **split)(i_hbm, dst)

    return move(*tables, idx.reshape(1, n))


def _dft_tables(seq):
    n = 2 * seq
    n1_full = n // FFT_N2
    n1_data = seq // FFT_N2
    k1 = np.arange(KH, dtype=np.float64)[None, :, None]
    n2 = np.arange(FFT_N2, dtype=np.float64)[:, None, None]
    n1 = np.arange(n1_full, dtype=np.float64)[None, None, :]
    ang = 2.0 * np.pi * k1 * (FFT_N2 * n1 + n2) / n
    fa = np.zeros((FFT_N2, 2 * KP, n1_full))
    fa[:, :KH] = np.cos(ang)
    fa[:, KP:KP + KH] = -np.sin(ang)
    wgt = np.full((KH,), 2.0)
    wgt[0] = 1.0
    wgt[KH - 1] = 1.0
    fai = np.zeros((FFT_N2, n1_data, 2 * KP))
    angt = np.transpose(ang[:, :, :n1_data], (0, 2, 1))
    fai[:, :, :KH] = np.cos(angt) * wgt / n
    fai[:, :, KP:KP + KH] = -np.sin(angt) * wgt / n
    kk = np.arange(FFT_N2, dtype=np.float64)
    a2 = 2.0 * np.pi * np.outer(kk, kk) / FFT_N2
    fr, fi = np.cos(a2), -np.sin(a2)
    fb = np.block([[fr, -fi], [fi, fr]])
    fbi = np.block([[fr, fi], [-fi, fr]])
    f32 = lambda a: jnp.asarray(a.astype(np.float32))
    pairs = lambda a: np.concatenate([a[0::2], a[1::2]], axis=2)
    return f32(pairs(fa[:, :, :n1_data])), f32(pairs(fa)), f32(fai), f32(fb), f32(fbi)


def _filter_features(seq, rows):
    f32 = np.float32
    t = np.linspace(0.0, 1.0, seq, dtype=f32)
    w = (f32(2.0 * math.pi) * np.arange(seq, dtype=f32) / f32(seq)).astype(f32)
    t2 = np.concatenate([t, t[::-1]])
    w2 = np.concatenate([w, w[::-1]])
    bands = np.linspace(1e-4, FILTER_BANDS - 1, FILTER_BANDS, dtype=f32)
    max_decay = math.log(DECAY_TARGET) / FAST_DECAY_PCT
    min_decay = math.log(DECAY_TARGET) / SLOW_DECAY_PCT
    deltas = jnp.linspace(min_decay, max_decay, D_HYENA, dtype=F32)

    def feats(k):
        tp = t2.reshape(-1, 2, rows // 2)[:, k].reshape(-1, 1)
        wp = w2.reshape(-1, 2, rows // 2)[:, k].reshape(-1, 1)
        pad = np.zeros((tp.shape[0], FILTER_ORDER - FILTER_EMB), f32)
        arg = (bands * wp).astype(f32)
        return np.concatenate([tp, np.cos(arg), -np.sin(arg), pad], axis=-1).astype(f32)

    zp = jnp.asarray(np.concatenate([feats(0), feats(1)], axis=-1))
    return zp, jnp.asarray(t2)[:, None], jnp.abs(deltas)[None, :]


def _rope_tables(seq):
    f32 = np.float32
    rows = seq // GRID_W
    row = np.repeat(np.arange(rows, dtype=f32), GRID_W)
    col = np.tile(np.arange(GRID_W, dtype=f32), rows)
    half = HEAD_DIM // 2
    inv_freq = (f32(ROPE_BASE) ** (-np.arange(0, half, 2, dtype=f32) / f32(half))).astype(f32)
    ang = np.concatenate([row[:, None] * inv_freq, col[:, None] * inv_freq], axis=-1).astype(f32)
    cos, sin = np.cos(ang), np.sin(ang)
    c64 = np.concatenate([cos, cos], axis=-1)
    s64 = np.concatenate([-sin, sin], axis=-1)
    return (jnp.asarray(np.concatenate([c64, c64], axis=-1).astype(f32)),
            jnp.asarray(np.concatenate([s64, s64], axis=-1).astype(f32)))


def _head_perm(n_heads):
    idx = []
    for h in range(n_heads):
        base = h * HEAD_DIM
        idx += [base + 2 * j for j in range(HEAD_DIM // 2)]
        idx += [base + 2 * j + 1 for j in range(HEAD_DIM // 2)]
    return np.asarray(idx, dtype=np.int32)


def _dup_heads(w):
    parts = []
    for g in range(N_KV_HEADS):
        blk = w[:, g * HEAD_DIM:(g + 1) * HEAD_DIM]
        parts += [blk, blk]
    return jnp.concatenate(parts, axis=1)


def kernel(x, c, ctx, c_ctx, ada_w, ada_b, w_in, hy_conv_w, hy_conv_b, hy_w1, hy_b1, hy_w2, hy_b2, hy_w3, hy_b3, hy_w4, hy_freq, hy_bias, attn_sinks, w_branch_hy, w_branch_attn, w_out, ln1_g, ln1_b, w_group, b_group, w_router, b_router, w_gate_e, w_up_e, w_down_e, ln2_g, ln2_b):
    batch, seq, d = x.shape
    n_ctx = ctx.shape[1]
    assert d == D_MODEL and ada_w.shape[0] == DEPTH == 1
    assert 2 * seq == FFT_N2 * FFT_N2 and seq % TM_PROJ == 0
    l = 0

    w = w_in[l]
    s0, s1, s2, s3 = 3 * D_HYENA, 3 * D_HYENA + D_ATTN, 3 * D_HYENA + D_ATTN + D_KV, 3 * D_HYENA + D_ATTN + 2 * D_KV
    w_q = w[:, s0:s1][:, _head_perm(N_HEADS)]
    w_k = _dup_heads(w[:, s1:s2][:, _head_perm(N_KV_HEADS)])
    w_v = _dup_heads(w[:, s2:s3])
    w_c = jnp.concatenate([w[:, :s0], w_q, w_k, w_v], axis=1).astype(BF16)
    w_kv = jnp.concatenate([w_k, w_v], axis=1).astype(BF16)
    w_g = w[:, s3:].astype(BF16)
    w_bh = w_branch_hy[l].astype(BF16)
    w_ba = w_branch_attn[l].astype(BF16)
    w_o = w_out[l].astype(BF16)
    pad_lanes = LANES - N_EXPERTS - N_GROUPS
    wr = jnp.concatenate([w_router[l], w_group[l], jnp.zeros((d, pad_lanes), F32)], axis=1)
    wr_hi = wr.astype(BF16)
    wr_hl = jnp.concatenate([wr_hi, (wr - wr_hi.astype(F32)).astype(BF16)], axis=1)
    br = jnp.concatenate([b_router[l], b_group[l], jnp.zeros((pad_lanes,), F32)])[None, :]
    cw = jnp.concatenate([hy_conv_w[l][:, 0, :], hy_conv_b[l][None, :],
                          jnp.zeros((SUBLANES - SHORT_CONV - 1, 3 * D_HYENA), F32)], axis=0)
    row2 = lambda a: a.reshape(1, -1)
    pair = lambda a: jnp.concatenate([a, a], axis=-1)
    zero_o = jnp.zeros((FILTER_ORDER, FILTER_ORDER), F32)
    bdiag = lambda a: jnp.concatenate([jnp.concatenate([a, zero_o], axis=1),
                                       jnp.concatenate([zero_o, a], axis=1)], axis=0)
    w1p = bdiag(jnp.concatenate([hy_w1[l], jnp.zeros((FILTER_ORDER - FILTER_EMB, FILTER_ORDER), F32)], axis=0))
    w4h = jnp.transpose(hy_w4[l].reshape(FILTER_ORDER, 2, D_HYENA), (1, 0, 2))
    zero_w4 = jnp.zeros_like(w4h)
    w4s = jnp.stack([jnp.concatenate([w4h, zero_w4], axis=1),
                     jnp.concatenate([zero_w4, w4h], axis=1)], axis=1)

    fa, fa_full, fai, fb, fbi = _dft_tables(seq)
    fa, fa_full, fai, fb, fbi = (a.astype(BF16) for a in (fa, fa_full, fai, fb, fbi))
    zp, t_col, absd = _filter_features(seq, FILT_ROWS)
    cos_t, sin_t = _rope_tables(seq)

    cond = jnp.concatenate([c, c_ctx[None], jnp.zeros((SUBLANES - batch - 1, d), F32)], axis=0)
    mods = _adaln(cond, ada_w[l], ada_b[l])
    m6 = [mods[:, k * d:(k + 1) * d].reshape(SUBLANES, 1, d) for k in range(6)]
    sh1, sc1, g1, sh2, sc2, g2 = m6

    kc, vc = _ctx_kv(ctx.reshape(batch * n_ctx, d), sh1[batch], sc1[batch], w_kv, n_ctx)

    x2d = x.reshape(batch * seq, d)
    x0c, vx, q, kd, vd, h_mod = _in_proj(x2d, sh1, sc1, w_c, cw, cos_t, sin_t, seq, TM_PROJ)
    h2u, ss = _filt_mlp(zp, w1p, pair(row2(hy_b1[l])), bdiag(hy_w2[l]), pair(row2(hy_b2[l])),
                        bdiag(hy_w3[l]), pair(row2(hy_b3[l])), w4s, pair(row2(hy_freq[l])), t_col, absd, FILT_ROWS)
    hf = _filt_fft(h2u, ss, row2(hy_bias[l]), fa_full, fb)
    y_hy = _hyena(x0c.reshape(batch, seq, D_HYENA), vx.reshape(batch, seq, D_HYENA), hf, fa, fai, fb, fbi)
    y_at = _attention(attn_sinks[l], q, kd, vd, kc, vc, batch, seq, TQ_ATTN)
    x1, gate, gsel, *t_pieces = _mix_out(x2d, h_mod, y_hy.reshape(batch * seq, D_HYENA), y_at, (g1, sh2, sc2),
                                         w_g, w_bh, w_ba, w_o, row2(ln1_g[l]), row2(ln1_b[l]), wr_hi, wr_hl, br,
                                         seq, TM_PROJ)

    n_tok = batch * seq
    n_slots = n_tok + N_GROUPS * MOE_CHUNK
    dest2d, meta = _slots(gsel)
    dest = dest2d.reshape(n_tok)
    *t_sorted, gate_sorted = _sc_move_rows(t_pieces + [gate], dest, n_slots, True, "moe_sort")
    y_sorted = _experts(meta[0], meta[1, :1], t_sorted, gate_sorted, w_gate_e[l], w_up_e[l], w_down_e[l])
    y_pieces = _sc_move_rows(y_sorted, dest, n_tok, False, "moe_unsort")
    out = _final(x1, g2, row2(ln2_g[l]), row2(ln2_b[l]), y_pieces, seq, TM_FINAL)
    return out.reshape(batch, seq, d)
```
